```python
import math
import jax
import jax.numpy as jnp
from jax import lax
import numpy as np

D_MODEL = 1024
BATCH = 8
SEQ = 2048
DEPTH = 1
DEC_BATCH = 128
DEC_SEQ = 1
PAST_LEN = 8192
PAGE_SIZE = 128

N_META = 16
SSD_HEADS = 8
SSD_HEAD_DIM = 64
SSD_WIDTH = SSD_HEADS * SSD_HEAD_DIM
SSD_GROUPS = 2
D_STATE = 64
CONV_W = 4
CONV_DIM = SSD_WIDTH + 2 * SSD_GROUPS * D_STATE
SSD_CHUNK = 128
MLA_HEADS = 8
QK_NOPE = 64
QK_ROPE = 32
QK_DIM = QK_NOPE + QK_ROPE
V_DIM = 64
MLA_WIDTH = MLA_HEADS * V_DIM
Q_LORA = 384
KV_LORA = 256
ROPE_THETA = 10000.0
Q_BLOCK = 128
ATTN_SCALE = QK_DIM ** -0.5
D_FF = -(-8 * D_MODEL // (3 * 256)) * 256
EPS = 1e-6
IN_COLS = SSD_WIDTH + CONV_DIM + SSD_HEADS + Q_LORA + KV_LORA + QK_ROPE
IN_SPLITS = (SSD_WIDTH, SSD_WIDTH + CONV_DIM, SSD_WIDTH + CONV_DIM + SSD_HEADS,
             SSD_WIDTH + CONV_DIM + SSD_HEADS + Q_LORA, IN_COLS - QK_ROPE)

kernel_name = 'hybrid_ssd_mla_meta_decoder_step'


def rmsnorm(x, g):
    xf = x.astype(jnp.float32)
    r = lax.rsqrt(jnp.mean(xf * xf, axis=-1, keepdims=True) + EPS)
    return (xf * r * g.astype(jnp.float32)).astype(x.dtype)


def rope_angles(pos):
    inv = ROPE_THETA ** (-jnp.arange(QK_ROPE // 2, dtype=jnp.float32) * (2.0 / QK_ROPE))
    ang = pos.astype(jnp.float32)[:, None] * inv[None, :]
    return jnp.cos(ang), jnp.sin(ang)


def apply_rope(x, cos, sin):
    xf = x.astype(jnp.float32)
    x1, x2 = xf[..., :QK_ROPE // 2], xf[..., QK_ROPE // 2:]
    return jnp.concatenate([x1 * cos - x2 * sin, x1 * sin + x2 * cos], axis=-1).astype(x.dtype)


def head_gain(g_nope, g_rope):
    return jnp.concatenate([g_nope, g_rope, g_rope], axis=-1)


def project(x, g, w):
    return jnp.split(rmsnorm(x, g) @ w, IN_SPLITS, axis=-1)


def causal_conv(xpad, w, b):
    L = xpad.shape[1] - (CONV_W - 1)
    out = b
    for k in range(CONV_W):
        out = out + xpad[:, k:k + L] * w[k]
    return out


def ssd_scan(x, dt, A, Bm, Cm, h0, chunk):
    b, l, h, p = x.shape
    n = Bm.shape[-1]
    c = l // chunk
    rep = h // Bm.shape[2]
    Bh = jnp.repeat(Bm.astype(jnp.float32), rep, axis=2).reshape(b, c, chunk, h, n)
    Ch = jnp.repeat(Cm.astype(jnp.float32), rep, axis=2).reshape(b, c, chunk, h, n)
    xf = x.astype(jnp.float32).reshape(b, c, chunk, h, p)
    dtc = dt.reshape(b, c, chunk, h)
    acs = jnp.cumsum(dtc * A, axis=2)
    seg = acs[:, :, :, None, :] - acs[:, :, None, :, :]
    causal = jnp.tril(jnp.ones((chunk, chunk), dtype=bool))[None, None, :, :, None]
    Lmat = jnp.exp(jnp.where(causal, seg, -jnp.inf))
    scores = jnp.einsum('bcihn,bcjhn->bcijh', Ch, Bh) * Lmat * dtc[:, :, None, :, :]
    y_diag = jnp.einsum('bcijh,bcjhp->bcihp', scores, xf)
    decay = jnp.exp(acs[:, :, -1:, :] - acs)
    states = jnp.einsum('bcjhn,bcjh,bcjhp->bchpn', Bh, decay * dtc, xf)
    chunk_decay = jnp.exp(acs[:, :, -1, :])

    def step(hc, inp):
        st, dec = inp
        return hc * dec[:, :, None, None] + st, hc

    h_final, h_in = lax.scan(step, h0.astype(jnp.float32),
                             (jnp.transpose(states, (1, 0, 2, 3, 4)), jnp.transpose(chunk_decay, (1, 0, 2))))
    h_in = jnp.transpose(h_in, (1, 0, 2, 3, 4))
    y_off = jnp.einsum('bcihn,bchpn,bcih->bcihp', Ch, h_in, jnp.exp(acs))
    return (y_diag + y_off).reshape(b, l, h, p), h_final


def ssd_branch(z, xpad, dt_raw, h0, segments, conv_w, conv_b, dt_bias, a_log, d_skip, norm_g):
    xa = jax.nn.silu(causal_conv(xpad, conv_w, conv_b))
    b, t = xa.shape[:2]
    gn = SSD_GROUPS * D_STATE
    xs = xa[..., :SSD_WIDTH].reshape(b, t, SSD_HEADS, SSD_HEAD_DIM)
    Bm = xa[..., SSD_WIDTH:SSD_WIDTH + gn].reshape(b, t, SSD_GROUPS, D_STATE)
    Cm = xa[..., SSD_WIDTH + gn:].reshape(b, t, SSD_GROUPS, D_STATE)
    dt = jax.nn.softplus(dt_raw.astype(jnp.float32) + dt_bias.astype(jnp.float32))
    A = -jnp.exp(a_log.astype(jnp.float32))
    ys = []
    h = h0.astype(jnp.float32)
    start = 0
    for length, chunk in segments:
        sl = slice(start, start + length)
        y, h = ssd_scan(xs[:, sl], dt[:, sl], A, Bm[:, sl], Cm[:, sl], h, chunk)
        ys.append(y)
        start += length
    y = jnp.concatenate(ys, axis=1)
    y = y + d_skip.astype(jnp.float32)[:, None] * xs.astype(jnp.float32)
    y = y.reshape(b, t, SSD_WIDTH) * jax.nn.silu(z.astype(jnp.float32))
    yg = y.reshape(b, t, SSD_GROUPS, SSD_WIDTH // SSD_GROUPS)
    yg = yg * lax.rsqrt(jnp.mean(yg * yg, axis=-1, keepdims=True) + EPS)
    out = (yg.reshape(b, t, SSD_WIDTH) * norm_g.astype(jnp.float32)).astype(z.dtype)
    return out, h, xpad[:, -(CONV_W - 1):]


def mla_queries(c_q, g_qn, w_uq, cos, sin, gq):
    b, t = c_q.shape[:2]
    q = (rmsnorm(c_q, g_qn) @ w_uq).reshape(b, t, MLA_HEADS, QK_DIM)
    q = jnp.concatenate([q[..., :QK_NOPE], apply_rope(q[..., QK_NOPE:], cos, sin)], axis=-1)
    return rmsnorm(q, gq)


def mla_keys_values(lat, kr, w_ukv, gk):
    b, t = lat.shape[:2]
    kv = (lat @ w_ukv).reshape(b, t, MLA_HEADS, QK_NOPE + V_DIM)
    kr_h = jnp.broadcast_to(kr[:, :, None, :], (b, t, MLA_HEADS, QK_ROPE)).astype(kv.dtype)
    k = jnp.concatenate([kv[..., :QK_NOPE], kr_h], axis=-1)
    return rmsnorm(k, gk), kv[..., QK_NOPE:]


def block_causal_attention(q, k, v):
    b, t, h, _ = q.shape
    nb = -(-t // Q_BLOCK)
    tp = nb * Q_BLOCK
    pad = ((0, 0), (0, tp - t), (0, 0), (0, 0))
    qp, kp, vp = jnp.pad(q, pad), jnp.pad(k, pad), jnp.pad(v, pad)
    kpos = jnp.arange(tp)

    def blk(start):
        qb = lax.dynamic_slice_in_dim(qp, start, Q_BLOCK, axis=1)
        s = jnp.einsum('bqhd,bkhd->bhqk', qb, kp).astype(jnp.float32) * ATTN_SCALE
        qpos = start + jnp.arange(Q_BLOCK)
        s = jnp.where(kpos[None, :] <= qpos[:, None], s, -jnp.inf)
        p = jax.nn.softmax(s, axis=-1)
        return jnp.einsum('bhqk,bkhd->bqhd', p.astype(vp.dtype), vp)

    o = lax.map(blk, jnp.arange(nb) * Q_BLOCK)
    return jnp.transpose(o, (1, 0, 2, 3, 4)).reshape(b, tp, h, V_DIM)[:, :t]


def paged_attention(q, k_new, v_new, lat_pool, kr_pool, layer, page_table, w_ukv, gk):
    b, s, h, _ = q.shape
    qf = q.astype(jnp.float32)

    def update(carry, k, v, mask):
        m, l, acc = carry
        sc = jnp.einsum('bshd,bkhd->bhsk', qf, k.astype(jnp.float32)) * ATTN_SCALE
        if mask is not None:
            sc = jnp.where(mask, sc, -jnp.inf)
        m_new = jnp.maximum(m, jnp.max(sc, axis=-1))
        p = jnp.exp(sc - m_new[..., None])
        corr = jnp.exp(m - m_new)
        acc = acc * corr[..., None] + jnp.einsum('bhsk,bkhd->bhsd', p, v.astype(jnp.float32))
        return m_new, l * corr + jnp.sum(p, axis=-1), acc

    def page_step(carry, pids):
        k, v = mla_keys_values(lat_pool[layer, pids], kr_pool[layer, pids], w_ukv, gk)
        return update(carry, k, v, None), None

    init = (jnp.full((b, h, s), -jnp.inf, jnp.float32), jnp.zeros((b, h, s), jnp.float32),
            jnp.zeros((b, h, s, V_DIM), jnp.float32))
    carry, _ = lax.scan(page_step, init, page_table.T)
    causal = jnp.tril(jnp.ones((s, s), dtype=bool))[None, None]
    m, l, acc = update(carry, k_new, v_new, causal)
    o = acc / l[..., None]
    return jnp.transpose(o, (0, 2, 1, 3)).astype(q.dtype)


def finish_layer(x, ssd_out, att, mla_out_g, w_out, ffn_norm_g, w_gate, w_up, w_down):
    b, t = x.shape[:2]
    mla_out = rmsnorm(att.reshape(b, t, MLA_WIDTH), mla_out_g)
    h = x + jnp.concatenate([ssd_out, mla_out], axis=-1) @ w_out
    n = rmsnorm(h, ffn_norm_g)
    return h + (jax.nn.silu(n @ w_gate) * (n @ w_up)) @ w_down


def setup_inputs(seed: int = 0) -> dict:
    key = jax.random.key(seed)
    ks = jax.random.split(key, 32)

    def nrm(i, shape, scale):
        return jax.random.normal(ks[i], shape, jnp.float32) * scale

    def gain(i, shape):
        return 1.0 + nrm(i, shape, 0.02)

    L = DEPTH
    n_pages = PAST_LEN // PAGE_SIZE
    n_used = DEC_BATCH * n_pages
    n_pool = n_used + max(1, n_used // 4)
    page_table = jax.random.permutation(ks[0], n_pool)[:n_used].reshape(DEC_BATCH, n_pages).astype(jnp.int32)
    dt0 = jnp.exp(jax.random.uniform(ks[1], (L, SSD_HEADS), jnp.float32,
                                     minval=math.log(1e-3), maxval=math.log(1e-1)))
    dt_bias = dt0 + jnp.log(-jnp.expm1(-dt0))
    a_log = jnp.log(jax.random.uniform(ks[2], (L, SSD_HEADS), jnp.float32, minval=1.0, maxval=16.0))
    mix = SSD_WIDTH + MLA_WIDTH
    return {
        'x_prompt': nrm(3, (BATCH, SEQ, D_MODEL), 1.0),
        'x_sample': nrm(4, (DEC_BATCH, DEC_SEQ, D_MODEL), 1.0),
        'cache_kv_latent': nrm(5, (L, n_pool, PAGE_SIZE, KV_LORA), 1.0),
        'cache_k_rope': nrm(6, (L, n_pool, PAGE_SIZE, QK_ROPE), 1.0),
        'state_conv': nrm(7, (L, DEC_BATCH, CONV_W - 1, CONV_DIM), 1.0),
        'state_ssm': nrm(8, (L, DEC_BATCH, SSD_HEADS, SSD_HEAD_DIM, D_STATE), 0.1),
        'page_table': page_table,
        'meta_tokens': nrm(9, (N_META, D_MODEL), 1.0),
        'attn_norm_g': gain(10, (L, D_MODEL)),
        'w_in': nrm(11, (L, D_MODEL, IN_COLS), D_MODEL ** -0.5),
        'conv_w': nrm(12, (L, CONV_W, CONV_DIM), CONV_W ** -0.5),
        'conv_b': nrm(13, (L, CONV_DIM), 0.02),
        'dt_bias': dt_bias,
        'a_log': a_log,
        'd_skip': gain(14, (L, SSD_HEADS)),
        'ssd_norm_g': gain(15, (L, SSD_WIDTH)),
        'q_norm_g': gain(16, (L, Q_LORA)),
        'w_uq': nrm(17, (L, Q_LORA, MLA_HEADS * QK_DIM), Q_LORA ** -0.5),
        'kv_norm_g': gain(18, (L, KV_LORA)),
        'w_ukv': nrm(19, (L, KV_LORA, MLA_HEADS * (QK_NOPE + V_DIM)), KV_LORA ** -0.5),
        'q_head_g_nope': gain(20, (L, QK_NOPE)),
        'q_head_g_rope': gain(21, (L, QK_ROPE // 2)),
        'k_head_g_nope': gain(22, (L, QK_NOPE)),
        'k_head_g_rope': gain(23, (L, QK_ROPE // 2)),
        'mla_out_g': gain(24, (L, MLA_WIDTH)),
        'w_out': nrm(25, (L, mix, D_MODEL), mix ** -0.5),
        'ffn_norm_g': gain(26, (L, D_MODEL)),
        'w_gate': nrm(27, (L, D_MODEL, D_FF), D_MODEL ** -0.5),
        'w_up': nrm(28, (L, D_MODEL, D_FF), D_MODEL ** -0.5),
        'w_down': nrm(29, (L, D_FF, D_MODEL), D_FF ** -0.5),
    }


def reference(x_prompt, x_sample, cache_kv_latent, cache_k_rope, state_conv, state_ssm, page_table,
              meta_tokens, attn_norm_g, w_in, conv_w, conv_b, dt_bias, a_log, d_skip, ssd_norm_g,
              q_norm_g, w_uq, kv_norm_g, w_ukv, q_head_g_nope, q_head_g_rope, k_head_g_nope,
              k_head_g_rope, mla_out_g, w_out, ffn_norm_g, w_gate, w_up, w_down):
    b = x_prompt.shape[0]
    hp = jnp.concatenate([jnp.broadcast_to(meta_tokens[None].astype(x_prompt.dtype), (b, N_META, D_MODEL)),
                          x_prompt], axis=1)
    hs = x_sample
    t = hp.shape[1]
    s = hs.shape[1]
    past = page_table.shape[1] * cache_kv_latent.shape[2]
    cos_p, sin_p = rope_angles(jnp.arange(t))
    cos_s, sin_s = rope_angles(past + jnp.arange(s))
    lat_p_l, kr_p_l, conv_p_l, ssm_p_l = [], [], [], []
    lat_s_l, kr_s_l, conv_s_l, ssm_s_l = [], [], [], []
    for l in range(DEPTH):
        gq = head_gain(q_head_g_nope[l], q_head_g_rope[l])
        gk = head_gain(k_head_g_nope[l], k_head_g_rope[l])
        ssd_w = (conv_w[l], conv_b[l], dt_bias[l], a_log[l], d_skip[l], ssd_norm_g[l])
        ffn_w = (mla_out_g[l], w_out[l], ffn_norm_g[l], w_gate[l], w_up[l], w_down[l])

        z, xbc, dt_raw, c_q, c_kv, kr_raw = project(hp, attn_norm_g[l], w_in[l])
        xpad = jnp.concatenate([jnp.zeros((b, CONV_W - 1, CONV_DIM), xbc.dtype), xbc], axis=1)
        h0 = jnp.zeros((b, SSD_HEADS, SSD_HEAD_DIM, D_STATE), jnp.float32)
        ssd_p, h_p, conv_p = ssd_branch(z, xpad, dt_raw, h0, ((N_META, N_META), (t - N_META, SSD_CHUNK)), *ssd_w)
        lat_p = rmsnorm(c_kv, kv_norm_g[l])
        kr_p = apply_rope(kr_raw, cos_p[None], sin_p[None])
        q = mla_queries(c_q, q_norm_g[l], w_uq[l], cos_p[None, :, None], sin_p[None, :, None], gq)
        k, v = mla_keys_values(lat_p, kr_p, w_ukv[l], gk)
        att_p = block_causal_attention(q, k, v)
        hp = finish_layer(hp, ssd_p, att_p, *ffn_w)

        z, xbc, dt_raw, c_q, c_kv, kr_raw = project(hs, attn_norm_g[l], w_in[l])
        xpad = jnp.concatenate([state_conv[l].astype(xbc.dtype), xbc], axis=1)
        ssd_s, h_s, conv_s = ssd_branch(z, xpad, dt_raw, state_ssm[l], ((s, s),), *ssd_w)
        lat_s = rmsnorm(c_kv, kv_norm_g[l])
        kr_s = apply_rope(kr_raw, cos_s[None], sin_s[None])
        q = mla_queries(c_q, q_norm_g[l], w_uq[l], cos_s[None, :, None], sin_s[None, :, None], gq)
        k, v = mla_keys_values(lat_s, kr_s, w_ukv[l], gk)
        att_s = paged_attention(q, k, v, cache_kv_latent, cache_k_rope, l, page_table, w_ukv[l], gk)
        hs = finish_layer(hs, ssd_s, att_s, *ffn_w)

        lat_p_l.append(lat_p)
        kr_p_l.append(kr_p)
        conv_p_l.append(conv_p)
        ssm_p_l.append(h_p.astype(x_prompt.dtype))
        lat_s_l.append(lat_s)
        kr_s_l.append(kr_s)
        conv_s_l.append(conv_s)
        ssm_s_l.append(h_s.astype(state_ssm.dtype))

    y_prompt = hp[:, N_META:]
    y_sample = hs
    return (y_prompt, y_sample,
            jnp.stack(lat_p_l), jnp.stack(kr_p_l), jnp.stack(conv_p_l), jnp.stack(ssm_p_l),
            jnp.stack(lat_s_l), jnp.stack(kr_s_l), jnp.stack(conv_s_l), jnp.stack(ssm_s_l))
```

```python
import functools

import jax
import jax.numpy as jnp
from jax import lax
from jax.experimental import pallas as pl
from jax.experimental.pallas import tpu as pltpu

f32 = jnp.float32
bf16 = jnp.bfloat16

D_MODEL = 1024
N_META = 16
SSD_HEADS = 8
SSD_HEAD_DIM = 64
SSD_WIDTH = SSD_HEADS * SSD_HEAD_DIM
SSD_GROUPS = 2
D_STATE = 64
CONV_W = 4
CONV_DIM = SSD_WIDTH + 2 * SSD_GROUPS * D_STATE
SSD_CHUNK = 128
MLA_HEADS = 8
QK_NOPE = 64
QK_ROPE = 32
QK_DIM = QK_NOPE + QK_ROPE
V_DIM = 64
MLA_WIDTH = MLA_HEADS * V_DIM
Q_LORA = 384
KV_LORA = 256
ROPE_THETA = 10000.0
ATTN_SCALE = QK_DIM ** -0.5
D_FF = 2816
EPS = 1e-6

LANE = 128
SUBLANE = 8
HT = LANE
ROPE_LO = QK_NOPE
ROPE_HALF = QK_ROPE // 2
FF_CHUNK = 256
N_FF_CHUNKS = D_FF // FF_CHUNK
MiB = 1024 * 1024

PC_Z = 0
PC_XBC = PC_Z + SSD_WIDTH
PC_CQ = PC_XBC + CONV_DIM
PC_CKV = PC_CQ + Q_LORA
PC_DT = PC_CKV + KV_LORA
PC_KR = PC_DT + LANE
PC_END = PC_KR + LANE

_NT = (((1,), (1,)), ((), ()))
_TN = (((0,), (0,)), ((), ()))
_HI = lax.Precision.HIGHEST


def _dot(a, b):
    return jnp.dot(a, b, preferred_element_type=f32)


def _dot_nt(a, b):
    return lax.dot_general(a, b, _NT, preferred_element_type=f32)


def _dot_tn(a, b):
    return lax.dot_general(a, b, _TN, preferred_element_type=f32)


def _rms(x, n):
    return lax.rsqrt(jnp.sum(x * x, axis=-1, keepdims=True) * (1.0 / n) + EPS)


def _silu(x):
    return x * jax.nn.sigmoid(x)


def _softplus(x):
    return jnp.maximum(x, 0.0) + jnp.log1p(jnp.exp(-jnp.abs(x)))


def _const_spec(shape):
    nd = len(shape)
    return pl.BlockSpec(shape, lambda *_: (0,) * nd, pipeline_mode=pl.Buffered(1))


def _params(sem, vmem_mib):
    return pltpu.CompilerParams(dimension_semantics=sem, vmem_limit_bytes=vmem_mib * MiB)


def _proj_body(x_ref, g_ref, w_ref, wdt_ref, z_ref, xbc_ref, cq_ref, ckv_ref, dt_ref, kr_ref, dtt_ref):
    x = x_ref[...]
    xn = (x * _rms(x, D_MODEL) * g_ref[...]).astype(bf16)
    z_ref[...] = _dot(xn, w_ref[:, PC_Z:PC_XBC])
    xbc_ref[...] = _dot(xn, w_ref[:, PC_XBC:PC_CQ])
    cq_ref[...] = _dot(xn, w_ref[:, PC_CQ:PC_CKV])
    ckv_ref[...] = _dot(xn, w_ref[:, PC_CKV:PC_DT])
    dt_ref[...] = _dot(xn, w_ref[:, PC_DT:PC_KR])
    kr_ref[...] = _dot(xn, w_ref[:, PC_KR:PC_END])
    dtt_ref[...] = _dot_nt(wdt_ref[...], xn)


def _project(x, g, w, wdt, tm):
    m = x.shape[0]
    row = lambda n: pl.BlockSpec((tm, n), lambda i: (i, 0))
    widths = (SSD_WIDTH, CONV_DIM, Q_LORA, KV_LORA, LANE, LANE)
    return pl.pallas_call(
        _proj_body,
        grid=(m // tm,),
        in_specs=[row(D_MODEL), _const_spec((1, D_MODEL)), _const_spec((D_MODEL, PC_END)), _const_spec((16, D_MODEL))],
        out_specs=[row(n) for n in widths] + [pl.BlockSpec((16, tm), lambda i: (0, i))],
        out_shape=[jax.ShapeDtypeStruct((m, n), f32) for n in widths] + [jax.ShapeDtypeStruct((16, m), f32)],
        compiler_params=_params(("parallel",), 40),
        name="proj",
    )(x, g, w, wdt)


def _mla_prep_body(cq_ref, ckv_ref, kr_ref, tc_ref, ts1_ref, ts2_ref, gqn_ref, gkv_ref, wuq_ref, wuk_ref, wuv_ref,
                   gq_ref, gk_ref, lat_ref, kro_ref, q_ref, k_ref, v_ref):
    tc, ts1, ts2 = tc_ref[...], ts1_ref[...], ts2_ref[...]

    def rope(x):
        return x * tc + pltpu.roll(x, LANE - ROPE_HALF, 1) * ts1 + pltpu.roll(x, ROPE_HALF, 1) * ts2

    ckv = ckv_ref[...]
    lat = ckv * _rms(ckv, KV_LORA) * gkv_ref[...]
    lat_ref[...] = lat
    kr = rope(kr_ref[...])
    kro_ref[...] = kr
    cq = cq_ref[...]
    cqn = (cq * _rms(cq, Q_LORA) * gqn_ref[...]).astype(bf16)
    latb = lat.astype(bf16)
    v_ref[...] = _dot(latb, wuv_ref[...]).astype(v_ref.dtype)
    gq, gk = gq_ref[...], gk_ref[...]
    for h in range(MLA_HEADS):
        sl = slice(h * HT, (h + 1) * HT)
        qh = rope(_dot(cqn, wuq_ref[:, sl]))
        q_ref[:, sl] = (qh * _rms(qh, QK_DIM) * gq).astype(q_ref.dtype)
        kh = _dot(latb, wuk_ref[:, sl]) + kr
        k_ref[:, sl] = (kh * _rms(kh, QK_DIM) * gk).astype(k_ref.dtype)


def _mla_prep(cq, ckv, kr, tabs, n_tab_blocks, gqn, gkv, wuq, wuk, wuv, gq, gk, tm, q_dtype):
    m = cq.shape[0]
    row = lambda n: pl.BlockSpec((tm, n), lambda i: (i, 0))
    tab = pl.BlockSpec((tm, LANE), lambda i: (i % n_tab_blocks, 0))
    widths = (KV_LORA, LANE, MLA_HEADS * HT, MLA_HEADS * HT, MLA_WIDTH)
    dtypes = (f32, f32, q_dtype, bf16, bf16)
    return pl.pallas_call(
        _mla_prep_body,
        grid=(m // tm,),
        in_specs=[row(Q_LORA), row(KV_LORA), row(LANE), tab, tab, tab,
                  _const_spec((1, Q_LORA)), _const_spec((1, KV_LORA)),
                  _const_spec((Q_LORA, MLA_HEADS * HT)), _const_spec((KV_LORA, MLA_HEADS * HT)),
                  _const_spec((KV_LORA, MLA_WIDTH)), _const_spec((1, HT)), _const_spec((1, HT))],
        out_specs=[row(n) for n in widths],
        out_shape=[jax.ShapeDtypeStruct((m, n), d) for n, d in zip(widths, dtypes)],
        compiler_params=_params(("parallel",), 40),
        name="mla_prep",
    )(cq, ckv, kr, *tabs, gqn, gkv, wuq, wuk, wuv, gq, gk)


def _ssd_body(xbc_ref, z_ref, dt_ref, dtt_ref, hist_ref, h0_ref, cw_ref, cb_ref, bias_r_ref, bias_c_ref,
              alog_r_ref, alog_c_ref, dsk_ref, ng_ref, tri_ref, y_ref, hout_ref, xp_ref, hs_ref, *, valid_from):
    c = pl.program_id(1)
    q = SSD_CHUNK

    @pl.when(c == 0)
    def _():
        xp_ref[0:SUBLANE, :] = hist_ref[...]
        hs_ref[...] = h0_ref[0]

    xbc = xbc_ref[...]
    xp_ref[SUBLANE:SUBLANE + q, :] = xbc
    conv = cb_ref[...]
    for k in range(CONV_W):
        lo = SUBLANE - (CONV_W - 1) + k
        conv = conv + xp_ref[lo:lo + q, :] * cw_ref[k:k + 1, :]
    xp_ref[0:SUBLANE, :] = xbc[q - SUBLANE:q, :]
    xa = _silu(conv)
    xs = xa[:, :SSD_WIDTH]
    bm = xa[:, SSD_WIDTH:SSD_WIDTH + LANE]
    cm = xa[:, SSD_WIDTH + LANE:]

    rows = lax.broadcasted_iota(jnp.int32, (q, q), 0)
    cols = lax.broadcasted_iota(jnp.int32, (q, q), 1)
    lane = lax.broadcasted_iota(jnp.int32, (1, LANE), 1)
    low = lane < D_STATE

    dtc = _softplus(dt_ref[...] + bias_r_ref[...])
    dtr = _softplus(dtt_ref[...] + bias_c_ref[...])
    if valid_from:
        dtc = jnp.where(rows >= valid_from, dtc, 0.0)
        dtr = jnp.where(lax.broadcasted_iota(jnp.int32, (16, q), 1) >= valid_from, dtr, 0.0)
    tri = tri_ref[...]
    acs_c = jnp.dot(tri, dtc * -jnp.exp(alog_r_ref[...]), precision=_HI, preferred_element_type=f32)
    acs_r = lax.dot_general(dtr * -jnp.exp(alog_c_ref[...]), tri, _NT, precision=_HI, preferred_element_type=f32)
    w_c = jnp.exp(acs_c[q - 1:q, :] - acs_c) * dtc
    e_c = jnp.exp(acs_c)
    cd_r = jnp.exp(acs_r[:, q - 1:q])

    causal = cols <= rows
    bb = bm.astype(bf16)
    dsk = dsk_ref[...]
    ys = []
    for g in range(SSD_GROUPS):
        gmask = (lane >= g * D_STATE) & (lane < (g + 1) * D_STATE)
        cg = jnp.where(gmask, cm, 0.0).astype(bf16)
        cb_g = _dot_nt(cg, bb)
        for pi in range(2):
            i = 2 * g + pi
            xpair = xs[:, i * LANE:(i + 1) * LANE]
            ypair = jnp.zeros((q, LANE), f32)
            for hh in range(2):
                h = 2 * i + hh
                seg = acs_c[:, h:h + 1] - acs_r[h:h + 1, :]
                lmat = jnp.exp(jnp.where(causal, seg, -jnp.inf))
                sc = (cb_g * lmat * dtr[h:h + 1, :]).astype(bf16)
                xh = jnp.where(low if hh == 0 else jnp.logical_not(low), xpair, 0.0).astype(bf16)
                ypair = ypair + _dot(sc, xh)
            h0, h1 = 2 * i, 2 * i + 1
            wp = jnp.where(low, w_c[:, h0:h0 + 1], w_c[:, h1:h1 + 1])
            ep = jnp.where(low, e_c[:, h0:h0 + 1], e_c[:, h1:h1 + 1])
            hst = hs_ref[i]
            yoff = _dot_nt(cg, hst.astype(bf16)) * ep
            st = _dot_tn((xpair * wp).astype(bf16), bb)
            cdb = jnp.concatenate([jnp.broadcast_to(cd_r[h0:h0 + 1, :], (SSD_HEAD_DIM, LANE)),
                                   jnp.broadcast_to(cd_r[h1:h1 + 1, :], (SSD_HEAD_DIM, LANE))], axis=0)
            hs_ref[i] = hst * cdb + st
            ys.append(ypair + yoff + dsk[:, i * LANE:(i + 1) * LANE] * xpair)
    y = jnp.concatenate(ys, axis=1) * _silu(z_ref[...])
    gw = SSD_WIDTH // SSD_GROUPS
    outs = []
    for g in range(SSD_GROUPS):
        yg = y[:, g * gw:(g + 1) * gw]
        outs.append(yg * _rms(yg, gw))
    y_ref[...] = (jnp.concatenate(outs, axis=1) * ng_ref[...]).astype(y_ref.dtype)

    @pl.when(c == pl.num_programs(1) - 1)
    def _():
        hout_ref[0] = hs_ref[...]


def _ssd(xbc, z, dt, dtt, hist_arr, hist_block, h0, consts, n_batch, n_chunks, row_block0, valid_from):
    q = SSD_CHUNK
    rb = lambda n: pl.BlockSpec((q, n), lambda b, c: (row_block0 + b * n_chunks + c, 0))
    cw, cb, bias_r, bias_c, alog_r, alog_c, dsk, ng, tri = consts
    return pl.pallas_call(
        functools.partial(_ssd_body, valid_from=valid_from),
        grid=(n_batch, n_chunks),
        in_specs=[rb(CONV_DIM), rb(SSD_WIDTH), rb(LANE),
                  pl.BlockSpec((16, q), lambda b, c: (0, row_block0 + b * n_chunks + c)),
                  pl.BlockSpec((SUBLANE, CONV_DIM), lambda b, c: (hist_block, 0)),
                  _const_spec((1, 4, LANE, LANE)),
                  _const_spec((CONV_W, CONV_DIM)), _const_spec((1, CONV_DIM)), _const_spec((1, LANE)),
                  _const_spec((16, 1)), _const_spec((1, LANE)), _const_spec((16, 1)),
                  _const_spec((1, SSD_WIDTH)), _const_spec((1, SSD_WIDTH)), _const_spec((q, q))],
        out_specs=[pl.BlockSpec((q, SSD_WIDTH), lambda b, c: (b * n_chunks + c, 0)),
                   pl.BlockSpec((1, 4, LANE, LANE), lambda b, c: (b, 0, 0, 0))],
        out_shape=[jax.ShapeDtypeStruct((n_batch * n_chunks * q, SSD_WIDTH), bf16),
                   jax.ShapeDtypeStruct((n_batch, 4, LANE, LANE), f32)],
        scratch_shapes=[pltpu.VMEM((SUBLANE + q, CONV_DIM), f32), pltpu.VMEM((4, LANE, LANE), f32)],
        compiler_params=_params(("parallel", "arbitrary"), 32),
        name="ssd",
    )(xbc, z, dt, dtt, hist_arr, h0, cw, cb, bias_r, bias_c, alog_r, alog_c, dsk, ng, tri)


def _flash_body(q_ref, k_ref, v_ref, km_ref, vm_ref, o_ref, *, tile):
    qi = pl.program_id(2)
    lane = lax.broadcasted_iota(jnp.int32, (1, LANE), 1)
    rows = lax.broadcasted_iota(jnp.int32, (tile, tile), 0)
    cols = lax.broadcasted_iota(jnp.int32, (tile, tile), 1)
    out = jnp.zeros((tile, LANE), f32)
    for hh in range(2):
        sl = slice(hh * HT, (hh + 1) * HT)
        q = q_ref[:, sl]
        vmask = (lane < V_DIM) if hh == 0 else (lane >= V_DIM)

        def step(kt, vt, carry, mask=None):
            m, l, acc = carry
            s = _dot_nt(q, kt)
            if mask is not None:
                s = jnp.where(mask, s, -jnp.inf)
            m2 = jnp.maximum(m, jnp.max(s, axis=-1, keepdims=True))
            p = jnp.exp(s - m2)
            a = jnp.exp(m - m2)
            vz = jnp.where(vmask, vt, jnp.zeros_like(vt))
            return m2, a * l + jnp.sum(p, axis=-1, keepdims=True), a * acc + _dot(p.astype(bf16), vz)

        carry = (jnp.full((tile, 1), -jnp.inf, f32), jnp.zeros((tile, 1), f32), jnp.zeros((tile, LANE), f32))
        carry = step(km_ref[:, sl], vm_ref[...], carry)

        def body(j, carry):
            off = pl.multiple_of(j * tile, tile)
            return step(k_ref[pl.ds(off, tile), sl], v_ref[pl.ds(off, tile), :], carry)

        carry = lax.fori_loop(0, qi, body, carry)
        off = pl.multiple_of(qi * tile, tile)
        m, l, acc = step(k_ref[pl.ds(off, tile), sl], v_ref[pl.ds(off, tile), :], carry, cols <= rows)
        out = out + acc / l
    o_ref[...] = out.astype(o_ref.dtype)


def _flash(q, k, v, k_small, v_small, meta_block, n_batch, seq, tile):
    nq = seq // tile
    return pl.pallas_call(
        functools.partial(_flash_body, tile=tile),
        grid=(n_batch, MLA_HEADS // 2, nq),
        in_specs=[pl.BlockSpec((tile, 2 * HT), lambda b, p, i: (b * nq + i, p)),
                  pl.BlockSpec((seq, 2 * HT), lambda b, p, i: (b, p)),
                  pl.BlockSpec((seq, LANE), lambda b, p, i: (b, p)),
                  pl.BlockSpec((N_META, 2 * HT), lambda b, p, i: (meta_block, p)),
                  pl.BlockSpec((N_META, LANE), lambda b, p, i: (meta_block, p))],
        out_specs=pl.BlockSpec((tile, LANE), lambda b, p, i: (b * nq + i, p)),
        out_shape=jax.ShapeDtypeStruct((n_batch * seq, MLA_WIDTH), f32),
        compiler_params=_params(("parallel", "parallel", "arbitrary"), 40),
        name="flash",
    )(q, k, v, k_small, v_small)


def _finish_body(x_ref, ssd_ref, att_ref, gm_ref, wo_ref, gf_ref, wg_ref, wu_ref, wd_ref, o_ref):
    att = att_ref[...]
    mla = (att * _rms(att, MLA_WIDTH) * gm_ref[...]).astype(bf16)
    h = x_ref[...] + (_dot(ssd_ref[...], wo_ref[0:SSD_WIDTH, :]) + _dot(mla, wo_ref[SSD_WIDTH:, :]))
    n = (h * _rms(h, D_MODEL) * gf_ref[...]).astype(bf16)
    ff = jnp.zeros_like(h)
    for c in range(N_FF_CHUNKS):
        a = (_silu(_dot(n, wg_ref[c])) * _dot(n, wu_ref[c])).astype(bf16)
        ff = ff + _dot(a, wd_ref[c])
    o_ref[...] = h + ff


def _finish(x, ssd, att, gm, wo, gf, wg, wu, wd, tm):
    m = x.shape[0]
    row = lambda n: pl.BlockSpec((tm, n), lambda i: (i, 0))
    return pl.pallas_call(
        _finish_body,
        grid=(m // tm,),
        in_specs=[row(D_MODEL), row(SSD_WIDTH), row(MLA_WIDTH), _const_spec((1, MLA_WIDTH)),
                  _const_spec((D_MODEL, D_MODEL)), _const_spec((1, D_MODEL)),
                  _const_spec((N_FF_CHUNKS, D_MODEL, FF_CHUNK)), _const_spec((N_FF_CHUNKS, D_MODEL, FF_CHUNK)),
                  _const_spec((N_FF_CHUNKS, FF_CHUNK, D_MODEL))],
        out_specs=row(D_MODEL),
        out_shape=jax.ShapeDtypeStruct((m, D_MODEL), f32),
        compiler_params=_params(("parallel",), 56),
        name="finish",
    )(x, ssd, att, gm, wo, gf, wg, wu, wd)


def _sssd_prep_body(xbc_ref, sc_ref, dt_ref, cw_ref, cb_ref, bias_r_ref, alog_r_ref, ind_ref,
                    xs_ref, xdt_ref, dec_ref, bc_ref):
    conv = cb_ref[...]
    for k in range(CONV_W - 1):
        conv = conv + sc_ref[k] * cw_ref[k:k + 1, :]
    conv = conv + xbc_ref[...] * cw_ref[CONV_W - 1:CONV_W, :]
    xa = _silu(conv)
    dt = _softplus(dt_ref[...] + bias_r_ref[...])
    dec = jnp.exp(dt * -jnp.exp(alog_r_ref[...]))
    ind = ind_ref[...]
    xs = xa[:, :SSD_WIDTH]
    xs_ref[...] = xs
    xdt_ref[...] = xs * jnp.dot(dt, ind, precision=_HI, preferred_element_type=f32)
    dec_ref[...] = jnp.dot(dec, ind, precision=_HI, preferred_element_type=f32)
    bc_ref[...] = xa[:, SSD_WIDTH:]


def _sssd_prep(xbc_small, sc, dt_small, cw, cb, bias_r, alog_r, ind, nseq):
    blk = lambda n: pl.BlockSpec((nseq, n), lambda i: (0, 0))
    widths = (SSD_WIDTH, SSD_WIDTH, SSD_WIDTH, 2 * LANE)
    return pl.pallas_call(
        _sssd_prep_body,
        grid=(1,),
        in_specs=[blk(CONV_DIM), _const_spec((CONV_W - 1, nseq, CONV_DIM)), blk(LANE), _const_spec((CONV_W, CONV_DIM)),
                  _const_spec((1, CONV_DIM)), _const_spec((1, LANE)), _const_spec((1, LANE)), _const_spec((LANE, SSD_WIDTH))],
        out_specs=[blk(n) for n in widths],
        out_shape=[jax.ShapeDtypeStruct((nseq, n), f32) for n in widths],
        compiler_params=_params(("arbitrary",), 32),
        name="sssd_prep",
    )(xbc_small, sc, dt_small, cw, cb, bias_r, alog_r, ind)


def _split_bf16(x):
    hi = x.astype(bf16).astype(f32)
    return hi, x - hi


def _sssd_step_body(h0_ref, xs_ref, xdt_ref, dec_ref, bc_ref, z_ref, dsk_ref, ng_ref, hn_ref, y_ref, *, nseq):
    hp = SSD_HEADS * SSD_HEAD_DIM
    row8 = lax.broadcasted_iota(jnp.int32, (SUBLANE, hp), 0)
    lane_hp = lax.broadcasted_iota(jnp.int32, (1, hp), 1)
    g0 = lane_hp < hp // SSD_GROUPS
    lane = lax.broadcasted_iota(jnp.int32, (1, LANE), 1)
    low = lane < D_STATE
    rrow = lax.broadcasted_iota(jnp.int32, (SUBLANE, 2 * LANE), 0)
    rlane = lax.broadcasted_iota(jnp.int32, (SUBLANE, 2 * LANE), 1)
    zeros8 = jnp.zeros((SUBLANE, hp), f32)
    for s in range(nseq):
        xdt = xdt_ref[s:s + 1, :]
        dec = dec_ref[s:s + 1, :]
        b = bc_ref[s:s + 1, 0:LANE]
        c = bc_ref[s:s + 1, LANE:]
        b8 = jnp.broadcast_to(b, (SUBLANE, LANE))
        c8 = jnp.broadcast_to(c, (SUBLANE, LANE))
        b_sw = pltpu.roll(b8, D_STATE, 1)
        c_sw = pltpu.roll(c8, D_STATE, 1)
        lm = jnp.where(row8 == 0, jnp.where(g0, xdt, 0.0),
                       jnp.where(row8 == 1, jnp.where(g0, 0.0, xdt), jnp.where(row8 == 2, dec, 0.0)))
        bsel = jnp.where(rrow[:, :LANE] == 0, b8, b_sw)
        rm = jnp.concatenate([jnp.where((rrow[:, :LANE] < 2) & (rlane[:, :LANE] < D_STATE), bsel, 0.0),
                              jnp.where((rrow[:, :LANE] == 2) & (rlane[:, :LANE] < D_STATE), 1.0, 0.0)], axis=1)
        lhi, llo = _split_bf16(lm)
        rhi, rlo = _split_bf16(rm)
        lst = jnp.concatenate([lhi, lhi, llo, zeros8], axis=0).astype(bf16)
        rst = jnp.concatenate([rhi, rlo, rhi, jnp.zeros_like(rhi)], axis=0).astype(bf16)
        outer = _dot_tn(lst, rst)
        h0 = h0_ref[s].reshape(hp, D_STATE)
        hn_ref[s] = (h0 * outer[:, LANE:LANE + D_STATE] + outer[:, 0:D_STATE]).reshape(SSD_HEADS, SSD_HEAD_DIM, D_STATE)
        csel = jnp.where(row8[:, :LANE] == 0, c8, jnp.where(row8[:, :LANE] == 1, c_sw, 0.0))[:, :D_STATE]
        ch = _dot_nt(csel.astype(bf16), h0.astype(bf16))
        chr_ = jnp.where(g0, ch[0:1, :], ch[1:2, :])
        bcp = b * c
        bc0 = jnp.sum(jnp.where(low, bcp, 0.0), axis=-1, keepdims=True)
        bc1 = jnp.sum(jnp.where(low, 0.0, bcp), axis=-1, keepdims=True)
        y_ref[s:s + 1, :] = dec * chr_ + xdt * jnp.where(g0, bc0, bc1) + dsk_ref[...] * xs_ref[s:s + 1, :]
    y = y_ref[...] * _silu(z_ref[...])
    gw = SSD_WIDTH // SSD_GROUPS
    outs = []
    for g in range(SSD_GROUPS):
        yg = y[:, g * gw:(g + 1) * gw]
        outs.append(yg * _rms(yg, gw))
    y_ref[...] = jnp.concatenate(outs, axis=1) * ng_ref[...]


def _sssd_step(h0, xs, xdt, dec, bc, z_small, dsk, ng, nseq_total, nseq):
    blk = lambda n: pl.BlockSpec((nseq, n), lambda i: (i, 0))
    hblk = pl.BlockSpec((nseq, SSD_HEADS, SSD_HEAD_DIM, D_STATE), lambda i: (i, 0, 0, 0))
    return pl.pallas_call(
        functools.partial(_sssd_step_body, nseq=nseq),
        grid=(nseq_total // nseq,),
        in_specs=[hblk, blk(SSD_WIDTH), blk(SSD_WIDTH), blk(SSD_WIDTH), blk(2 * LANE), blk(SSD_WIDTH),
                  _const_spec((1, SSD_WIDTH)), _const_spec((1, SSD_WIDTH))],
        out_specs=[hblk, blk(SSD_WIDTH)],
        out_shape=[jax.ShapeDtypeStruct((nseq_total, SSD_HEADS, SSD_HEAD_DIM, D_STATE), f32),
                   jax.ShapeDtypeStruct((nseq_total, SSD_WIDTH), f32)],
        compiler_params=_params(("parallel",), 32),
        name="sssd_step",
    )(h0, xs, xdt, dec, bc, z_small, dsk, ng)


def _absorb_body(q_ref, gk_ref, wabs_ref, qg_ref, a_ref):
    gk = gk_ref[...]
    for h in range(MLA_HEADS):
        sl = slice(h * HT, (h + 1) * HT)
        qg = q_ref[:, sl] * gk
        qg_ref[:, sl] = qg
        a_ref[:, h * KV_LORA:(h + 1) * KV_LORA] = jnp.dot(qg, wabs_ref[sl, :], precision=_HI, preferred_element_type=f32)
    a_ref[:, MLA_HEADS * KV_LORA:] = jnp.zeros((q_ref.shape[0], (16 - MLA_HEADS) * KV_LORA), f32)


def _absorb(q_small, gk, wabs, nseq):
    return pl.pallas_call(
        _absorb_body,
        grid=(1,),
        in_specs=[pl.BlockSpec((nseq, MLA_HEADS * HT), lambda i: (0, 0)), _const_spec((1, HT)),
                  _const_spec((MLA_HEADS * HT, KV_LORA))],
        out_specs=[pl.BlockSpec((nseq, MLA_HEADS * HT), lambda i: (0, 0)),
                   pl.BlockSpec((nseq, 16 * KV_LORA), lambda i: (0, 0))],
        out_shape=[jax.ShapeDtypeStruct((nseq, MLA_HEADS * HT), f32), jax.ShapeDtypeStruct((nseq, 16 * KV_LORA), f32)],
        compiler_params=_params(("arbitrary",), 32),
        name="absorb",
    )(q_small, gk, wabs)


def _decode_body(pt_ref, a_ref, c_ref, latn_ref, krn_ref, wukt_ref, wuv_ref, clat_ref, ckr_ref, o_ref,
                 lat_buf, kr_buf, waug, sems, *, n_pages, page, tile):
    b = pl.program_id(0)
    nb = pl.num_programs(0)
    slot = b % 2
    past = n_pages * page
    n_tiles = past // tile
    pages_per_tile = tile // page
    wrows = MLA_HEADS * QK_NOPE

    def lat_copy(pid, p, sl):
        return pltpu.make_async_copy(clat_ref.at[0, pid], lat_buf.at[sl, pl.ds(p * page, page), :], sems.at[0, sl])

    def kr_copy(pid, p, sl):
        return pltpu.make_async_copy(ckr_ref.at[0, pid], kr_buf.at[sl, pl.ds(p * page, page), :], sems.at[1, sl])

    def start_page(seq, p, sl):
        pid = pt_ref[seq * n_pages + p]
        lat_copy(pid, p, sl).start()
        kr_copy(pid, p, sl).start()

    @pl.when(b == 0)
    def _():
        waug[0:wrows, :] = wukt_ref[...]
        for sl in range(2):
            lat_buf[sl, past:past + page, :] = jnp.zeros((page, KV_LORA), f32)
            kr_buf[sl, past:past + page, :] = jnp.zeros((page, QK_ROPE), f32)

        def first(p, carry):
            start_page(0, p, 0)
            return carry

        lax.fori_loop(0, n_pages, first, 0)

    def wait_page(p, carry):
        lat_copy(0, p, slot).wait()
        kr_copy(0, p, slot).wait()
        return carry

    lax.fori_loop(0, n_pages, wait_page, 0)

    lat_buf[slot, past:past + 1, :] = latn_ref[0]
    kr_buf[slot, past:past + 1, :] = krn_ref[0]
    waug[wrows:wrows + 16, :] = a_ref[0, 0].astype(bf16)
    cb = c_ref[0].astype(bf16)
    ones = jnp.ones((SUBLANE, QK_ROPE), bf16)

    def attend(lat_t, kr_t, carry, mask=None):
        m, l, acc = carry
        n = lat_t.shape[0]
        lb = lat_t.astype(bf16)
        kt = _dot_nt(waug[...], lb)
        k3 = kt[0:wrows, :].reshape(MLA_HEADS, QK_NOPE, n)
        n2 = jnp.sum(k3 * k3, axis=1)
        s1 = kt[wrows:wrows + MLA_HEADS, :]
        s2 = _dot_nt(cb, kr_t.astype(bf16))
        kr2 = _dot_nt(ones, (kr_t * kr_t).astype(bf16))
        s = lax.rsqrt((n2 + kr2) * (1.0 / QK_DIM) + EPS) * (s1 + s2)
        if mask is not None:
            s = jnp.where(mask, s, -jnp.inf)
        m2 = jnp.maximum(m, jnp.max(s, axis=-1, keepdims=True))
        p = jnp.exp(s - m2)
        al = jnp.exp(m - m2)
        return m2, al * l + jnp.sum(p, axis=-1, keepdims=True), al * acc + _dot(p.astype(bf16), lb)

    def body(j, carry):
        @pl.when(b + 1 < nb)
        def _():
            for pp in range(pages_per_tile):
                start_page(b + 1, j * pages_per_tile + pp, 1 - slot)

        off = pl.multiple_of(j * tile, tile)
        return attend(lat_buf[slot, pl.ds(off, tile), :], kr_buf[slot, pl.ds(off, tile), :], carry)

    carry = (jnp.full((MLA_HEADS, 1), -jnp.inf, f32), jnp.zeros((MLA_HEADS, 1), f32), jnp.zeros((MLA_HEADS, KV_LORA), f32))
    carry = lax.fori_loop(0, n_tiles, body, carry)
    tail_mask = lax.broadcasted_iota(jnp.int32, (MLA_HEADS, page), 1) == 0
    m, l, acc = attend(lat_buf[slot, past:past + page, :], kr_buf[slot, past:past + page, :], carry, tail_mask)
    full = _dot((acc / l).astype(bf16), wuv_ref[...])
    hrow = lax.broadcasted_iota(jnp.int32, (MLA_HEADS, MLA_WIDTH), 0)
    hlane = lax.broadcasted_iota(jnp.int32, (MLA_HEADS, MLA_WIDTH), 1)
    o_ref[0] = jnp.sum(jnp.where(hlane // V_DIM == hrow, full, 0.0), axis=0, keepdims=True)


def _decode(page_table_flat, a, c, lat_new, kr_new, wukt, wuv, cache_lat, cache_kr, nseq, n_pages, page, tile):
    past = n_pages * page
    wrows = MLA_HEADS * QK_NOPE
    grid_spec = pltpu.PrefetchScalarGridSpec(
        num_scalar_prefetch=1,
        grid=(nseq,),
        in_specs=[pl.BlockSpec((1, 1, 16, KV_LORA), lambda b, pt: (b, 0, 0, 0)),
                  pl.BlockSpec((1, MLA_HEADS, QK_ROPE), lambda b, pt: (b, 0, 0)),
                  pl.BlockSpec((1, 1, KV_LORA), lambda b, pt: (b, 0, 0)),
                  pl.BlockSpec((1, 1, QK_ROPE), lambda b, pt: (b, 0, 0)),
                  pl.BlockSpec((wrows, KV_LORA), lambda b, pt: (0, 0), pipeline_mode=pl.Buffered(1)),
                  pl.BlockSpec((KV_LORA, MLA_WIDTH), lambda b, pt: (0, 0), pipeline_mode=pl.Buffered(1)),
                  pl.BlockSpec(memory_space=pl.ANY), pl.BlockSpec(memory_space=pl.ANY)],
        out_specs=pl.BlockSpec((1, 1, MLA_WIDTH), lambda b, pt: (b, 0, 0)),
        scratch_shapes=[pltpu.VMEM((2, past + page, KV_LORA), f32), pltpu.VMEM((2, past + page, QK_ROPE), f32),
                        pltpu.VMEM((wrows + 16, KV_LORA), bf16), pltpu.SemaphoreType.DMA((2, 2))],
    )
    return pl.pallas_call(
        functools.partial(_decode_body, n_pages=n_pages, page=page, tile=tile),
        grid_spec=grid_spec,
        out_shape=jax.ShapeDtypeStruct((nseq, 1, MLA_WIDTH), f32),
        compiler_params=_params(("arbitrary",), 48),
        name="decode",
    )(page_table_flat, a, c, lat_new, kr_new, wukt, wuv, cache_lat, cache_kr)


def _rope_tables(pos):
    inv = ROPE_THETA ** (-jnp.arange(ROPE_HALF, dtype=f32) * (2.0 / QK_ROPE))
    ang = pos.astype(f32)[:, None] * inv[None, :]
    cos, sin = jnp.cos(ang), jnp.sin(ang)
    n = pos.shape[0]
    one = jnp.ones((n, QK_NOPE), f32)
    z16 = jnp.zeros((n, ROPE_HALF), f32)
    z32 = jnp.zeros((n, HT - QK_DIM), f32)
    z64 = jnp.zeros((n, QK_NOPE), f32)
    tc = jnp.concatenate([one, cos, cos, z32], axis=1)
    ts1 = jnp.concatenate([z64, -sin, z16, z32], axis=1)
    ts2 = jnp.concatenate([z64, z16, sin, z32], axis=1)
    return tc, ts1, ts2


def _pad_lanes(x, n):
    return jnp.pad(x, ((0, 0), (0, n - x.shape[1])))


def kernel(x_prompt, x_sample, cache_kv_latent, cache_k_rope, state_conv, state_ssm, page_table, meta_tokens, attn_norm_g, w_in, conv_w, conv_b, dt_bias, a_log, d_skip, ssd_norm_g, q_norm_g, w_uq, kv_norm_g, w_ukv, q_head_g_nope, q_head_g_rope, k_head_g_nope, k_head_g_rope, mla_out_g, w_out, ffn_norm_g, w_gate, w_up, w_down):
    l = 0
    nb, seq, _ = x_prompt.shape
    nseq = x_sample.shape[0]
    n_pages, page = page_table.shape[1], cache_kv_latent.shape[2]
    past = n_pages * page
    n_small = 2 * LANE
    meta_lo = n_small - N_META

    s0, s1, s2, s3, s4 = (SSD_WIDTH, SSD_WIDTH + CONV_DIM, SSD_WIDTH + CONV_DIM + SSD_HEADS,
                          SSD_WIDTH + CONV_DIM + SSD_HEADS + Q_LORA, SSD_WIDTH + CONV_DIM + SSD_HEADS + Q_LORA + KV_LORA)
    wi = w_in[l]
    w_kr = jnp.pad(wi[:, s4:], ((0, 0), (ROPE_LO, HT - QK_DIM)))
    w_proj = jnp.concatenate([wi[:, :s1], wi[:, s2:s4], _pad_lanes(wi[:, s1:s2], LANE), w_kr], axis=1).astype(bf16)
    w_dtt = jnp.pad(wi[:, s1:s2].T, ((0, 16 - SSD_HEADS), (0, 0))).astype(bf16)
    g_attn = attn_norm_g[l][None, :]
    wuq = jnp.pad(w_uq[l].reshape(Q_LORA, MLA_HEADS, QK_DIM), ((0, 0), (0, 0), (0, HT - QK_DIM))).reshape(Q_LORA, -1).astype(bf16)
    wkv = w_ukv[l].reshape(KV_LORA, MLA_HEADS, QK_NOPE + V_DIM)
    wuk_f32 = jnp.pad(wkv[:, :, :QK_NOPE], ((0, 0), (0, 0), (0, HT - QK_NOPE))).reshape(KV_LORA, -1)
    wuk = wuk_f32.astype(bf16)
    wuv = wkv[:, :, QK_NOPE:].reshape(KV_LORA, MLA_WIDTH).astype(bf16)
    wukt = wkv[:, :, :QK_NOPE].reshape(KV_LORA, -1).T.astype(bf16)
    gq = _pad_lanes(jnp.concatenate([q_head_g_nope[l], q_head_g_rope[l], q_head_g_rope[l]])[None, :], HT) * ATTN_SCALE
    gk = _pad_lanes(jnp.concatenate([k_head_g_nope[l], k_head_g_rope[l], k_head_g_rope[l]])[None, :], HT)
    gqn, gkv = q_norm_g[l][None, :], kv_norm_g[l][None, :]
    cw, cb = conv_w[l], conv_b[l][None, :]
    bias_r = _pad_lanes(dt_bias[l][None, :], LANE)
    bias_c = jnp.pad(dt_bias[l][:, None], ((0, 16 - SSD_HEADS), (0, 0)))
    alog_r = _pad_lanes(a_log[l][None, :], LANE)
    alog_c = jnp.pad(a_log[l][:, None], ((0, 16 - SSD_HEADS), (0, 0)))
    dsk = jnp.repeat(d_skip[l], SSD_HEAD_DIM)[None, :]
    ng = ssd_norm_g[l][None, :]
    tri = jnp.tril(jnp.ones((SSD_CHUNK, SSD_CHUNK), f32))
    ind = (jnp.arange(LANE)[:, None] == jnp.arange(SSD_WIDTH)[None, :] // SSD_HEAD_DIM).astype(f32)
    ssd_consts = (cw, cb, bias_r, bias_c, alog_r, alog_c, dsk, ng, tri)
    gm, gf = mla_out_g[l][None, :], ffn_norm_g[l][None, :]
    wo = w_out[l].astype(bf16)
    wg = w_gate[l].astype(bf16).reshape(D_MODEL, N_FF_CHUNKS, FF_CHUNK).transpose(1, 0, 2)
    wu = w_up[l].astype(bf16).reshape(D_MODEL, N_FF_CHUNKS, FF_CHUNK).transpose(1, 0, 2)
    wd = w_down[l].astype(bf16).reshape(N_FF_CHUNKS, FF_CHUNK, D_MODEL)

    xp = x_prompt.reshape(nb * seq, D_MODEL)
    xs_rows = x_sample[:, 0, :]
    x_small = jnp.concatenate([xs_rows, jnp.zeros((meta_lo - nseq, D_MODEL), f32), meta_tokens.astype(f32)], axis=0)

    z_p, xbc_p, cq_p, ckv_p, dt_p, kr_p, dtt_p = _project(xp, g_attn, w_proj, w_dtt, 512)
    z_s, xbc_s, cq_s, ckv_s, dt_s, kr_s, dtt_s = _project(x_small, g_attn, w_proj, w_dtt, n_small)

    tabs_p = _rope_tables(N_META + jnp.arange(seq))
    pos_small = jnp.concatenate([jnp.full((nseq,), past), jnp.zeros((meta_lo - nseq,), jnp.int32), jnp.arange(N_META)])
    tabs_s = _rope_tables(pos_small)
    tm = 512
    lat_p, kro_p, q_p, k_p, v_p = _mla_prep(cq_p, ckv_p, kr_p, tabs_p, seq // tm, gqn, gkv, wuq, wuk, wuv, gq, gk, tm, bf16)
    lat_s, kro_s, q_s, k_s, v_s = _mla_prep(cq_s, ckv_s, kr_s, tabs_s, 1, gqn, gkv, wuq, wuk, wuv, gq, gk, n_small, f32)

    zero_h = jnp.zeros((1, 4, LANE, LANE), f32)
    zero_hist = jnp.zeros((SUBLANE, CONV_DIM), f32)
    _, h_meta = _ssd(xbc_s, z_s, dt_s, dtt_s, zero_hist, 0, zero_h, ssd_consts, 1, 1, 1, SSD_CHUNK - N_META)
    ssd_p, h_fin = _ssd(xbc_p, z_p, dt_p, dtt_p, xbc_s, n_small // SUBLANE - 1, h_meta, ssd_consts, nb, seq // SSD_CHUNK, 0, 0)

    att_p = _flash(q_p, k_p, v_p, k_s, v_s, n_small // N_META - 1, nb, seq, 512)
    y_prompt = _finish(xp, ssd_p, att_p, gm, wo, gf, wg, wu, wd, 512).reshape(nb, seq, D_MODEL)

    sc = jnp.transpose(state_conv[l], (1, 0, 2))
    xs_s, xdt_s, dec_s, bc_s = _sssd_prep(xbc_s, sc, dt_s, cw, cb, bias_r, alog_r, ind, nseq)
    h_new, ssd_s = _sssd_step(state_ssm[l], xs_s, xdt_s, dec_s, bc_s, z_s, dsk, ng, nseq, 8)
    qg, a_abs = _absorb(q_s, gk, wuk_f32.T, nseq)
    a_abs = a_abs.reshape(nseq, 1, 16, KV_LORA)
    c_abs = qg.reshape(nseq, MLA_HEADS, HT)[:, :, ROPE_LO:QK_DIM]
    lat_new = lat_s[:nseq][:, None, :]
    kr_new = kro_s[:nseq, ROPE_LO:QK_DIM][:, None, :]
    att_s = _decode(page_table.reshape(-1), a_abs, c_abs, lat_new, kr_new, wukt, wuv, cache_kv_latent, cache_k_rope,
                    nseq, n_pages, page, 512)
    y_sample = _finish(xs_rows, ssd_s.astype(bf16), att_s[:, 0, :], gm, wo, gf, wg, wu, wd, nseq)[:, None, :]

    def with_meta(small, main, width):
        meta = jnp.broadcast_to(small[meta_lo:][None], (nb, N_META, width))
        return jnp.concatenate([meta, main.reshape(nb, seq, width)], axis=1)[None]

    kv_latent_prompt = with_meta(lat_s, lat_p, KV_LORA)
    k_rope_prompt = with_meta(kro_s[:, ROPE_LO:QK_DIM], kro_p[:, ROPE_LO:QK_DIM], QK_ROPE)
    conv_prompt = xbc_p.reshape(nb, seq, CONV_DIM)[:, seq - (CONV_W - 1):][None]
    hf = h_fin.reshape(nb, 4, 2, SSD_HEAD_DIM, SSD_GROUPS, D_STATE)
    ssm_prompt = jnp.stack([hf[:, i, :, :, i // 2, :] for i in range(4)], axis=1).reshape(nb, SSD_HEADS, SSD_HEAD_DIM, D_STATE)[None]
    kv_latent_sample = lat_new[None]
    k_rope_sample = kr_new[None]
    conv_sample = jnp.concatenate([state_conv[l][:, 1:], xbc_s[:nseq][:, None, :]], axis=1)[None]
    ssm_sample = h_new[None]
    return (y_prompt, y_sample, kv_latent_prompt, k_rope_prompt, conv_prompt, ssm_prompt.astype(x_prompt.dtype),
            kv_latent_sample, k_rope_sample, conv_sample, ssm_sample.astype(state_ssm.dtype))
```

```python
import functools

import jax
import jax.numpy as jnp
from jax import lax
from jax.experimental import pallas as pl
from jax.experimental.pallas import tpu as pltpu

f32 = jnp.float32
bf16 = jnp.bfloat16

D_MODEL = 1024
N_META = 16
SSD_HEADS = 8
SSD_HEAD_DIM = 64
SSD_WIDTH = SSD_HEADS * SSD_HEAD_DIM
SSD_GROUPS = 2
D_STATE = 64
CONV_W = 4
CONV_DIM = SSD_WIDTH + 2 * SSD_GROUPS * D_STATE
SSD_CHUNK = 128
MLA_HEADS = 8
QK_NOPE = 64
QK_ROPE = 32
QK_DIM = QK_NOPE + QK_ROPE
V_DIM = 64
MLA_WIDTH = MLA_HEADS * V_DIM
Q_LORA = 384
KV_LORA = 256
ROPE_THETA = 10000.0
ATTN_SCALE = QK_DIM ** -0.5
D_FF = 2816
EPS = 1e-6

LANE = 128
SUBLANE = 8
HT = LANE
ROPE_LO = QK_NOPE
ROPE_HALF = QK_ROPE // 2
FF_CHUNK = 256
N_FF_CHUNKS = D_FF // FF_CHUNK
MiB = 1024 * 1024

PC_Z = 0
PC_XBC = PC_Z + SSD_WIDTH
PC_CQ = PC_XBC + CONV_DIM
PC_CKV = PC_CQ + Q_LORA
PC_DT = PC_CKV + KV_LORA
PC_KR = PC_DT + LANE
PC_END = PC_KR + LANE

_NT = (((1,), (1,)), ((), ()))
_TN = (((0,), (0,)), ((), ()))
_HI = lax.Precision.HIGHEST


def _dot(a, b):
    return jnp.dot(a, b, preferred_element_type=f32)


def _dot_nt(a, b):
    return lax.dot_general(a, b, _NT, preferred_element_type=f32)


def _dot_tn(a, b):
    return lax.dot_general(a, b, _TN, preferred_element_type=f32)


def _rms(x, n):
    return lax.rsqrt(jnp.sum(x * x, axis=-1, keepdims=True) * (1.0 / n) + EPS)


def _silu(x):
    return x * jax.nn.sigmoid(x)


def _softplus(x):
    return jnp.maximum(x, 0.0) + jnp.log1p(jnp.exp(-jnp.abs(x)))


def _const_spec(shape):
    nd = len(shape)
    return pl.BlockSpec(shape, lambda *_: (0,) * nd, pipeline_mode=pl.Buffered(1))


def _params(sem, vmem_mib):
    return pltpu.CompilerParams(dimension_semantics=sem, vmem_limit_bytes=vmem_mib * MiB)


def _proj_body(x_ref, g_ref, w_ref, wdt_ref, z_ref, xbc_ref, cq_ref, ckv_ref, dt_ref, kr_ref, dtt_ref):
    x = x_ref[...]
    xn = (x * _rms(x, D_MODEL) * g_ref[...]).astype(bf16)
    z_ref[...] = _dot(xn, w_ref[:, PC_Z:PC_XBC])
    xbc_ref[...] = _dot(xn, w_ref[:, PC_XBC:PC_CQ])
    cq_ref[...] = _dot(xn, w_ref[:, PC_CQ:PC_CKV])
    ckv_ref[...] = _dot(xn, w_ref[:, PC_CKV:PC_DT])
    dt_ref[...] = _dot(xn, w_ref[:, PC_DT:PC_KR])
    kr_ref[...] = _dot(xn, w_ref[:, PC_KR:PC_END])
    dtt_ref[...] = _dot_nt(wdt_ref[...], xn)


def _project(x, g, w, wdt, tm):
    m = x.shape[0]
    row = lambda n: pl.BlockSpec((tm, n), lambda i: (i, 0))
    widths = (SSD_WIDTH, CONV_DIM, Q_LORA, KV_LORA, LANE, LANE)
    return pl.pallas_call(
        _proj_body,
        grid=(m // tm,),
        in_specs=[row(D_MODEL), _const_spec((1, D_MODEL)), _const_spec((D_MODEL, PC_END)), _const_spec((16, D_MODEL))],
        out_specs=[row(n) for n in widths] + [pl.BlockSpec((16, tm), lambda i: (0, i))],
        out_shape=[jax.ShapeDtypeStruct((m, n), f32) for n in widths] + [jax.ShapeDtypeStruct((16, m), f32)],
        compiler_params=_params(("parallel",), 40),
        name="proj",
    )(x, g, w, wdt)


def _mla_prep_body(cq_ref, ckv_ref, kr_ref, tc_ref, ts1_ref, ts2_ref, gqn_ref, gkv_ref, wuq_ref, wuk_ref, wuv_ref,
                   gq_ref, gk_ref, lat_ref, kro_ref, q_ref, k_ref, v_ref):
    tc, ts1, ts2 = tc_ref[...], ts1_ref[...], ts2_ref[...]

    def rope(x):
        return x * tc + pltpu.roll(x, LANE - ROPE_HALF, 1) * ts1 + pltpu.roll(x, ROPE_HALF, 1) * ts2

    ckv = ckv_ref[...]
    lat = ckv * _rms(ckv, KV_LORA) * gkv_ref[...]
    lat_ref[...] = lat
    kr = rope(kr_ref[...])
    kro_ref[...] = kr
    cq = cq_ref[...]
    cqn = (cq * _rms(cq, Q_LORA) * gqn_ref[...]).astype(bf16)
    latb = lat.astype(bf16)
    v_ref[...] = _dot(latb, wuv_ref[...]).astype(v_ref.dtype)
    gq, gk = gq_ref[...], gk_ref[...]
    for h in range(MLA_HEADS):
        sl = slice(h * HT, (h + 1) * HT)
        qh = rope(_dot(cqn, wuq_ref[:, sl]))
        q_ref[:, sl] = (qh * _rms(qh, QK_DIM) * gq).astype(q_ref.dtype)
        kh = _dot(latb, wuk_ref[:, sl]) + kr
        k_ref[:, sl] = (kh * _rms(kh, QK_DIM) * gk).astype(k_ref.dtype)


def _mla_prep(cq, ckv, kr, tabs, n_tab_blocks, gqn, gkv, wuq, wuk, wuv, gq, gk, tm, q_dtype):
    m = cq.shape[0]
    row = lambda n: pl.BlockSpec((tm, n), lambda i: (i, 0))
    tab = pl.BlockSpec((tm, LANE), lambda i: (i % n_tab_blocks, 0))
    widths = (KV_LORA, LANE, MLA_HEADS * HT, MLA_HEADS * HT, MLA_WIDTH)
    dtypes = (f32, f32, q_dtype, bf16, bf16)
    return pl.pallas_call(
        _mla_prep_body,
        grid=(m // tm,),
        in_specs=[row(Q_LORA), row(KV_LORA), row(LANE), tab, tab, tab,
                  _const_spec((1, Q_LORA)), _const_spec((1, KV_LORA)),
                  _const_spec((Q_LORA, MLA_HEADS * HT)), _const_spec((KV_LORA, MLA_HEADS * HT)),
                  _const_spec((KV_LORA, MLA_WIDTH)), _const_spec((1, HT)), _const_spec((1, HT))],
        out_specs=[row(n) for n in widths],
        out_shape=[jax.ShapeDtypeStruct((m, n), d) for n, d in zip(widths, dtypes)],
        compiler_params=_params(("parallel",), 40),
        name="mla_prep",
    )(cq, ckv, kr, *tabs, gqn, gkv, wuq, wuk, wuv, gq, gk)


def _ssd_body(xbc_ref, z_ref, dt_ref, dtt_ref, hist_ref, h0_ref, cw_ref, cb_ref, bias_r_ref, bias_c_ref,
              alog_r_ref, alog_c_ref, dsk_ref, ng_ref, tri_ref, y_ref, hout_ref, xp_ref, hs_ref, *, valid_from):
    c = pl.program_id(1)
    q = SSD_CHUNK

    @pl.when(c == 0)
    def _():
        xp_ref[0:SUBLANE, :] = hist_ref[...]
        hs_ref[...] = h0_ref[0]

    xbc = xbc_ref[...]
    xp_ref[SUBLANE:SUBLANE + q, :] = xbc
    conv = cb_ref[...]
    for k in range(CONV_W):
        lo = SUBLANE - (CONV_W - 1) + k
        conv = conv + xp_ref[lo:lo + q, :] * cw_ref[k:k + 1, :]
    xp_ref[0:SUBLANE, :] = xbc[q - SUBLANE:q, :]
    xa = _silu(conv)
    xs = xa[:, :SSD_WIDTH]
    bm = xa[:, SSD_WIDTH:SSD_WIDTH + LANE]
    cm = xa[:, SSD_WIDTH + LANE:]

    rows = lax.broadcasted_iota(jnp.int32, (q, q), 0)
    cols = lax.broadcasted_iota(jnp.int32, (q, q), 1)
    lane = lax.broadcasted_iota(jnp.int32, (1, LANE), 1)
    low = lane < D_STATE

    dtc = _softplus(dt_ref[...] + bias_r_ref[...])
    dtr = _softplus(dtt_ref[...] + bias_c_ref[...])
    if valid_from:
        dtc = jnp.where(rows >= valid_from, dtc, 0.0)
        dtr = jnp.where(lax.broadcasted_iota(jnp.int32, (16, q), 1) >= valid_from, dtr, 0.0)
    tri = tri_ref[...]
    acs_c = jnp.dot(tri, dtc * -jnp.exp(alog_r_ref[...]), precision=_HI, preferred_element_type=f32)
    acs_r = lax.dot_general(dtr * -jnp.exp(alog_c_ref[...]), tri, _NT, precision=_HI, preferred_element_type=f32)
    w_c = jnp.exp(acs_c[q - 1:q, :] - acs_c) * dtc
    e_c = jnp.exp(acs_c)
    cd_r = jnp.exp(acs_r[:, q - 1:q])

    causal = cols <= rows
    bb = bm.astype(bf16)
    dsk = dsk_ref[...]
    ys = []
    for g in range(SSD_GROUPS):
        gmask = (lane >= g * D_STATE) & (lane < (g + 1) * D_STATE)
        cg = jnp.where(gmask, cm, 0.0).astype(bf16)
        cb_g = _dot_nt(cg, bb)
        for pi in range(2):
            i = 2 * g + pi
            xpair = xs[:, i * LANE:(i + 1) * LANE]
            ypair = jnp.zeros((q, LANE), f32)
            for hh in range(2):
                h = 2 * i + hh
                seg = acs_c[:, h:h + 1] - acs_r[h:h + 1, :]
                lmat = jnp.exp(jnp.where(causal, seg, -jnp.inf))
                sc = (cb_g * lmat * dtr[h:h + 1, :]).astype(bf16)
                xh = jnp.where(low if hh == 0 else jnp.logical_not(low), xpair, 0.0).astype(bf16)
                ypair = ypair + _dot(sc, xh)
            h0, h1 = 2 * i, 2 * i + 1
            wp = jnp.where(low, w_c[:, h0:h0 + 1], w_c[:, h1:h1 + 1])
            ep = jnp.where(low, e_c[:, h0:h0 + 1], e_c[:, h1:h1 + 1])
            hst = hs_ref[i]
            yoff = _dot_nt(cg, hst.astype(bf16)) * ep
            st = _dot_tn((xpair * wp).astype(bf16), bb)
            cdb = jnp.concatenate([jnp.broadcast_to(cd_r[h0:h0 + 1, :], (SSD_HEAD_DIM, LANE)),
                                   jnp.broadcast_to(cd_r[h1:h1 + 1, :], (SSD_HEAD_DIM, LANE))], axis=0)
            hs_ref[i] = hst * cdb + st
            ys.append(ypair + yoff + dsk[:, i * LANE:(i + 1) * LANE] * xpair)
    y = jnp.concatenate(ys, axis=1) * _silu(z_ref[...])
    gw = SSD_WIDTH // SSD_GROUPS
    outs = []
    for g in range(SSD_GROUPS):
        yg = y[:, g * gw:(g + 1) * gw]
        outs.append(yg * _rms(yg, gw))
    y_ref[...] = (jnp.concatenate(outs, axis=1) * ng_ref[...]).astype(y_ref.dtype)

    @pl.when(c == pl.num_programs(1) - 1)
    def _():
        hout_ref[0] = hs_ref[...]


def _ssd(xbc, z, dt, dtt, hist_arr, hist_block, h0, consts, n_batch, n_chunks, row_block0, valid_from):
    q = SSD_CHUNK
    rb = lambda n: pl.BlockSpec((q, n), lambda b, c: (row_block0 + b * n_chunks + c, 0))
    cw, cb, bias_r, bias_c, alog_r, alog_c, dsk, ng, tri = consts
    return pl.pallas_call(
        functools.partial(_ssd_body, valid_from=valid_from),
        grid=(n_batch, n_chunks),
        in_specs=[rb(CONV_DIM), rb(SSD_WIDTH), rb(LANE),
                  pl.BlockSpec((16, q), lambda b, c: (0, row_block0 + b * n_chunks + c)),
                  pl.BlockSpec((SUBLANE, CONV_DIM), lambda b, c: (hist_block, 0)),
                  _const_spec((1, 4, LANE, LANE)),
                  _const_spec((CONV_W, CONV_DIM)), _const_spec((1, CONV_DIM)), _const_spec((1, LANE)),
                  _const_spec((16, 1)), _const_spec((1, LANE)), _const_spec((16, 1)),
                  _const_spec((1, SSD_WIDTH)), _const_spec((1, SSD_WIDTH)), _const_spec((q, q))],
        out_specs=[pl.BlockSpec((q, SSD_WIDTH), lambda b, c: (b * n_chunks + c, 0)),
                   pl.BlockSpec((1, 4, LANE, LANE), lambda b, c: (b, 0, 0, 0))],
        out_shape=[jax.ShapeDtypeStruct((n_batch * n_chunks * q, SSD_WIDTH), bf16),
                   jax.ShapeDtypeStruct((n_batch, 4, LANE, LANE), f32)],
        scratch_shapes=[pltpu.VMEM((SUBLANE + q, CONV_DIM), f32), pltpu.VMEM((4, LANE, LANE), f32)],
        compiler_params=_params(("parallel", "arbitrary"), 32),
        name="ssd",
    )(xbc, z, dt, dtt, hist_arr, h0, cw, cb, bias_r, bias_c, alog_r, alog_c, dsk, ng, tri)


def _flash_body(q_ref, k_ref, v_ref, km_ref, vm_ref, o_ref, *, tile):
    qi = pl.program_id(2)
    lane = lax.broadcasted_iota(jnp.int32, (1, LANE), 1)
    rows = lax.broadcasted_iota(jnp.int32, (tile, tile), 0)
    cols = lax.broadcasted_iota(jnp.int32, (tile, tile), 1)
    qs = (q_ref[:, 0:HT], q_ref[:, HT:2 * HT])
    vmasks = (lane < V_DIM, lane >= V_DIM)

    def lane_fold(x, op):
        r = x[:, 0:LANE]
        for i in range(1, x.shape[1] // LANE):
            r = op(r, x[:, i * LANE:(i + 1) * LANE])
        return r

    def step(hh, kt, vt, carry, mask):
        m, l, acc = carry
        s = _dot_nt(qs[hh], kt)
        if mask is not None:
            s = jnp.where(mask, s, -jnp.inf)
        wide = s.shape[1] % LANE == 0
        m2 = jnp.maximum(m, jnp.max(lane_fold(s, jnp.maximum) if wide else s, axis=-1, keepdims=True))
        p = jnp.exp(s - m2)
        a = jnp.exp(m - m2)
        vz = jnp.where(vmasks[hh], vt, jnp.zeros_like(vt))
        psum = jnp.sum(lane_fold(p, jnp.add) if wide else p, axis=-1, keepdims=True)
        return m2, a * l + psum, a * acc + _dot(p.astype(bf16), vz)

    def both(k2, vt, carries, mask=None):
        return tuple(step(hh, k2[:, hh * HT:(hh + 1) * HT], vt, carries[hh], mask) for hh in range(2))

    init = (jnp.full((tile, 1), -jnp.inf, f32), jnp.zeros((tile, 1), f32), jnp.zeros((tile, LANE), f32))
    carries = both(km_ref[...], vm_ref[...], (init, init))

    def body(j, carries):
        off = pl.multiple_of(j * tile, tile)
        return both(k_ref[pl.ds(off, tile), :], v_ref[pl.ds(off, tile), :], carries)

    carries = lax.fori_loop(0, qi, body, carries)
    off = pl.multiple_of(qi * tile, tile)
    (_, l0, acc0), (_, l1, acc1) = both(k_ref[pl.ds(off, tile), :], v_ref[pl.ds(off, tile), :], carries, cols <= rows)
    o_ref[...] = (acc0 / l0 + acc1 / l1).astype(o_ref.dtype)


def _flash(q, k, v, k_small, v_small, meta_block, n_batch, seq, tile):
    nq = seq // tile
    return pl.pallas_call(
        functools.partial(_flash_body, tile=tile),
        grid=(n_batch, MLA_HEADS // 2, nq),
        in_specs=[pl.BlockSpec((tile, 2 * HT), lambda b, p, i: (b * nq + i, p)),
                  pl.BlockSpec((seq, 2 * HT), lambda b, p, i: (b, p)),
                  pl.BlockSpec((seq, LANE), lambda b, p, i: (b, p)),
                  pl.BlockSpec((N_META, 2 * HT), lambda b, p, i: (meta_block, p)),
                  pl.BlockSpec((N_META, LANE), lambda b, p, i: (meta_block, p))],
        out_specs=pl.BlockSpec((tile, LANE), lambda b, p, i: (b * nq + i, p)),
        out_shape=jax.ShapeDtypeStruct((n_batch * seq, MLA_WIDTH), f32),
        compiler_params=_params(("parallel", "parallel", "arbitrary"), 40),
        name="flash",
    )(q, k, v, k_small, v_small)


def _finish_body(x_ref, ssd_ref, att_ref, gm_ref, wo_ref, gf_ref, wg_ref, wu_ref, wd_ref, o_ref):
    att = att_ref[...]
    mla = (att * _rms(att, MLA_WIDTH) * gm_ref[...]).astype(bf16)
    h = x_ref[...] + (_dot(ssd_ref[...], wo_ref[0:SSD_WIDTH, :]) + _dot(mla, wo_ref[SSD_WIDTH:, :]))
    n = (h * _rms(h, D_MODEL) * gf_ref[...]).astype(bf16)
    ff = jnp.zeros_like(h)
    for c in range(N_FF_CHUNKS):
        cs = slice(c * FF_CHUNK, (c + 1) * FF_CHUNK)
        a = (_silu(_dot(n, wg_ref[:, cs])) * _dot(n, wu_ref[:, cs])).astype(bf16)
        ff = ff + _dot(a, wd_ref[cs, :])
    o_ref[...] = h + ff


def _finish(x, ssd, att, gm, wo, gf, wg, wu, wd, tm):
    m = x.shape[0]
    row = lambda n: pl.BlockSpec((tm, n), lambda i: (i, 0))
    return pl.pallas_call(
        _finish_body,
        grid=(m // tm,),
        in_specs=[row(D_MODEL), row(SSD_WIDTH), row(MLA_WIDTH), _const_spec((1, MLA_WIDTH)),
                  _const_spec((D_MODEL, D_MODEL)), _const_spec((1, D_MODEL)),
                  _const_spec((D_MODEL, D_FF)), _const_spec((D_MODEL, D_FF)), _const_spec((D_FF, D_MODEL))],
        out_specs=row(D_MODEL),
        out_shape=jax.ShapeDtypeStruct((m, D_MODEL), f32),
        compiler_params=_params(("parallel",), 56),
        name="finish",
    )(x, ssd, att, gm, wo, gf, wg, wu, wd)


def _sssd_prep_body(xbc_ref, sc_ref, dt_ref, cw_ref, cb_ref, bias_r_ref, alog_r_ref, ind_ref,
                    xs_ref, xdt_ref, dec_ref, bc_ref):
    conv = cb_ref[...]
    for k in range(CONV_W - 1):
        conv = conv + sc_ref[k] * cw_ref[k:k + 1, :]
    conv = conv + xbc_ref[...] * cw_ref[CONV_W - 1:CONV_W, :]
    xa = _silu(conv)
    dt = _softplus(dt_ref[...] + bias_r_ref[...])
    dec = jnp.exp(dt * -jnp.exp(alog_r_ref[...]))
    ind = ind_ref[...]
    xs = xa[:, :SSD_WIDTH]
    xs_ref[...] = xs
    xdt_ref[...] = xs * jnp.dot(dt, ind, precision=_HI, preferred_element_type=f32)
    dec_ref[...] = jnp.dot(dec, ind, precision=_HI, preferred_element_type=f32)
    bc_ref[...] = xa[:, SSD_WIDTH:]


def _sssd_prep(xbc_small, sc, dt_small, cw, cb, bias_r, alog_r, ind, nseq):
    blk = lambda n: pl.BlockSpec((nseq, n), lambda i: (0, 0))
    widths = (SSD_WIDTH, SSD_WIDTH, SSD_WIDTH, 2 * LANE)
    return pl.pallas_call(
        _sssd_prep_body,
        grid=(1,),
        in_specs=[blk(CONV_DIM), _const_spec((CONV_W - 1, nseq, CONV_DIM)), blk(LANE), _const_spec((CONV_W, CONV_DIM)),
                  _const_spec((1, CONV_DIM)), _const_spec((1, LANE)), _const_spec((1, LANE)), _const_spec((LANE, SSD_WIDTH))],
        out_specs=[blk(n) for n in widths],
        out_shape=[jax.ShapeDtypeStruct((nseq, n), f32) for n in widths],
        compiler_params=_params(("arbitrary",), 32),
        name="sssd_prep",
    )(xbc_small, sc, dt_small, cw, cb, bias_r, alog_r, ind)


def _split_bf16(x):
    hi = x.astype(bf16).astype(f32)
    return hi, x - hi


def _sssd_step_body(h0_ref, xs_ref, xdt_ref, dec_ref, bc_ref, z_ref, dsk_ref, ng_ref, hn_ref, y_ref, *, nseq):
    hp = SSD_HEADS * SSD_HEAD_DIM
    row8 = lax.broadcasted_iota(jnp.int32, (SUBLANE, hp), 0)
    lane_hp = lax.broadcasted_iota(jnp.int32, (1, hp), 1)
    g0 = lane_hp < hp // SSD_GROUPS
    lane = lax.broadcasted_iota(jnp.int32, (1, LANE), 1)
    low = lane < D_STATE
    rrow = lax.broadcasted_iota(jnp.int32, (SUBLANE, 2 * LANE), 0)
    rlane = lax.broadcasted_iota(jnp.int32, (SUBLANE, 2 * LANE), 1)
    zeros8 = jnp.zeros((SUBLANE, hp), f32)
    for s in range(nseq):
        xdt = xdt_ref[s:s + 1, :]
        dec = dec_ref[s:s + 1, :]
        b = bc_ref[s:s + 1, 0:LANE]
        c = bc_ref[s:s + 1, LANE:]
        b8 = jnp.broadcast_to(b, (SUBLANE, LANE))
        c8 = jnp.broadcast_to(c, (SUBLANE, LANE))
        b_sw = pltpu.roll(b8, D_STATE, 1)
        c_sw = pltpu.roll(c8, D_STATE, 1)
        lm = jnp.where(row8 == 0, jnp.where(g0, xdt, 0.0),
                       jnp.where(row8 == 1, jnp.where(g0, 0.0, xdt), jnp.where(row8 == 2, dec, 0.0)))
        bsel = jnp.where(rrow[:, :LANE] == 0, b8, b_sw)
        rm = jnp.concatenate([jnp.where((rrow[:, :LANE] < 2) & (rlane[:, :LANE] < D_STATE), bsel, 0.0),
                              jnp.where((rrow[:, :LANE] == 2) & (rlane[:, :LANE] < D_STATE), 1.0, 0.0)], axis=1)
        lhi, llo = _split_bf16(lm)
        rhi, rlo = _split_bf16(rm)
        lst = jnp.concatenate([lhi, lhi, llo, zeros8], axis=0).astype(bf16)
        rst = jnp.concatenate([rhi, rlo, rhi, jnp.zeros_like(rhi)], axis=0).astype(bf16)
        outer = _dot_tn(lst, rst)
        h0 = h0_ref[s].reshape(hp, D_STATE)
        hn_ref[s] = (h0 * outer[:, LANE:LANE + D_STATE] + outer[:, 0:D_STATE]).reshape(SSD_HEADS, SSD_HEAD_DIM, D_STATE)
        csel = jnp.where(row8[:, :LANE] == 0, c8, jnp.where(row8[:, :LANE] == 1, c_sw, 0.0))[:, :D_STATE]
        ch = _dot_nt(csel.astype(bf16), h0.astype(bf16))
        chr_ = jnp.where(g0, ch[0:1, :], ch[1:2, :])
        bcp = b * c
        bc0 = jnp.sum(jnp.where(low, bcp, 0.0), axis=-1, keepdims=True)
        bc1 = jnp.sum(jnp.where(low, 0.0, bcp), axis=-1, keepdims=True)
        y_ref[s:s + 1, :] = dec * chr_ + xdt * jnp.where(g0, bc0, bc1) + dsk_ref[...] * xs_ref[s:s + 1, :]
    y = y_ref[...] * _silu(z_ref[...])
    gw = SSD_WIDTH // SSD_GROUPS
    outs = []
    for g in range(SSD_GROUPS):
        yg = y[:, g * gw:(g + 1) * gw]
        outs.append(yg * _rms(yg, gw))
    y_ref[...] = jnp.concatenate(outs, axis=1) * ng_ref[...]


def _sssd_step(h0, xs, xdt, dec, bc, z_small, dsk, ng, nseq_total, nseq):
    blk = lambda n: pl.BlockSpec((nseq, n), lambda i: (i, 0))
    hblk = pl.BlockSpec((nseq, SSD_HEADS, SSD_HEAD_DIM, D_STATE), lambda i: (i, 0, 0, 0))
    return pl.pallas_call(
        functools.partial(_sssd_step_body, nseq=nseq),
        grid=(nseq_total // nseq,),
        in_specs=[hblk, blk(SSD_WIDTH), blk(SSD_WIDTH), blk(SSD_WIDTH), blk(2 * LANE), blk(SSD_WIDTH),
                  _const_spec((1, SSD_WIDTH)), _const_spec((1, SSD_WIDTH))],
        out_specs=[hblk, blk(SSD_WIDTH)],
        out_shape=[jax.ShapeDtypeStruct((nseq_total, SSD_HEADS, SSD_HEAD_DIM, D_STATE), f32),
                   jax.ShapeDtypeStruct((nseq_total, SSD_WIDTH), f32)],
        compiler_params=_params(("parallel",), 32),
        name="sssd_step",
    )(h0, xs, xdt, dec, bc, z_small, dsk, ng)


def _absorb_body(q_ref, gk_ref, wabs_ref, qg_ref, a_ref):
    gk = gk_ref[...]
    for h in range(MLA_HEADS):
        sl = slice(h * HT, (h + 1) * HT)
        qg = q_ref[:, sl] * gk
        qg_ref[:, sl] = qg
        a_ref[:, h * KV_LORA:(h + 1) * KV_LORA] = jnp.dot(qg, wabs_ref[sl, :], precision=_HI, preferred_element_type=f32)
    a_ref[:, MLA_HEADS * KV_LORA:] = jnp.zeros((q_ref.shape[0], (16 - MLA_HEADS) * KV_LORA), f32)


def _absorb(q_small, gk, wabs, nseq):
    return pl.pallas_call(
        _absorb_body,
        grid=(1,),
        in_specs=[pl.BlockSpec((nseq, MLA_HEADS * HT), lambda i: (0, 0)), _const_spec((1, HT)),
                  _const_spec((MLA_HEADS * HT, KV_LORA))],
        out_specs=[pl.BlockSpec((nseq, MLA_HEADS * HT), lambda i: (0, 0)),
                   pl.BlockSpec((nseq, 16 * KV_LORA), lambda i: (0, 0))],
        out_shape=[jax.ShapeDtypeStruct((nseq, MLA_HEADS * HT), f32), jax.ShapeDtypeStruct((nseq, 16 * KV_LORA), f32)],
        compiler_params=_params(("arbitrary",), 32),
        name="absorb",
    )(q_small, gk, wabs)


def _decode_body(pt_ref, a_ref, c_ref, latn_ref, krnt_ref, wukt_ref, wuv_ref, clat_ref, ckrt_ref, o_ref,
                 lat_buf, kr_buf, waug, latb, s_buf, sems, *, n_pages, page, ppt, unroll):
    b = pl.program_id(0)
    nb = pl.num_programs(0)
    slot = b % 2
    n_tiles = n_pages // ppt
    tile = ppt * page
    wrows = MLA_HEADS * QK_NOPE

    def start_page(seq, p, sl):
        pid = pt_ref[seq * n_pages + p]
        pltpu.make_async_copy(clat_ref.at[0, pid], lat_buf.at[sl, p], sems.at[0, sl]).start()
        pltpu.make_async_copy(ckrt_ref.at[0, pid], kr_buf.at[sl, p], sems.at[1, sl]).start()

    @pl.when(b == 0)
    def _():
        waug[0:wrows, :] = wukt_ref[...]

        def first(p, carry):
            start_page(0, p, 0)
            return carry

        lax.fori_loop(0, n_pages, first, 0)

    pltpu.make_async_copy(clat_ref.at[0, pl.ds(0, n_pages)], lat_buf.at[slot], sems.at[0, slot]).wait()
    pltpu.make_async_copy(ckrt_ref.at[0, pl.ds(0, n_pages)], kr_buf.at[slot], sems.at[1, slot]).wait()

    waug[wrows:wrows + 16, :] = a_ref[0, 0].astype(bf16)
    cb = c_ref[0].astype(bf16)

    def scores(lb, krt):
        n = lb.shape[0]
        kt = _dot_nt(waug[...], lb)
        k3 = kt[0:wrows, :].reshape(MLA_HEADS, QK_NOPE, n)
        n2 = jnp.sum(k3 * k3, axis=1)
        s1 = kt[wrows:wrows + MLA_HEADS, :]
        s2 = _dot(cb, krt.astype(bf16))
        kr2 = jnp.sum(krt * krt, axis=0, keepdims=True)
        return lax.rsqrt((n2 + kr2) * (1.0 / QK_DIM) + EPS) * (s1 + s2)

    def body(jj, carry):
        @pl.when(b + 1 < nb)
        def _():
            for pp in range(unroll * ppt):
                start_page(b + 1, jj * (unroll * ppt) + pp, 1 - slot)

        for u in range(unroll):
            j = jj * unroll + u
            lb = lat_buf[slot, pl.ds(j * ppt, ppt)].reshape(tile, KV_LORA).astype(bf16)
            latb[j] = lb
            kr_pages = kr_buf[slot, pl.ds(j * ppt, ppt)]
            s_buf[j] = scores(lb, jnp.concatenate([kr_pages[i] for i in range(ppt)], axis=-1))
        return carry

    lax.fori_loop(0, n_tiles // unroll, body, 0)

    first_row = lax.broadcasted_iota(jnp.int32, (page, KV_LORA), 0) == 0
    first_lane = lax.broadcasted_iota(jnp.int32, (1, page), 1) == 0
    lb_new = jnp.where(first_row, latn_ref[0], 0.0).astype(bf16)
    s_new = jnp.where(first_lane, scores(lb_new, jnp.where(first_lane, krnt_ref[0], 0.0)), -jnp.inf)

    s_all = jnp.concatenate([s_buf[j] for j in range(n_tiles)], axis=-1)
    m = jnp.maximum(jnp.max(s_all, axis=-1, keepdims=True), jnp.max(s_new, axis=-1, keepdims=True))
    p_all = jnp.exp(s_all - m)
    p_new = jnp.exp(s_new - m)
    l = jnp.sum(p_all, axis=-1, keepdims=True) + jnp.sum(p_new, axis=-1, keepdims=True)
    acc = _dot(p_all.astype(bf16), latb[...].reshape(n_tiles * tile, KV_LORA)) + _dot(p_new.astype(bf16), lb_new)
    full = _dot((acc / l).astype(bf16), wuv_ref[...])
    hrow = lax.broadcasted_iota(jnp.int32, (MLA_HEADS, MLA_WIDTH), 0)
    hlane = lax.broadcasted_iota(jnp.int32, (MLA_HEADS, MLA_WIDTH), 1)
    o_ref[0] = jnp.sum(jnp.where(hlane // V_DIM == hrow, full, 0.0), axis=0, keepdims=True)


def _decode(page_table_flat, a, c, lat_new, kr_new_t, wukt, wuv, cache_lat, cache_kr_t, nseq, n_pages, page, ppt, unroll):
    wrows = MLA_HEADS * QK_NOPE
    grid_spec = pltpu.PrefetchScalarGridSpec(
        num_scalar_prefetch=1,
        grid=(nseq,),
        in_specs=[pl.BlockSpec((1, 1, 16, KV_LORA), lambda b, pt: (b, 0, 0, 0)),
                  pl.BlockSpec((1, MLA_HEADS, QK_ROPE), lambda b, pt: (b, 0, 0)),
                  pl.BlockSpec((1, 1, KV_LORA), lambda b, pt: (b, 0, 0)),
                  pl.BlockSpec((1, QK_ROPE, 1), lambda b, pt: (b, 0, 0)),
                  pl.BlockSpec((wrows, KV_LORA), lambda b, pt: (0, 0), pipeline_mode=pl.Buffered(1)),
                  pl.BlockSpec((KV_LORA, MLA_WIDTH), lambda b, pt: (0, 0), pipeline_mode=pl.Buffered(1)),
                  pl.BlockSpec(memory_space=pl.ANY), pl.BlockSpec(memory_space=pl.ANY)],
        out_specs=pl.BlockSpec((1, 1, MLA_WIDTH), lambda b, pt: (b, 0, 0)),
        scratch_shapes=[pltpu.VMEM((2, n_pages, page, KV_LORA), f32), pltpu.VMEM((2, n_pages, QK_ROPE, page), f32),
                        pltpu.VMEM((wrows + 16, KV_LORA), bf16),
                        pltpu.VMEM((n_pages // ppt, ppt * page, KV_LORA), bf16),
                        pltpu.VMEM((n_pages // ppt, MLA_HEADS, ppt * page), f32), pltpu.SemaphoreType.DMA((2, 2))],
    )
    return pl.pallas_call(
        functools.partial(_decode_body, n_pages=n_pages, page=page, ppt=ppt, unroll=unroll),
        grid_spec=grid_spec,
        out_shape=jax.ShapeDtypeStruct((nseq, 1, MLA_WIDTH), f32),
        compiler_params=_params(("arbitrary",), 40),
        name="decode",
    )(page_table_flat, a, c, lat_new, kr_new_t, wukt, wuv, cache_lat, cache_kr_t)


def _rope_tables(pos):
    inv = ROPE_THETA ** (-jnp.arange(ROPE_HALF, dtype=f32) * (2.0 / QK_ROPE))
    ang = pos.astype(f32)[:, None] * inv[None, :]
    cos, sin = jnp.cos(ang), jnp.sin(ang)
    n = pos.shape[0]
    one = jnp.ones((n, QK_NOPE), f32)
    z16 = jnp.zeros((n, ROPE_HALF), f32)
    z32 = jnp.zeros((n, HT - QK_DIM), f32)
    z64 = jnp.zeros((n, QK_NOPE), f32)
    tc = jnp.concatenate([one, cos, cos, z32], axis=1)
    ts1 = jnp.concatenate([z64, -sin, z16, z32], axis=1)
    ts2 = jnp.concatenate([z64, z16, sin, z32], axis=1)
    return tc, ts1, ts2


def _pad_lanes(x, n):
    return jnp.pad(x, ((0, 0), (0, n - x.shape[1])))


def kernel(x_prompt, x_sample, cache_kv_latent, cache_k_rope, state_conv, state_ssm, page_table, meta_tokens, attn_norm_g, w_in, conv_w, conv_b, dt_bias, a_log, d_skip, ssd_norm_g, q_norm_g, w_uq, kv_norm_g, w_ukv, q_head_g_nope, q_head_g_rope, k_head_g_nope, k_head_g_rope, mla_out_g, w_out, ffn_norm_g, w_gate, w_up, w_down):
    l = 0
    nb, seq, _ = x_prompt.shape
    nseq = x_sample.shape[0]
    n_pages, page = page_table.shape[1], cache_kv_latent.shape[2]
    past = n_pages * page
    n_small = 2 * LANE
    meta_lo = n_small - N_META

    s0, s1, s2, s3, s4 = (SSD_WIDTH, SSD_WIDTH + CONV_DIM, SSD_WIDTH + CONV_DIM + SSD_HEADS,
                          SSD_WIDTH + CONV_DIM + SSD_HEADS + Q_LORA, SSD_WIDTH + CONV_DIM + SSD_HEADS + Q_LORA + KV_LORA)
    wi = w_in[l]
    w_kr = jnp.pad(wi[:, s4:], ((0, 0), (ROPE_LO, HT - QK_DIM)))
    w_proj = jnp.concatenate([wi[:, :s1], wi[:, s2:s4], _pad_lanes(wi[:, s1:s2], LANE), w_kr], axis=1).astype(bf16)
    w_dtt = jnp.pad(wi[:, s1:s2].T, ((0, 16 - SSD_HEADS), (0, 0))).astype(bf16)
    g_attn = attn_norm_g[l][None, :]
    wuq = jnp.pad(w_uq[l].reshape(Q_LORA, MLA_HEADS, QK_DIM), ((0, 0), (0, 0), (0, HT - QK_DIM))).reshape(Q_LORA, -1).astype(bf16)
    wkv = w_ukv[l].reshape(KV_LORA, MLA_HEADS, QK_NOPE + V_DIM)
    wuk_f32 = jnp.pad(wkv[:, :, :QK_NOPE], ((0, 0), (0, 0), (0, HT - QK_NOPE))).reshape(KV_LORA, -1)
    wuk = wuk_f32.astype(bf16)
    wuv = wkv[:, :, QK_NOPE:].reshape(KV_LORA, MLA_WIDTH).astype(bf16)
    wukt = wkv[:, :, :QK_NOPE].reshape(KV_LORA, -1).T.astype(bf16)
    gq = _pad_lanes(jnp.concatenate([q_head_g_nope[l], q_head_g_rope[l], q_head_g_rope[l]])[None, :], HT) * ATTN_SCALE
    gk = _pad_lanes(jnp.concatenate([k_head_g_nope[l], k_head_g_rope[l], k_head_g_rope[l]])[None, :], HT)
    gqn, gkv = q_norm_g[l][None, :], kv_norm_g[l][None, :]
    cw, cb = conv_w[l], conv_b[l][None, :]
    bias_r = _pad_lanes(dt_bias[l][None, :], LANE)
    bias_c = jnp.pad(dt_bias[l][:, None], ((0, 16 - SSD_HEADS), (0, 0)))
    alog_r = _pad_lanes(a_log[l][None, :], LANE)
    alog_c = jnp.pad(a_log[l][:, None], ((0, 16 - SSD_HEADS), (0, 0)))
    dsk = jnp.repeat(d_skip[l], SSD_HEAD_DIM)[None, :]
    ng = ssd_norm_g[l][None, :]
    tri = jnp.tril(jnp.ones((SSD_CHUNK, SSD_CHUNK), f32))
    ind = (jnp.arange(LANE)[:, None] == jnp.arange(SSD_WIDTH)[None, :] // SSD_HEAD_DIM).astype(f32)
    ssd_consts = (cw, cb, bias_r, bias_c, alog_r, alog_c, dsk, ng, tri)
    gm, gf = mla_out_g[l][None, :], ffn_norm_g[l][None, :]
    wo = w_out[l].astype(bf16)
    wg, wu, wd = w_gate[l].astype(bf16), w_up[l].astype(bf16), w_down[l].astype(bf16)

    xp = x_prompt.reshape(nb * seq, D_MODEL)
    xs_rows = x_sample[:, 0, :]
    x_small = jnp.concatenate([xs_rows, jnp.zeros((meta_lo - nseq, D_MODEL), f32), meta_tokens.astype(f32)], axis=0)

    z_p, xbc_p, cq_p, ckv_p, dt_p, kr_p, dtt_p = _project(xp, g_attn, w_proj, w_dtt, 512)
    z_s, xbc_s, cq_s, ckv_s, dt_s, kr_s, dtt_s = _project(x_small, g_attn, w_proj, w_dtt, n_small)

    tabs_p = _rope_tables(N_META + jnp.arange(seq))
    pos_small = jnp.concatenate([jnp.full((nseq,), past), jnp.zeros((meta_lo - nseq,), jnp.int32), jnp.arange(N_META)])
    tabs_s = _rope_tables(pos_small)
    tm = 512
    lat_p, kro_p, q_p, k_p, v_p = _mla_prep(cq_p, ckv_p, kr_p, tabs_p, seq // tm, gqn, gkv, wuq, wuk, wuv, gq, gk, tm, bf16)
    lat_s, kro_s, q_s, k_s, v_s = _mla_prep(cq_s, ckv_s, kr_s, tabs_s, 1, gqn, gkv, wuq, wuk, wuv, gq, gk, n_small, f32)

    zero_h = jnp.zeros((1, 4, LANE, LANE), f32)
    zero_hist = jnp.zeros((SUBLANE, CONV_DIM), f32)
    _, h_meta = _ssd(xbc_s, z_s, dt_s, dtt_s, zero_hist, 0, zero_h, ssd_consts, 1, 1, 1, SSD_CHUNK - N_META)
    ssd_p, h_fin = _ssd(xbc_p, z_p, dt_p, dtt_p, xbc_s, n_small // SUBLANE - 1, h_meta, ssd_consts, nb, seq // SSD_CHUNK, 0, 0)

    att_p = _flash(q_p, k_p, v_p, k_s, v_s, n_small // N_META - 1, nb, seq, 512)
    y_prompt = _finish(xp, ssd_p, att_p, gm, wo, gf, wg, wu, wd, 512).reshape(nb, seq, D_MODEL)

    sc = jnp.transpose(state_conv[l], (1, 0, 2))
    xs_s, xdt_s, dec_s, bc_s = _sssd_prep(xbc_s, sc, dt_s, cw, cb, bias_r, alog_r, ind, nseq)
    h_new, ssd_s = _sssd_step(state_ssm[l], xs_s, xdt_s, dec_s, bc_s, z_s, dsk, ng, nseq, 8)
    qg, a_abs = _absorb(q_s, gk, wuk_f32.T, nseq)
    a_abs = a_abs.reshape(nseq, 1, 16, KV_LORA)
    c_abs = qg.reshape(nseq, MLA_HEADS, HT)[:, :, ROPE_LO:QK_DIM]
    lat_new = lat_s[:nseq][:, None, :]
    kr_new = kro_s[:nseq, ROPE_LO:QK_DIM][:, None, :]
    att_s = _decode(page_table.reshape(-1), a_abs, c_abs, lat_new, jnp.swapaxes(kr_new, 1, 2), wukt, wuv,
                    cache_kv_latent, jnp.swapaxes(cache_k_rope, 2, 3), nseq, n_pages, page, 4, 4)
    y_sample = _finish(xs_rows, ssd_s.astype(bf16), att_s[:, 0, :], gm, wo, gf, wg, wu, wd, nseq)[:, None, :]

    def with_meta(small, main, width):
        meta = jnp.broadcast_to(small[meta_lo:][None], (nb, N_META, width))
        return jnp.concatenate([meta, main.reshape(nb, seq, width)], axis=1)[None]

    kv_latent_prompt = with_meta(lat_s, lat_p, KV_LORA)
    k_rope_prompt = with_meta(kro_s[:, ROPE_LO:QK_DIM], kro_p[:, ROPE_LO:QK_DIM], QK_ROPE)
    conv_prompt = xbc_p.reshape(nb, seq, CONV_DIM)[:, seq - (CONV_W - 1):][None]
    hf = h_fin.reshape(nb, 4, 2, SSD_HEAD_DIM, SSD_GROUPS, D_STATE)
    ssm_prompt = jnp.stack([hf[:, i, :, :, i // 2, :] for i in range(4)], axis=1).reshape(nb, SSD_HEADS, SSD_HEAD_DIM, D_STATE)[None]
    kv_latent_sample = lat_new[None]
    k_rope_sample = kr_new[None]
    conv_sample = jnp.concatenate([state_conv[l][:, 1:], xbc_s[:nseq][:, None, :]], axis=1)[None]
    ssm_sample = h_new[None]
    return (y_prompt, y_sample, kv_latent_prompt, k_rope_prompt, conv_prompt, ssm_prompt.astype(x_prompt.dtype),
            kv_latent_sample, k_rope_sample, conv_sample, ssm_sample.astype(state_ssm.dtype))
```

```python
import functools

import jax
import jax.numpy as jnp
from jax import lax
from jax.experimental import pallas as pl
from jax.experimental.pallas import tpu as pltpu

f32 = jnp.float32
bf16 = jnp.bfloat16

D_MODEL = 1024
N_META = 16
SSD_HEADS = 8
SSD_HEAD_DIM = 64
SSD_WIDTH = SSD_HEADS * SSD_HEAD_DIM
SSD_GROUPS = 2
D_STATE = 64
CONV_W = 4
CONV_DIM = SSD_WIDTH + 2 * SSD_GROUPS * D_STATE
SSD_CHUNK = 128
MLA_HEADS = 8
QK_NOPE = 64
QK_ROPE = 32
QK_DIM = QK_NOPE + QK_ROPE
V_DIM = 64
MLA_WIDTH = MLA_HEADS * V_DIM
Q_LORA = 384
KV_LORA = 256
ROPE_THETA = 10000.0
ATTN_SCALE = QK_DIM ** -0.5
D_FF = 2816
EPS = 1e-6

LANE = 128
SUBLANE = 8
HT = LANE
ROPE_LO = QK_NOPE
ROPE_HALF = QK_ROPE // 2
FF_CHUNK = 256
N_FF_CHUNKS = D_FF // FF_CHUNK
MiB = 1024 * 1024

PC_Z = 0
PC_XBC = PC_Z + SSD_WIDTH
PC_CQ = PC_XBC + CONV_DIM
PC_CKV = PC_CQ + Q_LORA
PC_DT = PC_CKV + KV_LORA
PC_KR = PC_DT + LANE
PC_END = PC_KR + LANE

_NT = (((1,), (1,)), ((), ()))
_TN = (((0,), (0,)), ((), ()))
_HI = lax.Precision.HIGHEST


def _dot(a, b):
    return jnp.dot(a, b, preferred_element_type=f32)


def _dot_nt(a, b):
    return lax.dot_general(a, b, _NT, preferred_element_type=f32)


def _dot_tn(a, b):
    return lax.dot_general(a, b, _TN, preferred_element_type=f32)


def _rms(x, n):
    return lax.rsqrt(jnp.sum(x * x, axis=-1, keepdims=True) * (1.0 / n) + EPS)


def _silu(x):
    return x * jax.nn.sigmoid(x)


def _softplus(x):
    return jnp.maximum(x, 0.0) + jnp.log1p(jnp.exp(-jnp.abs(x)))


def _const_spec(shape):
    nd = len(shape)
    return pl.BlockSpec(shape, lambda *_: (0,) * nd, pipeline_mode=pl.Buffered(1))


def _params(sem, vmem_mib):
    return pltpu.CompilerParams(dimension_semantics=sem, vmem_limit_bytes=vmem_mib * MiB)


def _proj_body(x_ref, g_ref, w_ref, wdt_ref, z_ref, xbc_ref, cq_ref, ckv_ref, dt_ref, kr_ref, dtt_ref):
    x = x_ref[...]
    xn = (x * _rms(x, D_MODEL) * g_ref[...]).astype(bf16)
    z_ref[...] = _dot(xn, w_ref[:, PC_Z:PC_XBC])
    xbc_ref[...] = _dot(xn, w_ref[:, PC_XBC:PC_CQ])
    cq_ref[...] = _dot(xn, w_ref[:, PC_CQ:PC_CKV])
    ckv_ref[...] = _dot(xn, w_ref[:, PC_CKV:PC_DT])
    dt_ref[...] = _dot(xn, w_ref[:, PC_DT:PC_KR])
    kr_ref[...] = _dot(xn, w_ref[:, PC_KR:PC_END])
    dtt_ref[...] = _dot_nt(wdt_ref[...], xn)


def _project(x, g, w, wdt, tm):
    m = x.shape[0]
    row = lambda n: pl.BlockSpec((tm, n), lambda i: (i, 0))
    widths = (SSD_WIDTH, CONV_DIM, Q_LORA, KV_LORA, LANE, LANE)
    return pl.pallas_call(
        _proj_body,
        grid=(m // tm,),
        in_specs=[row(D_MODEL), _const_spec((1, D_MODEL)), _const_spec((D_MODEL, PC_END)), _const_spec((16, D_MODEL))],
        out_specs=[row(n) for n in widths] + [pl.BlockSpec((16, tm), lambda i: (0, i))],
        out_shape=[jax.ShapeDtypeStruct((m, n), f32) for n in widths] + [jax.ShapeDtypeStruct((16, m), f32)],
        compiler_params=_params(("parallel",), 40),
        name="proj",
    )(x, g, w, wdt)


def _mla_prep_body(cq_ref, ckv_ref, kr_ref, tc_ref, ts1_ref, ts2_ref, gqn_ref, gkv_ref, wuq_ref, wuk_ref, wuv_ref,
                   gq_ref, gk_ref, lat_ref, kro_ref, q_ref, k_ref, vt_ref):
    tc, ts1, ts2 = tc_ref[...], ts1_ref[...], ts2_ref[...]

    def rope(x):
        return x * tc + pltpu.roll(x, LANE - ROPE_HALF, 1) * ts1 + pltpu.roll(x, ROPE_HALF, 1) * ts2

    ckv = ckv_ref[...]
    lat = ckv * _rms(ckv, KV_LORA) * gkv_ref[...]
    lat_ref[...] = lat
    kr = rope(kr_ref[...])
    kro_ref[...] = kr
    cq = cq_ref[...]
    cqn = (cq * _rms(cq, Q_LORA) * gqn_ref[...]).astype(bf16)
    latb = lat.astype(bf16)
    vt_ref[...] = _dot_nt(wuv_ref[...], latb).astype(vt_ref.dtype)
    gq, gk = gq_ref[...], gk_ref[...]
    for h in range(MLA_HEADS):
        sl = slice(h * HT, (h + 1) * HT)
        qh = rope(_dot(cqn, wuq_ref[:, sl]))
        q_ref[:, sl] = (qh * _rms(qh, QK_DIM) * gq).astype(q_ref.dtype)
        kh = _dot(latb, wuk_ref[:, sl]) + kr
        k_ref[:, sl] = (kh * _rms(kh, QK_DIM) * gk).astype(k_ref.dtype)


def _mla_prep(cq, ckv, kr, tabs, n_tab_blocks, gqn, gkv, wuq, wuk, wuvt, gq, gk, tm, q_dtype):
    m = cq.shape[0]
    row = lambda n: pl.BlockSpec((tm, n), lambda i: (i, 0))
    tab = pl.BlockSpec((tm, LANE), lambda i: (i % n_tab_blocks, 0))
    widths = (KV_LORA, LANE, MLA_HEADS * HT, MLA_HEADS * HT)
    dtypes = (f32, f32, q_dtype, bf16)
    return pl.pallas_call(
        _mla_prep_body,
        grid=(m // tm,),
        in_specs=[row(Q_LORA), row(KV_LORA), row(LANE), tab, tab, tab,
                  _const_spec((1, Q_LORA)), _const_spec((1, KV_LORA)),
                  _const_spec((Q_LORA, MLA_HEADS * HT)), _const_spec((KV_LORA, MLA_HEADS * HT)),
                  _const_spec((MLA_WIDTH, KV_LORA)), _const_spec((1, HT)), _const_spec((1, HT))],
        out_specs=[row(n) for n in widths] + [pl.BlockSpec((MLA_WIDTH, tm), lambda i: (0, i))],
        out_shape=[jax.ShapeDtypeStruct((m, n), d) for n, d in zip(widths, dtypes)]
                  + [jax.ShapeDtypeStruct((MLA_WIDTH, m), bf16)],
        compiler_params=_params(("parallel",), 40),
        name="mla_prep",
    )(cq, ckv, kr, *tabs, gqn, gkv, wuq, wuk, wuvt, gq, gk)


def _ssd_body(xbc_ref, z_ref, dt_ref, dtt_ref, hist_ref, h0_ref, cw_ref, cb_ref, bias_r_ref, bias_c_ref,
              alog_r_ref, alog_c_ref, dsk_ref, ng_ref, tri_ref, y_ref, hout_ref, xp_ref, hs_ref, *, valid_from):
    c = pl.program_id(1)
    q = SSD_CHUNK

    @pl.when(c == 0)
    def _():
        xp_ref[0:SUBLANE, :] = hist_ref[...]
        hs_ref[...] = h0_ref[0]

    xbc = xbc_ref[...]
    xp_ref[SUBLANE:SUBLANE + q, :] = xbc
    conv = cb_ref[...]
    for k in range(CONV_W):
        lo = SUBLANE - (CONV_W - 1) + k
        conv = conv + xp_ref[lo:lo + q, :] * cw_ref[k:k + 1, :]
    xp_ref[0:SUBLANE, :] = xbc[q - SUBLANE:q, :]
    xa = _silu(conv)
    xs = xa[:, :SSD_WIDTH]
    bm = xa[:, SSD_WIDTH:SSD_WIDTH + LANE]
    cm = xa[:, SSD_WIDTH + LANE:]

    rows = lax.broadcasted_iota(jnp.int32, (q, q), 0)
    cols = lax.broadcasted_iota(jnp.int32, (q, q), 1)
    lane = lax.broadcasted_iota(jnp.int32, (1, LANE), 1)
    low = lane < D_STATE

    dtc = _softplus(dt_ref[...] + bias_r_ref[...])
    dtr = _softplus(dtt_ref[...] + bias_c_ref[...])
    if valid_from:
        dtc = jnp.where(rows >= valid_from, dtc, 0.0)
        dtr = jnp.where(lax.broadcasted_iota(jnp.int32, (16, q), 1) >= valid_from, dtr, 0.0)
    tri = tri_ref[...]
    acs_c = jnp.dot(tri, dtc * -jnp.exp(alog_r_ref[...]), precision=_HI, preferred_element_type=f32)
    acs_r = lax.dot_general(dtr * -jnp.exp(alog_c_ref[...]), tri, _NT, precision=_HI, preferred_element_type=f32)
    w_c = jnp.exp(acs_c[q - 1:q, :] - acs_c) * dtc
    e_c = jnp.exp(acs_c)
    cd_r = jnp.exp(acs_r[:, q - 1:q])

    causal = cols <= rows
    bb = bm.astype(bf16)
    dsk = dsk_ref[...]
    ys = []
    for g in range(SSD_GROUPS):
        gmask = (lane >= g * D_STATE) & (lane < (g + 1) * D_STATE)
        cg = jnp.where(gmask, cm, 0.0).astype(bf16)
        cb_g = _dot_nt(cg, bb)
        for pi in range(2):
            i = 2 * g + pi
            xpair = xs[:, i * LANE:(i + 1) * LANE]
            ypair = jnp.zeros((q, LANE), f32)
            for hh in range(2):
                h = 2 * i + hh
                seg = acs_c[:, h:h + 1] - acs_r[h:h + 1, :]
                lmat = jnp.exp(jnp.where(causal, seg, -jnp.inf))
                sc = (cb_g * lmat * dtr[h:h + 1, :]).astype(bf16)
                xh = jnp.where(low if hh == 0 else jnp.logical_not(low), xpair, 0.0).astype(bf16)
                ypair = ypair + _dot(sc, xh)
            h0, h1 = 2 * i, 2 * i + 1
            wp = jnp.where(low, w_c[:, h0:h0 + 1], w_c[:, h1:h1 + 1])
            ep = jnp.where(low, e_c[:, h0:h0 + 1], e_c[:, h1:h1 + 1])
            hst = hs_ref[i]
            yoff = _dot_nt(cg, hst.astype(bf16)) * ep
            st = _dot_tn((xpair * wp).astype(bf16), bb)
            cdb = jnp.concatenate([jnp.broadcast_to(cd_r[h0:h0 + 1, :], (SSD_HEAD_DIM, LANE)),
                                   jnp.broadcast_to(cd_r[h1:h1 + 1, :], (SSD_HEAD_DIM, LANE))], axis=0)
            hs_ref[i] = hst * cdb + st
            ys.append(ypair + yoff + dsk[:, i * LANE:(i + 1) * LANE] * xpair)
    y = jnp.concatenate(ys, axis=1) * _silu(z_ref[...])
    gw = SSD_WIDTH // SSD_GROUPS
    outs = []
    for g in range(SSD_GROUPS):
        yg = y[:, g * gw:(g + 1) * gw]
        outs.append(yg * _rms(yg, gw))
    y_ref[...] = (jnp.concatenate(outs, axis=1) * ng_ref[...]).astype(y_ref.dtype)

    @pl.when(c == pl.num_programs(1) - 1)
    def _():
        hout_ref[0] = hs_ref[...]


def _ssd(xbc, z, dt, dtt, hist_arr, hist_block, h0, consts, n_batch, n_chunks, row_block0, valid_from):
    q = SSD_CHUNK
    rb = lambda n: pl.BlockSpec((q, n), lambda b, c: (row_block0 + b * n_chunks + c, 0))
    cw, cb, bias_r, bias_c, alog_r, alog_c, dsk, ng, tri = consts
    return pl.pallas_call(
        functools.partial(_ssd_body, valid_from=valid_from),
        grid=(n_batch, n_chunks),
        in_specs=[rb(CONV_DIM), rb(SSD_WIDTH), rb(LANE),
                  pl.BlockSpec((16, q), lambda b, c: (0, row_block0 + b * n_chunks + c)),
                  pl.BlockSpec((SUBLANE, CONV_DIM), lambda b, c: (hist_block, 0)),
                  _const_spec((1, 4, LANE, LANE)),
                  _const_spec((CONV_W, CONV_DIM)), _const_spec((1, CONV_DIM)), _const_spec((1, LANE)),
                  _const_spec((16, 1)), _const_spec((1, LANE)), _const_spec((16, 1)),
                  _const_spec((1, SSD_WIDTH)), _const_spec((1, SSD_WIDTH)), _const_spec((q, q))],
        out_specs=[pl.BlockSpec((q, SSD_WIDTH), lambda b, c: (b * n_chunks + c, 0)),
                   pl.BlockSpec((1, 4, LANE, LANE), lambda b, c: (b, 0, 0, 0))],
        out_shape=[jax.ShapeDtypeStruct((n_batch * n_chunks * q, SSD_WIDTH), bf16),
                   jax.ShapeDtypeStruct((n_batch, 4, LANE, LANE), f32)],
        scratch_shapes=[pltpu.VMEM((SUBLANE + q, CONV_DIM), f32), pltpu.VMEM((4, LANE, LANE), f32)],
        compiler_params=_params(("parallel", "arbitrary"), 32),
        name="ssd",
    )(xbc, z, dt, dtt, hist_arr, h0, cw, cb, bias_r, bias_c, alog_r, alog_c, dsk, ng, tri)


def _flash_body(q_ref, k_ref, vt_ref, km_ref, vtm_ref, o_ref, *, tq, tk, nh):
    qi = pl.program_id(2)
    qs = [q_ref[:, h * HT:(h + 1) * HT] for h in range(nh)]
    drow = lax.broadcasted_iota(jnp.int32, (2 * V_DIM, 1), 0)
    vmasks = (drow < V_DIM, drow >= V_DIM)
    krow = lax.broadcasted_iota(jnp.int32, (tk, tq), 0)
    qcol = lax.broadcasted_iota(jnp.int32, (tk, tq), 1)

    def heads(kall, vtall, carries, mask=None):
        sts = [_dot_nt(kall[:, h * HT:(h + 1) * HT], qs[h]) for h in range(nh)]
        out = []
        for h in range(nh):
            m, l, acct = carries[h]
            st = sts[h] if mask is None else jnp.where(mask, sts[h], -jnp.inf)
            m2 = jnp.maximum(m, jnp.max(st, axis=0, keepdims=True))
            pt = jnp.exp(st - m2)
            a = jnp.exp(m - m2)
            vtt = vtall[(h // 2) * LANE:(h // 2 + 1) * LANE, :]
            vz = jnp.where(vmasks[h % 2], vtt, jnp.zeros_like(vtt))
            out.append((m2, a * l + jnp.sum(pt, axis=0, keepdims=True), a * acct + _dot(vz, pt.astype(bf16))))
        return tuple(out)

    init = (jnp.full((1, tq), -jnp.inf, f32), jnp.zeros((1, tq), f32), jnp.zeros((2 * V_DIM, tq), f32))
    carries = heads(km_ref[...], vtm_ref[...], (init,) * nh)

    def body(j, carries):
        off = pl.multiple_of(j * tk, tk)
        return heads(k_ref[pl.ds(off, tk), :], vt_ref[:, pl.ds(off, tk)], carries)

    n_full = (qi * tq) // tk
    carries = lax.fori_loop(0, n_full, body, carries)
    off = pl.multiple_of(n_full * tk, tk)
    causal = krow <= qcol + (qi * tq - n_full * tk)
    res = heads(k_ref[pl.ds(off, tk), :], vt_ref[:, pl.ds(off, tk)], carries, causal)
    for pr in range(nh // 2):
        (_, l0, acc0), (_, l1, acc1) = res[2 * pr], res[2 * pr + 1]
        o_ref[:, pr * LANE:(pr + 1) * LANE] = (acc0 / l0 + acc1 / l1).T.astype(o_ref.dtype)


def _flash(q, k, vt, k_small, vt_meta, meta_block, n_batch, seq, tq, tk, nh):
    assert tk % tq == 0 and seq % tk == 0 and nh % 2 == 0 and MLA_HEADS % nh == 0
    nq = seq // tq
    return pl.pallas_call(
        functools.partial(_flash_body, tq=tq, tk=tk, nh=nh),
        grid=(n_batch, MLA_HEADS // nh, nq),
        in_specs=[pl.BlockSpec((tq, nh * HT), lambda b, p, i: (b * nq + i, p)),
                  pl.BlockSpec((seq, nh * HT), lambda b, p, i: (b, p)),
                  pl.BlockSpec((nh * V_DIM, seq), lambda b, p, i: (p, b)),
                  pl.BlockSpec((N_META, nh * HT), lambda b, p, i: (meta_block, p)),
                  pl.BlockSpec((nh * V_DIM, N_META), lambda b, p, i: (p, 0))],
        out_specs=pl.BlockSpec((tq, nh * V_DIM), lambda b, p, i: (b * nq + i, p)),
        out_shape=jax.ShapeDtypeStruct((n_batch * seq, MLA_WIDTH), f32),
        compiler_params=_params(("parallel", "parallel", "arbitrary"), 40),
        name="flash",
    )(q, k, vt, k_small, vt_meta)


def _finish_body(x_ref, ssd_ref, att_ref, gm_ref, wo_ref, gf_ref, wg_ref, wu_ref, wd_ref, o_ref):
    att = att_ref[...]
    mla = (att * _rms(att, MLA_WIDTH) * gm_ref[...]).astype(bf16)
    h = x_ref[...] + (_dot(ssd_ref[...], wo_ref[0:SSD_WIDTH, :]) + _dot(mla, wo_ref[SSD_WIDTH:, :]))
    n = (h * _rms(h, D_MODEL) * gf_ref[...]).astype(bf16)
    ff = jnp.zeros_like(h)
    for c in range(N_FF_CHUNKS):
        cs = slice(c * FF_CHUNK, (c + 1) * FF_CHUNK)
        a = (_silu(_dot(n, wg_ref[:, cs])) * _dot(n, wu_ref[:, cs])).astype(bf16)
        ff = ff + _dot(a, wd_ref[cs, :])
    o_ref[...] = h + ff


def _finish(x, ssd, att, gm, wo, gf, wg, wu, wd, tm):
    m = x.shape[0]
    row = lambda n: pl.BlockSpec((tm, n), lambda i: (i, 0))
    return pl.pallas_call(
        _finish_body,
        grid=(m // tm,),
        in_specs=[row(D_MODEL), row(SSD_WIDTH), row(MLA_WIDTH), _const_spec((1, MLA_WIDTH)),
                  _const_spec((D_MODEL, D_MODEL)), _const_spec((1, D_MODEL)),
                  _const_spec((D_MODEL, D_FF)), _const_spec((D_MODEL, D_FF)), _const_spec((D_FF, D_MODEL))],
        out_specs=row(D_MODEL),
        out_shape=jax.ShapeDtypeStruct((m, D_MODEL), f32),
        compiler_params=_params(("parallel",), 56),
        name="finish",
    )(x, ssd, att, gm, wo, gf, wg, wu, wd)


def _sssd_prep_body(xbc_ref, sc_ref, dt_ref, cw_ref, cb_ref, bias_r_ref, alog_r_ref, ind_ref,
                    xs_ref, xdt_ref, dec_ref, bc_ref):
    conv = cb_ref[...]
    for k in range(CONV_W - 1):
        conv = conv + sc_ref[k] * cw_ref[k:k + 1, :]
    conv = conv + xbc_ref[...] * cw_ref[CONV_W - 1:CONV_W, :]
    xa = _silu(conv)
    dt = _softplus(dt_ref[...] + bias_r_ref[...])
    dec = jnp.exp(dt * -jnp.exp(alog_r_ref[...]))
    ind = ind_ref[...]
    xs = xa[:, :SSD_WIDTH]
    xs_ref[...] = xs
    xdt_ref[...] = xs * jnp.dot(dt, ind, precision=_HI, preferred_element_type=f32)
    dec_ref[...] = jnp.dot(dec, ind, precision=_HI, preferred_element_type=f32)
    bc_ref[...] = xa[:, SSD_WIDTH:]


def _sssd_prep(xbc_small, sc, dt_small, cw, cb, bias_r, alog_r, ind, nseq):
    blk = lambda n: pl.BlockSpec((nseq, n), lambda i: (0, 0))
    widths = (SSD_WIDTH, SSD_WIDTH, SSD_WIDTH, 2 * LANE)
    return pl.pallas_call(
        _sssd_prep_body,
        grid=(1,),
        in_specs=[blk(CONV_DIM), _const_spec((CONV_W - 1, nseq, CONV_DIM)), blk(LANE), _const_spec((CONV_W, CONV_DIM)),
                  _const_spec((1, CONV_DIM)), _const_spec((1, LANE)), _const_spec((1, LANE)), _const_spec((LANE, SSD_WIDTH))],
        out_specs=[blk(n) for n in widths],
        out_shape=[jax.ShapeDtypeStruct((nseq, n), f32) for n in widths],
        compiler_params=_params(("arbitrary",), 32),
        name="sssd_prep",
    )(xbc_small, sc, dt_small, cw, cb, bias_r, alog_r, ind)


def _split_bf16(x):
    hi = x.astype(bf16).astype(f32)
    return hi, x - hi


def _sssd_step_body(h0_ref, xs_ref, xdt_ref, dec_ref, bc_ref, z_ref, dsk_ref, ng_ref, hn_ref, y_ref, *, nseq):
    hp = SSD_HEADS * SSD_HEAD_DIM
    row8 = lax.broadcasted_iota(jnp.int32, (SUBLANE, hp), 0)
    lane_hp = lax.broadcasted_iota(jnp.int32, (1, hp), 1)
    g0 = lane_hp < hp // SSD_GROUPS
    lane = lax.broadcasted_iota(jnp.int32, (1, LANE), 1)
    low = lane < D_STATE
    rrow = lax.broadcasted_iota(jnp.int32, (SUBLANE, 2 * LANE), 0)
    rlane = lax.broadcasted_iota(jnp.int32, (SUBLANE, 2 * LANE), 1)
    zeros8 = jnp.zeros((SUBLANE, hp), f32)
    for s in range(nseq):
        xdt = xdt_ref[s:s + 1, :]
        dec = dec_ref[s:s + 1, :]
        b = bc_ref[s:s + 1, 0:LANE]
        c = bc_ref[s:s + 1, LANE:]
        b8 = jnp.broadcast_to(b, (SUBLANE, LANE))
        c8 = jnp.broadcast_to(c, (SUBLANE, LANE))
        b_sw = pltpu.roll(b8, D_STATE, 1)
        c_sw = pltpu.roll(c8, D_STATE, 1)
        lm = jnp.where(row8 == 0, jnp.where(g0, xdt, 0.0),
                       jnp.where(row8 == 1, jnp.where(g0, 0.0, xdt), jnp.where(row8 == 2, dec, 0.0)))
        bsel = jnp.where(rrow[:, :LANE] == 0, b8, b_sw)
        rm = jnp.concatenate([jnp.where((rrow[:, :LANE] < 2) & (rlane[:, :LANE] < D_STATE), bsel, 0.0),
                              jnp.where((rrow[:, :LANE] == 2) & (rlane[:, :LANE] < D_STATE), 1.0, 0.0)], axis=1)
        lhi, llo = _split_bf16(lm)
        rhi, rlo = _split_bf16(rm)
        lst = jnp.concatenate([lhi, lhi, llo, zeros8], axis=0).astype(bf16)
        rst = jnp.concatenate([rhi, rlo, rhi, jnp.zeros_like(rhi)], axis=0).astype(bf16)
        outer = _dot_tn(lst, rst)
        h0 = h0_ref[s].reshape(hp, D_STATE)
        hn_ref[s] = (h0 * outer[:, LANE:LANE + D_STATE] + outer[:, 0:D_STATE]).reshape(SSD_HEADS, SSD_HEAD_DIM, D_STATE)
        csel = jnp.where(row8[:, :LANE] == 0, c8, jnp.where(row8[:, :LANE] == 1, c_sw, 0.0))[:, :D_STATE]
        ch = _dot_nt(csel.astype(bf16), h0.astype(bf16))
        chr_ = jnp.where(g0, ch[0:1, :], ch[1:2, :])
        bcp = b * c
        bc0 = jnp.sum(jnp.where(low, bcp, 0.0), axis=-1, keepdims=True)
        bc1 = jnp.sum(jnp.where(low, 0.0, bcp), axis=-1, keepdims=True)
        y_ref[s:s + 1, :] = dec * chr_ + xdt * jnp.where(g0, bc0, bc1) + dsk_ref[...] * xs_ref[s:s + 1, :]
    y = y_ref[...] * _silu(z_ref[...])
    gw = SSD_WIDTH // SSD_GROUPS
    outs = []
    for g in range(SSD_GROUPS):
        yg = y[:, g * gw:(g + 1) * gw]
        outs.append(yg * _rms(yg, gw))
    y_ref[...] = jnp.concatenate(outs, axis=1) * ng_ref[...]


def _sssd_step(h0, xs, xdt, dec, bc, z_small, dsk, ng, nseq_total, nseq):
    blk = lambda n: pl.BlockSpec((nseq, n), lambda i: (i, 0))
    hblk = pl.BlockSpec((nseq, SSD_HEADS, SSD_HEAD_DIM, D_STATE), lambda i: (i, 0, 0, 0))
    return pl.pallas_call(
        functools.partial(_sssd_step_body, nseq=nseq),
        grid=(nseq_total // nseq,),
        in_specs=[hblk, blk(SSD_WIDTH), blk(SSD_WIDTH), blk(SSD_WIDTH), blk(2 * LANE), blk(SSD_WIDTH),
                  _const_spec((1, SSD_WIDTH)), _const_spec((1, SSD_WIDTH))],
        out_specs=[hblk, blk(SSD_WIDTH)],
        out_shape=[jax.ShapeDtypeStruct((nseq_total, SSD_HEADS, SSD_HEAD_DIM, D_STATE), f32),
                   jax.ShapeDtypeStruct((nseq_total, SSD_WIDTH), f32)],
        compiler_params=_params(("parallel",), 32),
        name="sssd_step",
    )(h0, xs, xdt, dec, bc, z_small, dsk, ng)


def _absorb_body(q_ref, gk_ref, wabs_ref, qg_ref, a_ref):
    gk = gk_ref[...]
    for h in range(MLA_HEADS):
        sl = slice(h * HT, (h + 1) * HT)
        qg = q_ref[:, sl] * gk
        qg_ref[:, sl] = qg
        a_ref[:, h * KV_LORA:(h + 1) * KV_LORA] = jnp.dot(qg, wabs_ref[sl, :], precision=_HI, preferred_element_type=f32)
    a_ref[:, MLA_HEADS * KV_LORA:] = jnp.zeros((q_ref.shape[0], (16 - MLA_HEADS) * KV_LORA), f32)


def _absorb(q_small, gk, wabs, nseq):
    return pl.pallas_call(
        _absorb_body,
        grid=(1,),
        in_specs=[pl.BlockSpec((nseq, MLA_HEADS * HT), lambda i: (0, 0)), _const_spec((1, HT)),
                  _const_spec((MLA_HEADS * HT, KV_LORA))],
        out_specs=[pl.BlockSpec((nseq, MLA_HEADS * HT), lambda i: (0, 0)),
                   pl.BlockSpec((nseq, 16 * KV_LORA), lambda i: (0, 0))],
        out_shape=[jax.ShapeDtypeStruct((nseq, MLA_HEADS * HT), f32), jax.ShapeDtypeStruct((nseq, 16 * KV_LORA), f32)],
        compiler_params=_params(("arbitrary",), 32),
        name="absorb",
    )(q_small, gk, wabs)


def _decode_body(pt_ref, a_ref, c_ref, latn_ref, krnt_ref, wukt_ref, clat_ref, ckrt_ref, o_ref,
                 lat_buf, kr_buf, waug, latb, s_buf, sems, *, n_pages, page, ppt, unroll):
    b = pl.program_id(0)
    nb = pl.num_programs(0)
    slot = b % 2
    n_tiles = n_pages // ppt
    tile = ppt * page
    wrows = MLA_HEADS * QK_NOPE

    def start_page(seq, p, sl):
        pid = pt_ref[seq * n_pages + p]
        pltpu.make_async_copy(clat_ref.at[0, pid], lat_buf.at[sl, p], sems.at[0, sl]).start()
        pltpu.make_async_copy(ckrt_ref.at[0, pid], kr_buf.at[sl, p], sems.at[1, sl]).start()

    @pl.when(b == 0)
    def _():
        waug[0:wrows, :] = wukt_ref[...]

        def first(p, carry):
            start_page(0, p, 0)
            return carry

        lax.fori_loop(0, n_pages, first, 0)

    pltpu.make_async_copy(clat_ref.at[0, pl.ds(0, n_pages)], lat_buf.at[slot], sems.at[0, slot]).wait()
    pltpu.make_async_copy(ckrt_ref.at[0, pl.ds(0, n_pages)], kr_buf.at[slot], sems.at[1, slot]).wait()

    waug[wrows:wrows + 16, :] = a_ref[0, 0].astype(bf16)
    cb = c_ref[0].astype(bf16)

    def scores(lb, krt):
        n = lb.shape[0]
        kt = _dot_nt(waug[...], lb)
        k3 = kt[0:wrows, :].reshape(MLA_HEADS, QK_NOPE, n)
        n2 = jnp.sum(k3 * k3, axis=1)
        s1 = kt[wrows:wrows + MLA_HEADS, :]
        s2 = _dot(cb, krt.astype(bf16))
        kr2 = jnp.sum(krt * krt, axis=0, keepdims=True)
        return lax.rsqrt((n2 + kr2) * (1.0 / QK_DIM) + EPS) * (s1 + s2)

    def body(jj, carry):
        @pl.when(b + 1 < nb)
        def _():
            for pp in range(unroll * ppt):
                start_page(b + 1, jj * (unroll * ppt) + pp, 1 - slot)

        for u in range(unroll):
            j = jj * unroll + u
            lb = lat_buf[slot, pl.ds(j * ppt, ppt)].reshape(tile, KV_LORA).astype(bf16)
            latb[j] = lb
            kr_pages = kr_buf[slot, pl.ds(j * ppt, ppt)]
            s_buf[j] = scores(lb, jnp.concatenate([kr_pages[i] for i in range(ppt)], axis=-1))
        return carry

    lax.fori_loop(0, n_tiles // unroll, body, 0)

    first_row = lax.broadcasted_iota(jnp.int32, (page, KV_LORA), 0) == 0
    first_lane = lax.broadcasted_iota(jnp.int32, (1, page), 1) == 0
    lb_new = jnp.where(first_row, latn_ref[0], 0.0).astype(bf16)
    s_new = jnp.where(first_lane, scores(lb_new, jnp.where(first_lane, krnt_ref[0], 0.0)), -jnp.inf)

    s_all = jnp.concatenate([s_buf[j] for j in range(n_tiles)], axis=-1)
    m = jnp.maximum(jnp.max(s_all, axis=-1, keepdims=True), jnp.max(s_new, axis=-1, keepdims=True))
    p_all = jnp.exp(s_all - m)
    p_new = jnp.exp(s_new - m)
    l = jnp.sum(p_all, axis=-1, keepdims=True) + jnp.sum(p_new, axis=-1, keepdims=True)
    acc = _dot(p_all.astype(bf16), latb[...].reshape(n_tiles * tile, KV_LORA)) + _dot(p_new.astype(bf16), lb_new)
    o_ref[0] = acc / l


def _decode(page_table_flat, a, c, lat_new, kr_new_t, wukt, cache_lat, cache_kr_t, nseq, n_pages, page, ppt, unroll):
    wrows = MLA_HEADS * QK_NOPE
    grid_spec = pltpu.PrefetchScalarGridSpec(
        num_scalar_prefetch=1,
        grid=(nseq,),
        in_specs=[pl.BlockSpec((1, 1, 16, KV_LORA), lambda b, pt: (b, 0, 0, 0)),
                  pl.BlockSpec((1, MLA_HEADS, QK_ROPE), lambda b, pt: (b, 0, 0)),
                  pl.BlockSpec((1, 1, KV_LORA), lambda b, pt: (b, 0, 0)),
                  pl.BlockSpec((1, QK_ROPE, 1), lambda b, pt: (b, 0, 0)),
                  pl.BlockSpec((wrows, KV_LORA), lambda b, pt: (0, 0), pipeline_mode=pl.Buffered(1)),
                  pl.BlockSpec(memory_space=pl.ANY), pl.BlockSpec(memory_space=pl.ANY)],
        out_specs=pl.BlockSpec((1, MLA_HEADS, KV_LORA), lambda b, pt: (b, 0, 0)),
        scratch_shapes=[pltpu.VMEM((2, n_pages, page, KV_LORA), f32), pltpu.VMEM((2, n_pages, QK_ROPE, page), f32),
                        pltpu.VMEM((wrows + 16, KV_LORA), bf16),
                        pltpu.VMEM((n_pages // ppt, ppt * page, KV_LORA), bf16),
                        pltpu.VMEM((n_pages // ppt, MLA_HEADS, ppt * page), f32), pltpu.SemaphoreType.DMA((2, 2))],
    )
    return pl.pallas_call(
        functools.partial(_decode_body, n_pages=n_pages, page=page, ppt=ppt, unroll=unroll),
        grid_spec=grid_spec,
        out_shape=jax.ShapeDtypeStruct((nseq, MLA_HEADS, KV_LORA), f32),
        compiler_params=_params(("arbitrary",), 40),
        name="decode",
    )(page_table_flat, a, c, lat_new, kr_new_t, wukt, cache_lat, cache_kr_t)


def _uv_body(o_ref, wuv_ref, att_ref):
    for h in range(MLA_HEADS):
        oh = o_ref[:, h * KV_LORA:(h + 1) * KV_LORA].astype(bf16)
        att_ref[:, h * V_DIM:(h + 1) * V_DIM] = _dot(oh, wuv_ref[:, h * V_DIM:(h + 1) * V_DIM])


def _uv(o_flat, wuv):
    nseq = o_flat.shape[0]
    return pl.pallas_call(
        _uv_body,
        grid=(1,),
        in_specs=[pl.BlockSpec((nseq, MLA_HEADS * KV_LORA), lambda i: (0, 0)), _const_spec((KV_LORA, MLA_WIDTH))],
        out_specs=pl.BlockSpec((nseq, MLA_WIDTH), lambda i: (0, 0)),
        out_shape=jax.ShapeDtypeStruct((nseq, MLA_WIDTH), f32),
        compiler_params=_params(("arbitrary",), 32),
        name="uv",
    )(o_flat, wuv)


def _rope_tables(pos):
    inv = ROPE_THETA ** (-jnp.arange(ROPE_HALF, dtype=f32) * (2.0 / QK_ROPE))
    ang = pos.astype(f32)[:, None] * inv[None, :]
    cos, sin = jnp.cos(ang), jnp.sin(ang)
    n = pos.shape[0]
    one = jnp.ones((n, QK_NOPE), f32)
    z16 = jnp.zeros((n, ROPE_HALF), f32)
    z32 = jnp.zeros((n, HT - QK_DIM), f32)
    z64 = jnp.zeros((n, QK_NOPE), f32)
    tc = jnp.concatenate([one, cos, cos, z32], axis=1)
    ts1 = jnp.concatenate([z64, -sin, z16, z32], axis=1)
    ts2 = jnp.concatenate([z64, z16, sin, z32], axis=1)
    return tc, ts1, ts2


def _pad_lanes(x, n):
    return jnp.pad(x, ((0, 0), (0, n - x.shape[1])))


def kernel(x_prompt, x_sample, cache_kv_latent, cache_k_rope, state_conv, state_ssm, page_table, meta_tokens, attn_norm_g, w_in, conv_w, conv_b, dt_bias, a_log, d_skip, ssd_norm_g, q_norm_g, w_uq, kv_norm_g, w_ukv, q_head_g_nope, q_head_g_rope, k_head_g_nope, k_head_g_rope, mla_out_g, w_out, ffn_norm_g, w_gate, w_up, w_down):
    l = 0
    nb, seq, _ = x_prompt.shape
    nseq = x_sample.shape[0]
    n_pages, page = page_table.shape[1], cache_kv_latent.shape[2]
    past = n_pages * page
    n_small = 2 * LANE
    meta_lo = n_small - N_META

    s0, s1, s2, s3, s4 = (SSD_WIDTH, SSD_WIDTH + CONV_DIM, SSD_WIDTH + CONV_DIM + SSD_HEADS,
                          SSD_WIDTH + CONV_DIM + SSD_HEADS + Q_LORA, SSD_WIDTH + CONV_DIM + SSD_HEADS + Q_LORA + KV_LORA)
    wi = w_in[l]
    w_kr = jnp.pad(wi[:, s4:], ((0, 0), (ROPE_LO, HT - QK_DIM)))
    w_proj = jnp.concatenate([wi[:, :s1], wi[:, s2:s4], _pad_lanes(wi[:, s1:s2], LANE), w_kr], axis=1).astype(bf16)
    w_dtt = jnp.pad(wi[:, s1:s2].T, ((0, 16 - SSD_HEADS), (0, 0))).astype(bf16)
    g_attn = attn_norm_g[l][None, :]
    wuq = jnp.pad(w_uq[l].reshape(Q_LORA, MLA_HEADS, QK_DIM), ((0, 0), (0, 0), (0, HT - QK_DIM))).reshape(Q_LORA, -1).astype(bf16)
    wkv = w_ukv[l].reshape(KV_LORA, MLA_HEADS, QK_NOPE + V_DIM)
    wuk_f32 = jnp.pad(wkv[:, :, :QK_NOPE], ((0, 0), (0, 0), (0, HT - QK_NOPE))).reshape(KV_LORA, -1)
    wuk = wuk_f32.astype(bf16)
    wuv = wkv[:, :, QK_NOPE:].reshape(KV_LORA, MLA_WIDTH).astype(bf16)
    wukt = wkv[:, :, :QK_NOPE].reshape(KV_LORA, -1).T.astype(bf16)
    gq = _pad_lanes(jnp.concatenate([q_head_g_nope[l], q_head_g_rope[l], q_head_g_rope[l]])[None, :], HT) * ATTN_SCALE
    gk = _pad_lanes(jnp.concatenate([k_head_g_nope[l], k_head_g_rope[l], k_head_g_rope[l]])[None, :], HT)
    gqn, gkv = q_norm_g[l][None, :], kv_norm_g[l][None, :]
    cw, cb = conv_w[l], conv_b[l][None, :]
    bias_r = _pad_lanes(dt_bias[l][None, :], LANE)
    bias_c = jnp.pad(dt_bias[l][:, None], ((0, 16 - SSD_HEADS), (0, 0)))
    alog_r = _pad_lanes(a_log[l][None, :], LANE)
    alog_c = jnp.pad(a_log[l][:, None], ((0, 16 - SSD_HEADS), (0, 0)))
    dsk = jnp.repeat(d_skip[l], SSD_HEAD_DIM)[None, :]
    ng = ssd_norm_g[l][None, :]
    tri = jnp.tril(jnp.ones((SSD_CHUNK, SSD_CHUNK), f32))
    ind = (jnp.arange(LANE)[:, None] == jnp.arange(SSD_WIDTH)[None, :] // SSD_HEAD_DIM).astype(f32)
    ssd_consts = (cw, cb, bias_r, bias_c, alog_r, alog_c, dsk, ng, tri)
    gm, gf = mla_out_g[l][None, :], ffn_norm_g[l][None, :]
    wo = w_out[l].astype(bf16)
    wg, wu, wd = w_gate[l].astype(bf16), w_up[l].astype(bf16), w_down[l].astype(bf16)

    xp = x_prompt.reshape(nb * seq, D_MODEL)
    xs_rows = x_sample[:, 0, :]
    x_small = jnp.concatenate([xs_rows, jnp.zeros((meta_lo - nseq, D_MODEL), f32), meta_tokens.astype(f32)], axis=0)

    z_p, xbc_p, cq_p, ckv_p, dt_p, kr_p, dtt_p = _project(xp, g_attn, w_proj, w_dtt, 512)
    z_s, xbc_s, cq_s, ckv_s, dt_s, kr_s, dtt_s = _project(x_small, g_attn, w_proj, w_dtt, n_small)

    tabs_p = _rope_tables(N_META + jnp.arange(seq))
    pos_small = jnp.concatenate([jnp.full((nseq,), past), jnp.zeros((meta_lo - nseq,), jnp.int32), jnp.arange(N_META)])
    tabs_s = _rope_tables(pos_small)
    tm = 512
    lat_p, kro_p, q_p, k_p, vt_p = _mla_prep(cq_p, ckv_p, kr_p, tabs_p, seq // tm, gqn, gkv, wuq, wuk, wuv.T, gq, gk, tm, bf16)
    lat_s, kro_s, q_s, k_s, vt_s = _mla_prep(cq_s, ckv_s, kr_s, tabs_s, 1, gqn, gkv, wuq, wuk, wuv.T, gq, gk, n_small, f32)

    zero_h = jnp.zeros((1, 4, LANE, LANE), f32)
    zero_hist = jnp.zeros((SUBLANE, CONV_DIM), f32)
    _, h_meta = _ssd(xbc_s, z_s, dt_s, dtt_s, zero_hist, 0, zero_h, ssd_consts, 1, 1, 1, SSD_CHUNK - N_META)
    ssd_p, h_fin = _ssd(xbc_p, z_p, dt_p, dtt_p, xbc_s, n_small // SUBLANE - 1, h_meta, ssd_consts, nb, seq // SSD_CHUNK, 0, 0)

    att_p = _flash(q_p, k_p, vt_p, k_s, vt_s[:, meta_lo:], n_small // N_META - 1, nb, seq, 512, 512, 4)
    y_prompt = _finish(xp, ssd_p, att_p, gm, wo, gf, wg, wu, wd, 512).reshape(nb, seq, D_MODEL)

    sc = jnp.transpose(state_conv[l], (1, 0, 2))
    xs_s, xdt_s, dec_s, bc_s = _sssd_prep(xbc_s, sc, dt_s, cw, cb, bias_r, alog_r, ind, nseq)
    h_new, ssd_s = _sssd_step(state_ssm[l], xs_s, xdt_s, dec_s, bc_s, z_s, dsk, ng, nseq, 8)
    qg, a_abs = _absorb(q_s, gk, wuk_f32.T, nseq)
    a_abs = a_abs.reshape(nseq, 1, 16, KV_LORA)
    c_abs = qg.reshape(nseq, MLA_HEADS, HT)[:, :, ROPE_LO:QK_DIM]
    lat_new = lat_s[:nseq][:, None, :]
    kr_new = kro_s[:nseq, ROPE_LO:QK_DIM][:, None, :]
    o_lat = _decode(page_table.reshape(-1), a_abs, c_abs, lat_new, jnp.swapaxes(kr_new, 1, 2), wukt,
                    cache_kv_latent, jnp.swapaxes(cache_k_rope, 2, 3), nseq, n_pages, page, 4, 8)
    att_s = _uv(o_lat.reshape(nseq, MLA_HEADS * KV_LORA), wuv)
    y_sample = _finish(xs_rows, ssd_s.astype(bf16), att_s, gm, wo, gf, wg, wu, wd, nseq)[:, None, :]

    def with_meta(small, main, width):
        meta = jnp.broadcast_to(small[meta_lo:][None], (nb, N_META, width))
        return jnp.concatenate([meta, main.reshape(nb, seq, width)], axis=1)[None]

    kv_latent_prompt = with_meta(lat_s, lat_p, KV_LORA)
    k_rope_prompt = with_meta(kro_s[:, ROPE_LO:QK_DIM], kro_p[:, ROPE_LO:QK_DIM], QK_ROPE)
    conv_prompt = xbc_p.reshape(nb, seq, CONV_DIM)[:, seq - (CONV_W - 1):][None]
    hf = h_fin.reshape(nb, 4, 2, SSD_HEAD_DIM, SSD_GROUPS, D_STATE)
    ssm_prompt = jnp.stack([hf[:, i, :, :, i // 2, :] for i in range(4)], axis=1).reshape(nb, SSD_HEADS, SSD_HEAD_DIM, D_STATE)[None]
    kv_latent_sample = lat_new[None]
    k_rope_sample = kr_new[None]
    conv_sample = jnp.concatenate([state_conv[l][:, 1:], xbc_s[:nseq][:, None, :]], axis=1)[None]
    ssm_sample = h_new[None]
    return (y_prompt, y_sample, kv_latent_prompt, k_rope_prompt, conv_prompt, ssm_prompt.astype(x_prompt.dtype),
            kv_latent_sample, k_rope_sample, conv_sample, ssm_sample.astype(state_ssm.dtype))
```

```python
import functools

import jax
import jax.numpy as jnp
from jax import lax
from jax.experimental import pallas as pl
from jax.experimental.pallas import tpu as pltpu

f32 = jnp.float32
bf16 = jnp.bfloat16

D_MODEL = 1024
N_META = 16
SSD_HEADS = 8
SSD_HEAD_DIM = 64
SSD_WIDTH = SSD_HEADS * SSD_HEAD_DIM
SSD_GROUPS = 2
D_STATE = 64
CONV_W = 4
CONV_DIM = SSD_WIDTH + 2 * SSD_GROUPS * D_STATE
SSD_CHUNK = 128
MLA_HEADS = 8
QK_NOPE = 64
QK_ROPE = 32
QK_DIM = QK_NOPE + QK_ROPE
V_DIM = 64
MLA_WIDTH = MLA_HEADS * V_DIM
Q_LORA = 384
KV_LORA = 256
ROPE_THETA = 10000.0
ATTN_SCALE = QK_DIM ** -0.5
D_FF = 2816
EPS = 1e-6

LANE = 128
SUBLANE = 8
HT = LANE
ROPE_LO = QK_NOPE
ROPE_HALF = QK_ROPE // 2
FF_CHUNK = 256
N_FF_CHUNKS = D_FF // FF_CHUNK
MiB = 1024 * 1024

PC_Z = 0
PC_XBC = PC_Z + SSD_WIDTH
PC_CQ = PC_XBC + CONV_DIM
PC_CKV = PC_CQ + Q_LORA
PC_KR = PC_CKV + KV_LORA
PC_END = PC_KR + LANE

_NT = (((1,), (1,)), ((), ()))
_TN = (((0,), (0,)), ((), ()))
_HI = lax.Precision.HIGHEST


def _dot(a, b):
    return jnp.dot(a, b, preferred_element_type=f32)


def _dot_nt(a, b):
    return lax.dot_general(a, b, _NT, preferred_element_type=f32)


def _dot_tn(a, b):
    return lax.dot_general(a, b, _TN, preferred_element_type=f32)


def _rms(x, n):
    return lax.rsqrt(jnp.sum(x * x, axis=-1, keepdims=True) * (1.0 / n) + EPS)


def _silu(x):
    return x * jax.nn.sigmoid(x)


def _softplus(x):
    return jnp.maximum(x, 0.0) + jnp.log1p(jnp.exp(-jnp.abs(x)))


def _const_spec(shape):
    nd = len(shape)
    return pl.BlockSpec(shape, lambda *_: (0,) * nd, pipeline_mode=pl.Buffered(1))


def _params(sem, vmem_mib):
    return pltpu.CompilerParams(dimension_semantics=sem, vmem_limit_bytes=vmem_mib * MiB)


def _proj_body(x_ref, g_ref, w_ref, wdt_ref, z_ref, xbc_ref, cq_ref, ckv_ref, kr_ref, dtt_ref):
    x = x_ref[...]
    xn = (x * _rms(x, D_MODEL) * g_ref[...]).astype(bf16)
    z_ref[...] = _dot(xn, w_ref[:, PC_Z:PC_XBC])
    xbc_ref[...] = _dot(xn, w_ref[:, PC_XBC:PC_CQ])
    cq_ref[...] = _dot(xn, w_ref[:, PC_CQ:PC_CKV])
    ckv_ref[...] = _dot(xn, w_ref[:, PC_CKV:PC_KR])
    kr_ref[...] = _dot(xn, w_ref[:, PC_KR:PC_END])
    dtt_ref[...] = _dot_nt(wdt_ref[...], xn)


def _project(x, g, w, wdt, tm):
    m = x.shape[0]
    row = lambda n: pl.BlockSpec((tm, n), lambda i: (i, 0))
    widths = (SSD_WIDTH, CONV_DIM, Q_LORA, KV_LORA, LANE)
    return pl.pallas_call(
        _proj_body,
        grid=(m // tm,),
        in_specs=[row(D_MODEL), _const_spec((1, D_MODEL)), _const_spec((D_MODEL, PC_END)), _const_spec((16, D_MODEL))],
        out_specs=[row(n) for n in widths] + [pl.BlockSpec((16, tm), lambda i: (0, i))],
        out_shape=[jax.ShapeDtypeStruct((m, n), f32) for n in widths] + [jax.ShapeDtypeStruct((16, m), f32)],
        compiler_params=_params(("parallel",), 40),
        name="proj",
    )(x, g, w, wdt)


def _mla_prep_body(cq_ref, ckv_ref, kr_ref, tc_ref, ts1_ref, ts2_ref, gqn_ref, gkv_ref, wuq_ref, wuk_ref, wuv_ref,
                   gq_ref, gk_ref, lat_ref, kro_ref, q_ref, k_ref, vt_ref):
    tc, ts1, ts2 = tc_ref[...], ts1_ref[...], ts2_ref[...]

    def rope(x):
        return x * tc + pltpu.roll(x, LANE - ROPE_HALF, 1) * ts1 + pltpu.roll(x, ROPE_HALF, 1) * ts2

    ckv = ckv_ref[...]
    lat = ckv * _rms(ckv, KV_LORA) * gkv_ref[...]
    lat_ref[...] = lat
    kr = rope(kr_ref[...])
    kro_ref[...] = kr
    cq = cq_ref[...]
    cqn = (cq * _rms(cq, Q_LORA) * gqn_ref[...]).astype(bf16)
    latb = lat.astype(bf16)
    vt_ref[...] = _dot_nt(wuv_ref[...], latb).astype(vt_ref.dtype)
    gq, gk = gq_ref[...], gk_ref[...]
    for h in range(MLA_HEADS):
        sl = slice(h * HT, (h + 1) * HT)
        qh = rope(_dot(cqn, wuq_ref[:, sl]))
        q_ref[:, sl] = (qh * _rms(qh, QK_DIM) * gq).astype(q_ref.dtype)
        kh = _dot(latb, wuk_ref[:, sl]) + kr
        k_ref[:, sl] = (kh * _rms(kh, QK_DIM) * gk).astype(k_ref.dtype)


def _mla_prep(cq, ckv, kr, tabs, n_tab_blocks, gqn, gkv, wuq, wuk, wuvt, gq, gk, tm, q_dtype):
    m = cq.shape[0]
    row = lambda n: pl.BlockSpec((tm, n), lambda i: (i, 0))
    tab = pl.BlockSpec((tm, LANE), lambda i: (i % n_tab_blocks, 0))
    widths = (KV_LORA, LANE, MLA_HEADS * HT, MLA_HEADS * HT)
    dtypes = (f32, f32, q_dtype, bf16)
    return pl.pallas_call(
        _mla_prep_body,
        grid=(m // tm,),
        in_specs=[row(Q_LORA), row(KV_LORA), row(LANE), tab, tab, tab,
                  _const_spec((1, Q_LORA)), _const_spec((1, KV_LORA)),
                  _const_spec((Q_LORA, MLA_HEADS * HT)), _const_spec((KV_LORA, MLA_HEADS * HT)),
                  _const_spec((MLA_WIDTH, KV_LORA)), _const_spec((1, HT)), _const_spec((1, HT))],
        out_specs=[row(n) for n in widths] + [pl.BlockSpec((MLA_WIDTH, tm), lambda i: (0, i))],
        out_shape=[jax.ShapeDtypeStruct((m, n), d) for n, d in zip(widths, dtypes)]
                  + [jax.ShapeDtypeStruct((MLA_WIDTH, m), bf16)],
        compiler_params=_params(("parallel",), 40),
        name="mla_prep",
    )(cq, ckv, kr, *tabs, gqn, gkv, wuq, wuk, wuvt, gq, gk)


def _ssd_body(xbc_ref, z_ref, dtt_ref, hist_ref, h0_ref, cw_ref, cb_ref, bias_c_ref,
              alog_c_ref, dsk_ref, ng_ref, tri_ref, y_ref, hout_ref, xp_ref, hs_ref, *, valid_from):
    c = pl.program_id(1)
    q = SSD_CHUNK

    @pl.when(c == 0)
    def _():
        xp_ref[0:SUBLANE, :] = hist_ref[...]
        hs_ref[...] = h0_ref[0]

    xbc = xbc_ref[...]
    xp_ref[SUBLANE:SUBLANE + q, :] = xbc
    conv = cb_ref[...]
    for k in range(CONV_W):
        lo = SUBLANE - (CONV_W - 1) + k
        conv = conv + xp_ref[lo:lo + q, :] * cw_ref[k:k + 1, :]
    xp_ref[0:SUBLANE, :] = xbc[q - SUBLANE:q, :]
    xa = _silu(conv)
    xs = xa[:, :SSD_WIDTH]
    bm = xa[:, SSD_WIDTH:SSD_WIDTH + LANE]
    cm = xa[:, SSD_WIDTH + LANE:]

    rows = lax.broadcasted_iota(jnp.int32, (q, q), 0)
    cols = lax.broadcasted_iota(jnp.int32, (q, q), 1)
    lane = lax.broadcasted_iota(jnp.int32, (1, LANE), 1)
    low = lane < D_STATE

    dtr = _softplus(dtt_ref[...] + bias_c_ref[...])
    if valid_from:
        dtr = jnp.where(lax.broadcasted_iota(jnp.int32, (16, q), 1) >= valid_from, dtr, 0.0)
    dar = dtr * -jnp.exp(alog_c_ref[...])
    tri = tri_ref[...]
    dtc = dtr.T
    acs_r = lax.dot_general(dar, tri, _NT, precision=_HI, preferred_element_type=f32)
    acs_c = jnp.dot(tri, dar.T, precision=_HI, preferred_element_type=f32)
    w_c = jnp.exp(acs_c[q - 1:q, :] - acs_c) * dtc
    e_c = jnp.exp(acs_c)
    cd_r = jnp.exp(acs_r[:, q - 1:q])

    causal = cols <= rows
    bb = bm.astype(bf16)
    dsk = dsk_ref[...]
    ys = []
    for g in range(SSD_GROUPS):
        gmask = (lane >= g * D_STATE) & (lane < (g + 1) * D_STATE)
        cg = jnp.where(gmask, cm, 0.0).astype(bf16)
        cb_g = _dot_nt(cg, bb)
        for pi in range(2):
            i = 2 * g + pi
            xpair = xs[:, i * LANE:(i + 1) * LANE]
            ypair = jnp.zeros((q, LANE), f32)
            for hh in range(2):
                h = 2 * i + hh
                seg = acs_c[:, h:h + 1] - acs_r[h:h + 1, :]
                lmat = jnp.exp(jnp.where(causal, seg, -jnp.inf))
                sc = (cb_g * lmat * dtr[h:h + 1, :]).astype(bf16)
                xh = jnp.where(low if hh == 0 else jnp.logical_not(low), xpair, 0.0).astype(bf16)
                ypair = ypair + _dot(sc, xh)
            h0, h1 = 2 * i, 2 * i + 1
            wp = jnp.where(low, w_c[:, h0:h0 + 1], w_c[:, h1:h1 + 1])
            ep = jnp.where(low, e_c[:, h0:h0 + 1], e_c[:, h1:h1 + 1])
            hst = hs_ref[i]
            yoff = _dot_nt(cg, hst.astype(bf16)) * ep
            st = _dot_tn((xpair * wp).astype(bf16), bb)
            cdb = jnp.concatenate([jnp.broadcast_to(cd_r[h0:h0 + 1, :], (SSD_HEAD_DIM, LANE)),
                                   jnp.broadcast_to(cd_r[h1:h1 + 1, :], (SSD_HEAD_DIM, LANE))], axis=0)
            hs_ref[i] = hst * cdb + st
            ys.append(ypair + yoff + dsk[:, i * LANE:(i + 1) * LANE] * xpair)
    y = jnp.concatenate(ys, axis=1) * _silu(z_ref[...])
    gw = SSD_WIDTH // SSD_GROUPS
    outs = []
    for g in range(SSD_GROUPS):
        yg = y[:, g * gw:(g + 1) * gw]
        outs.append(yg * _rms(yg, gw))
    y_ref[...] = (jnp.concatenate(outs, axis=1) * ng_ref[...]).astype(y_ref.dtype)

    @pl.when(c == pl.num_programs(1) - 1)
    def _():
        hout_ref[0] = hs_ref[...]


def _ssd(xbc, z, dtt, hist_arr, hist_block, h0, consts, n_batch, n_chunks, row_block0, valid_from):
    q = SSD_CHUNK
    rb = lambda n: pl.BlockSpec((q, n), lambda b, c: (row_block0 + b * n_chunks + c, 0))
    cw, cb, bias_c, alog_c, dsk, ng, tri = consts
    return pl.pallas_call(
        functools.partial(_ssd_body, valid_from=valid_from),
        grid=(n_batch, n_chunks),
        in_specs=[rb(CONV_DIM), rb(SSD_WIDTH),
                  pl.BlockSpec((16, q), lambda b, c: (0, row_block0 + b * n_chunks + c)),
                  pl.BlockSpec((SUBLANE, CONV_DIM), lambda b, c: (hist_block, 0)),
                  _const_spec((1, 4, LANE, LANE)),
                  _const_spec((CONV_W, CONV_DIM)), _const_spec((1, CONV_DIM)),
                  _const_spec((16, 1)), _const_spec((16, 1)),
                  _const_spec((1, SSD_WIDTH)), _const_spec((1, SSD_WIDTH)), _const_spec((q, q))],
        out_specs=[pl.BlockSpec((q, SSD_WIDTH), lambda b, c: (b * n_chunks + c, 0)),
                   pl.BlockSpec((1, 4, LANE, LANE), lambda b, c: (b, 0, 0, 0))],
        out_shape=[jax.ShapeDtypeStruct((n_batch * n_chunks * q, SSD_WIDTH), bf16),
                   jax.ShapeDtypeStruct((n_batch, 4, LANE, LANE), f32)],
        scratch_shapes=[pltpu.VMEM((SUBLANE + q, CONV_DIM), f32), pltpu.VMEM((4, LANE, LANE), f32)],
        compiler_params=_params(("parallel", "arbitrary"), 32),
        name="ssd",
    )(xbc, z, dtt, hist_arr, h0, cw, cb, bias_c, alog_c, dsk, ng, tri)


def _flash_body(q_ref, k_ref, vt_ref, km_ref, vtm_ref, o_ref, *, tq, tk, nh):
    qi = pl.program_id(2)
    qs = [q_ref[:, h * HT:(h + 1) * HT] for h in range(nh)]
    drow = lax.broadcasted_iota(jnp.int32, (2 * V_DIM, 1), 0)
    vmasks = (drow < V_DIM, drow >= V_DIM)
    krow = lax.broadcasted_iota(jnp.int32, (tk, tq), 0)
    qcol = lax.broadcasted_iota(jnp.int32, (tk, tq), 1)

    def heads(kall, vtall, carries, mask=None):
        sts = [_dot_nt(kall[:, h * HT:(h + 1) * HT], qs[h]) for h in range(nh)]
        out = []
        for h in range(nh):
            m, l, acct = carries[h]
            st = sts[h] if mask is None else jnp.where(mask, sts[h], -jnp.inf)
            m2 = jnp.maximum(m, jnp.max(st, axis=0, keepdims=True))
            pt = jnp.exp(st - m2)
            a = jnp.exp(m - m2)
            vtt = vtall[(h // 2) * LANE:(h // 2 + 1) * LANE, :]
            vz = jnp.where(vmasks[h % 2], vtt, jnp.zeros_like(vtt))
            out.append((m2, a * l + jnp.sum(pt, axis=0, keepdims=True), a * acct + _dot(vz, pt.astype(bf16))))
        return tuple(out)

    init = (jnp.full((1, tq), -jnp.inf, f32), jnp.zeros((1, tq), f32), jnp.zeros((2 * V_DIM, tq), f32))
    carries = heads(km_ref[...], vtm_ref[...], (init,) * nh)

    def body(j, carries):
        off = pl.multiple_of(j * tk, tk)
        return heads(k_ref[pl.ds(off, tk), :], vt_ref[:, pl.ds(off, tk)], carries)

    n_full = (qi * tq) // tk
    carries = lax.fori_loop(0, n_full, body, carries)
    off = pl.multiple_of(n_full * tk, tk)
    causal = krow <= qcol + (qi * tq - n_full * tk)
    res = heads(k_ref[pl.ds(off, tk), :], vt_ref[:, pl.ds(off, tk)], carries, causal)
    for pr in range(nh // 2):
        (_, l0, acc0), (_, l1, acc1) = res[2 * pr], res[2 * pr + 1]
        o_ref[:, pr * LANE:(pr + 1) * LANE] = (acc0 / l0 + acc1 / l1).T.astype(o_ref.dtype)


def _flash(q, k, vt, k_small, vt_meta, meta_block, n_batch, seq, tq, tk, nh):
    assert tk % tq == 0 and seq % tk == 0 and nh % 2 == 0 and MLA_HEADS % nh == 0
    nq = seq // tq
    return pl.pallas_call(
        functools.partial(_flash_body, tq=tq, tk=tk, nh=nh),
        grid=(n_batch, MLA_HEADS // nh, nq),
        in_specs=[pl.BlockSpec((tq, nh * HT), lambda b, p, i: (b * nq + i, p)),
                  pl.BlockSpec((seq, nh * HT), lambda b, p, i: (b, p)),
                  pl.BlockSpec((nh * V_DIM, seq), lambda b, p, i: (p, b)),
                  pl.BlockSpec((N_META, nh * HT), lambda b, p, i: (meta_block, p)),
                  pl.BlockSpec((nh * V_DIM, N_META), lambda b, p, i: (p, 0))],
        out_specs=pl.BlockSpec((tq, nh * V_DIM), lambda b, p, i: (b * nq + i, p)),
        out_shape=jax.ShapeDtypeStruct((n_batch * seq, MLA_WIDTH), f32),
        compiler_params=_params(("parallel", "parallel", "arbitrary"), 40),
        name="flash",
    )(q, k, vt, k_small, vt_meta)


def _finish_body(x_ref, ssd_ref, att_ref, gm_ref, wo_ref, gf_ref, wg_ref, wu_ref, wd_ref, o_ref):
    att = att_ref[...]
    mla = (att * _rms(att, MLA_WIDTH) * gm_ref[...]).astype(bf16)
    h = x_ref[...] + (_dot(ssd_ref[...], wo_ref[0:SSD_WIDTH, :]) + _dot(mla, wo_ref[SSD_WIDTH:, :]))
    n = (h * _rms(h, D_MODEL) * gf_ref[...]).astype(bf16)
    ff = jnp.zeros_like(h)
    for c in range(N_FF_CHUNKS):
        cs = slice(c * FF_CHUNK, (c + 1) * FF_CHUNK)
        a = (_silu(_dot(n, wg_ref[:, cs])) * _dot(n, wu_ref[:, cs])).astype(bf16)
        ff = ff + _dot(a, wd_ref[cs, :])
    o_ref[...] = h + ff


def _finish(x, ssd, att, gm, wo, gf, wg, wu, wd, tm):
    m = x.shape[0]
    row = lambda n: pl.BlockSpec((tm, n), lambda i: (i, 0))
    return pl.pallas_call(
        _finish_body,
        grid=(m // tm,),
        in_specs=[row(D_MODEL), row(SSD_WIDTH), row(MLA_WIDTH), _const_spec((1, MLA_WIDTH)),
                  _const_spec((D_MODEL, D_MODEL)), _const_spec((1, D_MODEL)),
                  _const_spec((D_MODEL, D_FF)), _const_spec((D_MODEL, D_FF)), _const_spec((D_FF, D_MODEL))],
        out_specs=row(D_MODEL),
        out_shape=jax.ShapeDtypeStruct((m, D_MODEL), f32),
        compiler_params=_params(("parallel",), 56),
        name="finish",
    )(x, ssd, att, gm, wo, gf, wg, wu, wd)


def _sssd_prep_body(xbc_ref, sc_ref, cw_ref, cb_ref, xs_ref, xat_ref):
    conv = cb_ref[...]
    for k in range(CONV_W - 1):
        conv = conv + sc_ref[k] * cw_ref[k:k + 1, :]
    conv = conv + xbc_ref[...] * cw_ref[CONV_W - 1:CONV_W, :]
    xa = _silu(conv)
    xs_ref[...] = xa[:, :SSD_WIDTH]
    xat_ref[...] = xa.T


def _sssd_prep(xbc_small, sc, cw, cb, nseq):
    return pl.pallas_call(
        _sssd_prep_body,
        grid=(1,),
        in_specs=[pl.BlockSpec((nseq, CONV_DIM), lambda i: (0, 0)), _const_spec((CONV_W - 1, nseq, CONV_DIM)),
                  _const_spec((CONV_W, CONV_DIM)), _const_spec((1, CONV_DIM))],
        out_specs=[pl.BlockSpec((nseq, SSD_WIDTH), lambda i: (0, 0)), pl.BlockSpec((CONV_DIM, nseq), lambda i: (0, 0))],
        out_shape=[jax.ShapeDtypeStruct((nseq, SSD_WIDTH), f32), jax.ShapeDtypeStruct((CONV_DIM, nseq), f32)],
        compiler_params=_params(("arbitrary",), 32),
        name="sssd_prep",
    )(xbc_small, sc, cw, cb)


def _sssd_state_body(h0_ref, xst_ref, bt_ref, ct_ref, dtt_ref, bias_c_ref, alog_c_ref, hn_ref, yt_ref):
    h = pl.program_id(0)
    dt = _softplus(dtt_ref[pl.ds(h, 1), :] + bias_c_ref[pl.ds(h, 1), :])
    dec = jnp.exp(dt * -jnp.exp(alog_c_ref[pl.ds(h, 1), :]))
    bt, ct = bt_ref[...], ct_ref[...]

    def body(p, carry):
        xdt = xst_ref[pl.ds(p, 1), :] * dt
        hn = h0_ref[0, p] * dec + xdt * bt
        hn_ref[0, p] = hn
        yt_ref[pl.ds(p, 1), :] = jnp.sum(ct * hn, axis=0, keepdims=True)
        return carry

    lax.fori_loop(0, SSD_HEAD_DIM, body, 0, unroll=4)


def _sssd_state(h0t, xat, dtt_small, bias_c, alog_c, nseq):
    hpg = SSD_HEADS // SSD_GROUPS
    rows = lambda f: pl.BlockSpec((SSD_HEAD_DIM, nseq), f)
    hblk = pl.BlockSpec((1, SSD_HEAD_DIM, D_STATE, nseq), lambda h: (h, 0, 0, 0))
    return pl.pallas_call(
        _sssd_state_body,
        grid=(SSD_HEADS,),
        in_specs=[hblk, rows(lambda h: (h, 0)), rows(lambda h: (SSD_HEADS + h // hpg, 0)),
                  rows(lambda h: (SSD_HEADS + SSD_GROUPS + h // hpg, 0)),
                  pl.BlockSpec((16, nseq), lambda h: (0, 0)), _const_spec((16, 1)), _const_spec((16, 1))],
        out_specs=[hblk, rows(lambda h: (h, 0))],
        out_shape=[jax.ShapeDtypeStruct((SSD_HEADS, SSD_HEAD_DIM, D_STATE, nseq), f32),
                   jax.ShapeDtypeStruct((SSD_WIDTH, nseq), f32)],
        compiler_params=_params(("parallel",), 32),
        name="sssd_state",
    )(h0t, xat, xat, xat, dtt_small, bias_c, alog_c)


def _sssd_gate_body(yt_ref, xs_ref, z_ref, dsk_ref, ng_ref, o_ref):
    y = (yt_ref[...].T + dsk_ref[...] * xs_ref[...]) * _silu(z_ref[...])
    gw = SSD_WIDTH // SSD_GROUPS
    outs = []
    for g in range(SSD_GROUPS):
        yg = y[:, g * gw:(g + 1) * gw]
        outs.append(yg * _rms(yg, gw))
    o_ref[...] = (jnp.concatenate(outs, axis=1) * ng_ref[...]).astype(o_ref.dtype)


def _sssd_gate(yt, xs, z_small, dsk, ng, nseq):
    blk = pl.BlockSpec((nseq, SSD_WIDTH), lambda i: (0, 0))
    return pl.pallas_call(
        _sssd_gate_body,
        grid=(1,),
        in_specs=[pl.BlockSpec((SSD_WIDTH, nseq), lambda i: (0, 0)), blk, blk,
                  _const_spec((1, SSD_WIDTH)), _const_spec((1, SSD_WIDTH))],
        out_specs=blk,
        out_shape=jax.ShapeDtypeStruct((nseq, SSD_WIDTH), bf16),
        compiler_params=_params(("arbitrary",), 32),
        name="sssd_gate",
    )(yt, xs, z_small, dsk, ng)


def _absorb_body(q_ref, gk_ref, wabs_ref, qg_ref, a_ref):
    gk = gk_ref[...]
    for h in range(MLA_HEADS):
        sl = slice(h * HT, (h + 1) * HT)
        qg = q_ref[:, sl] * gk
        qg_ref[:, sl] = qg
        a_ref[:, h * KV_LORA:(h + 1) * KV_LORA] = jnp.dot(qg, wabs_ref[sl, :], precision=_HI, preferred_element_type=f32)
    a_ref[:, MLA_HEADS * KV_LORA:] = jnp.zeros((q_ref.shape[0], (16 - MLA_HEADS) * KV_LORA), f32)


def _absorb(q_small, gk, wabs, nseq):
    return pl.pallas_call(
        _absorb_body,
        grid=(1,),
        in_specs=[pl.BlockSpec((nseq, MLA_HEADS * HT), lambda i: (0, 0)), _const_spec((1, HT)),
                  _const_spec((MLA_HEADS * HT, KV_LORA))],
        out_specs=[pl.BlockSpec((nseq, MLA_HEADS * HT), lambda i: (0, 0)),
                   pl.BlockSpec((nseq, 16 * KV_LORA), lambda i: (0, 0))],
        out_shape=[jax.ShapeDtypeStruct((nseq, MLA_HEADS * HT), f32), jax.ShapeDtypeStruct((nseq, 16 * KV_LORA), f32)],
        compiler_params=_params(("arbitrary",), 32),
        name="absorb",
    )(q_small, gk, wabs)


def _decode_body(pt_ref, a_ref, c_ref, latn_ref, krnt_ref, wukt_ref, clat_ref, ckrt_ref, o_ref,
                 lat_buf, kr_buf, waug, latb, s_buf, sems, *, n_pages, page, ppt, unroll, skew):
    b = pl.program_id(0)
    nb = pl.num_programs(0)
    slot = b % 2
    n_tiles = n_pages // ppt
    tile = ppt * page
    wrows = MLA_HEADS * QK_NOPE

    def start_page(seq, p, sl):
        pid = pt_ref[seq * n_pages + p]
        pltpu.make_async_copy(clat_ref.at[0, pid], lat_buf.at[sl, p], sems.at[0, sl]).start()
        pltpu.make_async_copy(ckrt_ref.at[0, pid], kr_buf.at[sl, p], sems.at[1, sl]).start()

    @pl.when(b == 0)
    def _():
        waug[0:wrows, :] = wukt_ref[...]

        def first(p, carry):
            start_page(0, p, 0)
            return carry

        lax.fori_loop(0, n_pages, first, 0)

    pltpu.make_async_copy(clat_ref.at[0, pl.ds(0, n_pages)], lat_buf.at[slot], sems.at[0, slot]).wait()
    pltpu.make_async_copy(ckrt_ref.at[0, pl.ds(0, n_pages)], kr_buf.at[slot], sems.at[1, slot]).wait()

    waug[wrows:wrows + 16, :] = a_ref[0, 0].astype(bf16)
    cb = c_ref[0].astype(bf16)

    def score_dots(lb, krt):
        kt = _dot_nt(waug[...], lb)
        return kt, _dot(cb, krt.astype(bf16)), krt

    def score_finish(kt, s2, krt):
        n = kt.shape[1]
        k3 = kt[0:wrows, :].reshape(MLA_HEADS, QK_NOPE, n)
        n2 = jnp.sum(k3 * k3, axis=1)
        kr2 = jnp.sum(krt * krt, axis=0, keepdims=True)
        return lax.rsqrt((n2 + kr2) * (1.0 / QK_DIM) + EPS) * (kt[wrows:wrows + MLA_HEADS, :] + s2)

    def scores(lb, krt):
        return score_finish(*score_dots(lb, krt))

    def body(jj, carry):
        @pl.when(b + 1 < nb)
        def _():
            for pp in range(unroll * ppt):
                start_page(b + 1, jj * (unroll * ppt) + pp, 1 - slot)

        pending = []
        for u in range(unroll + skew):
            if u < unroll:
                j = jj * unroll + u
                lb = lat_buf[slot, pl.ds(j * ppt, ppt)].reshape(tile, KV_LORA).astype(bf16)
                latb[j] = lb
                kr_pages = kr_buf[slot, pl.ds(j * ppt, ppt)]
                pending.append((j, score_dots(lb, jnp.concatenate([kr_pages[i] for i in range(ppt)], axis=-1))))
            if u >= skew:
                j, dots = pending.pop(0)
                s_buf[j] = score_finish(*dots)
        return carry

    lax.fori_loop(0, n_tiles // unroll, body, 0)

    first_row = lax.broadcasted_iota(jnp.int32, (page, KV_LORA), 0) == 0
    first_lane = lax.broadcasted_iota(jnp.int32, (1, page), 1) == 0
    lb_new = jnp.where(first_row, latn_ref[0], 0.0).astype(bf16)
    s_new = jnp.where(first_lane, scores(lb_new, jnp.where(first_lane, krnt_ref[0], 0.0)), -jnp.inf)

    s_all = jnp.concatenate([s_buf[j] for j in range(n_tiles)], axis=-1)
    m = jnp.maximum(jnp.max(s_all, axis=-1, keepdims=True), jnp.max(s_new, axis=-1, keepdims=True))
    p_all = jnp.exp(s_all - m)
    p_new = jnp.exp(s_new - m)
    l = jnp.sum(p_all, axis=-1, keepdims=True) + jnp.sum(p_new, axis=-1, keepdims=True)
    acc = _dot(p_all.astype(bf16), latb[...].reshape(n_tiles * tile, KV_LORA)) + _dot(p_new.astype(bf16), lb_new)
    o_ref[0] = acc / l


def _decode(page_table_flat, a, c, lat_new, kr_new_t, wukt, cache_lat, cache_kr_t, nseq, n_pages, page, ppt, unroll, skew):
    wrows = MLA_HEADS * QK_NOPE
    grid_spec = pltpu.PrefetchScalarGridSpec(
        num_scalar_prefetch=1,
        grid=(nseq,),
        in_specs=[pl.BlockSpec((1, 1, 16, KV_LORA), lambda b, pt: (b, 0, 0, 0)),
                  pl.BlockSpec((1, MLA_HEADS, QK_ROPE), lambda b, pt: (b, 0, 0)),
                  pl.BlockSpec((1, 1, KV_LORA), lambda b, pt: (b, 0, 0)),
                  pl.BlockSpec((1, QK_ROPE, 1), lambda b, pt: (b, 0, 0)),
                  pl.BlockSpec((wrows, KV_LORA), lambda b, pt: (0, 0), pipeline_mode=pl.Buffered(1)),
                  pl.BlockSpec(memory_space=pl.ANY), pl.BlockSpec(memory_space=pl.ANY)],
        out_specs=pl.BlockSpec((1, MLA_HEADS, KV_LORA), lambda b, pt: (b, 0, 0)),
        scratch_shapes=[pltpu.VMEM((2, n_pages, page, KV_LORA), f32), pltpu.VMEM((2, n_pages, QK_ROPE, page), f32),
                        pltpu.VMEM((wrows + 16, KV_LORA), bf16),
                        pltpu.VMEM((n_pages // ppt, ppt * page, KV_LORA), bf16),
                        pltpu.VMEM((n_pages // ppt, MLA_HEADS, ppt * page), f32), pltpu.SemaphoreType.DMA((2, 2))],
    )
    return pl.pallas_call(
        functools.partial(_decode_body, n_pages=n_pages, page=page, ppt=ppt, unroll=unroll, skew=skew),
        grid_spec=grid_spec,
        out_shape=jax.ShapeDtypeStruct((nseq, MLA_HEADS, KV_LORA), f32),
        compiler_params=_params(("arbitrary",), 40),
        name="decode",
    )(page_table_flat, a, c, lat_new, kr_new_t, wukt, cache_lat, cache_kr_t)


def _uv_body(o_ref, wuv_ref, att_ref):
    for h in range(MLA_HEADS):
        oh = o_ref[:, h * KV_LORA:(h + 1) * KV_LORA].astype(bf16)
        att_ref[:, h * V_DIM:(h + 1) * V_DIM] = _dot(oh, wuv_ref[:, h * V_DIM:(h + 1) * V_DIM])


def _uv(o_flat, wuv):
    nseq = o_flat.shape[0]
    return pl.pallas_call(
        _uv_body,
        grid=(1,),
        in_specs=[pl.BlockSpec((nseq, MLA_HEADS * KV_LORA), lambda i: (0, 0)), _const_spec((KV_LORA, MLA_WIDTH))],
        out_specs=pl.BlockSpec((nseq, MLA_WIDTH), lambda i: (0, 0)),
        out_shape=jax.ShapeDtypeStruct((nseq, MLA_WIDTH), f32),
        compiler_params=_params(("arbitrary",), 32),
        name="uv",
    )(o_flat, wuv)


def _rope_tables(pos):
    inv = ROPE_THETA ** (-jnp.arange(ROPE_HALF, dtype=f32) * (2.0 / QK_ROPE))
    ang = pos.astype(f32)[:, None] * inv[None, :]
    cos, sin = jnp.cos(ang), jnp.sin(ang)
    n = pos.shape[0]
    one = jnp.ones((n, QK_NOPE), f32)
    z16 = jnp.zeros((n, ROPE_HALF), f32)
    z32 = jnp.zeros((n, HT - QK_DIM), f32)
    z64 = jnp.zeros((n, QK_NOPE), f32)
    tc = jnp.concatenate([one, cos, cos, z32], axis=1)
    ts1 = jnp.concatenate([z64, -sin, z16, z32], axis=1)
    ts2 = jnp.concatenate([z64, z16, sin, z32], axis=1)
    return tc, ts1, ts2


def _pad_lanes(x, n):
    return jnp.pad(x, ((0, 0), (0, n - x.shape[1])))


def kernel(x_prompt, x_sample, cache_kv_latent, cache_k_rope, state_conv, state_ssm, page_table, meta_tokens, attn_norm_g, w_in, conv_w, conv_b, dt_bias, a_log, d_skip, ssd_norm_g, q_norm_g, w_uq, kv_norm_g, w_ukv, q_head_g_nope, q_head_g_rope, k_head_g_nope, k_head_g_rope, mla_out_g, w_out, ffn_norm_g, w_gate, w_up, w_down):
    l = 0
    nb, seq, _ = x_prompt.shape
    nseq = x_sample.shape[0]
    n_pages, page = page_table.shape[1], cache_kv_latent.shape[2]
    past = n_pages * page
    n_small = 2 * LANE
    meta_lo = n_small - N_META

    s0, s1, s2, s3, s4 = (SSD_WIDTH, SSD_WIDTH + CONV_DIM, SSD_WIDTH + CONV_DIM + SSD_HEADS,
                          SSD_WIDTH + CONV_DIM + SSD_HEADS + Q_LORA, SSD_WIDTH + CONV_DIM + SSD_HEADS + Q_LORA + KV_LORA)
    wi = w_in[l]
    w_kr = jnp.pad(wi[:, s4:], ((0, 0), (ROPE_LO, HT - QK_DIM)))
    w_proj = jnp.concatenate([wi[:, :s1], wi[:, s2:s4], w_kr], axis=1).astype(bf16)
    w_dtt = jnp.pad(wi[:, s1:s2].T, ((0, 16 - SSD_HEADS), (0, 0))).astype(bf16)
    g_attn = attn_norm_g[l][None, :]
    wuq = jnp.pad(w_uq[l].reshape(Q_LORA, MLA_HEADS, QK_DIM), ((0, 0), (0, 0), (0, HT - QK_DIM))).reshape(Q_LORA, -1).astype(bf16)
    wkv = w_ukv[l].reshape(KV_LORA, MLA_HEADS, QK_NOPE + V_DIM)
    wuk_f32 = jnp.pad(wkv[:, :, :QK_NOPE], ((0, 0), (0, 0), (0, HT - QK_NOPE))).reshape(KV_LORA, -1)
    wuk = wuk_f32.astype(bf16)
    wuv = wkv[:, :, QK_NOPE:].reshape(KV_LORA, MLA_WIDTH).astype(bf16)
    wukt = wkv[:, :, :QK_NOPE].reshape(KV_LORA, -1).T.astype(bf16)
    gq = _pad_lanes(jnp.concatenate([q_head_g_nope[l], q_head_g_rope[l], q_head_g_rope[l]])[None, :], HT) * ATTN_SCALE
    gk = _pad_lanes(jnp.concatenate([k_head_g_nope[l], k_head_g_rope[l], k_head_g_rope[l]])[None, :], HT)
    gqn, gkv = q_norm_g[l][None, :], kv_norm_g[l][None, :]
    cw, cb = conv_w[l], conv_b[l][None, :]
    bias_c = jnp.pad(dt_bias[l][:, None], ((0, 16 - SSD_HEADS), (0, 0)))
    alog_c = jnp.pad(a_log[l][:, None], ((0, 16 - SSD_HEADS), (0, 0)))
    dsk = jnp.repeat(d_skip[l], SSD_HEAD_DIM)[None, :]
    ng = ssd_norm_g[l][None, :]
    tri = jnp.tril(jnp.ones((SSD_CHUNK, SSD_CHUNK), f32))
    ssd_consts = (cw, cb, bias_c, alog_c, dsk, ng, tri)
    gm, gf = mla_out_g[l][None, :], ffn_norm_g[l][None, :]
    wo = w_out[l].astype(bf16)
    wg, wu, wd = w_gate[l].astype(bf16), w_up[l].astype(bf16), w_down[l].astype(bf16)

    xp = x_prompt.reshape(nb * seq, D_MODEL)
    xs_rows = x_sample[:, 0, :]
    x_small = jnp.concatenate([xs_rows, jnp.zeros((meta_lo - nseq, D_MODEL), f32), meta_tokens.astype(f32)], axis=0)

    z_p, xbc_p, cq_p, ckv_p, kr_p, dtt_p = _project(xp, g_attn, w_proj, w_dtt, 512)
    z_s, xbc_s, cq_s, ckv_s, kr_s, dtt_s = _project(x_small, g_attn, w_proj, w_dtt, n_small)

    tabs_p = _rope_tables(N_META + jnp.arange(seq))
    pos_small = jnp.concatenate([jnp.full((nseq,), past), jnp.zeros((meta_lo - nseq,), jnp.int32), jnp.arange(N_META)])
    tabs_s = _rope_tables(pos_small)
    tm = 512
    lat_p, kro_p, q_p, k_p, vt_p = _mla_prep(cq_p, ckv_p, kr_p, tabs_p, seq // tm, gqn, gkv, wuq, wuk, wuv.T, gq, gk, tm, bf16)
    lat_s, kro_s, q_s, k_s, vt_s = _mla_prep(cq_s, ckv_s, kr_s, tabs_s, 1, gqn, gkv, wuq, wuk, wuv.T, gq, gk, n_small, f32)

    zero_h = jnp.zeros((1, 4, LANE, LANE), f32)
    zero_hist = jnp.zeros((SUBLANE, CONV_DIM), f32)
    _, h_meta = _ssd(xbc_s, z_s, dtt_s, zero_hist, 0, zero_h, ssd_consts, 1, 1, 1, SSD_CHUNK - N_META)
    ssd_p, h_fin = _ssd(xbc_p, z_p, dtt_p, xbc_s, n_small // SUBLANE - 1, h_meta, ssd_consts, nb, seq // SSD_CHUNK, 0, 0)

    att_p = _flash(q_p, k_p, vt_p, k_s, vt_s[:, meta_lo:], n_small // N_META - 1, nb, seq, 512, 512, 4)
    y_prompt = _finish(xp, ssd_p, att_p, gm, wo, gf, wg, wu, wd, 512).reshape(nb, seq, D_MODEL)

    sc = jnp.transpose(state_conv[l], (1, 0, 2))
    xs_s, xat_s = _sssd_prep(xbc_s, sc, cw, cb, nseq)
    h_new_t, yt_s = _sssd_state(jnp.transpose(state_ssm[l], (1, 2, 3, 0)), xat_s, dtt_s, bias_c, alog_c, nseq)
    h_new = jnp.transpose(h_new_t, (3, 0, 1, 2))
    ssd_s = _sssd_gate(yt_s, xs_s, z_s, dsk, ng, nseq)
    qg, a_abs = _absorb(q_s, gk, wuk_f32.T, nseq)
    a_abs = a_abs.reshape(nseq, 1, 16, KV_LORA)
    c_abs = qg.reshape(nseq, MLA_HEADS, HT)[:, :, ROPE_LO:QK_DIM]
    lat_new = lat_s[:nseq][:, None, :]
    kr_new = kro_s[:nseq, ROPE_LO:QK_DIM][:, None, :]
    o_lat = _decode(page_table.reshape(-1), a_abs, c_abs, lat_new, jnp.swapaxes(kr_new, 1, 2), wukt,
                    cache_kv_latent, jnp.swapaxes(cache_k_rope, 2, 3), nseq, n_pages, page, 4, 8, 0)
    att_s = _uv(o_lat.reshape(nseq, MLA_HEADS * KV_LORA), wuv)
    y_sample = _finish(xs_rows, ssd_s, att_s, gm, wo, gf, wg, wu, wd, nseq)[:, None, :]

    def with_meta(small, main, width):
        meta = jnp.broadcast_to(small[meta_lo:][None], (nb, N_META, width))
        return jnp.concatenate([meta, main.reshape(nb, seq, width)], axis=1)[None]

    kv_latent_prompt = with_meta(lat_s, lat_p, KV_LORA)
    k_rope_prompt = with_meta(kro_s[:, ROPE_LO:QK_DIM], kro_p[:, ROPE_LO:QK_DIM], QK_ROPE)
    conv_prompt = xbc_p.reshape(nb, seq, CONV_DIM)[:, seq - (CONV_W - 1):][None]
    hf = h_fin.reshape(nb, 4, 2, SSD_HEAD_DIM, SSD_GROUPS, D_STATE)
    ssm_prompt = jnp.stack([hf[:, i, :, :, i // 2, :] for i in range(4)], axis=1).reshape(nb, SSD_HEADS, SSD_HEAD_DIM, D_STATE)[None]
    kv_latent_sample = lat_new[None]
    k_rope_sample = kr_new[None]
    conv_sample = jnp.concatenate([state_conv[l][:, 1:], xbc_s[:nseq][:, None, :]], axis=1)[None]
    ssm_sample = h_new[None]
    return (y_prompt, y_sample, kv_latent_prompt, k_rope_prompt, conv_prompt, ssm_prompt.astype(x_prompt.dtype),
            kv_latent_sample, k_rope_sample, conv_sample, ssm_sample.astype(state_ssm.dtype))
```

```python
import functools

import jax
import jax.numpy as jnp
from jax import lax
from jax.experimental import pallas as pl
from jax.experimental.pallas import tpu as pltpu

f32 = jnp.float32
bf16 = jnp.bfloat16

D_MODEL = 1024
N_META = 16
SSD_HEADS = 8
SSD_HEAD_DIM = 64
SSD_WIDTH = SSD_HEADS * SSD_HEAD_DIM
SSD_GROUPS = 2
D_STATE = 64
CONV_W = 4
CONV_DIM = SSD_WIDTH + 2 * SSD_GROUPS * D_STATE
SSD_CHUNK = 128
MLA_HEADS = 8
QK_NOPE = 64
QK_ROPE = 32
QK_DIM = QK_NOPE + QK_ROPE
V_DIM = 64
MLA_WIDTH = MLA_HEADS * V_DIM
Q_LORA = 384
KV_LORA = 256
ROPE_THETA = 10000.0
ATTN_SCALE = QK_DIM ** -0.5
D_FF = 2816
EPS = 1e-6

LANE = 128
SUBLANE = 8
HT = LANE
ROPE_LO = QK_NOPE
ROPE_HALF = QK_ROPE // 2
FF_CHUNK = 256
N_FF_CHUNKS = D_FF // FF_CHUNK
MiB = 1024 * 1024

PC_Z = 0
PC_XBC = PC_Z + SSD_WIDTH
PC_CQ = PC_XBC + CONV_DIM
PC_CKV = PC_CQ + Q_LORA
PC_KR = PC_CKV + KV_LORA
PC_END = PC_KR + LANE

_NT = (((1,), (1,)), ((), ()))
_TN = (((0,), (0,)), ((), ()))
_HI = lax.Precision.HIGHEST


def _dot(a, b):
    return jnp.dot(a, b, preferred_element_type=f32)


def _dot_nt(a, b):
    return lax.dot_general(a, b, _NT, preferred_element_type=f32)


def _dot_tn(a, b):
    return lax.dot_general(a, b, _TN, preferred_element_type=f32)


def _rms(x, n):
    return lax.rsqrt(jnp.sum(x * x, axis=-1, keepdims=True) * (1.0 / n) + EPS)


def _silu(x):
    return x * jax.nn.sigmoid(x)


def _softplus(x):
    return jnp.maximum(x, 0.0) + jnp.log1p(jnp.exp(-jnp.abs(x)))


def _const_spec(shape):
    nd = len(shape)
    return pl.BlockSpec(shape, lambda *_: (0,) * nd, pipeline_mode=pl.Buffered(1))


def _params(sem, vmem_mib):
    return pltpu.CompilerParams(dimension_semantics=sem, vmem_limit_bytes=vmem_mib * MiB)


def _proj_body(x_ref, g_ref, w_ref, z_ref, xbc_ref, cq_ref, ckv_ref, kr_ref, dtt_ref):
    x = x_ref[...]
    xn = (x * _rms(x, D_MODEL) * g_ref[...]).astype(bf16)
    z_ref[...] = _dot(xn, w_ref[:, PC_Z:PC_XBC])
    xbc_ref[...] = _dot(xn, w_ref[:, PC_XBC:PC_CQ])
    cq_ref[...] = _dot(xn, w_ref[:, PC_CQ:PC_CKV])
    ckv_ref[...] = _dot(xn, w_ref[:, PC_CKV:PC_KR])
    last = _dot(xn, w_ref[:, PC_KR:PC_END])
    lane = lax.broadcasted_iota(jnp.int32, (1, LANE), 1)
    kr_ref[...] = jnp.where(lane >= ROPE_LO, last, 0.0)
    dtt_ref[...] = last.T[0:16, :]


def _project(x, g, w, tm):
    m = x.shape[0]
    row = lambda n: pl.BlockSpec((tm, n), lambda i: (i, 0))
    widths = (SSD_WIDTH, CONV_DIM, Q_LORA, KV_LORA, LANE)
    return pl.pallas_call(
        _proj_body,
        grid=(m // tm,),
        in_specs=[row(D_MODEL), _const_spec((1, D_MODEL)), _const_spec((D_MODEL, PC_END))],
        out_specs=[row(n) for n in widths] + [pl.BlockSpec((16, tm), lambda i: (0, i))],
        out_shape=[jax.ShapeDtypeStruct((m, n), f32) for n in widths] + [jax.ShapeDtypeStruct((16, m), f32)],
        compiler_params=_params(("parallel",), 40),
        name="proj",
    )(x, g, w)


def _mla_prep_body(cq_ref, ckv_ref, kr_ref, tc_ref, ts1_ref, ts2_ref, gqn_ref, gkv_ref, wuq_ref, wuk_ref, wuv_ref,
                   gq_ref, gk_ref, lat_ref, kro_ref, q_ref, k_ref, vt_ref):
    tc, ts1, ts2 = tc_ref[...], ts1_ref[...], ts2_ref[...]
    tsw = ts1 + ts2

    def rope(x):
        return x * tc + pltpu.roll(x, LANE - ROPE_HALF, 1) * ts1 + pltpu.roll(x, ROPE_HALF, 1) * ts2

    ckv = ckv_ref[...]
    lat = ckv * _rms(ckv, KV_LORA) * gkv_ref[...]
    lat_ref[...] = lat
    kr = rope(kr_ref[...])
    kro_ref[...] = kr
    cq = cq_ref[...]
    cqn = (cq * _rms(cq, Q_LORA) * gqn_ref[...]).astype(bf16)
    latb = lat.astype(bf16)
    vt_ref[...] = _dot_nt(wuv_ref[...], latb).astype(vt_ref.dtype)
    gq, gk = gq_ref[...], gk_ref[...]
    for h in range(MLA_HEADS):
        sl = slice(h * HT, (h + 1) * HT)
        qh = _dot(cqn, wuq_ref[:, sl]) * tc + _dot(cqn, wuq_ref[:, MLA_HEADS * HT + h * HT:MLA_HEADS * HT + (h + 1) * HT]) * tsw
        q_ref[:, sl] = (qh * _rms(qh, QK_DIM) * gq).astype(q_ref.dtype)
        kh = _dot(latb, wuk_ref[:, sl]) + kr
        k_ref[:, sl] = (kh * _rms(kh, QK_DIM) * gk).astype(k_ref.dtype)


def _mla_prep(cq, ckv, kr, tabs, n_tab_blocks, gqn, gkv, wuq, wuk, wuvt, gq, gk, tm, q_dtype):
    m = cq.shape[0]
    row = lambda n: pl.BlockSpec((tm, n), lambda i: (i, 0))
    tab = pl.BlockSpec((tm, LANE), lambda i: (i % n_tab_blocks, 0))
    widths = (KV_LORA, LANE, MLA_HEADS * HT, MLA_HEADS * HT)
    dtypes = (f32, f32, q_dtype, bf16)
    return pl.pallas_call(
        _mla_prep_body,
        grid=(m // tm,),
        in_specs=[row(Q_LORA), row(KV_LORA), row(LANE), tab, tab, tab,
                  _const_spec((1, Q_LORA)), _const_spec((1, KV_LORA)),
                  _const_spec((Q_LORA, 2 * MLA_HEADS * HT)), _const_spec((KV_LORA, MLA_HEADS * HT)),
                  _const_spec((MLA_WIDTH, KV_LORA)), _const_spec((1, HT)), _const_spec((1, HT))],
        out_specs=[row(n) for n in widths] + [pl.BlockSpec((MLA_WIDTH, tm), lambda i: (0, i))],
        out_shape=[jax.ShapeDtypeStruct((m, n), d) for n, d in zip(widths, dtypes)]
                  + [jax.ShapeDtypeStruct((MLA_WIDTH, m), bf16)],
        compiler_params=_params(("parallel",), 40),
        name="mla_prep",
    )(cq, ckv, kr, *tabs, gqn, gkv, wuq, wuk, wuvt, gq, gk)


def _ssd_body(xbc_ref, z_ref, dtt_ref, hist_ref, h0_ref, cw_ref, cb_ref, bias_c_ref,
              alog_c_ref, dsk_ref, ng_ref, tri_ref, y_ref, hout_ref, xp_ref, hs_ref, *, valid_from):
    c = pl.program_id(1)
    q = SSD_CHUNK

    @pl.when(c == 0)
    def _():
        xp_ref[0:SUBLANE, :] = hist_ref[...]
        hs_ref[...] = h0_ref[0]

    xbc = xbc_ref[...]
    xp_ref[SUBLANE:SUBLANE + q, :] = xbc
    conv = cb_ref[...]
    for k in range(CONV_W):
        lo = SUBLANE - (CONV_W - 1) + k
        conv = conv + xp_ref[lo:lo + q, :] * cw_ref[k:k + 1, :]
    xp_ref[0:SUBLANE, :] = xbc[q - SUBLANE:q, :]
    xa = _silu(conv)
    xs = xa[:, :SSD_WIDTH]
    bm = xa[:, SSD_WIDTH:SSD_WIDTH + LANE]
    cm = xa[:, SSD_WIDTH + LANE:]

    rows = lax.broadcasted_iota(jnp.int32, (q, q), 0)
    cols = lax.broadcasted_iota(jnp.int32, (q, q), 1)
    lane = lax.broadcasted_iota(jnp.int32, (1, LANE), 1)
    low = lane < D_STATE

    dtr = _softplus(dtt_ref[...] + bias_c_ref[...])
    if valid_from:
        dtr = jnp.where(lax.broadcasted_iota(jnp.int32, (16, q), 1) >= valid_from, dtr, 0.0)
    dar = dtr * -jnp.exp(alog_c_ref[...])
    tri = tri_ref[...]
    dtc = dtr.T
    acs_r = lax.dot_general(dar, tri, _NT, precision=_HI, preferred_element_type=f32)
    acs_c = jnp.dot(tri, dar.T, precision=_HI, preferred_element_type=f32)
    w_c = jnp.exp(acs_c[q - 1:q, :] - acs_c) * dtc
    e_c = jnp.exp(acs_c)
    cd_r = jnp.exp(acs_r[:, q - 1:q])

    causal = cols <= rows
    bb = bm.astype(bf16)
    dsk = dsk_ref[...]
    ys = []
    for g in range(SSD_GROUPS):
        gmask = (lane >= g * D_STATE) & (lane < (g + 1) * D_STATE)
        cg = jnp.where(gmask, cm, 0.0).astype(bf16)
        cb_g = _dot_nt(cg, bb)
        for pi in range(2):
            i = 2 * g + pi
            xpair = xs[:, i * LANE:(i + 1) * LANE]
            ypair = jnp.zeros((q, LANE), f32)
            for hh in range(2):
                h = 2 * i + hh
                seg = acs_c[:, h:h + 1] - acs_r[h:h + 1, :]
                lmat = jnp.exp(jnp.where(causal, seg, -jnp.inf))
                sc = (cb_g * lmat * dtr[h:h + 1, :]).astype(bf16)
                xh = jnp.where(low if hh == 0 else jnp.logical_not(low), xpair, 0.0).astype(bf16)
                ypair = ypair + _dot(sc, xh)
            h0, h1 = 2 * i, 2 * i + 1
            wp = jnp.where(low, w_c[:, h0:h0 + 1], w_c[:, h1:h1 + 1])
            ep = jnp.where(low, e_c[:, h0:h0 + 1], e_c[:, h1:h1 + 1])
            hst = hs_ref[i]
            yoff = _dot_nt(cg, hst.astype(bf16)) * ep
            st = _dot_tn((xpair * wp).astype(bf16), bb)
            cdb = jnp.concatenate([jnp.broadcast_to(cd_r[h0:h0 + 1, :], (SSD_HEAD_DIM, LANE)),
                                   jnp.broadcast_to(cd_r[h1:h1 + 1, :], (SSD_HEAD_DIM, LANE))], axis=0)
            hs_ref[i] = hst * cdb + st
            ys.append(ypair + yoff + dsk[:, i * LANE:(i + 1) * LANE] * xpair)
    y = jnp.concatenate(ys, axis=1) * _silu(z_ref[...])
    gw = SSD_WIDTH // SSD_GROUPS
    outs = []
    for g in range(SSD_GROUPS):
        yg = y[:, g * gw:(g + 1) * gw]
        outs.append(yg * _rms(yg, gw))
    y_ref[...] = (jnp.concatenate(outs, axis=1) * ng_ref[...]).astype(y_ref.dtype)

    @pl.when(c == pl.num_programs(1) - 1)
    def _():
        hout_ref[0] = hs_ref[...]


def _ssd(xbc, z, dtt, hist_arr, hist_block, h0, consts, n_batch, n_chunks, row_block0, valid_from):
    q = SSD_CHUNK
    rb = lambda n: pl.BlockSpec((q, n), lambda b, c: (row_block0 + b * n_chunks + c, 0))
    cw, cb, bias_c, alog_c, dsk, ng, tri = consts
    return pl.pallas_call(
        functools.partial(_ssd_body, valid_from=valid_from),
        grid=(n_batch, n_chunks),
        in_specs=[rb(CONV_DIM), rb(SSD_WIDTH),
                  pl.BlockSpec((16, q), lambda b, c: (0, row_block0 + b * n_chunks + c)),
                  pl.BlockSpec((SUBLANE, CONV_DIM), lambda b, c: (hist_block, 0)),
                  _const_spec((1, 4, LANE, LANE)),
                  _const_spec((CONV_W, CONV_DIM)), _const_spec((1, CONV_DIM)),
                  _const_spec((16, 1)), _const_spec((16, 1)),
                  _const_spec((1, SSD_WIDTH)), _const_spec((1, SSD_WIDTH)), _const_spec((q, q))],
        out_specs=[pl.BlockSpec((q, SSD_WIDTH), lambda b, c: (b * n_chunks + c, 0)),
                   pl.BlockSpec((1, 4, LANE, LANE), lambda b, c: (b, 0, 0, 0))],
        out_shape=[jax.ShapeDtypeStruct((n_batch * n_chunks * q, SSD_WIDTH), bf16),
                   jax.ShapeDtypeStruct((n_batch, 4, LANE, LANE), f32)],
        scratch_shapes=[pltpu.VMEM((SUBLANE + q, CONV_DIM), f32), pltpu.VMEM((4, LANE, LANE), f32)],
        compiler_params=_params(("parallel", "arbitrary"), 32),
        name="ssd",
    )(xbc, z, dtt, hist_arr, h0, cw, cb, bias_c, alog_c, dsk, ng, tri)


def _flash_body(q_ref, k_ref, vt_ref, km_ref, vtm_ref, o_ref, *, tq, tk, nh):
    qi = pl.program_id(2)
    qs = [q_ref[:, h * HT:(h + 1) * HT] for h in range(nh)]
    drow = lax.broadcasted_iota(jnp.int32, (2 * V_DIM, 1), 0)
    vmasks = (drow < V_DIM, drow >= V_DIM)
    krow = lax.broadcasted_iota(jnp.int32, (tk + N_META, tq), 0)
    qcol = lax.broadcasted_iota(jnp.int32, (tk + N_META, tq), 1)

    def heads(kall, vtall, carries, mask=None):
        sts = [_dot_nt(kall[:, h * HT:(h + 1) * HT], qs[h]) for h in range(nh)]
        out = []
        for h in range(nh):
            m, l, acct = carries[h]
            st = sts[h] if mask is None else jnp.where(mask, sts[h], -jnp.inf)
            m2 = jnp.maximum(m, jnp.max(st, axis=0, keepdims=True))
            pt = jnp.exp(st - m2)
            a = jnp.exp(m - m2)
            vtt = vtall[(h // 2) * LANE:(h // 2 + 1) * LANE, :]
            vz = jnp.where(vmasks[h % 2], vtt, jnp.zeros_like(vtt))
            out.append((m2, a * l + jnp.sum(pt, axis=0, keepdims=True), a * acct + _dot(vz, pt.astype(bf16))))
        return tuple(out)

    init = (jnp.full((1, tq), -jnp.inf, f32), jnp.zeros((1, tq), f32), jnp.zeros((2 * V_DIM, tq), f32))

    def body(j, carries):
        off = pl.multiple_of(j * tk, tk)
        return heads(k_ref[pl.ds(off, tk), :], vt_ref[:, pl.ds(off, tk)], carries)

    n_full = (qi * tq) // tk
    carries = lax.fori_loop(0, n_full, body, (init,) * nh)
    off = pl.multiple_of(n_full * tk, tk)
    k_last = jnp.concatenate([k_ref[pl.ds(off, tk), :], km_ref[...]], axis=0)
    vt_last = jnp.concatenate([vt_ref[:, pl.ds(off, tk)], vtm_ref[...]], axis=1)
    visible = (krow >= tk) | (krow <= qcol + (qi * tq - n_full * tk))
    res = heads(k_last, vt_last, carries, visible)
    for pr in range(nh // 2):
        (_, l0, acc0), (_, l1, acc1) = res[2 * pr], res[2 * pr + 1]
        o_ref[:, pr * LANE:(pr + 1) * LANE] = (acc0 / l0 + acc1 / l1).T.astype(o_ref.dtype)


def _flash(q, k, vt, k_small, vt_meta, meta_block, n_batch, seq, tq, tk, nh):
    assert tk % tq == 0 and seq % tk == 0 and nh % 2 == 0 and MLA_HEADS % nh == 0
    nq = seq // tq
    return pl.pallas_call(
        functools.partial(_flash_body, tq=tq, tk=tk, nh=nh),
        grid=(n_batch, MLA_HEADS // nh, nq),
        in_specs=[pl.BlockSpec((tq, nh * HT), lambda b, p, i: (b * nq + i, p)),
                  pl.BlockSpec((seq, nh * HT), lambda b, p, i: (b, p)),
                  pl.BlockSpec((nh * V_DIM, seq), lambda b, p, i: (p, b)),
                  pl.BlockSpec((N_META, nh * HT), lambda b, p, i: (meta_block, p)),
                  pl.BlockSpec((nh * V_DIM, N_META), lambda b, p, i: (p, 0))],
        out_specs=pl.BlockSpec((tq, nh * V_DIM), lambda b, p, i: (b * nq + i, p)),
        out_shape=jax.ShapeDtypeStruct((n_batch * seq, MLA_WIDTH), f32),
        compiler_params=_params(("parallel", "parallel", "arbitrary"), 40),
        name="flash",
    )(q, k, vt, k_small, vt_meta)


def _finish_body(x_ref, ssd_ref, att_ref, gm_ref, wo_ref, gf_ref, wg_ref, wu_ref, wd_ref, o_ref):
    att = att_ref[...]
    mla = (att * _rms(att, MLA_WIDTH) * gm_ref[...]).astype(bf16)
    h = x_ref[...] + (_dot(ssd_ref[...], wo_ref[0:SSD_WIDTH, :]) + _dot(mla, wo_ref[SSD_WIDTH:, :]))
    n = (h * _rms(h, D_MODEL) * gf_ref[...]).astype(bf16)
    ff = jnp.zeros_like(h)
    for c in range(N_FF_CHUNKS):
        cs = slice(c * FF_CHUNK, (c + 1) * FF_CHUNK)
        a = (_silu(_dot(n, wg_ref[:, cs])) * _dot(n, wu_ref[:, cs])).astype(bf16)
        ff = ff + _dot(a, wd_ref[cs, :])
    o_ref[...] = h + ff


def _finish(x, ssd, att, gm, wo, gf, wg, wu, wd, tm):
    m = x.shape[0]
    row = lambda n: pl.BlockSpec((tm, n), lambda i: (i, 0))
    return pl.pallas_call(
        _finish_body,
        grid=(m // tm,),
        in_specs=[row(D_MODEL), row(SSD_WIDTH), row(MLA_WIDTH), _const_spec((1, MLA_WIDTH)),
                  _const_spec((D_MODEL, D_MODEL)), _const_spec((1, D_MODEL)),
                  _const_spec((D_MODEL, D_FF)), _const_spec((D_MODEL, D_FF)), _const_spec((D_FF, D_MODEL))],
        out_specs=row(D_MODEL),
        out_shape=jax.ShapeDtypeStruct((m, D_MODEL), f32),
        compiler_params=_params(("parallel",), 56),
        name="finish",
    )(x, ssd, att, gm, wo, gf, wg, wu, wd)


def _sssd_prep_body(xbc_ref, sc_ref, cw_ref, cb_ref, xs_ref, xat_ref):
    conv = cb_ref[...]
    for k in range(CONV_W - 1):
        conv = conv + sc_ref[k] * cw_ref[k:k + 1, :]
    conv = conv + xbc_ref[...] * cw_ref[CONV_W - 1:CONV_W, :]
    xa = _silu(conv)
    xs_ref[...] = xa[:, :SSD_WIDTH]
    xat_ref[...] = xa.T


def _sssd_prep(xbc_small, sc, cw, cb, nseq):
    return pl.pallas_call(
        _sssd_prep_body,
        grid=(1,),
        in_specs=[pl.BlockSpec((nseq, CONV_DIM), lambda i: (0, 0)), _const_spec((CONV_W - 1, nseq, CONV_DIM)),
                  _const_spec((CONV_W, CONV_DIM)), _const_spec((1, CONV_DIM))],
        out_specs=[pl.BlockSpec((nseq, SSD_WIDTH), lambda i: (0, 0)), pl.BlockSpec((CONV_DIM, nseq), lambda i: (0, 0))],
        out_shape=[jax.ShapeDtypeStruct((nseq, SSD_WIDTH), f32), jax.ShapeDtypeStruct((CONV_DIM, nseq), f32)],
        compiler_params=_params(("arbitrary",), 32),
        name="sssd_prep",
    )(xbc_small, sc, cw, cb)


def _sssd_state_body(h0_ref, xst_ref, bt_ref, ct_ref, dtt_ref, bias_c_ref, alog_c_ref, hn_ref, yt_ref):
    h = pl.program_id(0)
    dt = _softplus(dtt_ref[pl.ds(h, 1), :] + bias_c_ref[pl.ds(h, 1), :])
    dec = jnp.exp(dt * -jnp.exp(alog_c_ref[pl.ds(h, 1), :]))
    bt, ct = bt_ref[...], ct_ref[...]

    def body(p, carry):
        xdt = xst_ref[pl.ds(p, 1), :] * dt
        hn = h0_ref[0, p] * dec + xdt * bt
        hn_ref[0, p] = hn
        yt_ref[pl.ds(p, 1), :] = jnp.sum(ct * hn, axis=0, keepdims=True)
        return carry

    lax.fori_loop(0, SSD_HEAD_DIM, body, 0, unroll=4)


def _sssd_state(h0t, xat, dtt_small, bias_c, alog_c, nseq):
    hpg = SSD_HEADS // SSD_GROUPS
    rows = lambda f: pl.BlockSpec((SSD_HEAD_DIM, nseq), f)
    hblk = pl.BlockSpec((1, SSD_HEAD_DIM, D_STATE, nseq), lambda h: (h, 0, 0, 0))
    return pl.pallas_call(
        _sssd_state_body,
        grid=(SSD_HEADS,),
        in_specs=[hblk, rows(lambda h: (h, 0)), rows(lambda h: (SSD_HEADS + h // hpg, 0)),
                  rows(lambda h: (SSD_HEADS + SSD_GROUPS + h // hpg, 0)),
                  pl.BlockSpec((16, nseq), lambda h: (0, 0)), _const_spec((16, 1)), _const_spec((16, 1))],
        out_specs=[hblk, rows(lambda h: (h, 0))],
        out_shape=[jax.ShapeDtypeStruct((SSD_HEADS, SSD_HEAD_DIM, D_STATE, nseq), f32),
                   jax.ShapeDtypeStruct((SSD_WIDTH, nseq), f32)],
        compiler_params=_params(("parallel",), 32),
        name="sssd_state",
    )(h0t, xat, xat, xat, dtt_small, bias_c, alog_c)


def _sssd_gate_body(yt_ref, xs_ref, z_ref, dsk_ref, ng_ref, o_ref):
    y = (yt_ref[...].T + dsk_ref[...] * xs_ref[...]) * _silu(z_ref[...])
    gw = SSD_WIDTH // SSD_GROUPS
    outs = []
    for g in range(SSD_GROUPS):
        yg = y[:, g * gw:(g + 1) * gw]
        outs.append(yg * _rms(yg, gw))
    o_ref[...] = (jnp.concatenate(outs, axis=1) * ng_ref[...]).astype(o_ref.dtype)


def _sssd_gate(yt, xs, z_small, dsk, ng, nseq):
    blk = pl.BlockSpec((nseq, SSD_WIDTH), lambda i: (0, 0))
    return pl.pallas_call(
        _sssd_gate_body,
        grid=(1,),
        in_specs=[pl.BlockSpec((SSD_WIDTH, nseq), lambda i: (0, 0)), blk, blk,
                  _const_spec((1, SSD_WIDTH)), _const_spec((1, SSD_WIDTH))],
        out_specs=blk,
        out_shape=jax.ShapeDtypeStruct((nseq, SSD_WIDTH), bf16),
        compiler_params=_params(("arbitrary",), 32),
        name="sssd_gate",
    )(yt, xs, z_small, dsk, ng)


def _absorb_body(q_ref, gk_ref, wabs_ref, qg_ref, a_ref):
    gk = gk_ref[...]
    for h in range(MLA_HEADS):
        sl = slice(h * HT, (h + 1) * HT)
        qg = q_ref[:, sl] * gk
        qg_ref[:, sl] = qg
        a_ref[:, h * KV_LORA:(h + 1) * KV_LORA] = jnp.dot(qg, wabs_ref[sl, :], precision=_HI, preferred_element_type=f32)
    a_ref[:, MLA_HEADS * KV_LORA:] = jnp.zeros((q_ref.shape[0], (16 - MLA_HEADS) * KV_LORA), f32)


def _absorb(q_small, gk, wabs, nseq):
    return pl.pallas_call(
        _absorb_body,
        grid=(1,),
        in_specs=[pl.BlockSpec((nseq, MLA_HEADS * HT), lambda i: (0, 0)), _const_spec((1, HT)),
                  _const_spec((MLA_HEADS * HT, KV_LORA))],
        out_specs=[pl.BlockSpec((nseq, MLA_HEADS * HT), lambda i: (0, 0)),
                   pl.BlockSpec((nseq, 16 * KV_LORA), lambda i: (0, 0))],
        out_shape=[jax.ShapeDtypeStruct((nseq, MLA_HEADS * HT), f32), jax.ShapeDtypeStruct((nseq, 16 * KV_LORA), f32)],
        compiler_params=_params(("arbitrary",), 32),
        name="absorb",
    )(q_small, gk, wabs)


def _decode_body(pt_ref, a_ref, c_ref, latn_ref, krnt_ref, wukt_ref, clat_ref, ckrt_ref, o_ref,
                 lat_buf, kr_buf, waug, latb, s_buf, sems, *, n_pages, page, ppt, unroll, skew):
    b = pl.program_id(0)
    nb = pl.num_programs(0)
    slot = b % 2
    n_tiles = n_pages // ppt
    tile = ppt * page
    wrows = MLA_HEADS * QK_NOPE

    def start_page(seq, p, sl):
        pid = pt_ref[seq * n_pages + p]
        pltpu.make_async_copy(clat_ref.at[0, pid], lat_buf.at[sl, p], sems.at[0, sl]).start()
        pltpu.make_async_copy(ckrt_ref.at[0, pid], kr_buf.at[sl, p], sems.at[1, sl]).start()

    @pl.when(b == 0)
    def _():
        waug[0:wrows, :] = wukt_ref[...]

        def first(p, carry):
            start_page(0, p, 0)
            return carry

        lax.fori_loop(0, n_pages, first, 0)

    pltpu.make_async_copy(clat_ref.at[0, pl.ds(0, n_pages)], lat_buf.at[slot], sems.at[0, slot]).wait()
    pltpu.make_async_copy(ckrt_ref.at[0, pl.ds(0, n_pages)], kr_buf.at[slot], sems.at[1, slot]).wait()

    waug[wrows:wrows + 16, :] = a_ref[0, 0].astype(bf16)
    cb = c_ref[0].astype(bf16)

    def score_dots(lb, krt):
        kt = _dot_nt(waug[...], lb)
        return kt, _dot(cb, krt.astype(bf16)), krt

    def score_finish(kt, s2, krt):
        n = kt.shape[1]
        k3 = kt[0:wrows, :].reshape(MLA_HEADS, QK_NOPE, n)
        n2 = jnp.sum(k3 * k3, axis=1)
        kr2 = jnp.sum(krt * krt, axis=0, keepdims=True)
        return lax.rsqrt((n2 + kr2) * (1.0 / QK_DIM) + EPS) * (kt[wrows:wrows + MLA_HEADS, :] + s2)

    def scores(lb, krt):
        return score_finish(*score_dots(lb, krt))

    def body(jj, carry):
        @pl.when(b + 1 < nb)
        def _():
            for pp in range(unroll * ppt):
                start_page(b + 1, jj * (unroll * ppt) + pp, 1 - slot)

        pending = []
        for u in range(unroll + skew):
            if u < unroll:
                j = jj * unroll + u
                lb = lat_buf[slot, pl.ds(j * ppt, ppt)].reshape(tile, KV_LORA).astype(bf16)
                latb[j] = lb
                kr_pages = kr_buf[slot, pl.ds(j * ppt, ppt)]
                pending.append((j, score_dots(lb, jnp.concatenate([kr_pages[i] for i in range(ppt)], axis=-1))))
            if u >= skew:
                j, dots = pending.pop(0)
                s_buf[j] = score_finish(*dots)
        return carry

    lax.fori_loop(0, n_tiles // unroll, body, 0)

    first_row = lax.broadcasted_iota(jnp.int32, (page, KV_LORA), 0) == 0
    first_lane = lax.broadcasted_iota(jnp.int32, (1, page), 1) == 0
    lb_new = jnp.where(first_row, latn_ref[0], 0.0).astype(bf16)
    s_new = jnp.where(first_lane, scores(lb_new, jnp.where(first_lane, krnt_ref[0], 0.0)), -jnp.inf)

    s_all = jnp.concatenate([s_buf[j] for j in range(n_tiles)], axis=-1)
    m = jnp.maximum(jnp.max(s_all, axis=-1, keepdims=True), jnp.max(s_new, axis=-1, keepdims=True))
    p_all = jnp.exp(s_all - m)
    p_new = jnp.exp(s_new - m)
    l = jnp.sum(p_all, axis=-1, keepdims=True) + jnp.sum(p_new, axis=-1, keepdims=True)
    acc = _dot(p_all.astype(bf16), latb[...].reshape(n_tiles * tile, KV_LORA)) + _dot(p_new.astype(bf16), lb_new)
    o_ref[0] = acc / l


def _decode(page_table_flat, a, c, lat_new, kr_new_t, wukt, cache_lat, cache_kr_t, nseq, n_pages, page, ppt, unroll, skew):
    wrows = MLA_HEADS * QK_NOPE
    grid_spec = pltpu.PrefetchScalarGridSpec(
        num_scalar_prefetch=1,
        grid=(nseq,),
        in_specs=[pl.BlockSpec((1, 1, 16, KV_LORA), lambda b, pt: (b, 0, 0, 0)),
                  pl.BlockSpec((1, MLA_HEADS, QK_ROPE), lambda b, pt: (b, 0, 0)),
                  pl.BlockSpec((1, 1, KV_LORA), lambda b, pt: (b, 0, 0)),
                  pl.BlockSpec((1, QK_ROPE, 1), lambda b, pt: (b, 0, 0)),
                  pl.BlockSpec((wrows, KV_LORA), lambda b, pt: (0, 0), pipeline_mode=pl.Buffered(1)),
                  pl.BlockSpec(memory_space=pl.ANY), pl.BlockSpec(memory_space=pl.ANY)],
        out_specs=pl.BlockSpec((1, MLA_HEADS, KV_LORA), lambda b, pt: (b, 0, 0)),
        scratch_shapes=[pltpu.VMEM((2, n_pages, page, KV_LORA), f32), pltpu.VMEM((2, n_pages, QK_ROPE, page), f32),
                        pltpu.VMEM((wrows + 16, KV_LORA), bf16),
                        pltpu.VMEM((n_pages // ppt, ppt * page, KV_LORA), bf16),
                        pltpu.VMEM((n_pages // ppt, MLA_HEADS, ppt * page), f32), pltpu.SemaphoreType.DMA((2, 2))],
    )
    return pl.pallas_call(
        functools.partial(_decode_body, n_pages=n_pages, page=page, ppt=ppt, unroll=unroll, skew=skew),
        grid_spec=grid_spec,
        out_shape=jax.ShapeDtypeStruct((nseq, MLA_HEADS, KV_LORA), f32),
        compiler_params=_params(("arbitrary",), 40),
        name="decode",
    )(page_table_flat, a, c, lat_new, kr_new_t, wukt, cache_lat, cache_kr_t)


def _uv_body(o_ref, wuv_ref, att_ref):
    for h in range(MLA_HEADS):
        oh = o_ref[:, h * KV_LORA:(h + 1) * KV_LORA].astype(bf16)
        att_ref[:, h * V_DIM:(h + 1) * V_DIM] = _dot(oh, wuv_ref[:, h * V_DIM:(h + 1) * V_DIM])


def _uv(o_flat, wuv):
    nseq = o_flat.shape[0]
    return pl.pallas_call(
        _uv_body,
        grid=(1,),
        in_specs=[pl.BlockSpec((nseq, MLA_HEADS * KV_LORA), lambda i: (0, 0)), _const_spec((KV_LORA, MLA_WIDTH))],
        out_specs=pl.BlockSpec((nseq, MLA_WIDTH), lambda i: (0, 0)),
        out_shape=jax.ShapeDtypeStruct((nseq, MLA_WIDTH), f32),
        compiler_params=_params(("arbitrary",), 32),
        name="uv",
    )(o_flat, wuv)


def _rope_tables(pos):
    inv = ROPE_THETA ** (-jnp.arange(ROPE_HALF, dtype=f32) * (2.0 / QK_ROPE))
    ang = pos.astype(f32)[:, None] * inv[None, :]
    cos, sin = jnp.cos(ang), jnp.sin(ang)
    n = pos.shape[0]
    one = jnp.ones((n, QK_NOPE), f32)
    z16 = jnp.zeros((n, ROPE_HALF), f32)
    z32 = jnp.zeros((n, HT - QK_DIM), f32)
    z64 = jnp.zeros((n, QK_NOPE), f32)
    tc = jnp.concatenate([one, cos, cos, z32], axis=1)
    ts1 = jnp.concatenate([z64, -sin, z16, z32], axis=1)
    ts2 = jnp.concatenate([z64, z16, sin, z32], axis=1)
    return tc, ts1, ts2


def _pad_lanes(x, n):
    return jnp.pad(x, ((0, 0), (0, n - x.shape[1])))


def kernel(x_prompt, x_sample, cache_kv_latent, cache_k_rope, state_conv, state_ssm, page_table, meta_tokens, attn_norm_g, w_in, conv_w, conv_b, dt_bias, a_log, d_skip, ssd_norm_g, q_norm_g, w_uq, kv_norm_g, w_ukv, q_head_g_nope, q_head_g_rope, k_head_g_nope, k_head_g_rope, mla_out_g, w_out, ffn_norm_g, w_gate, w_up, w_down):
    l = 0
    nb, seq, _ = x_prompt.shape
    nseq = x_sample.shape[0]
    n_pages, page = page_table.shape[1], cache_kv_latent.shape[2]
    past = n_pages * page
    n_small = 2 * LANE
    meta_lo = n_small - N_META

    s0, s1, s2, s3, s4 = (SSD_WIDTH, SSD_WIDTH + CONV_DIM, SSD_WIDTH + CONV_DIM + SSD_HEADS,
                          SSD_WIDTH + CONV_DIM + SSD_HEADS + Q_LORA, SSD_WIDTH + CONV_DIM + SSD_HEADS + Q_LORA + KV_LORA)
    wi = w_in[l]
    w_last = jnp.concatenate([_pad_lanes(wi[:, s1:s2], ROPE_LO), _pad_lanes(wi[:, s4:], HT - ROPE_LO)], axis=1)
    w_proj = jnp.concatenate([wi[:, :s1], wi[:, s2:s4], w_last], axis=1).astype(bf16)
    g_attn = attn_norm_g[l][None, :]
    wq3 = w_uq[l].reshape(Q_LORA, MLA_HEADS, QK_DIM)
    zq = jnp.zeros((Q_LORA, MLA_HEADS, QK_NOPE), f32)
    wq_partner = jnp.concatenate([zq, wq3[:, :, QK_NOPE + ROPE_HALF:], wq3[:, :, QK_NOPE:QK_NOPE + ROPE_HALF]], axis=2)
    pad_head = lambda w: jnp.pad(w, ((0, 0), (0, 0), (0, HT - QK_DIM))).reshape(Q_LORA, -1)
    wuq = jnp.concatenate([pad_head(wq3), pad_head(wq_partner)], axis=1).astype(bf16)
    wkv = w_ukv[l].reshape(KV_LORA, MLA_HEADS, QK_NOPE + V_DIM)
    wuk_f32 = jnp.pad(wkv[:, :, :QK_NOPE], ((0, 0), (0, 0), (0, HT - QK_NOPE))).reshape(KV_LORA, -1)
    wuk = wuk_f32.astype(bf16)
    wuv = wkv[:, :, QK_NOPE:].reshape(KV_LORA, MLA_WIDTH).astype(bf16)
    wukt = wkv[:, :, :QK_NOPE].reshape(KV_LORA, -1).T.astype(bf16)
    gq = _pad_lanes(jnp.concatenate([q_head_g_nope[l], q_head_g_rope[l], q_head_g_rope[l]])[None, :], HT) * ATTN_SCALE
    gk = _pad_lanes(jnp.concatenate([k_head_g_nope[l], k_head_g_rope[l], k_head_g_rope[l]])[None, :], HT)
    gqn, gkv = q_norm_g[l][None, :], kv_norm_g[l][None, :]
    cw, cb = conv_w[l], conv_b[l][None, :]
    bias_c = jnp.pad(dt_bias[l][:, None], ((0, 16 - SSD_HEADS), (0, 0)))
    alog_c = jnp.pad(a_log[l][:, None], ((0, 16 - SSD_HEADS), (0, 0)))
    dsk = jnp.repeat(d_skip[l], SSD_HEAD_DIM)[None, :]
    ng = ssd_norm_g[l][None, :]
    tri = jnp.tril(jnp.ones((SSD_CHUNK, SSD_CHUNK), f32))
    ssd_consts = (cw, cb, bias_c, alog_c, dsk, ng, tri)
    gm, gf = mla_out_g[l][None, :], ffn_norm_g[l][None, :]
    wo = w_out[l].astype(bf16)
    wg, wu, wd = w_gate[l].astype(bf16), w_up[l].astype(bf16), w_down[l].astype(bf16)

    xp = x_prompt.reshape(nb * seq, D_MODEL)
    xs_rows = x_sample[:, 0, :]
    x_small = jnp.concatenate([xs_rows, jnp.zeros((meta_lo - nseq, D_MODEL), f32), meta_tokens.astype(f32)], axis=0)

    z_p, xbc_p, cq_p, ckv_p, kr_p, dtt_p = _project(xp, g_attn, w_proj, 512)
    z_s, xbc_s, cq_s, ckv_s, kr_s, dtt_s = _project(x_small, g_attn, w_proj, n_small)

    tabs_p = _rope_tables(N_META + jnp.arange(seq))
    pos_small = jnp.concatenate([jnp.full((nseq,), past), jnp.zeros((meta_lo - nseq,), jnp.int32), jnp.arange(N_META)])
    tabs_s = _rope_tables(pos_small)
    tm = 512
    lat_p, kro_p, q_p, k_p, vt_p = _mla_prep(cq_p, ckv_p, kr_p, tabs_p, seq // tm, gqn, gkv, wuq, wuk, wuv.T, gq, gk, tm, bf16)
    lat_s, kro_s, q_s, k_s, vt_s = _mla_prep(cq_s, ckv_s, kr_s, tabs_s, 1, gqn, gkv, wuq, wuk, wuv.T, gq, gk, n_small, f32)

    zero_h = jnp.zeros((1, 4, LANE, LANE), f32)
    zero_hist = jnp.zeros((SUBLANE, CONV_DIM), f32)
    _, h_meta = _ssd(xbc_s, z_s, dtt_s, zero_hist, 0, zero_h, ssd_consts, 1, 1, 1, SSD_CHUNK - N_META)
    ssd_p, h_fin = _ssd(xbc_p, z_p, dtt_p, xbc_s, n_small // SUBLANE - 1, h_meta, ssd_consts, nb, seq // SSD_CHUNK, 0, 0)

    att_p = _flash(q_p, k_p, vt_p, k_s, vt_s[:, meta_lo:], n_small // N_META - 1, nb, seq, 512, 512, 4)
    y_prompt = _finish(xp, ssd_p, att_p, gm, wo, gf, wg, wu, wd, 512).reshape(nb, seq, D_MODEL)

    sc = jnp.transpose(state_conv[l], (1, 0, 2))
    xs_s, xat_s = _sssd_prep(xbc_s, sc, cw, cb, nseq)
    h_new_t, yt_s = _sssd_state(jnp.transpose(state_ssm[l], (1, 2, 3, 0)), xat_s, dtt_s, bias_c, alog_c, nseq)
    h_new = jnp.transpose(h_new_t, (3, 0, 1, 2))
    ssd_s = _sssd_gate(yt_s, xs_s, z_s, dsk, ng, nseq)
    qg, a_abs = _absorb(q_s, gk, wuk_f32.T, nseq)
    a_abs = a_abs.reshape(nseq, 1, 16, KV_LORA)
    c_abs = qg.reshape(nseq, MLA_HEADS, HT)[:, :, ROPE_LO:QK_DIM]
    lat_new = lat_s[:nseq][:, None, :]
    kr_new = kro_s[:nseq, ROPE_LO:QK_DIM][:, None, :]
    o_lat = _decode(page_table.reshape(-1), a_abs, c_abs, lat_new, jnp.swapaxes(kr_new, 1, 2), wukt,
                    cache_kv_latent, jnp.swapaxes(cache_k_rope, 2, 3), nseq, n_pages, page, 4, 8, 0)
    att_s = _uv(o_lat.reshape(nseq, MLA_HEADS * KV_LORA), wuv)
    y_sample = _finish(xs_rows, ssd_s, att_s, gm, wo, gf, wg, wu, wd, nseq)[:, None, :]

    def with_meta(small, main, width):
        meta = jnp.broadcast_to(small[meta_lo:][None], (nb, N_META, width))
        return jnp.concatenate([meta, main.reshape(nb, seq, width)], axis=1)[None]

    kv_latent_prompt = with_meta(lat_s, lat_p, KV_LORA)
    k_rope_prompt = with_meta(kro_s[:, ROPE_LO:QK_DIM], kro_p[:, ROPE_LO:QK_DIM], QK_ROPE)
    conv_prompt = xbc_p.reshape(nb, seq, CONV_DIM)[:, seq - (CONV_W - 1):][None]
    hf = h_fin.reshape(nb, 4, 2, SSD_HEAD_DIM, SSD_GROUPS, D_STATE)
    ssm_prompt = jnp.stack([hf[:, i, :, :, i // 2, :] for i in range(4)], axis=1).reshape(nb, SSD_HEADS, SSD_HEAD_DIM, D_STATE)[None]
    kv_latent_sample = lat_new[None]
    k_rope_sample = kr_new[None]
    conv_sample = jnp.concatenate([state_conv[l][:, 1:], xbc_s[:nseq][:, None, :]], axis=1)[None]
    ssm_sample = h_new[None]
    return (y_prompt, y_sample, kv_latent_prompt, k_rope_prompt, conv_prompt, ssm_prompt.astype(x_prompt.dtype),
            kv_latent_sample, k_rope_sample, conv_sample, ssm_sample.astype(state_ssm.dtype))
```

```python
import functools

import jax
import jax.numpy as jnp
from jax import lax
from jax.experimental import pallas as pl
from jax.experimental.pallas import tpu as pltpu

f32 = jnp.float32
bf16 = jnp.bfloat16

D_MODEL = 1024
N_META = 16
SSD_HEADS = 8
SSD_HEAD_DIM = 64
SSD_WIDTH = SSD_HEADS * SSD_HEAD_DIM
SSD_GROUPS = 2
D_STATE = 64
CONV_W = 4
CONV_DIM = SSD_WIDTH + 2 * SSD_GROUPS * D_STATE
SSD_CHUNK = 128
MLA_HEADS = 8
QK_NOPE = 64
QK_ROPE = 32
QK_DIM = QK_NOPE + QK_ROPE
V_DIM = 64
MLA_WIDTH = MLA_HEADS * V_DIM
Q_LORA = 384
KV_LORA = 256
ROPE_THETA = 10000.0
ATTN_SCALE = QK_DIM ** -0.5
D_FF = 2816
EPS = 1e-6

LANE = 128
SUBLANE = 8
HT = LANE
ROPE_LO = QK_NOPE
ROPE_HALF = QK_ROPE // 2
FF_CHUNK = 256
N_FF_CHUNKS = D_FF // FF_CHUNK
MiB = 1024 * 1024

PC_Z = 0
PC_XBC = PC_Z + SSD_WIDTH
PC_CQ = PC_XBC + CONV_DIM
PC_CKV = PC_CQ + Q_LORA
PC_KR = PC_CKV + KV_LORA
PC_END = PC_KR + LANE

_NT = (((1,), (1,)), ((), ()))
_TN = (((0,), (0,)), ((), ()))
_HI = lax.Precision.HIGHEST


def _dot(a, b):
    return jnp.dot(a, b, preferred_element_type=f32)


def _dot_nt(a, b):
    return lax.dot_general(a, b, _NT, preferred_element_type=f32)


def _dot_tn(a, b):
    return lax.dot_general(a, b, _TN, preferred_element_type=f32)


def _rms(x, n):
    return lax.rsqrt(jnp.sum(x * x, axis=-1, keepdims=True) * (1.0 / n) + EPS)


def _silu(x):
    return x * jax.nn.sigmoid(x)


def _softplus(x):
    return jnp.maximum(x, 0.0) + jnp.log1p(jnp.exp(-jnp.abs(x)))


def _const_spec(shape):
    nd = len(shape)
    return pl.BlockSpec(shape, lambda *_: (0,) * nd, pipeline_mode=pl.Buffered(1))


def _params(sem, vmem_mib):
    return pltpu.CompilerParams(dimension_semantics=sem, vmem_limit_bytes=vmem_mib * MiB)


def _proj_body(x_ref, g_ref, w_ref, z_ref, xbc_ref, cq_ref, ckv_ref, kr_ref, dtt_ref):
    x = x_ref[...]
    xn = (x * _rms(x, D_MODEL) * g_ref[...]).astype(bf16)
    z_ref[...] = _dot(xn, w_ref[:, PC_Z:PC_XBC])
    xbc_ref[...] = _dot(xn, w_ref[:, PC_XBC:PC_CQ])
    cq_ref[...] = _dot(xn, w_ref[:, PC_CQ:PC_CKV])
    ckv_ref[...] = _dot(xn, w_ref[:, PC_CKV:PC_KR])
    last = _dot(xn, w_ref[:, PC_KR:PC_END])
    lane = lax.broadcasted_iota(jnp.int32, (1, LANE), 1)
    kr_ref[...] = jnp.where(lane >= ROPE_LO, last, 0.0)
    dtt_ref[...] = last.T[0:16, :]


def _project(x, g, w, tm):
    m = x.shape[0]
    row = lambda n: pl.BlockSpec((tm, n), lambda i: (i, 0))
    widths = (SSD_WIDTH, CONV_DIM, Q_LORA, KV_LORA, LANE)
    return pl.pallas_call(
        _proj_body,
        grid=(m // tm,),
        in_specs=[row(D_MODEL), _const_spec((1, D_MODEL)), _const_spec((D_MODEL, PC_END))],
        out_specs=[row(n) for n in widths] + [pl.BlockSpec((16, tm), lambda i: (0, i))],
        out_shape=[jax.ShapeDtypeStruct((m, n), f32) for n in widths] + [jax.ShapeDtypeStruct((16, m), f32)],
        compiler_params=_params(("parallel",), 40),
        name="proj",
    )(x, g, w)


def _mla_prep_body(cq_ref, ckv_ref, kr_ref, tc_ref, ts1_ref, ts2_ref, gqn_ref, gkv_ref, wuq_ref, wuk_ref, wuv_ref,
                   gq_ref, gk_ref, lat_ref, kro_ref, q_ref, k_ref, vt_ref):
    tc, ts1, ts2 = tc_ref[...], ts1_ref[...], ts2_ref[...]
    tsw = ts1 + ts2

    def rope(x):
        return x * tc + pltpu.roll(x, LANE - ROPE_HALF, 1) * ts1 + pltpu.roll(x, ROPE_HALF, 1) * ts2

    ckv = ckv_ref[...]
    lat = ckv * _rms(ckv, KV_LORA) * gkv_ref[...]
    lat_ref[...] = lat
    kr = rope(kr_ref[...])
    kro_ref[...] = kr
    cq = cq_ref[...]
    cqn = (cq * _rms(cq, Q_LORA) * gqn_ref[...]).astype(bf16)
    latb = lat.astype(bf16)
    vt_ref[...] = _dot_nt(wuv_ref[...], latb).astype(vt_ref.dtype)
    gq, gk = gq_ref[...], gk_ref[...]
    for h in range(MLA_HEADS):
        sl = slice(h * HT, (h + 1) * HT)
        qh = _dot(cqn, wuq_ref[:, sl]) * tc + _dot(cqn, wuq_ref[:, MLA_HEADS * HT + h * HT:MLA_HEADS * HT + (h + 1) * HT]) * tsw
        q_ref[:, sl] = (qh * _rms(qh, QK_DIM) * gq).astype(q_ref.dtype)
        kh = _dot(latb, wuk_ref[:, sl]) + kr
        k_ref[:, sl] = (kh * _rms(kh, QK_DIM) * gk).astype(k_ref.dtype)


def _mla_prep(cq, ckv, kr, tabs, n_tab_blocks, gqn, gkv, wuq, wuk, wuvt, gq, gk, tm, q_dtype):
    m = cq.shape[0]
    row = lambda n: pl.BlockSpec((tm, n), lambda i: (i, 0))
    tab = pl.BlockSpec((tm, LANE), lambda i: (i % n_tab_blocks, 0))
    widths = (KV_LORA, LANE, MLA_HEADS * HT, MLA_HEADS * HT)
    dtypes = (f32, f32, q_dtype, bf16)
    return pl.pallas_call(
        _mla_prep_body,
        grid=(m // tm,),
        in_specs=[row(Q_LORA), row(KV_LORA), row(LANE), tab, tab, tab,
                  _const_spec((1, Q_LORA)), _const_spec((1, KV_LORA)),
                  _const_spec((Q_LORA, 2 * MLA_HEADS * HT)), _const_spec((KV_LORA, MLA_HEADS * HT)),
                  _const_spec((MLA_WIDTH, KV_LORA)), _const_spec((1, HT)), _const_spec((1, HT))],
        out_specs=[row(n) for n in widths] + [pl.BlockSpec((MLA_WIDTH, tm), lambda i: (0, i))],
        out_shape=[jax.ShapeDtypeStruct((m, n), d) for n, d in zip(widths, dtypes)]
                  + [jax.ShapeDtypeStruct((MLA_WIDTH, m), bf16)],
        compiler_params=_params(("parallel",), 40),
        name="mla_prep",
    )(cq, ckv, kr, *tabs, gqn, gkv, wuq, wuk, wuvt, gq, gk)


def _ssd_body(xbc_ref, z_ref, dtt_ref, hist_ref, h0_ref, cw_ref, cb_ref, bias_c_ref,
              alog_c_ref, dsk_ref, ng_ref, tri_ref, y_ref, hout_ref, xp_ref, hs_ref, *, valid_from):
    c = pl.program_id(1)
    q = SSD_CHUNK

    @pl.when(c == 0)
    def _():
        xp_ref[0:SUBLANE, :] = hist_ref[...]
        hs_ref[...] = h0_ref[0]

    xbc = xbc_ref[...]
    xp_ref[SUBLANE:SUBLANE + q, :] = xbc
    conv = cb_ref[...]
    for k in range(CONV_W):
        lo = SUBLANE - (CONV_W - 1) + k
        conv = conv + xp_ref[lo:lo + q, :] * cw_ref[k:k + 1, :]
    xp_ref[0:SUBLANE, :] = xbc[q - SUBLANE:q, :]
    xa = _silu(conv)
    xs = xa[:, :SSD_WIDTH]
    bm = xa[:, SSD_WIDTH:SSD_WIDTH + LANE]
    cm = xa[:, SSD_WIDTH + LANE:]

    rows = lax.broadcasted_iota(jnp.int32, (q, q), 0)
    cols = lax.broadcasted_iota(jnp.int32, (q, q), 1)
    lane = lax.broadcasted_iota(jnp.int32, (1, LANE), 1)
    low = lane < D_STATE

    dtr = _softplus(dtt_ref[...] + bias_c_ref[...])
    if valid_from:
        dtr = jnp.where(lax.broadcasted_iota(jnp.int32, (16, q), 1) >= valid_from, dtr, 0.0)
    dar = dtr * -jnp.exp(alog_c_ref[...])
    tri = tri_ref[...]
    dtc = dtr.T
    acs_r = lax.dot_general(dar, tri, _NT, precision=_HI, preferred_element_type=f32)
    acs_c = jnp.dot(tri, dar.T, precision=_HI, preferred_element_type=f32)
    w_c = jnp.exp(acs_c[q - 1:q, :] - acs_c) * dtc
    e_c = jnp.exp(acs_c)
    cd_r = jnp.exp(acs_r[:, q - 1:q])

    causal = cols <= rows
    bb = bm.astype(bf16)
    dsk = dsk_ref[...]
    ys = []
    for g in range(SSD_GROUPS):
        gmask = (lane >= g * D_STATE) & (lane < (g + 1) * D_STATE)
        cg = jnp.where(gmask, cm, 0.0).astype(bf16)
        cb_g = _dot_nt(cg, bb)
        for pi in range(2):
            i = 2 * g + pi
            xpair = xs[:, i * LANE:(i + 1) * LANE]
            ypair = jnp.zeros((q, LANE), f32)
            for hh in range(2):
                h = 2 * i + hh
                seg = acs_c[:, h:h + 1] - acs_r[h:h + 1, :]
                lmat = jnp.exp(jnp.where(causal, seg, -jnp.inf))
                sc = (cb_g * lmat * dtr[h:h + 1, :]).astype(bf16)
                xh = jnp.where(low if hh == 0 else jnp.logical_not(low), xpair, 0.0).astype(bf16)
                ypair = ypair + _dot(sc, xh)
            h0, h1 = 2 * i, 2 * i + 1
            wp = jnp.where(low, w_c[:, h0:h0 + 1], w_c[:, h1:h1 + 1])
            ep = jnp.where(low, e_c[:, h0:h0 + 1], e_c[:, h1:h1 + 1])
            hst = hs_ref[i]
            yoff = _dot_nt(cg, hst.astype(bf16)) * ep
            st = _dot_tn((xpair * wp).astype(bf16), bb)
            cdb = jnp.concatenate([jnp.broadcast_to(cd_r[h0:h0 + 1, :], (SSD_HEAD_DIM, LANE)),
                                   jnp.broadcast_to(cd_r[h1:h1 + 1, :], (SSD_HEAD_DIM, LANE))], axis=0)
            hs_ref[i] = hst * cdb + st
            ys.append(ypair + yoff + dsk[:, i * LANE:(i + 1) * LANE] * xpair)
    y = jnp.concatenate(ys, axis=1) * _silu(z_ref[...])
    gw = SSD_WIDTH // SSD_GROUPS
    outs = []
    for g in range(SSD_GROUPS):
        yg = y[:, g * gw:(g + 1) * gw]
        outs.append(yg * _rms(yg, gw))
    y_ref[...] = (jnp.concatenate(outs, axis=1) * ng_ref[...]).astype(y_ref.dtype)

    @pl.when(c == pl.num_programs(1) - 1)
    def _():
        hout_ref[0] = hs_ref[...]


def _ssd(xbc, z, dtt, hist_arr, hist_block, h0, consts, n_batch, n_chunks, row_block0, valid_from):
    q = SSD_CHUNK
    rb = lambda n: pl.BlockSpec((q, n), lambda b, c: (row_block0 + b * n_chunks + c, 0))
    cw, cb, bias_c, alog_c, dsk, ng, tri = consts
    return pl.pallas_call(
        functools.partial(_ssd_body, valid_from=valid_from),
        grid=(n_batch, n_chunks),
        in_specs=[rb(CONV_DIM), rb(SSD_WIDTH),
                  pl.BlockSpec((16, q), lambda b, c: (0, row_block0 + b * n_chunks + c)),
                  pl.BlockSpec((SUBLANE, CONV_DIM), lambda b, c: (hist_block, 0)),
                  _const_spec((1, 4, LANE, LANE)),
                  _const_spec((CONV_W, CONV_DIM)), _const_spec((1, CONV_DIM)),
                  _const_spec((16, 1)), _const_spec((16, 1)),
                  _const_spec((1, SSD_WIDTH)), _const_spec((1, SSD_WIDTH)), _const_spec((q, q))],
        out_specs=[pl.BlockSpec((q, SSD_WIDTH), lambda b, c: (b * n_chunks + c, 0)),
                   pl.BlockSpec((1, 4, LANE, LANE), lambda b, c: (b, 0, 0, 0))],
        out_shape=[jax.ShapeDtypeStruct((n_batch * n_chunks * q, SSD_WIDTH), bf16),
                   jax.ShapeDtypeStruct((n_batch, 4, LANE, LANE), f32)],
        scratch_shapes=[pltpu.VMEM((SUBLANE + q, CONV_DIM), f32), pltpu.VMEM((4, LANE, LANE), f32)],
        compiler_params=_params(("parallel", "arbitrary"), 32),
        name="ssd",
    )(xbc, z, dtt, hist_arr, h0, cw, cb, bias_c, alog_c, dsk, ng, tri)


def _flash_body(q_ref, k_ref, vt_ref, km_ref, vtm_ref, o_ref, *, tq, tk, nh):
    qi = pl.program_id(2)
    qs = [q_ref[:, h * HT:(h + 1) * HT] for h in range(nh)]
    drow = lax.broadcasted_iota(jnp.int32, (2 * V_DIM, 1), 0)
    vmasks = (drow < V_DIM, drow >= V_DIM)
    krow = lax.broadcasted_iota(jnp.int32, (tk + N_META, tq), 0)
    qcol = lax.broadcasted_iota(jnp.int32, (tk + N_META, tq), 1)

    def heads(kall, vtall, carries, mask=None):
        sts = [_dot_nt(kall[:, h * HT:(h + 1) * HT], qs[h]) for h in range(nh)]
        out = []
        for h in range(nh):
            m, l, acct = carries[h]
            st = sts[h] if mask is None else jnp.where(mask, sts[h], -jnp.inf)
            m2 = jnp.maximum(m, jnp.max(st, axis=0, keepdims=True))
            pt = jnp.exp(st - m2)
            a = jnp.exp(m - m2)
            vtt = vtall[(h // 2) * LANE:(h // 2 + 1) * LANE, :]
            vz = jnp.where(vmasks[h % 2], vtt, jnp.zeros_like(vtt))
            out.append((m2, a * l + jnp.sum(pt, axis=0, keepdims=True), a * acct + _dot(vz, pt.astype(bf16))))
        return tuple(out)

    init = (jnp.full((1, tq), -jnp.inf, f32), jnp.zeros((1, tq), f32), jnp.zeros((2 * V_DIM, tq), f32))

    def body(j, carries):
        off = pl.multiple_of(j * tk, tk)
        return heads(k_ref[pl.ds(off, tk), :], vt_ref[:, pl.ds(off, tk)], carries)

    n_full = (qi * tq) // tk
    carries = lax.fori_loop(0, n_full, body, (init,) * nh)
    off = pl.multiple_of(n_full * tk, tk)
    k_last = jnp.concatenate([k_ref[pl.ds(off, tk), :], km_ref[...]], axis=0)
    vt_last = jnp.concatenate([vt_ref[:, pl.ds(off, tk)], vtm_ref[...]], axis=1)
    visible = (krow >= tk) | (krow <= qcol + (qi * tq - n_full * tk))
    res = heads(k_last, vt_last, carries, visible)
    for pr in range(nh // 2):
        (_, l0, acc0), (_, l1, acc1) = res[2 * pr], res[2 * pr + 1]
        o_ref[:, pr * LANE:(pr + 1) * LANE] = (acc0 / l0 + acc1 / l1).T.astype(o_ref.dtype)


def _flash(q, k, vt, k_small, vt_meta, meta_block, n_batch, seq, tq, tk, nh):
    assert tk % tq == 0 and seq % tk == 0 and nh % 2 == 0 and MLA_HEADS % nh == 0
    nq = seq // tq
    return pl.pallas_call(
        functools.partial(_flash_body, tq=tq, tk=tk, nh=nh),
        grid=(n_batch, MLA_HEADS // nh, nq),
        in_specs=[pl.BlockSpec((tq, nh * HT), lambda b, p, i: (b * nq + i, p)),
                  pl.BlockSpec((seq, nh * HT), lambda b, p, i: (b, p)),
                  pl.BlockSpec((nh * V_DIM, seq), lambda b, p, i: (p, b)),
                  pl.BlockSpec((N_META, nh * HT), lambda b, p, i: (meta_block, p)),
                  pl.BlockSpec((nh * V_DIM, N_META), lambda b, p, i: (p, 0))],
        out_specs=pl.BlockSpec((tq, nh * V_DIM), lambda b, p, i: (b * nq + i, p)),
        out_shape=jax.ShapeDtypeStruct((n_batch * seq, MLA_WIDTH), f32),
        compiler_params=_params(("parallel", "parallel", "arbitrary"), 40),
        name="flash",
    )(q, k, vt, k_small, vt_meta)


def _finish_body(x_ref, ssd_ref, att_ref, gm_ref, wo_ref, gf_ref, wg_ref, wu_ref, wd_ref, o_ref):
    att = att_ref[...]
    mla = (att * _rms(att, MLA_WIDTH) * gm_ref[...]).astype(bf16)
    h = x_ref[...] + (_dot(ssd_ref[...], wo_ref[0:SSD_WIDTH, :]) + _dot(mla, wo_ref[SSD_WIDTH:, :]))
    n = (h * _rms(h, D_MODEL) * gf_ref[...]).astype(bf16)
    ff = jnp.zeros_like(h)
    for c in range(N_FF_CHUNKS):
        cs = slice(c * FF_CHUNK, (c + 1) * FF_CHUNK)
        a = (_silu(_dot(n, wg_ref[:, cs])) * _dot(n, wu_ref[:, cs])).astype(bf16)
        ff = ff + _dot(a, wd_ref[cs, :])
    o_ref[...] = h + ff


def _finish(x, ssd, att, gm, wo, gf, wg, wu, wd, tm):
    m = x.shape[0]
    row = lambda n: pl.BlockSpec((tm, n), lambda i: (i, 0))
    return pl.pallas_call(
        _finish_body,
        grid=(m // tm,),
        in_specs=[row(D_MODEL), row(SSD_WIDTH), row(MLA_WIDTH), _const_spec((1, MLA_WIDTH)),
                  _const_spec((D_MODEL, D_MODEL)), _const_spec((1, D_MODEL)),
                  _const_spec((D_MODEL, D_FF)), _const_spec((D_MODEL, D_FF)), _const_spec((D_FF, D_MODEL))],
        out_specs=row(D_MODEL),
        out_shape=jax.ShapeDtypeStruct((m, D_MODEL), f32),
        compiler_params=_params(("parallel",), 56),
        name="finish",
    )(x, ssd, att, gm, wo, gf, wg, wu, wd)


def _sssd_prep_body(xbc_ref, sc_ref, cw_ref, cb_ref, xs_ref, xat_ref):
    conv = cb_ref[...]
    for k in range(CONV_W - 1):
        conv = conv + sc_ref[k] * cw_ref[k:k + 1, :]
    conv = conv + xbc_ref[...] * cw_ref[CONV_W - 1:CONV_W, :]
    xa = _silu(conv)
    xs_ref[...] = xa[:, :SSD_WIDTH]
    xat_ref[...] = xa.T


def _sssd_prep(xbc_small, sc, cw, cb, nseq):
    return pl.pallas_call(
        _sssd_prep_body,
        grid=(1,),
        in_specs=[pl.BlockSpec((nseq, CONV_DIM), lambda i: (0, 0)), _const_spec((CONV_W - 1, nseq, CONV_DIM)),
                  _const_spec((CONV_W, CONV_DIM)), _const_spec((1, CONV_DIM))],
        out_specs=[pl.BlockSpec((nseq, SSD_WIDTH), lambda i: (0, 0)), pl.BlockSpec((CONV_DIM, nseq), lambda i: (0, 0))],
        out_shape=[jax.ShapeDtypeStruct((nseq, SSD_WIDTH), f32), jax.ShapeDtypeStruct((CONV_DIM, nseq), f32)],
        compiler_params=_params(("arbitrary",), 32),
        name="sssd_prep",
    )(xbc_small, sc, cw, cb)


def _sssd_state_body(h0_ref, xst_ref, bt_ref, ct_ref, dtt_ref, bias_c_ref, alog_c_ref, hn_ref, yt_ref):
    h = pl.program_id(0)
    dt = _softplus(dtt_ref[pl.ds(h, 1), :] + bias_c_ref[pl.ds(h, 1), :])
    dec = jnp.exp(dt * -jnp.exp(alog_c_ref[pl.ds(h, 1), :]))
    bt, ct = bt_ref[...], ct_ref[...]

    def body(p, carry):
        xdt = xst_ref[pl.ds(p, 1), :] * dt
        hn = h0_ref[0, p] * dec + xdt * bt
        hn_ref[0, p] = hn
        yt_ref[pl.ds(p, 1), :] = jnp.sum(ct * hn, axis=0, keepdims=True)
        return carry

    lax.fori_loop(0, SSD_HEAD_DIM, body, 0, unroll=4)


def _sssd_state(h0t, xat, dtt_small, bias_c, alog_c, nseq):
    hpg = SSD_HEADS // SSD_GROUPS
    rows = lambda f: pl.BlockSpec((SSD_HEAD_DIM, nseq), f)
    hblk = pl.BlockSpec((1, SSD_HEAD_DIM, D_STATE, nseq), lambda h: (h, 0, 0, 0))
    return pl.pallas_call(
        _sssd_state_body,
        grid=(SSD_HEADS,),
        in_specs=[hblk, rows(lambda h: (h, 0)), rows(lambda h: (SSD_HEADS + h // hpg, 0)),
                  rows(lambda h: (SSD_HEADS + SSD_GROUPS + h // hpg, 0)),
                  pl.BlockSpec((16, nseq), lambda h: (0, 0)), _const_spec((16, 1)), _const_spec((16, 1))],
        out_specs=[hblk, rows(lambda h: (h, 0))],
        out_shape=[jax.ShapeDtypeStruct((SSD_HEADS, SSD_HEAD_DIM, D_STATE, nseq), f32),
                   jax.ShapeDtypeStruct((SSD_WIDTH, nseq), f32)],
        compiler_params=_params(("parallel",), 32),
        name="sssd_state",
    )(h0t, xat, xat, xat, dtt_small, bias_c, alog_c)


def _sssd_gate_body(yt_ref, xs_ref, z_ref, dsk_ref, ng_ref, o_ref):
    y = (yt_ref[...].T + dsk_ref[...] * xs_ref[...]) * _silu(z_ref[...])
    gw = SSD_WIDTH // SSD_GROUPS
    outs = []
    for g in range(SSD_GROUPS):
        yg = y[:, g * gw:(g + 1) * gw]
        outs.append(yg * _rms(yg, gw))
    o_ref[...] = (jnp.concatenate(outs, axis=1) * ng_ref[...]).astype(o_ref.dtype)


def _sssd_gate(yt, xs, z_small, dsk, ng, nseq):
    blk = pl.BlockSpec((nseq, SSD_WIDTH), lambda i: (0, 0))
    return pl.pallas_call(
        _sssd_gate_body,
        grid=(1,),
        in_specs=[pl.BlockSpec((SSD_WIDTH, nseq), lambda i: (0, 0)), blk, blk,
                  _const_spec((1, SSD_WIDTH)), _const_spec((1, SSD_WIDTH))],
        out_specs=blk,
        out_shape=jax.ShapeDtypeStruct((nseq, SSD_WIDTH), bf16),
        compiler_params=_params(("arbitrary",), 32),
        name="sssd_gate",
    )(yt, xs, z_small, dsk, ng)


def _absorb_body(q_ref, gk_ref, wabs_ref, qg_ref, a_ref):
    gk = gk_ref[...]
    for h in range(MLA_HEADS):
        sl = slice(h * HT, (h + 1) * HT)
        qg = q_ref[:, sl] * gk
        qg_ref[:, sl] = qg
        a_ref[:, h * KV_LORA:(h + 1) * KV_LORA] = jnp.dot(qg, wabs_ref[sl, :], precision=_HI, preferred_element_type=f32)
    a_ref[:, MLA_HEADS * KV_LORA:] = jnp.zeros((q_ref.shape[0], (16 - MLA_HEADS) * KV_LORA), f32)


def _absorb(q_small, gk, wabs, nseq):
    return pl.pallas_call(
        _absorb_body,
        grid=(1,),
        in_specs=[pl.BlockSpec((nseq, MLA_HEADS * HT), lambda i: (0, 0)), _const_spec((1, HT)),
                  _const_spec((MLA_HEADS * HT, KV_LORA))],
        out_specs=[pl.BlockSpec((nseq, MLA_HEADS * HT), lambda i: (0, 0)),
                   pl.BlockSpec((nseq, 16 * KV_LORA), lambda i: (0, 0))],
        out_shape=[jax.ShapeDtypeStruct((nseq, MLA_HEADS * HT), f32), jax.ShapeDtypeStruct((nseq, 16 * KV_LORA), f32)],
        compiler_params=_params(("arbitrary",), 32),
        name="absorb",
    )(q_small, gk, wabs)


def _decode_body(pt_ref, a_ref, c_ref, latn_ref, krnt_ref, wukt_ref, clat_ref, ckrt_ref, o_ref,
                 lat_buf, kr_buf, waug, latb, s_buf, sems, *, n_pages, page, ppt, unroll):
    b = pl.program_id(0)
    nb = pl.num_programs(0)
    slot = b % 2
    n_tiles = n_pages // ppt
    tile = ppt * page
    wrows = MLA_HEADS * QK_NOPE

    def start_page(seq, p, sl):
        pid = pt_ref[seq * n_pages + p]
        pltpu.make_async_copy(clat_ref.at[0, pid], lat_buf.at[sl, p], sems.at[0, sl]).start()
        pltpu.make_async_copy(ckrt_ref.at[0, pid], kr_buf.at[sl, p], sems.at[1, sl]).start()

    @pl.when(b == 0)
    def _():
        waug[0:wrows, :] = wukt_ref[...]

        def first(p, carry):
            start_page(0, p, 0)
            return carry

        lax.fori_loop(0, n_pages, first, 0)

    pltpu.make_async_copy(clat_ref.at[0, pl.ds(0, n_pages)], lat_buf.at[slot], sems.at[0, slot]).wait()
    pltpu.make_async_copy(ckrt_ref.at[0, pl.ds(0, n_pages)], kr_buf.at[slot], sems.at[1, slot]).wait()

    waug[wrows:wrows + 16, :] = a_ref[0, 0].astype(bf16)
    cb = c_ref[0].astype(bf16)

    def score_dots(lb, krt):
        kt = _dot_nt(waug[...], lb)
        return kt, _dot(cb, krt.astype(bf16)), krt

    def score_finish(kt, s2, krt):
        n = kt.shape[1]
        k3 = kt[0:wrows, :].reshape(MLA_HEADS, QK_NOPE, n)
        n2 = jnp.sum(k3 * k3, axis=1)
        kr2 = jnp.sum(krt * krt, axis=0, keepdims=True)
        return lax.rsqrt((n2 + kr2) * (1.0 / QK_DIM) + EPS) * (kt[wrows:wrows + MLA_HEADS, :] + s2)

    def trip(jj):
        @pl.when(b + 1 < nb)
        def _():
            for pp in range(unroll * ppt):
                start_page(b + 1, jj * (unroll * ppt) + pp, 1 - slot)

        for u in range(unroll):
            j = jj * unroll + u
            lb = lat_buf[slot, pl.ds(j * ppt, ppt)].reshape(tile, KV_LORA).astype(bf16)
            latb[j] = lb
            kr_pages = kr_buf[slot, pl.ds(j * ppt, ppt)]
            s_buf[j] = score_finish(*score_dots(lb, jnp.concatenate([kr_pages[i] for i in range(ppt)], axis=-1)))

    def body(jj, carry):
        trip(jj)
        return carry

    lax.fori_loop(0, n_tiles // unroll, body, 0)
    first_row = lax.broadcasted_iota(jnp.int32, (page, KV_LORA), 0) == 0
    first_lane = lax.broadcasted_iota(jnp.int32, (1, page), 1) == 0
    lb_new = jnp.where(first_row, latn_ref[0], 0.0).astype(bf16)
    new_dots = score_dots(lb_new, jnp.where(first_lane, krnt_ref[0], 0.0))
    s_new = jnp.where(first_lane, score_finish(*new_dots), -jnp.inf)

    s_all = jnp.concatenate([s_buf[j] for j in range(n_tiles)], axis=-1)
    m = jnp.maximum(jnp.max(s_all, axis=-1, keepdims=True), jnp.max(s_new, axis=-1, keepdims=True))
    p_all = jnp.exp(s_all - m)
    p_new = jnp.exp(s_new - m)
    l = jnp.sum(p_all, axis=-1, keepdims=True) + jnp.sum(p_new, axis=-1, keepdims=True)
    acc = _dot(p_all.astype(bf16), latb[...].reshape(n_tiles * tile, KV_LORA)) + _dot(p_new.astype(bf16), lb_new)
    o_ref[0] = acc / l


def _decode(page_table_flat, a, c, lat_new, kr_new_t, wukt, cache_lat, cache_kr_t, nseq, n_pages, page, ppt, unroll):
    wrows = MLA_HEADS * QK_NOPE
    grid_spec = pltpu.PrefetchScalarGridSpec(
        num_scalar_prefetch=1,
        grid=(nseq,),
        in_specs=[pl.BlockSpec((1, 1, 16, KV_LORA), lambda b, pt: (b, 0, 0, 0)),
                  pl.BlockSpec((1, MLA_HEADS, QK_ROPE), lambda b, pt: (b, 0, 0)),
                  pl.BlockSpec((1, 1, KV_LORA), lambda b, pt: (b, 0, 0)),
                  pl.BlockSpec((1, QK_ROPE, 1), lambda b, pt: (b, 0, 0)),
                  pl.BlockSpec((wrows, KV_LORA), lambda b, pt: (0, 0), pipeline_mode=pl.Buffered(1)),
                  pl.BlockSpec(memory_space=pl.ANY), pl.BlockSpec(memory_space=pl.ANY)],
        out_specs=pl.BlockSpec((1, MLA_HEADS, KV_LORA), lambda b, pt: (b, 0, 0)),
        scratch_shapes=[pltpu.VMEM((2, n_pages, page, KV_LORA), f32), pltpu.VMEM((2, n_pages, QK_ROPE, page), f32),
                        pltpu.VMEM((wrows + 16, KV_LORA), bf16),
                        pltpu.VMEM((n_pages // ppt, ppt * page, KV_LORA), bf16),
                        pltpu.VMEM((n_pages // ppt, MLA_HEADS, ppt * page), f32), pltpu.SemaphoreType.DMA((2, 2))],
    )
    return pl.pallas_call(
        functools.partial(_decode_body, n_pages=n_pages, page=page, ppt=ppt, unroll=unroll),
        grid_spec=grid_spec,
        out_shape=jax.ShapeDtypeStruct((nseq, MLA_HEADS, KV_LORA), f32),
        compiler_params=_params(("arbitrary",), 40),
        name="decode",
    )(page_table_flat, a, c, lat_new, kr_new_t, wukt, cache_lat, cache_kr_t)


def _uv_body(o_ref, wuv_ref, att_ref):
    for h in range(MLA_HEADS):
        oh = o_ref[:, h * KV_LORA:(h + 1) * KV_LORA].astype(bf16)
        att_ref[:, h * V_DIM:(h + 1) * V_DIM] = _dot(oh, wuv_ref[:, h * V_DIM:(h + 1) * V_DIM])


def _uv(o_flat, wuv):
    nseq = o_flat.shape[0]
    return pl.pallas_call(
        _uv_body,
        grid=(1,),
        in_specs=[pl.BlockSpec((nseq, MLA_HEADS * KV_LORA), lambda i: (0, 0)), _const_spec((KV_LORA, MLA_WIDTH))],
        out_specs=pl.BlockSpec((nseq, MLA_WIDTH), lambda i: (0, 0)),
        out_shape=jax.ShapeDtypeStruct((nseq, MLA_WIDTH), f32),
        compiler_params=_params(("arbitrary",), 32),
        name="uv",
    )(o_flat, wuv)


def _rope_tables(pos):
    inv = ROPE_THETA ** (-jnp.arange(ROPE_HALF, dtype=f32) * (2.0 / QK_ROPE))
    ang = pos.astype(f32)[:, None] * inv[None, :]
    cos, sin = jnp.cos(ang), jnp.sin(ang)
    n = pos.shape[0]
    one = jnp.ones((n, QK_NOPE), f32)
    z16 = jnp.zeros((n, ROPE_HALF), f32)
    z32 = jnp.zeros((n, HT - QK_DIM), f32)
    z64 = jnp.zeros((n, QK_NOPE), f32)
    tc = jnp.concatenate([one, cos, cos, z32], axis=1)
    ts1 = jnp.concatenate([z64, -sin, z16, z32], axis=1)
    ts2 = jnp.concatenate([z64, z16, sin, z32], axis=1)
    return tc, ts1, ts2


def _pad_lanes(x, n):
    return jnp.pad(x, ((0, 0), (0, n - x.shape[1])))


def kernel(x_prompt, x_sample, cache_kv_latent, cache_k_rope, state_conv, state_ssm, page_table, meta_tokens, attn_norm_g, w_in, conv_w, conv_b, dt_bias, a_log, d_skip, ssd_norm_g, q_norm_g, w_uq, kv_norm_g, w_ukv, q_head_g_nope, q_head_g_rope, k_head_g_nope, k_head_g_rope, mla_out_g, w_out, ffn_norm_g, w_gate, w_up, w_down):
    l = 0
    nb, seq, _ = x_prompt.shape
    nseq = x_sample.shape[0]
    n_pages, page = page_table.shape[1], cache_kv_latent.shape[2]
    past = n_pages * page
    n_small = 2 * LANE
    meta_lo = n_small - N_META

    s0, s1, s2, s3, s4 = (SSD_WIDTH, SSD_WIDTH + CONV_DIM, SSD_WIDTH + CONV_DIM + SSD_HEADS,
                          SSD_WIDTH + CONV_DIM + SSD_HEADS + Q_LORA, SSD_WIDTH + CONV_DIM + SSD_HEADS + Q_LORA + KV_LORA)
    wi = w_in[l]
    w_last = jnp.concatenate([_pad_lanes(wi[:, s1:s2], ROPE_LO), _pad_lanes(wi[:, s4:], HT - ROPE_LO)], axis=1)
    w_proj = jnp.concatenate([wi[:, :s1], wi[:, s2:s4], w_last], axis=1).astype(bf16)
    g_attn = attn_norm_g[l][None, :]
    wq3 = w_uq[l].reshape(Q_LORA, MLA_HEADS, QK_DIM)
    zq = jnp.zeros((Q_LORA, MLA_HEADS, QK_NOPE), f32)
    wq_partner = jnp.concatenate([zq, wq3[:, :, QK_NOPE + ROPE_HALF:], wq3[:, :, QK_NOPE:QK_NOPE + ROPE_HALF]], axis=2)
    pad_head = lambda w: jnp.pad(w, ((0, 0), (0, 0), (0, HT - QK_DIM))).reshape(Q_LORA, -1)
    wuq = jnp.concatenate([pad_head(wq3), pad_head(wq_partner)], axis=1).astype(bf16)
    wkv = w_ukv[l].reshape(KV_LORA, MLA_HEADS, QK_NOPE + V_DIM)
    wuk_f32 = jnp.pad(wkv[:, :, :QK_NOPE], ((0, 0), (0, 0), (0, HT - QK_NOPE))).reshape(KV_LORA, -1)
    wuk = wuk_f32.astype(bf16)
    wuv = wkv[:, :, QK_NOPE:].reshape(KV_LORA, MLA_WIDTH).astype(bf16)
    wukt = wkv[:, :, :QK_NOPE].reshape(KV_LORA, -1).T.astype(bf16)
    gq = _pad_lanes(jnp.concatenate([q_head_g_nope[l], q_head_g_rope[l], q_head_g_rope[l]])[None, :], HT) * ATTN_SCALE
    gk = _pad_lanes(jnp.concatenate([k_head_g_nope[l], k_head_g_rope[l], k_head_g_rope[l]])[None, :], HT)
    gqn, gkv = q_norm_g[l][None, :], kv_norm_g[l][None, :]
    cw, cb = conv_w[l], conv_b[l][None, :]
    bias_c = jnp.pad(dt_bias[l][:, None], ((0, 16 - SSD_HEADS), (0, 0)))
    alog_c = jnp.pad(a_log[l][:, None], ((0, 16 - SSD_HEADS), (0, 0)))
    dsk = jnp.repeat(d_skip[l], SSD_HEAD_DIM)[None, :]
    ng = ssd_norm_g[l][None, :]
    tri = jnp.tril(jnp.ones((SSD_CHUNK, SSD_CHUNK), f32))
    ssd_consts = (cw, cb, bias_c, alog_c, dsk, ng, tri)
    gm, gf = mla_out_g[l][None, :], ffn_norm_g[l][None, :]
    wo = w_out[l].astype(bf16)
    wg, wu, wd = w_gate[l].astype(bf16), w_up[l].astype(bf16), w_down[l].astype(bf16)

    xp = x_prompt.reshape(nb * seq, D_MODEL)
    xs_rows = x_sample[:, 0, :]
    x_small = jnp.concatenate([xs_rows, jnp.zeros((meta_lo - nseq, D_MODEL), f32), meta_tokens.astype(f32)], axis=0)

    z_p, xbc_p, cq_p, ckv_p, kr_p, dtt_p = _project(xp, g_attn, w_proj, 512)
    z_s, xbc_s, cq_s, ckv_s, kr_s, dtt_s = _project(x_small, g_attn, w_proj, n_small)

    tabs_p = _rope_tables(N_META + jnp.arange(seq))
    pos_small = jnp.concatenate([jnp.full((nseq,), past), jnp.zeros((meta_lo - nseq,), jnp.int32), jnp.arange(N_META)])
    tabs_s = _rope_tables(pos_small)
    tm = 512
    lat_p, kro_p, q_p, k_p, vt_p = _mla_prep(cq_p, ckv_p, kr_p, tabs_p, seq // tm, gqn, gkv, wuq, wuk, wuv.T, gq, gk, tm, bf16)
    lat_s, kro_s, q_s, k_s, vt_s = _mla_prep(cq_s, ckv_s, kr_s, tabs_s, 1, gqn, gkv, wuq, wuk, wuv.T, gq, gk, n_small, f32)

    zero_h = jnp.zeros((1, 4, LANE, LANE), f32)
    zero_hist = jnp.zeros((SUBLANE, CONV_DIM), f32)
    _, h_meta = _ssd(xbc_s, z_s, dtt_s, zero_hist, 0, zero_h, ssd_consts, 1, 1, 1, SSD_CHUNK - N_META)
    ssd_p, h_fin = _ssd(xbc_p, z_p, dtt_p, xbc_s, n_small // SUBLANE - 1, h_meta, ssd_consts, nb, seq // SSD_CHUNK, 0, 0)

    att_p = _flash(q_p, k_p, vt_p, k_s, vt_s[:, meta_lo:], n_small // N_META - 1, nb, seq, 512, 512, 4)
    y_prompt = _finish(xp, ssd_p, att_p, gm, wo, gf, wg, wu, wd, 512).reshape(nb, seq, D_MODEL)

    sc = jnp.transpose(state_conv[l], (1, 0, 2))
    xs_s, xat_s = _sssd_prep(xbc_s, sc, cw, cb, nseq)
    h_new_t, yt_s = _sssd_state(jnp.transpose(state_ssm[l], (1, 2, 3, 0)), xat_s, dtt_s, bias_c, alog_c, nseq)
    h_new = jnp.transpose(h_new_t, (3, 0, 1, 2))
    ssd_s = _sssd_gate(yt_s, xs_s, z_s, dsk, ng, nseq)
    qg, a_abs = _absorb(q_s, gk, wuk_f32.T, nseq)
    a_abs = a_abs.reshape(nseq, 1, 16, KV_LORA)
    c_abs = qg.reshape(nseq, MLA_HEADS, HT)[:, :, ROPE_LO:QK_DIM]
    lat_new = lat_s[:nseq][:, None, :]
    kr_new = kro_s[:nseq, ROPE_LO:QK_DIM][:, None, :]
    o_lat = _decode(page_table.reshape(-1), a_abs, c_abs, lat_new, jnp.swapaxes(kr_new, 1, 2), wukt,
                    cache_kv_latent, jnp.swapaxes(cache_k_rope, 2, 3), nseq, n_pages, page, 32, 1)
    att_s = _uv(o_lat.reshape(nseq, MLA_HEADS * KV_LORA), wuv)
    y_sample = _finish(xs_rows, ssd_s, att_s, gm, wo, gf, wg, wu, wd, nseq)[:, None, :]

    def with_meta(small, main, width):
        meta = jnp.broadcast_to(small[meta_lo:][None], (nb, N_META, width))
        return jnp.concatenate([meta, main.reshape(nb, seq, width)], axis=1)[None]

    kv_latent_prompt = with_meta(lat_s, lat_p, KV_LORA)
    k_rope_prompt = with_meta(kro_s[:, ROPE_LO:QK_DIM], kro_p[:, ROPE_LO:QK_DIM], QK_ROPE)
    conv_prompt = xbc_p.reshape(nb, seq, CONV_DIM)[:, seq - (CONV_W - 1):][None]
    hf = h_fin.reshape(nb, 4, 2, SSD_HEAD_DIM, SSD_GROUPS, D_STATE)
    ssm_prompt = jnp.stack([hf[:, i, :, :, i // 2, :] for i in range(4)], axis=1).reshape(nb, SSD_HEADS, SSD_HEAD_DIM, D_STATE)[None]
    kv_latent_sample = lat_new[None]
    k_rope_sample = kr_new[None]
    conv_sample = jnp.concatenate([state_conv[l][:, 1:], xbc_s[:nseq][:, None, :]], axis=1)[None]
    ssm_sample = h_new[None]
    return (y_prompt, y_sample, kv_latent_prompt, k_rope_prompt, conv_prompt, ssm_prompt.astype(x_prompt.dtype),
            kv_latent_sample, k_rope_sample, conv_sample, ssm_sample.astype(state_ssm.dtype))
```

```python
import functools

import jax
import jax.numpy as jnp
from jax import lax
from jax.experimental import pallas as pl
from jax.experimental.pallas import tpu as pltpu

f32 = jnp.float32
bf16 = jnp.bfloat16

D_MODEL = 1024
N_META = 16
SSD_HEADS = 8
SSD_HEAD_DIM = 64
SSD_WIDTH = SSD_HEADS * SSD_HEAD_DIM
SSD_GROUPS = 2
D_STATE = 64
CONV_W = 4
CONV_DIM = SSD_WIDTH + 2 * SSD_GROUPS * D_STATE
SSD_CHUNK = 128
MLA_HEADS = 8
QK_NOPE = 64
QK_ROPE = 32
QK_DIM = QK_NOPE + QK_ROPE
V_DIM = 64
MLA_WIDTH = MLA_HEADS * V_DIM
Q_LORA = 384
KV_LORA = 256
ROPE_THETA = 10000.0
ATTN_SCALE = QK_DIM ** -0.5
D_FF = 2816
EPS = 1e-6

LANE = 128
SUBLANE = 8
HT = LANE
ROPE_LO = QK_NOPE
ROPE_HALF = QK_ROPE // 2
FF_CHUNK = 256
N_FF_CHUNKS = D_FF // FF_CHUNK
DECODE_PAGES_PER_TILE = 32
DMA_CHUNKS = 8
MiB = 1024 * 1024

PC_Z = 0
PC_XBC = PC_Z + SSD_WIDTH
PC_CQ = PC_XBC + CONV_DIM
PC_CKV = PC_CQ + Q_LORA
PC_KR = PC_CKV + KV_LORA
PC_END = PC_KR + LANE

_NT = (((1,), (1,)), ((), ()))
_TN = (((0,), (0,)), ((), ()))
_HI = lax.Precision.HIGHEST


def _dot(a, b):
    return jnp.dot(a, b, preferred_element_type=f32)


def _dot_nt(a, b):
    return lax.dot_general(a, b, _NT, preferred_element_type=f32)


def _dot_tn(a, b):
    return lax.dot_general(a, b, _TN, preferred_element_type=f32)


def _rms(x, n):
    return lax.rsqrt(jnp.sum(x * x, axis=-1, keepdims=True) * (1.0 / n) + EPS)


def _silu(x):
    return x * jax.nn.sigmoid(x)


def _softplus(x):
    return jnp.maximum(x, 0.0) + jnp.log1p(jnp.exp(-jnp.abs(x)))


def _const_spec(shape):
    nd = len(shape)
    return pl.BlockSpec(shape, lambda *_: (0,) * nd, pipeline_mode=pl.Buffered(1))


def _params(sem, vmem_mib):
    return pltpu.CompilerParams(dimension_semantics=sem, vmem_limit_bytes=vmem_mib * MiB)


def _proj_body(x_ref, g_ref, w_ref, z_ref, xbc_ref, cq_ref, ckv_ref, kr_ref, dtt_ref):
    x = x_ref[...]
    xn = (x * _rms(x, D_MODEL) * g_ref[...]).astype(bf16)
    z_ref[...] = _dot(xn, w_ref[:, PC_Z:PC_XBC])
    xbc_ref[...] = _dot(xn, w_ref[:, PC_XBC:PC_CQ])
    cq_ref[...] = _dot(xn, w_ref[:, PC_CQ:PC_CKV])
    ckv_ref[...] = _dot(xn, w_ref[:, PC_CKV:PC_KR])
    last = _dot(xn, w_ref[:, PC_KR:PC_END])
    lane = lax.broadcasted_iota(jnp.int32, (1, LANE), 1)
    kr_ref[...] = jnp.where(lane >= ROPE_LO, last, 0.0)
    dtt_ref[...] = last.T[0:16, :]


def _project(x, g, w, tm):
    m = x.shape[0]
    row = lambda n: pl.BlockSpec((tm, n), lambda i: (i, 0))
    widths = (SSD_WIDTH, CONV_DIM, Q_LORA, KV_LORA, LANE)
    return pl.pallas_call(
        _proj_body,
        grid=(m // tm,),
        in_specs=[row(D_MODEL), _const_spec((1, D_MODEL)), _const_spec((D_MODEL, PC_END))],
        out_specs=[row(n) for n in widths] + [pl.BlockSpec((16, tm), lambda i: (0, i))],
        out_shape=[jax.ShapeDtypeStruct((m, n), f32) for n in widths] + [jax.ShapeDtypeStruct((16, m), f32)],
        compiler_params=_params(("parallel",), 40),
        name="proj",
    )(x, g, w)


def _mla_prep_body(cq_ref, ckv_ref, kr_ref, tc_ref, ts1_ref, ts2_ref, gqn_ref, gkv_ref, wuq_ref, wuk_ref, wuv_ref,
                   gq_ref, gk_ref, lat_ref, kro_ref, q_ref, k_ref, vt_ref):
    tc, ts1, ts2 = tc_ref[...], ts1_ref[...], ts2_ref[...]
    tsw = ts1 + ts2

    def rope(x):
        return x * tc + pltpu.roll(x, LANE - ROPE_HALF, 1) * ts1 + pltpu.roll(x, ROPE_HALF, 1) * ts2

    ckv = ckv_ref[...]
    lat = ckv * _rms(ckv, KV_LORA) * gkv_ref[...]
    lat_ref[...] = lat
    kr = rope(kr_ref[...])
    kro_ref[...] = kr
    cq = cq_ref[...]
    cqn = (cq * _rms(cq, Q_LORA) * gqn_ref[...]).astype(bf16)
    latb = lat.astype(bf16)
    vt_ref[...] = _dot_nt(wuv_ref[...], latb).astype(vt_ref.dtype)
    gq, gk = gq_ref[...], gk_ref[...]
    for h in range(MLA_HEADS):
        sl = slice(h * HT, (h + 1) * HT)
        qh = _dot(cqn, wuq_ref[:, sl]) * tc + _dot(cqn, wuq_ref[:, MLA_HEADS * HT + h * HT:MLA_HEADS * HT + (h + 1) * HT]) * tsw
        q_ref[:, sl] = (qh * _rms(qh, QK_DIM) * gq).astype(q_ref.dtype)
        kh = _dot(latb, wuk_ref[:, sl]) + kr
        k_ref[:, sl] = (kh * _rms(kh, QK_DIM) * gk).astype(k_ref.dtype)


def _mla_prep(cq, ckv, kr, tabs, n_tab_blocks, gqn, gkv, wuq, wuk, wuvt, gq, gk, tm, q_dtype):
    m = cq.shape[0]
    row = lambda n: pl.BlockSpec((tm, n), lambda i: (i, 0))
    tab = pl.BlockSpec((tm, LANE), lambda i: (i % n_tab_blocks, 0))
    widths = (KV_LORA, LANE, MLA_HEADS * HT, MLA_HEADS * HT)
    dtypes = (f32, f32, q_dtype, bf16)
    return pl.pallas_call(
        _mla_prep_body,
        grid=(m // tm,),
        in_specs=[row(Q_LORA), row(KV_LORA), row(LANE), tab, tab, tab,
                  _const_spec((1, Q_LORA)), _const_spec((1, KV_LORA)),
                  _const_spec((Q_LORA, 2 * MLA_HEADS * HT)), _const_spec((KV_LORA, MLA_HEADS * HT)),
                  _const_spec((MLA_WIDTH, KV_LORA)), _const_spec((1, HT)), _const_spec((1, HT))],
        out_specs=[row(n) for n in widths] + [pl.BlockSpec((MLA_WIDTH, tm), lambda i: (0, i))],
        out_shape=[jax.ShapeDtypeStruct((m, n), d) for n, d in zip(widths, dtypes)]
                  + [jax.ShapeDtypeStruct((MLA_WIDTH, m), bf16)],
        compiler_params=_params(("parallel",), 40),
        name="mla_prep",
    )(cq, ckv, kr, *tabs, gqn, gkv, wuq, wuk, wuvt, gq, gk)


def _ssd_body(xbc_ref, z_ref, dtt_ref, hist_ref, h0_ref, cw_ref, cb_ref, bias_c_ref,
              alog_c_ref, dsk_ref, ng_ref, tri_ref, y_ref, hout_ref, xp_ref, hs_ref, *, valid_from):
    c = pl.program_id(1)
    q = SSD_CHUNK

    @pl.when(c == 0)
    def _():
        xp_ref[0:SUBLANE, :] = hist_ref[...]
        hs_ref[...] = h0_ref[0]

    xbc = xbc_ref[...]
    xp_ref[SUBLANE:SUBLANE + q, :] = xbc
    conv = cb_ref[...]
    for k in range(CONV_W):
        lo = SUBLANE - (CONV_W - 1) + k
        conv = conv + xp_ref[lo:lo + q, :] * cw_ref[k:k + 1, :]
    xp_ref[0:SUBLANE, :] = xbc[q - SUBLANE:q, :]
    xa = _silu(conv)
    xs = xa[:, :SSD_WIDTH]
    bm = xa[:, SSD_WIDTH:SSD_WIDTH + LANE]
    cm = xa[:, SSD_WIDTH + LANE:]

    rows = lax.broadcasted_iota(jnp.int32, (q, q), 0)
    cols = lax.broadcasted_iota(jnp.int32, (q, q), 1)
    lane = lax.broadcasted_iota(jnp.int32, (1, LANE), 1)
    low = lane < D_STATE

    dtr = _softplus(dtt_ref[...] + bias_c_ref[...])
    if valid_from:
        dtr = jnp.where(lax.broadcasted_iota(jnp.int32, (16, q), 1) >= valid_from, dtr, 0.0)
    dar = dtr * -jnp.exp(alog_c_ref[...])
    tri = tri_ref[...]
    dtc = dtr.T
    acs_r = lax.dot_general(dar, tri, _NT, precision=_HI, preferred_element_type=f32)
    acs_c = jnp.dot(tri, dar.T, precision=_HI, preferred_element_type=f32)
    w_c = jnp.exp(acs_c[q - 1:q, :] - acs_c) * dtc
    e_c = jnp.exp(acs_c)
    cd_r = jnp.exp(acs_r[:, q - 1:q])

    causal = cols <= rows
    bb = bm.astype(bf16)
    dsk = dsk_ref[...]
    ys = []
    for g in range(SSD_GROUPS):
        gmask = (lane >= g * D_STATE) & (lane < (g + 1) * D_STATE)
        cg = jnp.where(gmask, cm, 0.0).astype(bf16)
        cb_g = _dot_nt(cg, bb)
        for pi in range(2):
            i = 2 * g + pi
            xpair = xs[:, i * LANE:(i + 1) * LANE]
            ypair = jnp.zeros((q, LANE), f32)
            for hh in range(2):
                h = 2 * i + hh
                seg = acs_c[:, h:h + 1] - acs_r[h:h + 1, :]
                lmat = jnp.exp(jnp.where(causal, seg, -jnp.inf))
                sc = (cb_g * lmat * dtr[h:h + 1, :]).astype(bf16)
                xh = jnp.where(low if hh == 0 else jnp.logical_not(low), xpair, 0.0).astype(bf16)
                ypair = ypair + _dot(sc, xh)
            h0, h1 = 2 * i, 2 * i + 1
            wp = jnp.where(low, w_c[:, h0:h0 + 1], w_c[:, h1:h1 + 1])
            ep = jnp.where(low, e_c[:, h0:h0 + 1], e_c[:, h1:h1 + 1])
            hst = hs_ref[i]
            yoff = _dot_nt(cg, hst.astype(bf16)) * ep
            st = _dot_tn((xpair * wp).astype(bf16), bb)
            cdb = jnp.concatenate([jnp.broadcast_to(cd_r[h0:h0 + 1, :], (SSD_HEAD_DIM, LANE)),
                                   jnp.broadcast_to(cd_r[h1:h1 + 1, :], (SSD_HEAD_DIM, LANE))], axis=0)
            hs_ref[i] = hst * cdb + st
            ys.append(ypair + yoff + dsk[:, i * LANE:(i + 1) * LANE] * xpair)
    y = jnp.concatenate(ys, axis=1) * _silu(z_ref[...])
    gw = SSD_WIDTH // SSD_GROUPS
    outs = []
    for g in range(SSD_GROUPS):
        yg = y[:, g * gw:(g + 1) * gw]
        outs.append(yg * _rms(yg, gw))
    y_ref[...] = (jnp.concatenate(outs, axis=1) * ng_ref[...]).astype(y_ref.dtype)

    @pl.when(c == pl.num_programs(1) - 1)
    def _():
        hout_ref[0] = hs_ref[...]


def _ssd(xbc, z, dtt, hist_arr, hist_block, h0, consts, n_batch, n_chunks, row_block0, valid_from):
    q = SSD_CHUNK
    rb = lambda n: pl.BlockSpec((q, n), lambda b, c: (row_block0 + b * n_chunks + c, 0))
    cw, cb, bias_c, alog_c, dsk, ng, tri = consts
    return pl.pallas_call(
        functools.partial(_ssd_body, valid_from=valid_from),
        grid=(n_batch, n_chunks),
        in_specs=[rb(CONV_DIM), rb(SSD_WIDTH),
                  pl.BlockSpec((16, q), lambda b, c: (0, row_block0 + b * n_chunks + c)),
                  pl.BlockSpec((SUBLANE, CONV_DIM), lambda b, c: (hist_block, 0)),
                  _const_spec((1, 4, LANE, LANE)),
                  _const_spec((CONV_W, CONV_DIM)), _const_spec((1, CONV_DIM)),
                  _const_spec((16, 1)), _const_spec((16, 1)),
                  _const_spec((1, SSD_WIDTH)), _const_spec((1, SSD_WIDTH)), _const_spec((q, q))],
        out_specs=[pl.BlockSpec((q, SSD_WIDTH), lambda b, c: (b * n_chunks + c, 0)),
                   pl.BlockSpec((1, 4, LANE, LANE), lambda b, c: (b, 0, 0, 0))],
        out_shape=[jax.ShapeDtypeStruct((n_batch * n_chunks * q, SSD_WIDTH), bf16),
                   jax.ShapeDtypeStruct((n_batch, 4, LANE, LANE), f32)],
        scratch_shapes=[pltpu.VMEM((SUBLANE + q, CONV_DIM), f32), pltpu.VMEM((4, LANE, LANE), f32)],
        compiler_params=_params(("parallel", "arbitrary"), 32),
        name="ssd",
    )(xbc, z, dtt, hist_arr, h0, cw, cb, bias_c, alog_c, dsk, ng, tri)


def _flash_body(q_ref, k_ref, vt_ref, km_ref, vtm_ref, o_ref, *, tq, tk, nh):
    qi = pl.program_id(2)
    qs = [q_ref[:, h * HT:(h + 1) * HT] for h in range(nh)]
    drow = lax.broadcasted_iota(jnp.int32, (2 * V_DIM, 1), 0)
    vmasks = (drow < V_DIM, drow >= V_DIM)
    krow = lax.broadcasted_iota(jnp.int32, (tk + N_META, tq), 0)
    qcol = lax.broadcasted_iota(jnp.int32, (tk + N_META, tq), 1)

    def heads(kall, vtall, carries, mask=None):
        sts = [_dot_nt(kall[:, h * HT:(h + 1) * HT], qs[h]) for h in range(nh)]
        out = []
        for h in range(nh):
            m, l, acct = carries[h]
            st = sts[h] if mask is None else jnp.where(mask, sts[h], -jnp.inf)
            m2 = jnp.maximum(m, jnp.max(st, axis=0, keepdims=True))
            pt = jnp.exp(st - m2)
            a = jnp.exp(m - m2)
            vtt = vtall[(h // 2) * LANE:(h // 2 + 1) * LANE, :]
            vz = jnp.where(vmasks[h % 2], vtt, jnp.zeros_like(vtt))
            out.append((m2, a * l + jnp.sum(pt, axis=0, keepdims=True), a * acct + _dot(vz, pt.astype(bf16))))
        return tuple(out)

    init = (jnp.full((1, tq), -jnp.inf, f32), jnp.zeros((1, tq), f32), jnp.zeros((2 * V_DIM, tq), f32))

    def body(j, carries):
        off = pl.multiple_of(j * tk, tk)
        return heads(k_ref[pl.ds(off, tk), :], vt_ref[:, pl.ds(off, tk)], carries)

    n_full = (qi * tq) // tk
    carries = lax.fori_loop(0, n_full, body, (init,) * nh)
    off = pl.multiple_of(n_full * tk, tk)
    k_last = jnp.concatenate([k_ref[pl.ds(off, tk), :], km_ref[...]], axis=0)
    vt_last = jnp.concatenate([vt_ref[:, pl.ds(off, tk)], vtm_ref[...]], axis=1)
    visible = (krow >= tk) | (krow <= qcol + (qi * tq - n_full * tk))
    res = heads(k_last, vt_last, carries, visible)
    for pr in range(nh // 2):
        (_, l0, acc0), (_, l1, acc1) = res[2 * pr], res[2 * pr + 1]
        o_ref[:, pr * LANE:(pr + 1) * LANE] = (acc0 / l0 + acc1 / l1).T.astype(o_ref.dtype)


def _flash(q, k, vt, k_small, vt_meta, meta_block, n_batch, seq, tq, tk, nh):
    assert tk % tq == 0 and seq % tk == 0 and nh % 2 == 0 and MLA_HEADS % nh == 0
    nq = seq // tq
    return pl.pallas_call(
        functools.partial(_flash_body, tq=tq, tk=tk, nh=nh),
        grid=(n_batch, MLA_HEADS // nh, nq),
        in_specs=[pl.BlockSpec((tq, nh * HT), lambda b, p, i: (b * nq + i, p)),
                  pl.BlockSpec((seq, nh * HT), lambda b, p, i: (b, p)),
                  pl.BlockSpec((nh * V_DIM, seq), lambda b, p, i: (p, b)),
                  pl.BlockSpec((N_META, nh * HT), lambda b, p, i: (meta_block, p)),
                  pl.BlockSpec((nh * V_DIM, N_META), lambda b, p, i: (p, 0))],
        out_specs=pl.BlockSpec((tq, nh * V_DIM), lambda b, p, i: (b * nq + i, p)),
        out_shape=jax.ShapeDtypeStruct((n_batch * seq, MLA_WIDTH), f32),
        compiler_params=_params(("parallel", "parallel", "arbitrary"), 40),
        name="flash",
    )(q, k, vt, k_small, vt_meta)


def _finish_body(x_ref, ssd_ref, att_ref, gm_ref, wo_ref, gf_ref, wg_ref, wu_ref, wd_ref, o_ref):
    att = att_ref[...]
    mla = (att * _rms(att, MLA_WIDTH) * gm_ref[...]).astype(bf16)
    h = x_ref[...] + (_dot(ssd_ref[...], wo_ref[0:SSD_WIDTH, :]) + _dot(mla, wo_ref[SSD_WIDTH:, :]))
    n = (h * _rms(h, D_MODEL) * gf_ref[...]).astype(bf16)
    ff = jnp.zeros_like(h)
    for c in range(N_FF_CHUNKS):
        cs = slice(c * FF_CHUNK, (c + 1) * FF_CHUNK)
        a = (_silu(_dot(n, wg_ref[:, cs])) * _dot(n, wu_ref[:, cs])).astype(bf16)
        ff = ff + _dot(a, wd_ref[cs, :])
    o_ref[...] = h + ff


def _finish(x, ssd, att, gm, wo, gf, wg, wu, wd, tm):
    m = x.shape[0]
    row = lambda n: pl.BlockSpec((tm, n), lambda i: (i, 0))
    return pl.pallas_call(
        _finish_body,
        grid=(m // tm,),
        in_specs=[row(D_MODEL), row(SSD_WIDTH), row(MLA_WIDTH), _const_spec((1, MLA_WIDTH)),
                  _const_spec((D_MODEL, D_MODEL)), _const_spec((1, D_MODEL)),
                  _const_spec((D_MODEL, D_FF)), _const_spec((D_MODEL, D_FF)), _const_spec((D_FF, D_MODEL))],
        out_specs=row(D_MODEL),
        out_shape=jax.ShapeDtypeStruct((m, D_MODEL), f32),
        compiler_params=_params(("parallel",), 56),
        name="finish",
    )(x, ssd, att, gm, wo, gf, wg, wu, wd)


def _sssd_prep_body(xbc_ref, sc_ref, cw_ref, cb_ref, xs_ref, xat_ref):
    conv = cb_ref[...]
    for k in range(CONV_W - 1):
        conv = conv + sc_ref[k] * cw_ref[k:k + 1, :]
    conv = conv + xbc_ref[...] * cw_ref[CONV_W - 1:CONV_W, :]
    xa = _silu(conv)
    xs_ref[...] = xa[:, :SSD_WIDTH]
    xat_ref[...] = xa.T


def _sssd_prep(xbc_small, sc, cw, cb, nseq):
    return pl.pallas_call(
        _sssd_prep_body,
        grid=(1,),
        in_specs=[pl.BlockSpec((nseq, CONV_DIM), lambda i: (0, 0)), _const_spec((CONV_W - 1, nseq, CONV_DIM)),
                  _const_spec((CONV_W, CONV_DIM)), _const_spec((1, CONV_DIM))],
        out_specs=[pl.BlockSpec((nseq, SSD_WIDTH), lambda i: (0, 0)), pl.BlockSpec((CONV_DIM, nseq), lambda i: (0, 0))],
        out_shape=[jax.ShapeDtypeStruct((nseq, SSD_WIDTH), f32), jax.ShapeDtypeStruct((CONV_DIM, nseq), f32)],
        compiler_params=_params(("arbitrary",), 32),
        name="sssd_prep",
    )(xbc_small, sc, cw, cb)


def _sssd_state_body(h0_ref, xst_ref, bt_ref, ct_ref, dtt_ref, bias_c_ref, alog_c_ref, hn_ref, yt_ref):
    h = pl.program_id(0)
    dt = _softplus(dtt_ref[pl.ds(h, 1), :] + bias_c_ref[pl.ds(h, 1), :])
    dec = jnp.exp(dt * -jnp.exp(alog_c_ref[pl.ds(h, 1), :]))
    bt, ct = bt_ref[...], ct_ref[...]

    def body(p, carry):
        xdt = xst_ref[pl.ds(p, 1), :] * dt
        hn = h0_ref[0, p] * dec + xdt * bt
        hn_ref[0, p] = hn
        yt_ref[pl.ds(p, 1), :] = jnp.sum(ct * hn, axis=0, keepdims=True)
        return carry

    lax.fori_loop(0, SSD_HEAD_DIM, body, 0, unroll=4)


def _sssd_state(h0t, xat, dtt_small, bias_c, alog_c, nseq):
    hpg = SSD_HEADS // SSD_GROUPS
    rows = lambda f: pl.BlockSpec((SSD_HEAD_DIM, nseq), f)
    hblk = pl.BlockSpec((1, SSD_HEAD_DIM, D_STATE, nseq), lambda h: (h, 0, 0, 0))
    return pl.pallas_call(
        _sssd_state_body,
        grid=(SSD_HEADS,),
        in_specs=[hblk, rows(lambda h: (h, 0)), rows(lambda h: (SSD_HEADS + h // hpg, 0)),
                  rows(lambda h: (SSD_HEADS + SSD_GROUPS + h // hpg, 0)),
                  pl.BlockSpec((16, nseq), lambda h: (0, 0)), _const_spec((16, 1)), _const_spec((16, 1))],
        out_specs=[hblk, rows(lambda h: (h, 0))],
        out_shape=[jax.ShapeDtypeStruct((SSD_HEADS, SSD_HEAD_DIM, D_STATE, nseq), f32),
                   jax.ShapeDtypeStruct((SSD_WIDTH, nseq), f32)],
        compiler_params=_params(("parallel",), 32),
        name="sssd_state",
    )(h0t, xat, xat, xat, dtt_small, bias_c, alog_c)


def _sssd_gate_body(yt_ref, xs_ref, z_ref, dsk_ref, ng_ref, o_ref):
    y = (yt_ref[...].T + dsk_ref[...] * xs_ref[...]) * _silu(z_ref[...])
    gw = SSD_WIDTH // SSD_GROUPS
    outs = []
    for g in range(SSD_GROUPS):
        yg = y[:, g * gw:(g + 1) * gw]
        outs.append(yg * _rms(yg, gw))
    o_ref[...] = (jnp.concatenate(outs, axis=1) * ng_ref[...]).astype(o_ref.dtype)


def _sssd_gate(yt, xs, z_small, dsk, ng, nseq):
    blk = pl.BlockSpec((nseq, SSD_WIDTH), lambda i: (0, 0))
    return pl.pallas_call(
        _sssd_gate_body,
        grid=(1,),
        in_specs=[pl.BlockSpec((SSD_WIDTH, nseq), lambda i: (0, 0)), blk, blk,
                  _const_spec((1, SSD_WIDTH)), _const_spec((1, SSD_WIDTH))],
        out_specs=blk,
        out_shape=jax.ShapeDtypeStruct((nseq, SSD_WIDTH), bf16),
        compiler_params=_params(("arbitrary",), 32),
        name="sssd_gate",
    )(yt, xs, z_small, dsk, ng)


def _absorb_body(q_ref, gk_ref, wabs_ref, qg_ref, a_ref):
    gk = gk_ref[...]
    for h in range(MLA_HEADS):
        sl = slice(h * HT, (h + 1) * HT)
        qg = q_ref[:, sl] * gk
        qg_ref[:, sl] = qg
        a_ref[:, h * KV_LORA:(h + 1) * KV_LORA] = jnp.dot(qg, wabs_ref[sl, :], precision=_HI, preferred_element_type=f32)
    a_ref[:, MLA_HEADS * KV_LORA:] = jnp.zeros((q_ref.shape[0], (16 - MLA_HEADS) * KV_LORA), f32)


def _absorb(q_small, gk, wabs, nseq):
    return pl.pallas_call(
        _absorb_body,
        grid=(1,),
        in_specs=[pl.BlockSpec((nseq, MLA_HEADS * HT), lambda i: (0, 0)), _const_spec((1, HT)),
                  _const_spec((MLA_HEADS * HT, KV_LORA))],
        out_specs=[pl.BlockSpec((nseq, MLA_HEADS * HT), lambda i: (0, 0)),
                   pl.BlockSpec((nseq, 16 * KV_LORA), lambda i: (0, 0))],
        out_shape=[jax.ShapeDtypeStruct((nseq, MLA_HEADS * HT), f32), jax.ShapeDtypeStruct((nseq, 16 * KV_LORA), f32)],
        compiler_params=_params(("arbitrary",), 32),
        name="absorb",
    )(q_small, gk, wabs)


def _decode_body(pt_ref, a_ref, c_ref, latn_ref, krnt_ref, wukt_ref, clat_ref, ckrt_ref, o_ref,
                 lat_buf, kr_buf, waug, latb, s_buf, sems, *, n_pages, page, ppt):
    b = pl.program_id(0)
    nb = pl.num_programs(0)
    slot = b % 2
    n_tiles = n_pages // ppt
    tile = ppt * page
    wrows = MLA_HEADS * QK_NOPE

    def start_page(seq, p, sl):
        pid = pt_ref[seq * n_pages + p]
        pltpu.make_async_copy(clat_ref.at[0, pid], lat_buf.at[sl, p], sems.at[0, sl]).start()
        pltpu.make_async_copy(ckrt_ref.at[0, pid], kr_buf.at[sl, p], sems.at[1, sl]).start()

    @pl.when(b == 0)
    def _():
        waug[0:wrows, :] = wukt_ref[...]

        def first(p, carry):
            start_page(0, p, 0)
            return carry

        lax.fori_loop(0, n_pages, first, 0)

    pltpu.make_async_copy(clat_ref.at[0, pl.ds(0, n_pages)], lat_buf.at[slot], sems.at[0, slot]).wait()
    pltpu.make_async_copy(ckrt_ref.at[0, pl.ds(0, n_pages)], kr_buf.at[slot], sems.at[1, slot]).wait()

    waug[wrows:wrows + 16, :] = a_ref[0, 0].astype(bf16)
    cb = c_ref[0].astype(bf16)

    def score_dots(lb, krt):
        kt = _dot_nt(waug[...], lb)
        return kt, _dot(cb, krt.astype(bf16)), krt

    def score_finish(kt, s2, krt):
        n = kt.shape[1]
        k3 = kt[0:wrows, :].reshape(MLA_HEADS, QK_NOPE, n)
        n2 = jnp.sum(k3 * k3, axis=1)
        kr2 = jnp.sum(krt * krt, axis=0, keepdims=True)
        return lax.rsqrt((n2 + kr2) * (1.0 / QK_DIM) + EPS) * (kt[wrows:wrows + MLA_HEADS, :] + s2)

    seq_next = jnp.minimum(b + 1, nb - 1)
    cw, cp = tile // DMA_CHUNKS, ppt // DMA_CHUNKS

    def body(j, carry):
        lb = lat_buf[slot, pl.ds(j * ppt, ppt)].reshape(tile, KV_LORA).astype(bf16)
        latb[j] = lb
        kr_pages = kr_buf[slot, pl.ds(j * ppt, ppt)]
        kt, s2, krt = score_dots(lb, jnp.concatenate([kr_pages[i] for i in range(ppt)], axis=-1))
        for c in range(DMA_CHUNKS):
            for pp in range(cp):
                start_page(seq_next, j * ppt + c * cp + pp, 1 - slot)
            cs = slice(c * cw, (c + 1) * cw)
            s_buf[j, :, cs] = score_finish(kt[:, cs], s2[:, cs], krt[:, cs])
        return carry

    lax.fori_loop(0, n_tiles, body, 0)
    first_row = lax.broadcasted_iota(jnp.int32, (page, KV_LORA), 0) == 0
    first_lane = lax.broadcasted_iota(jnp.int32, (1, page), 1) == 0
    lb_new = jnp.where(first_row, latn_ref[0], 0.0).astype(bf16)
    new_dots = score_dots(lb_new, jnp.where(first_lane, krnt_ref[0], 0.0))
    s_new = jnp.where(first_lane, score_finish(*new_dots), -jnp.inf)

    s_all = jnp.concatenate([s_buf[j] for j in range(n_tiles)], axis=-1)
    m = jnp.maximum(jnp.max(s_all, axis=-1, keepdims=True), jnp.max(s_new, axis=-1, keepdims=True))
    p_all = jnp.exp(s_all - m)
    p_new = jnp.exp(s_new - m)
    l = jnp.sum(p_all, axis=-1, keepdims=True) + jnp.sum(p_new, axis=-1, keepdims=True)
    acc = _dot(p_all.astype(bf16), latb[...].reshape(n_tiles * tile, KV_LORA)) + _dot(p_new.astype(bf16), lb_new)
    o_ref[0] = acc / l

    @pl.when(b == nb - 1)
    def _():
        pltpu.make_async_copy(clat_ref.at[0, pl.ds(0, n_pages)], lat_buf.at[1 - slot], sems.at[0, 1 - slot]).wait()
        pltpu.make_async_copy(ckrt_ref.at[0, pl.ds(0, n_pages)], kr_buf.at[1 - slot], sems.at[1, 1 - slot]).wait()


def _decode(page_table_flat, a, c, lat_new, kr_new_t, wukt, cache_lat, cache_kr_t, nseq, n_pages, page, ppt):
    assert n_pages % ppt == 0 and ppt % DMA_CHUNKS == 0
    wrows = MLA_HEADS * QK_NOPE
    grid_spec = pltpu.PrefetchScalarGridSpec(
        num_scalar_prefetch=1,
        grid=(nseq,),
        in_specs=[pl.BlockSpec((1, 1, 16, KV_LORA), lambda b, pt: (b, 0, 0, 0)),
                  pl.BlockSpec((1, MLA_HEADS, QK_ROPE), lambda b, pt: (b, 0, 0)),
                  pl.BlockSpec((1, 1, KV_LORA), lambda b, pt: (b, 0, 0)),
                  pl.BlockSpec((1, QK_ROPE, 1), lambda b, pt: (b, 0, 0)),
                  pl.BlockSpec((wrows, KV_LORA), lambda b, pt: (0, 0), pipeline_mode=pl.Buffered(1)),
                  pl.BlockSpec(memory_space=pl.ANY), pl.BlockSpec(memory_space=pl.ANY)],
        out_specs=pl.BlockSpec((1, MLA_HEADS, KV_LORA), lambda b, pt: (b, 0, 0)),
        scratch_shapes=[pltpu.VMEM((2, n_pages, page, KV_LORA), f32), pltpu.VMEM((2, n_pages, QK_ROPE, page), f32),
                        pltpu.VMEM((wrows + 16, KV_LORA), bf16),
                        pltpu.VMEM((n_pages // ppt, ppt * page, KV_LORA), bf16),
                        pltpu.VMEM((n_pages // ppt, MLA_HEADS, ppt * page), f32), pltpu.SemaphoreType.DMA((2, 2))],
    )
    return pl.pallas_call(
        functools.partial(_decode_body, n_pages=n_pages, page=page, ppt=ppt),
        grid_spec=grid_spec,
        out_shape=jax.ShapeDtypeStruct((nseq, MLA_HEADS, KV_LORA), f32),
        compiler_params=_params(("arbitrary",), 40),
        name="decode",
    )(page_table_flat, a, c, lat_new, kr_new_t, wukt, cache_lat, cache_kr_t)


def _uv_body(o_ref, wuv_ref, att_ref):
    for h in range(MLA_HEADS):
        oh = o_ref[:, h * KV_LORA:(h + 1) * KV_LORA].astype(bf16)
        att_ref[:, h * V_DIM:(h + 1) * V_DIM] = _dot(oh, wuv_ref[:, h * V_DIM:(h + 1) * V_DIM])


def _uv(o_flat, wuv):
    nseq = o_flat.shape[0]
    return pl.pallas_call(
        _uv_body,
        grid=(1,),
        in_specs=[pl.BlockSpec((nseq, MLA_HEADS * KV_LORA), lambda i: (0, 0)), _const_spec((KV_LORA, MLA_WIDTH))],
        out_specs=pl.BlockSpec((nseq, MLA_WIDTH), lambda i: (0, 0)),
        out_shape=jax.ShapeDtypeStruct((nseq, MLA_WIDTH), f32),
        compiler_params=_params(("arbitrary",), 32),
        name="uv",
    )(o_flat, wuv)


def _rope_tables(pos):
    inv = ROPE_THETA ** (-jnp.arange(ROPE_HALF, dtype=f32) * (2.0 / QK_ROPE))
    ang = pos.astype(f32)[:, None] * inv[None, :]
    cos, sin = jnp.cos(ang), jnp.sin(ang)
    n = pos.shape[0]
    one = jnp.ones((n, QK_NOPE), f32)
    z16 = jnp.zeros((n, ROPE_HALF), f32)
    z32 = jnp.zeros((n, HT - QK_DIM), f32)
    z64 = jnp.zeros((n, QK_NOPE), f32)
    tc = jnp.concatenate([one, cos, cos, z32], axis=1)
    ts1 = jnp.concatenate([z64, -sin, z16, z32], axis=1)
    ts2 = jnp.concatenate([z64, z16, sin, z32], axis=1)
    return tc, ts1, ts2


def _pad_lanes(x, n):
    return jnp.pad(x, ((0, 0), (0, n - x.shape[1])))


def kernel(x_prompt, x_sample, cache_kv_latent, cache_k_rope, state_conv, state_ssm, page_table, meta_tokens, attn_norm_g, w_in, conv_w, conv_b, dt_bias, a_log, d_skip, ssd_norm_g, q_norm_g, w_uq, kv_norm_g, w_ukv, q_head_g_nope, q_head_g_rope, k_head_g_nope, k_head_g_rope, mla_out_g, w_out, ffn_norm_g, w_gate, w_up, w_down):
    l = 0
    nb, seq, _ = x_prompt.shape
    nseq = x_sample.shape[0]
    n_pages, page = page_table.shape[1], cache_kv_latent.shape[2]
    past = n_pages * page
    n_small = 2 * LANE
    meta_lo = n_small - N_META

    s0, s1, s2, s3, s4 = (SSD_WIDTH, SSD_WIDTH + CONV_DIM, SSD_WIDTH + CONV_DIM + SSD_HEADS,
                          SSD_WIDTH + CONV_DIM + SSD_HEADS + Q_LORA, SSD_WIDTH + CONV_DIM + SSD_HEADS + Q_LORA + KV_LORA)
    wi = w_in[l]
    w_last = jnp.concatenate([_pad_lanes(wi[:, s1:s2], ROPE_LO), _pad_lanes(wi[:, s4:], HT - ROPE_LO)], axis=1)
    w_proj = jnp.concatenate([wi[:, :s1], wi[:, s2:s4], w_last], axis=1).astype(bf16)
    g_attn = attn_norm_g[l][None, :]
    wq3 = w_uq[l].reshape(Q_LORA, MLA_HEADS, QK_DIM)
    zq = jnp.zeros((Q_LORA, MLA_HEADS, QK_NOPE), f32)
    wq_partner = jnp.concatenate([zq, wq3[:, :, QK_NOPE + ROPE_HALF:], wq3[:, :, QK_NOPE:QK_NOPE + ROPE_HALF]], axis=2)
    pad_head = lambda w: jnp.pad(w, ((0, 0), (0, 0), (0, HT - QK_DIM))).reshape(Q_LORA, -1)
    wuq = jnp.concatenate([pad_head(wq3), pad_head(wq_partner)], axis=1).astype(bf16)
    wkv = w_ukv[l].reshape(KV_LORA, MLA_HEADS, QK_NOPE + V_DIM)
    wuk_f32 = jnp.pad(wkv[:, :, :QK_NOPE], ((0, 0), (0, 0), (0, HT - QK_NOPE))).reshape(KV_LORA, -1)
    wuk = wuk_f32.astype(bf16)
    wuv = wkv[:, :, QK_NOPE:].reshape(KV_LORA, MLA_WIDTH).astype(bf16)
    wukt = wkv[:, :, :QK_NOPE].reshape(KV_LORA, -1).T.astype(bf16)
    gq = _pad_lanes(jnp.concatenate([q_head_g_nope[l], q_head_g_rope[l], q_head_g_rope[l]])[None, :], HT) * ATTN_SCALE
    gk = _pad_lanes(jnp.concatenate([k_head_g_nope[l], k_head_g_rope[l], k_head_g_rope[l]])[None, :], HT)
    gqn, gkv = q_norm_g[l][None, :], kv_norm_g[l][None, :]
    cw, cb = conv_w[l], conv_b[l][None, :]
    bias_c = jnp.pad(dt_bias[l][:, None], ((0, 16 - SSD_HEADS), (0, 0)))
    alog_c = jnp.pad(a_log[l][:, None], ((0, 16 - SSD_HEADS), (0, 0)))
    dsk = jnp.repeat(d_skip[l], SSD_HEAD_DIM)[None, :]
    ng = ssd_norm_g[l][None, :]
    tri = jnp.tril(jnp.ones((SSD_CHUNK, SSD_CHUNK), f32))
    ssd_consts = (cw, cb, bias_c, alog_c, dsk, ng, tri)
    gm, gf = mla_out_g[l][None, :], ffn_norm_g[l][None, :]
    wo = w_out[l].astype(bf16)
    wg, wu, wd = w_gate[l].astype(bf16), w_up[l].astype(bf16), w_down[l].astype(bf16)

    xp = x_prompt.reshape(nb * seq, D_MODEL)
    xs_rows = x_sample[:, 0, :]
    x_small = jnp.concatenate([xs_rows, jnp.zeros((meta_lo - nseq, D_MODEL), f32), meta_tokens.astype(f32)], axis=0)

    z_p, xbc_p, cq_p, ckv_p, kr_p, dtt_p = _project(xp, g_attn, w_proj, 512)
    z_s, xbc_s, cq_s, ckv_s, kr_s, dtt_s = _project(x_small, g_attn, w_proj, n_small)

    tabs_p = _rope_tables(N_META + jnp.arange(seq))
    pos_small = jnp.concatenate([jnp.full((nseq,), past), jnp.zeros((meta_lo - nseq,), jnp.int32), jnp.arange(N_META)])
    tabs_s = _rope_tables(pos_small)
    tm = 512
    lat_p, kro_p, q_p, k_p, vt_p = _mla_prep(cq_p, ckv_p, kr_p, tabs_p, seq // tm, gqn, gkv, wuq, wuk, wuv.T, gq, gk, tm, bf16)
    lat_s, kro_s, q_s, k_s, vt_s = _mla_prep(cq_s, ckv_s, kr_s, tabs_s, 1, gqn, gkv, wuq, wuk, wuv.T, gq, gk, n_small, f32)

    zero_h = jnp.zeros((1, 4, LANE, LANE), f32)
    zero_hist = jnp.zeros((SUBLANE, CONV_DIM), f32)
    _, h_meta = _ssd(xbc_s, z_s, dtt_s, zero_hist, 0, zero_h, ssd_consts, 1, 1, 1, SSD_CHUNK - N_META)
    ssd_p, h_fin = _ssd(xbc_p, z_p, dtt_p, xbc_s, n_small // SUBLANE - 1, h_meta, ssd_consts, nb, seq // SSD_CHUNK, 0, 0)

    att_p = _flash(q_p, k_p, vt_p, k_s, vt_s[:, meta_lo:], n_small // N_META - 1, nb, seq, 512, 512, 4)
    y_prompt = _finish(xp, ssd_p, att_p, gm, wo, gf, wg, wu, wd, 512).reshape(nb, seq, D_MODEL)

    sc = jnp.transpose(state_conv[l], (1, 0, 2))
    xs_s, xat_s = _sssd_prep(xbc_s, sc, cw, cb, nseq)
    h_new_t, yt_s = _sssd_state(jnp.transpose(state_ssm[l], (1, 2, 3, 0)), xat_s, dtt_s, bias_c, alog_c, nseq)
    h_new = jnp.transpose(h_new_t, (3, 0, 1, 2))
    ssd_s = _sssd_gate(yt_s, xs_s, z_s, dsk, ng, nseq)
    qg, a_abs = _absorb(q_s, gk, wuk_f32.T, nseq)
    a_abs = a_abs.reshape(nseq, 1, 16, KV_LORA)
    c_abs = qg.reshape(nseq, MLA_HEADS, HT)[:, :, ROPE_LO:QK_DIM]
    lat_new = lat_s[:nseq][:, None, :]
    kr_new = kro_s[:nseq, ROPE_LO:QK_DIM][:, None, :]
    o_lat = _decode(page_table.reshape(-1), a_abs, c_abs, lat_new, jnp.swapaxes(kr_new, 1, 2), wukt,
                    cache_kv_latent, jnp.swapaxes(cache_k_rope, 2, 3), nseq, n_pages, page, DECODE_PAGES_PER_TILE)
    att_s = _uv(o_lat.reshape(nseq, MLA_HEADS * KV_LORA), wuv)
    y_sample = _finish(xs_rows, ssd_s, att_s, gm, wo, gf, wg, wu, wd, nseq)[:, None, :]

    def with_meta(small, main, width):
        meta = jnp.broadcast_to(small[meta_lo:][None], (nb, N_META, width))
        return jnp.concatenate([meta, main.reshape(nb, seq, width)], axis=1)[None]

    kv_latent_prompt = with_meta(lat_s, lat_p, KV_LORA)
    k_rope_prompt = with_meta(kro_s[:, ROPE_LO:QK_DIM], kro_p[:, ROPE_LO:QK_DIM], QK_ROPE)
    conv_prompt = xbc_p.reshape(nb, seq, CONV_DIM)[:, seq - (CONV_W - 1):][None]
    hf = h_fin.reshape(nb, 4, 2, SSD_HEAD_DIM, SSD_GROUPS, D_STATE)
    ssm_prompt = jnp.stack([hf[:, i, :, :, i // 2, :] for i in range(4)], axis=1).reshape(nb, SSD_HEADS, SSD_HEAD_DIM, D_STATE)[None]
    kv_latent_sample = lat_new[None]
    k_rope_sample = kr_new[None]
    conv_sample = jnp.concatenate([state_conv[l][:, 1:], xbc_s[:nseq][:, None, :]], axis=1)[None]
    ssm_sample = h_new[None]
    return (y_prompt, y_sample, kv_latent_prompt, k_rope_prompt, conv_prompt, ssm_prompt.astype(x_prompt.dtype),
            kv_latent_sample, k_rope_sample, conv_sample, ssm_sample.astype(state_ssm.dtype))
```

```python
import functools

import jax
import jax.numpy as jnp
from jax import lax
from jax.experimental import pallas as pl
from jax.experimental.pallas import tpu as pltpu

f32 = jnp.float32
bf16 = jnp.bfloat16

D_MODEL = 1024
N_META = 16
SSD_HEADS = 8
SSD_HEAD_DIM = 64
SSD_WIDTH = SSD_HEADS * SSD_HEAD_DIM
SSD_GROUPS = 2
D_STATE = 64
CONV_W = 4
CONV_DIM = SSD_WIDTH + 2 * SSD_GROUPS * D_STATE
SSD_CHUNK = 128
MLA_HEADS = 8
QK_NOPE = 64
QK_ROPE = 32
QK_DIM = QK_NOPE + QK_ROPE
V_DIM = 64
MLA_WIDTH = MLA_HEADS * V_DIM
Q_LORA = 384
KV_LORA = 256
ROPE_THETA = 10000.0
ATTN_SCALE = QK_DIM ** -0.5
D_FF = 2816
EPS = 1e-6

LANE = 128
SUBLANE = 8
HT = LANE
ROPE_LO = QK_NOPE
ROPE_HALF = QK_ROPE // 2
FF_CHUNK = 256
N_FF_CHUNKS = D_FF // FF_CHUNK
DECODE_PAGES_PER_TILE = 32
MiB = 1024 * 1024

PC_Z = 0
PC_XBC = PC_Z + SSD_WIDTH
PC_CQ = PC_XBC + CONV_DIM
PC_CKV = PC_CQ + Q_LORA
PC_KR = PC_CKV + KV_LORA
PC_END = PC_KR + LANE

_NT = (((1,), (1,)), ((), ()))
_TN = (((0,), (0,)), ((), ()))
_HI = lax.Precision.HIGHEST


def _dot(a, b):
    return jnp.dot(a, b, preferred_element_type=f32)


def _dot_nt(a, b):
    return lax.dot_general(a, b, _NT, preferred_element_type=f32)


def _dot_tn(a, b):
    return lax.dot_general(a, b, _TN, preferred_element_type=f32)


def _rms(x, n):
    return lax.rsqrt(jnp.sum(x * x, axis=-1, keepdims=True) * (1.0 / n) + EPS)


def _silu(x):
    return x * jax.nn.sigmoid(x)


def _softplus(x):
    return jnp.maximum(x, 0.0) + jnp.log1p(jnp.exp(-jnp.abs(x)))


def _const_spec(shape):
    nd = len(shape)
    return pl.BlockSpec(shape, lambda *_: (0,) * nd, pipeline_mode=pl.Buffered(1))


def _params(sem, vmem_mib):
    return pltpu.CompilerParams(dimension_semantics=sem, vmem_limit_bytes=vmem_mib * MiB)


def _proj_body(x_ref, g_ref, w_ref, z_ref, xbc_ref, cq_ref, ckv_ref, kr_ref, dtt_ref):
    x = x_ref[...]
    xn = (x * _rms(x, D_MODEL) * g_ref[...]).astype(bf16)
    z_ref[...] = _dot(xn, w_ref[:, PC_Z:PC_XBC])
    xbc_ref[...] = _dot(xn, w_ref[:, PC_XBC:PC_CQ])
    cq_ref[...] = _dot(xn, w_ref[:, PC_CQ:PC_CKV])
    ckv_ref[...] = _dot(xn, w_ref[:, PC_CKV:PC_KR])
    last = _dot(xn, w_ref[:, PC_KR:PC_END])
    lane = lax.broadcasted_iota(jnp.int32, (1, LANE), 1)
    kr_ref[...] = jnp.where(lane >= ROPE_LO, last, 0.0)
    dtt_ref[...] = last.T[0:16, :]


def _project(x, g, w, tm):
    m = x.shape[0]
    row = lambda n: pl.BlockSpec((tm, n), lambda i: (i, 0))
    widths = (SSD_WIDTH, CONV_DIM, Q_LORA, KV_LORA, LANE)
    return pl.pallas_call(
        _proj_body,
        grid=(m // tm,),
        in_specs=[row(D_MODEL), _const_spec((1, D_MODEL)), _const_spec((D_MODEL, PC_END))],
        out_specs=[row(n) for n in widths] + [pl.BlockSpec((16, tm), lambda i: (0, i))],
        out_shape=[jax.ShapeDtypeStruct((m, n), f32) for n in widths] + [jax.ShapeDtypeStruct((16, m), f32)],
        compiler_params=_params(("parallel",), 40),
        name="proj",
    )(x, g, w)


def _mla_prep_body(cq_ref, ckv_ref, kr_ref, tc_ref, ts1_ref, ts2_ref, gqn_ref, gkv_ref, wuq_ref, wuk_ref, wuv_ref,
                   gq_ref, gk_ref, lat_ref, kro_ref, q_ref, k_ref, vt_ref):
    tc, ts1, ts2 = tc_ref[...], ts1_ref[...], ts2_ref[...]
    tsw = ts1 + ts2

    def rope(x):
        return x * tc + pltpu.roll(x, LANE - ROPE_HALF, 1) * ts1 + pltpu.roll(x, ROPE_HALF, 1) * ts2

    ckv = ckv_ref[...]
    lat = ckv * _rms(ckv, KV_LORA) * gkv_ref[...]
    lat_ref[...] = lat
    kr = rope(kr_ref[...])
    kro_ref[...] = kr
    cq = cq_ref[...]
    cqn = (cq * _rms(cq, Q_LORA) * gqn_ref[...]).astype(bf16)
    latb = lat.astype(bf16)
    vt_ref[...] = _dot_nt(wuv_ref[...], latb).astype(vt_ref.dtype)
    gq, gk = gq_ref[...], gk_ref[...]
    for h in range(MLA_HEADS):
        sl = slice(h * HT, (h + 1) * HT)
        qh = _dot(cqn, wuq_ref[:, sl]) * tc + _dot(cqn, wuq_ref[:, MLA_HEADS * HT + h * HT:MLA_HEADS * HT + (h + 1) * HT]) * tsw
        q_ref[:, sl] = (qh * _rms(qh, QK_DIM) * gq).astype(q_ref.dtype)
        kh = _dot(latb, wuk_ref[:, sl]) + kr
        k_ref[:, sl] = (kh * _rms(kh, QK_DIM) * gk).astype(k_ref.dtype)


def _mla_prep(cq, ckv, kr, tabs, n_tab_blocks, gqn, gkv, wuq, wuk, wuvt, gq, gk, tm, q_dtype):
    m = cq.shape[0]
    row = lambda n: pl.BlockSpec((tm, n), lambda i: (i, 0))
    tab = pl.BlockSpec((tm, LANE), lambda i: (i % n_tab_blocks, 0))
    widths = (KV_LORA, LANE, MLA_HEADS * HT, MLA_HEADS * HT)
    dtypes = (f32, f32, q_dtype, bf16)
    return pl.pallas_call(
        _mla_prep_body,
        grid=(m // tm,),
        in_specs=[row(Q_LORA), row(KV_LORA), row(LANE), tab, tab, tab,
                  _const_spec((1, Q_LORA)), _const_spec((1, KV_LORA)),
                  _const_spec((Q_LORA, 2 * MLA_HEADS * HT)), _const_spec((KV_LORA, MLA_HEADS * HT)),
                  _const_spec((MLA_WIDTH, KV_LORA)), _const_spec((1, HT)), _const_spec((1, HT))],
        out_specs=[row(n) for n in widths] + [pl.BlockSpec((MLA_WIDTH, tm), lambda i: (0, i))],
        out_shape=[jax.ShapeDtypeStruct((m, n), d) for n, d in zip(widths, dtypes)]
                  + [jax.ShapeDtypeStruct((MLA_WIDTH, m), bf16)],
        compiler_params=_params(("parallel",), 40),
        name="mla_prep",
    )(cq, ckv, kr, *tabs, gqn, gkv, wuq, wuk, wuvt, gq, gk)


def _ssd_body(xbc_ref, z_ref, dtt_ref, hist_ref, h0_ref, cw_ref, cb_ref, bias_c_ref,
              alog_c_ref, dsk_ref, ng_ref, tri_ref, y_ref, hout_ref, xp_ref, hs_ref, *, valid_from):
    c = pl.program_id(1)
    q = SSD_CHUNK

    @pl.when(c == 0)
    def _():
        xp_ref[0:SUBLANE, :] = hist_ref[...]
        hs_ref[...] = h0_ref[0]

    xbc = xbc_ref[...]
    xp_ref[SUBLANE:SUBLANE + q, :] = xbc
    conv = cb_ref[...]
    for k in range(CONV_W):
        lo = SUBLANE - (CONV_W - 1) + k
        conv = conv + xp_ref[lo:lo + q, :] * cw_ref[k:k + 1, :]
    xp_ref[0:SUBLANE, :] = xbc[q - SUBLANE:q, :]
    xa = _silu(conv)
    xs = xa[:, :SSD_WIDTH]
    bm = xa[:, SSD_WIDTH:SSD_WIDTH + LANE]
    cm = xa[:, SSD_WIDTH + LANE:]

    rows = lax.broadcasted_iota(jnp.int32, (q, q), 0)
    cols = lax.broadcasted_iota(jnp.int32, (q, q), 1)
    lane = lax.broadcasted_iota(jnp.int32, (1, LANE), 1)
    low = lane < D_STATE

    dtr = _softplus(dtt_ref[...] + bias_c_ref[...])
    if valid_from:
        dtr = jnp.where(lax.broadcasted_iota(jnp.int32, (16, q), 1) >= valid_from, dtr, 0.0)
    dar = dtr * -jnp.exp(alog_c_ref[...])
    tri = tri_ref[...]
    dtc = dtr.T
    acs_r = lax.dot_general(dar, tri, _NT, precision=_HI, preferred_element_type=f32)
    acs_c = jnp.dot(tri, dar.T, precision=_HI, preferred_element_type=f32)
    w_c = jnp.exp(acs_c[q - 1:q, :] - acs_c) * dtc
    e_c = jnp.exp(acs_c)
    cd_r = jnp.exp(acs_r[:, q - 1:q])

    causal = cols <= rows
    bb = bm.astype(bf16)
    dsk = dsk_ref[...]
    ys = []
    for g in range(SSD_GROUPS):
        gmask = (lane >= g * D_STATE) & (lane < (g + 1) * D_STATE)
        cg = jnp.where(gmask, cm, 0.0).astype(bf16)
        cb_g = _dot_nt(cg, bb)
        for pi in range(2):
            i = 2 * g + pi
            xpair = xs[:, i * LANE:(i + 1) * LANE]
            ypair = jnp.zeros((q, LANE), f32)
            for hh in range(2):
                h = 2 * i + hh
                seg = acs_c[:, h:h + 1] - acs_r[h:h + 1, :]
                lmat = jnp.exp(jnp.where(causal, seg, -jnp.inf))
                sc = (cb_g * lmat * dtr[h:h + 1, :]).astype(bf16)
                xh = jnp.where(low if hh == 0 else jnp.logical_not(low), xpair, 0.0).astype(bf16)
                ypair = ypair + _dot(sc, xh)
            h0, h1 = 2 * i, 2 * i + 1
            wp = jnp.where(low, w_c[:, h0:h0 + 1], w_c[:, h1:h1 + 1])
            ep = jnp.where(low, e_c[:, h0:h0 + 1], e_c[:, h1:h1 + 1])
            hst = hs_ref[i]
            yoff = _dot_nt(cg, hst.astype(bf16)) * ep
            st = _dot_tn((xpair * wp).astype(bf16), bb)
            cdb = jnp.concatenate([jnp.broadcast_to(cd_r[h0:h0 + 1, :], (SSD_HEAD_DIM, LANE)),
                                   jnp.broadcast_to(cd_r[h1:h1 + 1, :], (SSD_HEAD_DIM, LANE))], axis=0)
            hs_ref[i] = hst * cdb + st
            ys.append(ypair + yoff + dsk[:, i * LANE:(i + 1) * LANE] * xpair)
    y = jnp.concatenate(ys, axis=1) * _silu(z_ref[...])
    gw = SSD_WIDTH // SSD_GROUPS
    outs = []
    for g in range(SSD_GROUPS):
        yg = y[:, g * gw:(g + 1) * gw]
        outs.append(yg * _rms(yg, gw))
    y_ref[...] = (jnp.concatenate(outs, axis=1) * ng_ref[...]).astype(y_ref.dtype)

    @pl.when(c == pl.num_programs(1) - 1)
    def _():
        hout_ref[0] = hs_ref[...]


def _ssd(xbc, z, dtt, hist_arr, hist_block, h0, consts, n_batch, n_chunks, row_block0, valid_from):
    q = SSD_CHUNK
    rb = lambda n: pl.BlockSpec((q, n), lambda b, c: (row_block0 + b * n_chunks + c, 0))
    cw, cb, bias_c, alog_c, dsk, ng, tri = consts
    return pl.pallas_call(
        functools.partial(_ssd_body, valid_from=valid_from),
        grid=(n_batch, n_chunks),
        in_specs=[rb(CONV_DIM), rb(SSD_WIDTH),
                  pl.BlockSpec((16, q), lambda b, c: (0, row_block0 + b * n_chunks + c)),
                  pl.BlockSpec((SUBLANE, CONV_DIM), lambda b, c: (hist_block, 0)),
                  _const_spec((1, 4, LANE, LANE)),
                  _const_spec((CONV_W, CONV_DIM)), _const_spec((1, CONV_DIM)),
                  _const_spec((16, 1)), _const_spec((16, 1)),
                  _const_spec((1, SSD_WIDTH)), _const_spec((1, SSD_WIDTH)), _const_spec((q, q))],
        out_specs=[pl.BlockSpec((q, SSD_WIDTH), lambda b, c: (b * n_chunks + c, 0)),
                   pl.BlockSpec((1, 4, LANE, LANE), lambda b, c: (b, 0, 0, 0))],
        out_shape=[jax.ShapeDtypeStruct((n_batch * n_chunks * q, SSD_WIDTH), bf16),
                   jax.ShapeDtypeStruct((n_batch, 4, LANE, LANE), f32)],
        scratch_shapes=[pltpu.VMEM((SUBLANE + q, CONV_DIM), f32), pltpu.VMEM((4, LANE, LANE), f32)],
        compiler_params=_params(("parallel", "arbitrary"), 32),
        name="ssd",
    )(xbc, z, dtt, hist_arr, h0, cw, cb, bias_c, alog_c, dsk, ng, tri)


def _flash_body(q_ref, k_ref, vt_ref, km_ref, vtm_ref, o_ref, *, tq, tk, nh):
    qi = pl.program_id(2)
    qs = [q_ref[:, h * HT:(h + 1) * HT] for h in range(nh)]
    drow = lax.broadcasted_iota(jnp.int32, (2 * V_DIM, 1), 0)
    vmasks = (drow < V_DIM, drow >= V_DIM)
    krow = lax.broadcasted_iota(jnp.int32, (tk + N_META, tq), 0)
    qcol = lax.broadcasted_iota(jnp.int32, (tk + N_META, tq), 1)

    def heads(kall, vtall, carries, mask=None):
        sts = [_dot_nt(kall[:, h * HT:(h + 1) * HT], qs[h]) for h in range(nh)]
        out = []
        for h in range(nh):
            m, l, acct = carries[h]
            st = sts[h] if mask is None else jnp.where(mask, sts[h], -jnp.inf)
            m2 = jnp.maximum(m, jnp.max(st, axis=0, keepdims=True))
            pt = jnp.exp(st - m2)
            a = jnp.exp(m - m2)
            vtt = vtall[(h // 2) * LANE:(h // 2 + 1) * LANE, :]
            vz = jnp.where(vmasks[h % 2], vtt, jnp.zeros_like(vtt))
            out.append((m2, a * l + jnp.sum(pt, axis=0, keepdims=True), a * acct + _dot(vz, pt.astype(bf16))))
        return tuple(out)

    init = (jnp.full((1, tq), -jnp.inf, f32), jnp.zeros((1, tq), f32), jnp.zeros((2 * V_DIM, tq), f32))

    def body(j, carries):
        off = pl.multiple_of(j * tk, tk)
        return heads(k_ref[pl.ds(off, tk), :], vt_ref[:, pl.ds(off, tk)], carries)

    n_full = (qi * tq) // tk
    carries = lax.fori_loop(0, n_full, body, (init,) * nh)
    off = pl.multiple_of(n_full * tk, tk)
    k_last = jnp.concatenate([k_ref[pl.ds(off, tk), :], km_ref[...]], axis=0)
    vt_last = jnp.concatenate([vt_ref[:, pl.ds(off, tk)], vtm_ref[...]], axis=1)
    visible = (krow >= tk) | (krow <= qcol + (qi * tq - n_full * tk))
    res = heads(k_last, vt_last, carries, visible)
    for pr in range(nh // 2):
        (_, l0, acc0), (_, l1, acc1) = res[2 * pr], res[2 * pr + 1]
        o_ref[:, pr * LANE:(pr + 1) * LANE] = (acc0 / l0 + acc1 / l1).T.astype(o_ref.dtype)


def _flash(q, k, vt, k_small, vt_meta, meta_block, n_batch, seq, tq, tk, nh):
    assert tk % tq == 0 and seq % tk == 0 and nh % 2 == 0 and MLA_HEADS % nh == 0
    nq = seq // tq
    return pl.pallas_call(
        functools.partial(_flash_body, tq=tq, tk=tk, nh=nh),
        grid=(n_batch, MLA_HEADS // nh, nq),
        in_specs=[pl.BlockSpec((tq, nh * HT), lambda b, p, i: (b * nq + i, p)),
                  pl.BlockSpec((seq, nh * HT), lambda b, p, i: (b, p)),
                  pl.BlockSpec((nh * V_DIM, seq), lambda b, p, i: (p, b)),
                  pl.BlockSpec((N_META, nh * HT), lambda b, p, i: (meta_block, p)),
                  pl.BlockSpec((nh * V_DIM, N_META), lambda b, p, i: (p, 0))],
        out_specs=pl.BlockSpec((tq, nh * V_DIM), lambda b, p, i: (b * nq + i, p)),
        out_shape=jax.ShapeDtypeStruct((n_batch * seq, MLA_WIDTH), f32),
        compiler_params=_params(("parallel", "parallel", "arbitrary"), 40),
        name="flash",
    )(q, k, vt, k_small, vt_meta)


def _finish_body(x_ref, ssd_ref, att_ref, gm_ref, wo_ref, gf_ref, wg_ref, wu_ref, wd_ref, o_ref):
    att = att_ref[...]
    mla = (att * _rms(att, MLA_WIDTH) * gm_ref[...]).astype(bf16)
    h = x_ref[...] + (_dot(ssd_ref[...], wo_ref[0:SSD_WIDTH, :]) + _dot(mla, wo_ref[SSD_WIDTH:, :]))
    n = (h * _rms(h, D_MODEL) * gf_ref[...]).astype(bf16)
    ff = jnp.zeros_like(h)
    for c in range(N_FF_CHUNKS):
        cs = slice(c * FF_CHUNK, (c + 1) * FF_CHUNK)
        a = (_silu(_dot(n, wg_ref[:, cs])) * _dot(n, wu_ref[:, cs])).astype(bf16)
        ff = ff + _dot(a, wd_ref[cs, :])
    o_ref[...] = h + ff


def _finish(x, ssd, att, gm, wo, gf, wg, wu, wd, tm):
    m = x.shape[0]
    row = lambda n: pl.BlockSpec((tm, n), lambda i: (i, 0))
    return pl.pallas_call(
        _finish_body,
        grid=(m // tm,),
        in_specs=[row(D_MODEL), row(SSD_WIDTH), row(MLA_WIDTH), _const_spec((1, MLA_WIDTH)),
                  _const_spec((D_MODEL, D_MODEL)), _const_spec((1, D_MODEL)),
                  _const_spec((D_MODEL, D_FF)), _const_spec((D_MODEL, D_FF)), _const_spec((D_FF, D_MODEL))],
        out_specs=row(D_MODEL),
        out_shape=jax.ShapeDtypeStruct((m, D_MODEL), f32),
        compiler_params=_params(("parallel",), 56),
        name="finish",
    )(x, ssd, att, gm, wo, gf, wg, wu, wd)


def _sssd_prep_body(xbc_ref, sc_ref, cw_ref, cb_ref, xs_ref, xat_ref):
    conv = cb_ref[...]
    for k in range(CONV_W - 1):
        conv = conv + sc_ref[k] * cw_ref[k:k + 1, :]
    conv = conv + xbc_ref[...] * cw_ref[CONV_W - 1:CONV_W, :]
    xa = _silu(conv)
    xs_ref[...] = xa[:, :SSD_WIDTH]
    xat_ref[...] = xa.T


def _sssd_prep(xbc_small, sc, cw, cb, nseq):
    return pl.pallas_call(
        _sssd_prep_body,
        grid=(1,),
        in_specs=[pl.BlockSpec((nseq, CONV_DIM), lambda i: (0, 0)), _const_spec((CONV_W - 1, nseq, CONV_DIM)),
                  _const_spec((CONV_W, CONV_DIM)), _const_spec((1, CONV_DIM))],
        out_specs=[pl.BlockSpec((nseq, SSD_WIDTH), lambda i: (0, 0)), pl.BlockSpec((CONV_DIM, nseq), lambda i: (0, 0))],
        out_shape=[jax.ShapeDtypeStruct((nseq, SSD_WIDTH), f32), jax.ShapeDtypeStruct((CONV_DIM, nseq), f32)],
        compiler_params=_params(("arbitrary",), 32),
        name="sssd_prep",
    )(xbc_small, sc, cw, cb)


def _sssd_state_body(h0_ref, xst_ref, bt_ref, ct_ref, dtt_ref, bias_c_ref, alog_c_ref, hn_ref, yt_ref):
    h = pl.program_id(0)
    dt = _softplus(dtt_ref[pl.ds(h, 1), :] + bias_c_ref[pl.ds(h, 1), :])
    dec = jnp.exp(dt * -jnp.exp(alog_c_ref[pl.ds(h, 1), :]))
    bt, ct = bt_ref[...], ct_ref[...]

    def body(p, carry):
        xdt = xst_ref[pl.ds(p, 1), :] * dt
        hn = h0_ref[0, p] * dec + xdt * bt
        hn_ref[0, p] = hn
        yt_ref[pl.ds(p, 1), :] = jnp.sum(ct * hn, axis=0, keepdims=True)
        return carry

    lax.fori_loop(0, SSD_HEAD_DIM, body, 0, unroll=4)


def _sssd_state(h0t, xat, dtt_small, bias_c, alog_c, nseq):
    hpg = SSD_HEADS // SSD_GROUPS
    rows = lambda f: pl.BlockSpec((SSD_HEAD_DIM, nseq), f)
    hblk = pl.BlockSpec((1, SSD_HEAD_DIM, D_STATE, nseq), lambda h: (h, 0, 0, 0))
    return pl.pallas_call(
        _sssd_state_body,
        grid=(SSD_HEADS,),
        in_specs=[hblk, rows(lambda h: (h, 0)), rows(lambda h: (SSD_HEADS + h // hpg, 0)),
                  rows(lambda h: (SSD_HEADS + SSD_GROUPS + h // hpg, 0)),
                  pl.BlockSpec((16, nseq), lambda h: (0, 0)), _const_spec((16, 1)), _const_spec((16, 1))],
        out_specs=[hblk, rows(lambda h: (h, 0))],
        out_shape=[jax.ShapeDtypeStruct((SSD_HEADS, SSD_HEAD_DIM, D_STATE, nseq), f32),
                   jax.ShapeDtypeStruct((SSD_WIDTH, nseq), f32)],
        compiler_params=_params(("parallel",), 32),
        name="sssd_state",
    )(h0t, xat, xat, xat, dtt_small, bias_c, alog_c)


def _sssd_gate_body(yt_ref, xs_ref, z_ref, dsk_ref, ng_ref, o_ref):
    y = (yt_ref[...].T + dsk_ref[...] * xs_ref[...]) * _silu(z_ref[...])
    gw = SSD_WIDTH // SSD_GROUPS
    outs = []
    for g in range(SSD_GROUPS):
        yg = y[:, g * gw:(g + 1) * gw]
        outs.append(yg * _rms(yg, gw))
    o_ref[...] = (jnp.concatenate(outs, axis=1) * ng_ref[...]).astype(o_ref.dtype)


def _sssd_gate(yt, xs, z_small, dsk, ng, nseq):
    blk = pl.BlockSpec((nseq, SSD_WIDTH), lambda i: (0, 0))
    return pl.pallas_call(
        _sssd_gate_body,
        grid=(1,),
        in_specs=[pl.BlockSpec((SSD_WIDTH, nseq), lambda i: (0, 0)), blk, blk,
                  _const_spec((1, SSD_WIDTH)), _const_spec((1, SSD_WIDTH))],
        out_specs=blk,
        out_shape=jax.ShapeDtypeStruct((nseq, SSD_WIDTH), bf16),
        compiler_params=_params(("arbitrary",), 32),
        name="sssd_gate",
    )(yt, xs, z_small, dsk, ng)


def _absorb_body(q_ref, gk_ref, wabs_ref, lat_ref, kr_ref, wuk_ref, qg_ref, a_ref, snew_ref):
    gk = gk_ref[...]
    latb = lat_ref[...].astype(bf16)
    latf = latb.astype(f32)
    kr = kr_ref[...]
    krf = kr.astype(bf16).astype(f32)
    kr2 = jnp.sum(kr * kr, axis=-1, keepdims=True)
    s_new = []
    for h in range(MLA_HEADS):
        sl = slice(h * HT, (h + 1) * HT)
        qg = q_ref[:, sl] * gk
        qg_ref[:, sl] = qg
        a = jnp.dot(qg, wabs_ref[sl, :], precision=_HI, preferred_element_type=f32)
        a_ref[:, h * KV_LORA:(h + 1) * KV_LORA] = a
        kn = _dot(latb, wuk_ref[:, sl])
        n2 = jnp.sum(kn * kn, axis=-1, keepdims=True)
        s12 = (jnp.sum(a.astype(bf16).astype(f32) * latf, axis=-1, keepdims=True)
               + jnp.sum(qg.astype(bf16).astype(f32) * krf, axis=-1, keepdims=True))
        s_new.append(lax.rsqrt((n2 + kr2) * (1.0 / QK_DIM) + EPS) * s12)
    a_ref[:, MLA_HEADS * KV_LORA:] = jnp.zeros((q_ref.shape[0], (16 - MLA_HEADS) * KV_LORA), f32)
    snew_ref[...] = jnp.concatenate(s_new, axis=1)


def _absorb(q_small, gk, wabs, lat_small, kr_small, wuk, nseq):
    blk = lambda n: pl.BlockSpec((nseq, n), lambda i: (0, 0))
    return pl.pallas_call(
        _absorb_body,
        grid=(1,),
        in_specs=[blk(MLA_HEADS * HT), _const_spec((1, HT)), _const_spec((MLA_HEADS * HT, KV_LORA)),
                  blk(KV_LORA), blk(HT), _const_spec((KV_LORA, MLA_HEADS * HT))],
        out_specs=[blk(MLA_HEADS * HT), blk(16 * KV_LORA), blk(MLA_HEADS)],
        out_shape=[jax.ShapeDtypeStruct((nseq, MLA_HEADS * HT), f32), jax.ShapeDtypeStruct((nseq, 16 * KV_LORA), f32),
                   jax.ShapeDtypeStruct((nseq, MLA_HEADS), f32)],
        compiler_params=_params(("arbitrary",), 32),
        name="absorb",
    )(q_small, gk, wabs, lat_small, kr_small, wuk)


def _decode_body(pt_ref, a_ref, c_ref, latn_ref, snew_ref, wukt_ref, clat_ref, ckrt_ref, o_ref,
                 lat_buf, kr_buf, waug, latb, s_buf, sems, *, n_pages, page, ppt):
    b = pl.program_id(0)
    nb = pl.num_programs(0)
    slot = b % 2
    n_tiles = n_pages // ppt
    tile = ppt * page
    wrows = MLA_HEADS * QK_NOPE

    def start_page(seq, p, sl):
        pid = pt_ref[seq * n_pages + p]
        pltpu.make_async_copy(clat_ref.at[0, pid], lat_buf.at[sl, p], sems.at[0, sl]).start()
        pltpu.make_async_copy(ckrt_ref.at[0, pid], kr_buf.at[sl, p], sems.at[1, sl]).start()

    @pl.when(b == 0)
    def _():
        waug[0:wrows, :] = wukt_ref[...]

        def first(p, carry):
            start_page(0, p, 0)
            return carry

        lax.fori_loop(0, n_pages, first, 0)

    pltpu.make_async_copy(clat_ref.at[0, pl.ds(0, n_pages)], lat_buf.at[slot], sems.at[0, slot]).wait()
    pltpu.make_async_copy(ckrt_ref.at[0, pl.ds(0, n_pages)], kr_buf.at[slot], sems.at[1, slot]).wait()

    waug[wrows:wrows + 16, :] = a_ref[0, 0].astype(bf16)
    cb = c_ref[0].astype(bf16)

    def score_dots(lb, krt):
        kt = _dot_nt(waug[...], lb)
        return kt, _dot(cb, krt.astype(bf16)), krt

    def score_finish(kt, s2, krt):
        n = kt.shape[1]
        k3 = kt[0:wrows, :].reshape(MLA_HEADS, QK_NOPE, n)
        n2 = jnp.sum(k3 * k3, axis=1)
        kr2 = jnp.sum(krt * krt, axis=0, keepdims=True)
        return lax.rsqrt((n2 + kr2) * (1.0 / QK_DIM) + EPS) * (kt[wrows:wrows + MLA_HEADS, :] + s2)

    def body(j, carry):
        @pl.when(b + 1 < nb)
        def _():
            for pp in range(ppt):
                start_page(b + 1, j * ppt + pp, 1 - slot)

        lb = lat_buf[slot, pl.ds(j * ppt, ppt)].reshape(tile, KV_LORA).astype(bf16)
        latb[j] = lb
        kr_pages = kr_buf[slot, pl.ds(j * ppt, ppt)]
        s_buf[j] = score_finish(*score_dots(lb, jnp.concatenate([kr_pages[i] for i in range(ppt)], axis=-1)))
        return carry

    lax.fori_loop(0, n_tiles, body, 0)
    s_new = snew_ref[0]
    s_all = jnp.concatenate([s_buf[j] for j in range(n_tiles)], axis=-1)
    m = jnp.maximum(jnp.max(s_all, axis=-1, keepdims=True), s_new)
    p_all = jnp.exp(s_all - m)
    p_new = jnp.exp(s_new - m)
    l = jnp.sum(p_all, axis=-1, keepdims=True) + p_new
    acc = _dot(p_all.astype(bf16), latb[...].reshape(n_tiles * tile, KV_LORA)) + p_new * latn_ref[0]
    o_ref[0] = acc / l


def _decode(page_table_flat, a, c, lat_new, s_new, wukt, cache_lat, cache_kr_t, nseq, n_pages, page, ppt):
    assert n_pages % ppt == 0
    wrows = MLA_HEADS * QK_NOPE
    grid_spec = pltpu.PrefetchScalarGridSpec(
        num_scalar_prefetch=1,
        grid=(nseq,),
        in_specs=[pl.BlockSpec((1, 1, 16, KV_LORA), lambda b, pt: (b, 0, 0, 0)),
                  pl.BlockSpec((1, MLA_HEADS, QK_ROPE), lambda b, pt: (b, 0, 0)),
                  pl.BlockSpec((1, 1, KV_LORA), lambda b, pt: (b, 0, 0)),
                  pl.BlockSpec((1, MLA_HEADS, 1), lambda b, pt: (b, 0, 0)),
                  pl.BlockSpec((wrows, KV_LORA), lambda b, pt: (0, 0), pipeline_mode=pl.Buffered(1)),
                  pl.BlockSpec(memory_space=pl.ANY), pl.BlockSpec(memory_space=pl.ANY)],
        out_specs=pl.BlockSpec((1, MLA_HEADS, KV_LORA), lambda b, pt: (b, 0, 0)),
        scratch_shapes=[pltpu.VMEM((2, n_pages, page, KV_LORA), f32), pltpu.VMEM((2, n_pages, QK_ROPE, page), f32),
                        pltpu.VMEM((wrows + 16, KV_LORA), bf16),
                        pltpu.VMEM((n_pages // ppt, ppt * page, KV_LORA), bf16),
                        pltpu.VMEM((n_pages // ppt, MLA_HEADS, ppt * page), f32), pltpu.SemaphoreType.DMA((2, 2))],
    )
    return pl.pallas_call(
        functools.partial(_decode_body, n_pages=n_pages, page=page, ppt=ppt),
        grid_spec=grid_spec,
        out_shape=jax.ShapeDtypeStruct((nseq, MLA_HEADS, KV_LORA), f32),
        compiler_params=_params(("arbitrary",), 40),
        name="decode",
    )(page_table_flat, a, c, lat_new, s_new, wukt, cache_lat, cache_kr_t)


def _uv_body(o_ref, wuv_ref, att_ref):
    for h in range(MLA_HEADS):
        oh = o_ref[:, h * KV_LORA:(h + 1) * KV_LORA].astype(bf16)
        att_ref[:, h * V_DIM:(h + 1) * V_DIM] = _dot(oh, wuv_ref[:, h * V_DIM:(h + 1) * V_DIM])


def _uv(o_flat, wuv):
    nseq = o_flat.shape[0]
    return pl.pallas_call(
        _uv_body,
        grid=(1,),
        in_specs=[pl.BlockSpec((nseq, MLA_HEADS * KV_LORA), lambda i: (0, 0)), _const_spec((KV_LORA, MLA_WIDTH))],
        out_specs=pl.BlockSpec((nseq, MLA_WIDTH), lambda i: (0, 0)),
        out_shape=jax.ShapeDtypeStruct((nseq, MLA_WIDTH), f32),
        compiler_params=_params(("arbitrary",), 32),
        name="uv",
    )(o_flat, wuv)


def _rope_tables(pos):
    inv = ROPE_THETA ** (-jnp.arange(ROPE_HALF, dtype=f32) * (2.0 / QK_ROPE))
    ang = pos.astype(f32)[:, None] * inv[None, :]
    cos, sin = jnp.cos(ang), jnp.sin(ang)
    n = pos.shape[0]
    one = jnp.ones((n, QK_NOPE), f32)
    z16 = jnp.zeros((n, ROPE_HALF), f32)
    z32 = jnp.zeros((n, HT - QK_DIM), f32)
    z64 = jnp.zeros((n, QK_NOPE), f32)
    tc = jnp.concatenate([one, cos, cos, z32], axis=1)
    ts1 = jnp.concatenate([z64, -sin, z16, z32], axis=1)
    ts2 = jnp.concatenate([z64, z16, sin, z32], axis=1)
    return tc, ts1, ts2


def _pad_lanes(x, n):
    return jnp.pad(x, ((0, 0), (0, n - x.shape[1])))


def kernel(x_prompt, x_sample, cache_kv_latent, cache_k_rope, state_conv, state_ssm, page_table, meta_tokens, attn_norm_g, w_in, conv_w, conv_b, dt_bias, a_log, d_skip, ssd_norm_g, q_norm_g, w_uq, kv_norm_g, w_ukv, q_head_g_nope, q_head_g_rope, k_head_g_nope, k_head_g_rope, mla_out_g, w_out, ffn_norm_g, w_gate, w_up, w_down):
    l = 0
    nb, seq, _ = x_prompt.shape
    nseq = x_sample.shape[0]
    n_pages, page = page_table.shape[1], cache_kv_latent.shape[2]
    past = n_pages * page
    n_small = 2 * LANE
    meta_lo = n_small - N_META

    s0, s1, s2, s3, s4 = (SSD_WIDTH, SSD_WIDTH + CONV_DIM, SSD_WIDTH + CONV_DIM + SSD_HEADS,
                          SSD_WIDTH + CONV_DIM + SSD_HEADS + Q_LORA, SSD_WIDTH + CONV_DIM + SSD_HEADS + Q_LORA + KV_LORA)
    wi = w_in[l]
    w_last = jnp.concatenate([_pad_lanes(wi[:, s1:s2], ROPE_LO), _pad_lanes(wi[:, s4:], HT - ROPE_LO)], axis=1)
    w_proj = jnp.concatenate([wi[:, :s1], wi[:, s2:s4], w_last], axis=1).astype(bf16)
    g_attn = attn_norm_g[l][None, :]
    wq3 = w_uq[l].reshape(Q_LORA, MLA_HEADS, QK_DIM)
    zq = jnp.zeros((Q_LORA, MLA_HEADS, QK_NOPE), f32)
    wq_partner = jnp.concatenate([zq, wq3[:, :, QK_NOPE + ROPE_HALF:], wq3[:, :, QK_NOPE:QK_NOPE + ROPE_HALF]], axis=2)
    pad_head = lambda w: jnp.pad(w, ((0, 0), (0, 0), (0, HT - QK_DIM))).reshape(Q_LORA, -1)
    wuq = jnp.concatenate([pad_head(wq3), pad_head(wq_partner)], axis=1).astype(bf16)
    wkv = w_ukv[l].reshape(KV_LORA, MLA_HEADS, QK_NOPE + V_DIM)
    wuk_f32 = jnp.pad(wkv[:, :, :QK_NOPE], ((0, 0), (0, 0), (0, HT - QK_NOPE))).reshape(KV_LORA, -1)
    wuk = wuk_f32.astype(bf16)
    wuv = wkv[:, :, QK_NOPE:].reshape(KV_LORA, MLA_WIDTH).astype(bf16)
    wukt = wkv[:, :, :QK_NOPE].reshape(KV_LORA, -1).T.astype(bf16)
    gq = _pad_lanes(jnp.concatenate([q_head_g_nope[l], q_head_g_rope[l], q_head_g_rope[l]])[None, :], HT) * ATTN_SCALE
    gk = _pad_lanes(jnp.concatenate([k_head_g_nope[l], k_head_g_rope[l], k_head_g_rope[l]])[None, :], HT)
    gqn, gkv = q_norm_g[l][None, :], kv_norm_g[l][None, :]
    cw, cb = conv_w[l], conv_b[l][None, :]
    bias_c = jnp.pad(dt_bias[l][:, None], ((0, 16 - SSD_HEADS), (0, 0)))
    alog_c = jnp.pad(a_log[l][:, None], ((0, 16 - SSD_HEADS), (0, 0)))
    dsk = jnp.repeat(d_skip[l], SSD_HEAD_DIM)[None, :]
    ng = ssd_norm_g[l][None, :]
    tri = jnp.tril(jnp.ones((SSD_CHUNK, SSD_CHUNK), f32))
    ssd_consts = (cw, cb, bias_c, alog_c, dsk, ng, tri)
    gm, gf = mla_out_g[l][None, :], ffn_norm_g[l][None, :]
    wo = w_out[l].astype(bf16)
    wg, wu, wd = w_gate[l].astype(bf16), w_up[l].astype(bf16), w_down[l].astype(bf16)

    xp = x_prompt.reshape(nb * seq, D_MODEL)
    xs_rows = x_sample[:, 0, :]
    x_small = jnp.concatenate([xs_rows, jnp.zeros((meta_lo - nseq, D_MODEL), f32), meta_tokens.astype(f32)], axis=0)

    z_p, xbc_p, cq_p, ckv_p, kr_p, dtt_p = _project(xp, g_attn, w_proj, 512)
    z_s, xbc_s, cq_s, ckv_s, kr_s, dtt_s = _project(x_small, g_attn, w_proj, n_small)

    tabs_p = _rope_tables(N_META + jnp.arange(seq))
    pos_small = jnp.concatenate([jnp.full((nseq,), past), jnp.zeros((meta_lo - nseq,), jnp.int32), jnp.arange(N_META)])
    tabs_s = _rope_tables(pos_small)
    tm = 512
    lat_p, kro_p, q_p, k_p, vt_p = _mla_prep(cq_p, ckv_p, kr_p, tabs_p, seq // tm, gqn, gkv, wuq, wuk, wuv.T, gq, gk, tm, bf16)
    lat_s, kro_s, q_s, k_s, vt_s = _mla_prep(cq_s, ckv_s, kr_s, tabs_s, 1, gqn, gkv, wuq, wuk, wuv.T, gq, gk, n_small, f32)

    zero_h = jnp.zeros((1, 4, LANE, LANE), f32)
    zero_hist = jnp.zeros((SUBLANE, CONV_DIM), f32)
    _, h_meta = _ssd(xbc_s, z_s, dtt_s, zero_hist, 0, zero_h, ssd_consts, 1, 1, 1, SSD_CHUNK - N_META)
    ssd_p, h_fin = _ssd(xbc_p, z_p, dtt_p, xbc_s, n_small // SUBLANE - 1, h_meta, ssd_consts, nb, seq // SSD_CHUNK, 0, 0)

    att_p = _flash(q_p, k_p, vt_p, k_s, vt_s[:, meta_lo:], n_small // N_META - 1, nb, seq, 512, 512, 4)
    y_prompt = _finish(xp, ssd_p, att_p, gm, wo, gf, wg, wu, wd, 512).reshape(nb, seq, D_MODEL)

    sc = jnp.transpose(state_conv[l], (1, 0, 2))
    xs_s, xat_s = _sssd_prep(xbc_s, sc, cw, cb, nseq)
    h_new_t, yt_s = _sssd_state(jnp.transpose(state_ssm[l], (1, 2, 3, 0)), xat_s, dtt_s, bias_c, alog_c, nseq)
    h_new = jnp.transpose(h_new_t, (3, 0, 1, 2))
    ssd_s = _sssd_gate(yt_s, xs_s, z_s, dsk, ng, nseq)
    qg, a_abs, s_new = _absorb(q_s, gk, wuk_f32.T, lat_s, kro_s, wuk, nseq)
    a_abs = a_abs.reshape(nseq, 1, 16, KV_LORA)
    c_abs = qg.reshape(nseq, MLA_HEADS, HT)[:, :, ROPE_LO:QK_DIM]
    lat_new = lat_s[:nseq][:, None, :]
    kr_new = kro_s[:nseq, ROPE_LO:QK_DIM][:, None, :]
    o_lat = _decode(page_table.reshape(-1), a_abs, c_abs, lat_new, s_new[:, :, None], wukt,
                    cache_kv_latent, jnp.swapaxes(cache_k_rope, 2, 3), nseq, n_pages, page, DECODE_PAGES_PER_TILE)
    att_s = _uv(o_lat.reshape(nseq, MLA_HEADS * KV_LORA), wuv)
    y_sample = _finish(xs_rows, ssd_s, att_s, gm, wo, gf, wg, wu, wd, nseq)[:, None, :]

    def with_meta(small, main, width):
        meta = jnp.broadcast_to(small[meta_lo:][None], (nb, N_META, width))
        return jnp.concatenate([meta, main.reshape(nb, seq, width)], axis=1)[None]

    kv_latent_prompt = with_meta(lat_s, lat_p, KV_LORA)
    k_rope_prompt = with_meta(kro_s[:, ROPE_LO:QK_DIM], kro_p[:, ROPE_LO:QK_DIM], QK_ROPE)
    conv_prompt = xbc_p.reshape(nb, seq, CONV_DIM)[:, seq - (CONV_W - 1):][None]
    hf = h_fin.reshape(nb, 4, 2, SSD_HEAD_DIM, SSD_GROUPS, D_STATE)
    ssm_prompt = jnp.stack([hf[:, i, :, :, i // 2, :] for i in range(4)], axis=1).reshape(nb, SSD_HEADS, SSD_HEAD_DIM, D_STATE)[None]
    kv_latent_sample = lat_new[None]
    k_rope_sample = kr_new[None]
    conv_sample = jnp.concatenate([state_conv[l][:, 1:], xbc_s[:nseq][:, None, :]], axis=1)[None]
    ssm_sample = h_new[None]
    return (y_prompt, y_sample, kv_latent_prompt, k_rope_prompt, conv_prompt, ssm_prompt.astype(x_prompt.dtype),
            kv_latent_sample, k_rope_sample, conv_sample, ssm_sample.astype(state_ssm.dtype))
```

```python
import functools

import jax
import jax.numpy as jnp
from jax import lax
from jax.experimental import pallas as pl
from jax.experimental.pallas import tpu as pltpu

f32 = jnp.float32
bf16 = jnp.bfloat16

D_MODEL = 1024
N_META = 16
SSD_HEADS = 8
SSD_HEAD_DIM = 64
SSD_WIDTH = SSD_HEADS * SSD_HEAD_DIM
SSD_GROUPS = 2
D_STATE = 64
CONV_W = 4
CONV_DIM = SSD_WIDTH + 2 * SSD_GROUPS * D_STATE
SSD_CHUNK = 128
MLA_HEADS = 8
QK_NOPE = 64
QK_ROPE = 32
QK_DIM = QK_NOPE + QK_ROPE
V_DIM = 64
MLA_WIDTH = MLA_HEADS * V_DIM
Q_LORA = 384
KV_LORA = 256
ROPE_THETA = 10000.0
ATTN_SCALE = QK_DIM ** -0.5
D_FF = 2816
EPS = 1e-6

LANE = 128
SUBLANE = 8
HT = LANE
ROPE_LO = QK_NOPE
ROPE_HALF = QK_ROPE // 2
FF_CHUNK = 256
N_FF_CHUNKS = D_FF // FF_CHUNK
DECODE_PAGES_PER_TILE = 32
MiB = 1024 * 1024

PC_Z = 0
PC_XBC = PC_Z + SSD_WIDTH
PC_CQ = PC_XBC + CONV_DIM
PC_CKV = PC_CQ + Q_LORA
PC_KR = PC_CKV + KV_LORA
PC_END = PC_KR + LANE

_NT = (((1,), (1,)), ((), ()))
_TN = (((0,), (0,)), ((), ()))
_HI = lax.Precision.HIGHEST


def _dot(a, b):
    return jnp.dot(a, b, preferred_element_type=f32)


def _dot_nt(a, b):
    return lax.dot_general(a, b, _NT, preferred_element_type=f32)


def _dot_tn(a, b):
    return lax.dot_general(a, b, _TN, preferred_element_type=f32)


def _rms(x, n):
    return lax.rsqrt(jnp.sum(x * x, axis=-1, keepdims=True) * (1.0 / n) + EPS)


def _silu(x):
    return x * jax.nn.sigmoid(x)


def _softplus(x):
    return jnp.maximum(x, 0.0) + jnp.log1p(jnp.exp(-jnp.abs(x)))


def _const_spec(shape):
    nd = len(shape)
    return pl.BlockSpec(shape, lambda *_: (0,) * nd, pipeline_mode=pl.Buffered(1))


def _params(sem, vmem_mib):
    return pltpu.CompilerParams(dimension_semantics=sem, vmem_limit_bytes=vmem_mib * MiB)


def _proj_body(x_ref, g_ref, w_ref, z_ref, xbc_ref, cq_ref, ckv_ref, kr_ref, dtt_ref):
    x = x_ref[...]
    xn = (x * _rms(x, D_MODEL) * g_ref[...]).astype(bf16)
    z_ref[...] = _dot(xn, w_ref[:, PC_Z:PC_XBC])
    xbc_ref[...] = _dot(xn, w_ref[:, PC_XBC:PC_CQ])
    cq_ref[...] = _dot(xn, w_ref[:, PC_CQ:PC_CKV])
    ckv_ref[...] = _dot(xn, w_ref[:, PC_CKV:PC_KR])
    last = _dot(xn, w_ref[:, PC_KR:PC_END])
    lane = lax.broadcasted_iota(jnp.int32, (1, LANE), 1)
    kr_ref[...] = jnp.where(lane >= ROPE_LO, last, 0.0)
    dtt_ref[...] = last.T[0:16, :]


def _project(x, g, w, tm):
    m = x.shape[0]
    row = lambda n: pl.BlockSpec((tm, n), lambda i: (i, 0))
    widths = (SSD_WIDTH, CONV_DIM, Q_LORA, KV_LORA, LANE)
    return pl.pallas_call(
        _proj_body,
        grid=(m // tm,),
        in_specs=[row(D_MODEL), _const_spec((1, D_MODEL)), _const_spec((D_MODEL, PC_END))],
        out_specs=[row(n) for n in widths] + [pl.BlockSpec((16, tm), lambda i: (0, i))],
        out_shape=[jax.ShapeDtypeStruct((m, n), f32) for n in widths] + [jax.ShapeDtypeStruct((16, m), f32)],
        compiler_params=_params(("parallel",), 40),
        name="proj",
    )(x, g, w)


def _mla_prep_body(cq_ref, ckv_ref, kr_ref, tc_ref, ts1_ref, ts2_ref, gqn_ref, gkv_ref, wuq_ref, wuk_ref, wuv_ref,
                   gq_ref, gk_ref, lat_ref, kro_ref, q_ref, k_ref, vt_ref):
    tc, ts1, ts2 = tc_ref[...], ts1_ref[...], ts2_ref[...]
    tsw = ts1 + ts2

    def rope(x):
        return x * tc + pltpu.roll(x, LANE - ROPE_HALF, 1) * ts1 + pltpu.roll(x, ROPE_HALF, 1) * ts2

    ckv = ckv_ref[...]
    lat = ckv * _rms(ckv, KV_LORA) * gkv_ref[...]
    lat_ref[...] = lat
    kr = rope(kr_ref[...])
    kro_ref[...] = kr
    cq = cq_ref[...]
    cqn = (cq * _rms(cq, Q_LORA) * gqn_ref[...]).astype(bf16)
    latb = lat.astype(bf16)
    vt_ref[...] = _dot_nt(wuv_ref[...], latb).astype(vt_ref.dtype)
    gq, gk = gq_ref[...], gk_ref[...]
    q_all = _dot(cqn, wuq_ref[:, 0:MLA_HEADS * HT])
    qp_all = _dot(cqn, wuq_ref[:, MLA_HEADS * HT:])
    k_all = _dot(latb, wuk_ref[...])
    for h in range(MLA_HEADS):
        sl = slice(h * HT, (h + 1) * HT)
        qh = q_all[:, sl] * tc + qp_all[:, sl] * tsw
        q_ref[:, sl] = (qh * _rms(qh, QK_DIM) * gq).astype(q_ref.dtype)
        kh = k_all[:, sl] + kr
        k_ref[:, sl] = (kh * _rms(kh, QK_DIM) * gk).astype(k_ref.dtype)


def _mla_prep(cq, ckv, kr, tabs, n_tab_blocks, gqn, gkv, wuq, wuk, wuvt, gq, gk, tm, q_dtype):
    m = cq.shape[0]
    row = lambda n: pl.BlockSpec((tm, n), lambda i: (i, 0))
    tab = pl.BlockSpec((tm, LANE), lambda i: (i % n_tab_blocks, 0))
    widths = (KV_LORA, LANE, MLA_HEADS * HT, MLA_HEADS * HT)
    dtypes = (f32, f32, q_dtype, bf16)
    return pl.pallas_call(
        _mla_prep_body,
        grid=(m // tm,),
        in_specs=[row(Q_LORA), row(KV_LORA), row(LANE), tab, tab, tab,
                  _const_spec((1, Q_LORA)), _const_spec((1, KV_LORA)),
                  _const_spec((Q_LORA, 2 * MLA_HEADS * HT)), _const_spec((KV_LORA, MLA_HEADS * HT)),
                  _const_spec((MLA_WIDTH, KV_LORA)), _const_spec((1, HT)), _const_spec((1, HT))],
        out_specs=[row(n) for n in widths] + [pl.BlockSpec((MLA_WIDTH, tm), lambda i: (0, i))],
        out_shape=[jax.ShapeDtypeStruct((m, n), d) for n, d in zip(widths, dtypes)]
                  + [jax.ShapeDtypeStruct((MLA_WIDTH, m), bf16)],
        compiler_params=_params(("parallel",), 40),
        name="mla_prep",
    )(cq, ckv, kr, *tabs, gqn, gkv, wuq, wuk, wuvt, gq, gk)


def _ssd_body(xbc_ref, z_ref, dtt_ref, hist_ref, h0_ref, cw_ref, cb_ref, bias_c_ref,
              alog_c_ref, dsk_ref, ng_ref, tri_ref, y_ref, hout_ref, xp_ref, hs_ref, *, valid_from):
    c = pl.program_id(1)
    q = SSD_CHUNK

    @pl.when(c == 0)
    def _():
        xp_ref[0:SUBLANE, :] = hist_ref[...]
        hs_ref[...] = h0_ref[0]

    xbc = xbc_ref[...]
    xp_ref[SUBLANE:SUBLANE + q, :] = xbc
    conv = cb_ref[...]
    for k in range(CONV_W):
        lo = SUBLANE - (CONV_W - 1) + k
        conv = conv + xp_ref[lo:lo + q, :] * cw_ref[k:k + 1, :]
    xp_ref[0:SUBLANE, :] = xbc[q - SUBLANE:q, :]
    xa = _silu(conv)
    xs = xa[:, :SSD_WIDTH]
    bm = xa[:, SSD_WIDTH:SSD_WIDTH + LANE]
    cm = xa[:, SSD_WIDTH + LANE:]

    rows = lax.broadcasted_iota(jnp.int32, (q, q), 0)
    cols = lax.broadcasted_iota(jnp.int32, (q, q), 1)
    lane = lax.broadcasted_iota(jnp.int32, (1, LANE), 1)
    low = lane < D_STATE

    dtr = _softplus(dtt_ref[...] + bias_c_ref[...])
    if valid_from:
        dtr = jnp.where(lax.broadcasted_iota(jnp.int32, (16, q), 1) >= valid_from, dtr, 0.0)
    dar = dtr * -jnp.exp(alog_c_ref[...])
    tri = tri_ref[...]
    dtc = dtr.T
    acs_r = lax.dot_general(dar, tri, _NT, precision=_HI, preferred_element_type=f32)
    acs_c = jnp.dot(tri, dar.T, precision=_HI, preferred_element_type=f32)
    w_c = jnp.exp(acs_c[q - 1:q, :] - acs_c) * dtc
    e_c = jnp.exp(acs_c)
    cd_r = jnp.exp(acs_r[:, q - 1:q])

    causal = cols <= rows
    bb = bm.astype(bf16)
    dsk = dsk_ref[...]
    ys = []
    for g in range(SSD_GROUPS):
        gmask = (lane >= g * D_STATE) & (lane < (g + 1) * D_STATE)
        cg = jnp.where(gmask, cm, 0.0).astype(bf16)
        cb_g = _dot_nt(cg, bb)
        for pi in range(2):
            i = 2 * g + pi
            xpair = xs[:, i * LANE:(i + 1) * LANE]
            ypair = jnp.zeros((q, LANE), f32)
            for hh in range(2):
                h = 2 * i + hh
                seg = acs_c[:, h:h + 1] - acs_r[h:h + 1, :]
                lmat = jnp.exp(jnp.where(causal, seg, -jnp.inf))
                sc = (cb_g * lmat * dtr[h:h + 1, :]).astype(bf16)
                xh = jnp.where(low if hh == 0 else jnp.logical_not(low), xpair, 0.0).astype(bf16)
                ypair = ypair + _dot(sc, xh)
            h0, h1 = 2 * i, 2 * i + 1
            wp = jnp.where(low, w_c[:, h0:h0 + 1], w_c[:, h1:h1 + 1])
            ep = jnp.where(low, e_c[:, h0:h0 + 1], e_c[:, h1:h1 + 1])
            hst = hs_ref[i]
            yoff = _dot_nt(cg, hst.astype(bf16)) * ep
            st = _dot_tn((xpair * wp).astype(bf16), bb)
            cdb = jnp.concatenate([jnp.broadcast_to(cd_r[h0:h0 + 1, :], (SSD_HEAD_DIM, LANE)),
                                   jnp.broadcast_to(cd_r[h1:h1 + 1, :], (SSD_HEAD_DIM, LANE))], axis=0)
            hs_ref[i] = hst * cdb + st
            ys.append(ypair + yoff + dsk[:, i * LANE:(i + 1) * LANE] * xpair)
    y = jnp.concatenate(ys, axis=1) * _silu(z_ref[...])
    gw = SSD_WIDTH // SSD_GROUPS
    outs = []
    for g in range(SSD_GROUPS):
        yg = y[:, g * gw:(g + 1) * gw]
        outs.append(yg * _rms(yg, gw))
    y_ref[...] = (jnp.concatenate(outs, axis=1) * ng_ref[...]).astype(y_ref.dtype)

    @pl.when(c == pl.num_programs(1) - 1)
    def _():
        hout_ref[0] = hs_ref[...]


def _ssd(xbc, z, dtt, hist_arr, hist_block, h0, consts, n_batch, n_chunks, row_block0, valid_from):
    q = SSD_CHUNK
    rb = lambda n: pl.BlockSpec((q, n), lambda b, c: (row_block0 + b * n_chunks + c, 0))
    cw, cb, bias_c, alog_c, dsk, ng, tri = consts
    return pl.pallas_call(
        functools.partial(_ssd_body, valid_from=valid_from),
        grid=(n_batch, n_chunks),
        in_specs=[rb(CONV_DIM), rb(SSD_WIDTH),
                  pl.BlockSpec((16, q), lambda b, c: (0, row_block0 + b * n_chunks + c)),
                  pl.BlockSpec((SUBLANE, CONV_DIM), lambda b, c: (hist_block, 0)),
                  _const_spec((1, 4, LANE, LANE)),
                  _const_spec((CONV_W, CONV_DIM)), _const_spec((1, CONV_DIM)),
                  _const_spec((16, 1)), _const_spec((16, 1)),
                  _const_spec((1, SSD_WIDTH)), _const_spec((1, SSD_WIDTH)), _const_spec((q, q))],
        out_specs=[pl.BlockSpec((q, SSD_WIDTH), lambda b, c: (b * n_chunks + c, 0)),
                   pl.BlockSpec((1, 4, LANE, LANE), lambda b, c: (b, 0, 0, 0))],
        out_shape=[jax.ShapeDtypeStruct((n_batch * n_chunks * q, SSD_WIDTH), bf16),
                   jax.ShapeDtypeStruct((n_batch, 4, LANE, LANE), f32)],
        scratch_shapes=[pltpu.VMEM((SUBLANE + q, CONV_DIM), f32), pltpu.VMEM((4, LANE, LANE), f32)],
        compiler_params=_params(("parallel", "arbitrary"), 32),
        name="ssd",
    )(xbc, z, dtt, hist_arr, h0, cw, cb, bias_c, alog_c, dsk, ng, tri)


def _flash_body(q_ref, k_ref, vt_ref, km_ref, vtm_ref, o_ref, *, tq, tk, nh):
    qi = pl.program_id(2)
    qs = [q_ref[:, h * HT:(h + 1) * HT] for h in range(nh)]
    drow = lax.broadcasted_iota(jnp.int32, (2 * V_DIM, 1), 0)
    vmasks = (drow < V_DIM, drow >= V_DIM)
    half = tk // 2

    def heads(kall, vtall, carries, mask=None, qlo=0):
        sts = [_dot_nt(kall[:, h * HT:(h + 1) * HT], qs[h][qlo:, :]) for h in range(nh)]
        out = []
        for h in range(nh):
            m0, l0, acc0 = carries[h]
            m, l, acct = m0[:, qlo:], l0[:, qlo:], acc0[:, qlo:]
            st = sts[h] if mask is None else jnp.where(mask, sts[h], -jnp.inf)
            m2 = jnp.maximum(m, jnp.max(st, axis=0, keepdims=True))
            pt = jnp.exp(st - m2)
            a = jnp.exp(m - m2)
            vtt = vtall[(h // 2) * LANE:(h // 2 + 1) * LANE, :]
            vz = jnp.where(vmasks[h % 2], vtt, jnp.zeros_like(vtt))
            new = (m2, a * l + jnp.sum(pt, axis=0, keepdims=True), a * acct + _dot(vz, pt.astype(bf16)))
            if qlo:
                new = tuple(jnp.concatenate([old[:, :qlo], x], axis=1) for old, x in zip(carries[h], new))
            out.append(new)
        return tuple(out)

    init = (jnp.full((1, tq), -jnp.inf, f32), jnp.zeros((1, tq), f32), jnp.zeros((2 * V_DIM, tq), f32))

    def body(j, carries):
        off = pl.multiple_of(j * tk, tk)
        return heads(k_ref[pl.ds(off, tk), :], vt_ref[:, pl.ds(off, tk)], carries)

    carries = lax.fori_loop(0, qi, body, (init,) * nh)
    off = pl.multiple_of(qi * tk, tk)
    k_a = jnp.concatenate([k_ref[pl.ds(off, half), :], km_ref[...]], axis=0)
    vt_a = jnp.concatenate([vt_ref[:, pl.ds(off, half)], vtm_ref[...]], axis=1)
    krow = lax.broadcasted_iota(jnp.int32, (half + N_META, tq), 0)
    qcol = lax.broadcasted_iota(jnp.int32, (half + N_META, tq), 1)
    carries = heads(k_a, vt_a, carries, (krow >= half) | (krow <= qcol))
    off_b = pl.multiple_of(qi * tk + half, half)
    krow = lax.broadcasted_iota(jnp.int32, (half, tq - half), 0)
    qcol = lax.broadcasted_iota(jnp.int32, (half, tq - half), 1)
    res = heads(k_ref[pl.ds(off_b, half), :], vt_ref[:, pl.ds(off_b, half)], carries, krow <= qcol, qlo=half)
    for pr in range(nh // 2):
        (_, l0, acc0), (_, l1, acc1) = res[2 * pr], res[2 * pr + 1]
        o_ref[:, pr * LANE:(pr + 1) * LANE] = (acc0 / l0 + acc1 / l1).T.astype(o_ref.dtype)


def _flash(q, k, vt, k_small, vt_meta, meta_block, n_batch, seq, tq, tk, nh):
    assert tk == tq and seq % tk == 0 and nh % 2 == 0 and MLA_HEADS % nh == 0
    nq = seq // tq
    return pl.pallas_call(
        functools.partial(_flash_body, tq=tq, tk=tk, nh=nh),
        grid=(n_batch, MLA_HEADS // nh, nq),
        in_specs=[pl.BlockSpec((tq, nh * HT), lambda b, p, i: (b * nq + i, p)),
                  pl.BlockSpec((seq, nh * HT), lambda b, p, i: (b, p)),
                  pl.BlockSpec((nh * V_DIM, seq), lambda b, p, i: (p, b)),
                  pl.BlockSpec((N_META, nh * HT), lambda b, p, i: (meta_block, p)),
                  pl.BlockSpec((nh * V_DIM, N_META), lambda b, p, i: (p, 0))],
        out_specs=pl.BlockSpec((tq, nh * V_DIM), lambda b, p, i: (b * nq + i, p)),
        out_shape=jax.ShapeDtypeStruct((n_batch * seq, MLA_WIDTH), f32),
        compiler_params=_params(("parallel", "parallel", "arbitrary"), 40),
        name="flash",
    )(q, k, vt, k_small, vt_meta)


def _finish_body(x_ref, ssd_ref, att_ref, gm_ref, wo_ref, gf_ref, wg_ref, wu_ref, wd_ref, o_ref):
    att = att_ref[...]
    mla = (att * _rms(att, MLA_WIDTH) * gm_ref[...]).astype(bf16)
    h = x_ref[...] + (_dot(ssd_ref[...], wo_ref[0:SSD_WIDTH, :]) + _dot(mla, wo_ref[SSD_WIDTH:, :]))
    n = (h * _rms(h, D_MODEL) * gf_ref[...]).astype(bf16)
    ff = jnp.zeros_like(h)
    for c in range(N_FF_CHUNKS):
        cs = slice(c * FF_CHUNK, (c + 1) * FF_CHUNK)
        a = (_silu(_dot(n, wg_ref[:, cs])) * _dot(n, wu_ref[:, cs])).astype(bf16)
        ff = ff + _dot(a, wd_ref[cs, :])
    o_ref[...] = h + ff


def _finish(x, ssd, att, gm, wo, gf, wg, wu, wd, tm):
    m = x.shape[0]
    row = lambda n: pl.BlockSpec((tm, n), lambda i: (i, 0))
    return pl.pallas_call(
        _finish_body,
        grid=(m // tm,),
        in_specs=[row(D_MODEL), row(SSD_WIDTH), row(MLA_WIDTH), _const_spec((1, MLA_WIDTH)),
                  _const_spec((D_MODEL, D_MODEL)), _const_spec((1, D_MODEL)),
                  _const_spec((D_MODEL, D_FF)), _const_spec((D_MODEL, D_FF)), _const_spec((D_FF, D_MODEL))],
        out_specs=row(D_MODEL),
        out_shape=jax.ShapeDtypeStruct((m, D_MODEL), f32),
        compiler_params=_params(("parallel",), 56),
        name="finish",
    )(x, ssd, att, gm, wo, gf, wg, wu, wd)


def _sssd_prep_body(xbc_ref, sc_ref, cw_ref, cb_ref, xs_ref, xat_ref):
    conv = cb_ref[...]
    for k in range(CONV_W - 1):
        conv = conv + sc_ref[k] * cw_ref[k:k + 1, :]
    conv = conv + xbc_ref[...] * cw_ref[CONV_W - 1:CONV_W, :]
    xa = _silu(conv)
    xs_ref[...] = xa[:, :SSD_WIDTH]
    xat_ref[...] = xa.T


def _sssd_prep(xbc_small, sc, cw, cb, nseq):
    return pl.pallas_call(
        _sssd_prep_body,
        grid=(1,),
        in_specs=[pl.BlockSpec((nseq, CONV_DIM), lambda i: (0, 0)), _const_spec((CONV_W - 1, nseq, CONV_DIM)),
                  _const_spec((CONV_W, CONV_DIM)), _const_spec((1, CONV_DIM))],
        out_specs=[pl.BlockSpec((nseq, SSD_WIDTH), lambda i: (0, 0)), pl.BlockSpec((CONV_DIM, nseq), lambda i: (0, 0))],
        out_shape=[jax.ShapeDtypeStruct((nseq, SSD_WIDTH), f32), jax.ShapeDtypeStruct((CONV_DIM, nseq), f32)],
        compiler_params=_params(("arbitrary",), 32),
        name="sssd_prep",
    )(xbc_small, sc, cw, cb)


def _sssd_state_body(h0_ref, xst_ref, bt_ref, ct_ref, dtt_ref, bias_c_ref, alog_c_ref, hn_ref, yt_ref):
    h = pl.program_id(0)
    dt = _softplus(dtt_ref[pl.ds(h, 1), :] + bias_c_ref[pl.ds(h, 1), :])
    dec = jnp.exp(dt * -jnp.exp(alog_c_ref[pl.ds(h, 1), :]))
    bt, ct = bt_ref[...], ct_ref[...]

    def body(p, carry):
        xdt = xst_ref[pl.ds(p, 1), :] * dt
        hn = h0_ref[0, p] * dec + xdt * bt
        hn_ref[0, p] = hn
        yt_ref[pl.ds(p, 1), :] = jnp.sum(ct * hn, axis=0, keepdims=True)
        return carry

    lax.fori_loop(0, SSD_HEAD_DIM, body, 0, unroll=4)


def _sssd_state(h0t, xat, dtt_small, bias_c, alog_c, nseq):
    hpg = SSD_HEADS // SSD_GROUPS
    rows = lambda f: pl.BlockSpec((SSD_HEAD_DIM, nseq), f)
    hblk = pl.BlockSpec((1, SSD_HEAD_DIM, D_STATE, nseq), lambda h: (h, 0, 0, 0))
    return pl.pallas_call(
        _sssd_state_body,
        grid=(SSD_HEADS,),
        in_specs=[hblk, rows(lambda h: (h, 0)), rows(lambda h: (SSD_HEADS + h // hpg, 0)),
                  rows(lambda h: (SSD_HEADS + SSD_GROUPS + h // hpg, 0)),
                  pl.BlockSpec((16, nseq), lambda h: (0, 0)), _const_spec((16, 1)), _const_spec((16, 1))],
        out_specs=[hblk, rows(lambda h: (h, 0))],
        out_shape=[jax.ShapeDtypeStruct((SSD_HEADS, SSD_HEAD_DIM, D_STATE, nseq), f32),
                   jax.ShapeDtypeStruct((SSD_WIDTH, nseq), f32)],
        compiler_params=_params(("parallel",), 32),
        name="sssd_state",
    )(h0t, xat, xat, xat, dtt_small, bias_c, alog_c)


def _sssd_gate_body(yt_ref, xs_ref, z_ref, dsk_ref, ng_ref, o_ref):
    y = (yt_ref[...].T + dsk_ref[...] * xs_ref[...]) * _silu(z_ref[...])
    gw = SSD_WIDTH // SSD_GROUPS
    outs = []
    for g in range(SSD_GROUPS):
        yg = y[:, g * gw:(g + 1) * gw]
        outs.append(yg * _rms(yg, gw))
    o_ref[...] = (jnp.concatenate(outs, axis=1) * ng_ref[...]).astype(o_ref.dtype)


def _sssd_gate(yt, xs, z_small, dsk, ng, nseq):
    blk = pl.BlockSpec((nseq, SSD_WIDTH), lambda i: (0, 0))
    return pl.pallas_call(
        _sssd_gate_body,
        grid=(1,),
        in_specs=[pl.BlockSpec((SSD_WIDTH, nseq), lambda i: (0, 0)), blk, blk,
                  _const_spec((1, SSD_WIDTH)), _const_spec((1, SSD_WIDTH))],
        out_specs=blk,
        out_shape=jax.ShapeDtypeStruct((nseq, SSD_WIDTH), bf16),
        compiler_params=_params(("arbitrary",), 32),
        name="sssd_gate",
    )(yt, xs, z_small, dsk, ng)


def _absorb_body(q_ref, gk_ref, wabs_ref, lat_ref, kr_ref, wuk_ref, qg_ref, a_ref, snew_ref):
    gk = gk_ref[...]
    latb = lat_ref[...].astype(bf16)
    latf = latb.astype(f32)
    kr = kr_ref[...]
    krf = kr.astype(bf16).astype(f32)
    kr2 = jnp.sum(kr * kr, axis=-1, keepdims=True)
    s_new = []
    for h in range(MLA_HEADS):
        sl = slice(h * HT, (h + 1) * HT)
        qg = q_ref[:, sl] * gk
        qg_ref[:, sl] = qg
        a = jnp.dot(qg, wabs_ref[sl, :], precision=_HI, preferred_element_type=f32)
        a_ref[:, h * KV_LORA:(h + 1) * KV_LORA] = a
        kn = _dot(latb, wuk_ref[:, sl])
        n2 = jnp.sum(kn * kn, axis=-1, keepdims=True)
        s12 = (jnp.sum(a.astype(bf16).astype(f32) * latf, axis=-1, keepdims=True)
               + jnp.sum(qg.astype(bf16).astype(f32) * krf, axis=-1, keepdims=True))
        s_new.append(lax.rsqrt((n2 + kr2) * (1.0 / QK_DIM) + EPS) * s12)
    a_ref[:, MLA_HEADS * KV_LORA:] = jnp.zeros((q_ref.shape[0], (16 - MLA_HEADS) * KV_LORA), f32)
    snew_ref[...] = jnp.concatenate(s_new, axis=1)


def _absorb(q_small, gk, wabs, lat_small, kr_small, wuk, nseq):
    blk = lambda n: pl.BlockSpec((nseq, n), lambda i: (0, 0))
    return pl.pallas_call(
        _absorb_body,
        grid=(1,),
        in_specs=[blk(MLA_HEADS * HT), _const_spec((1, HT)), _const_spec((MLA_HEADS * HT, KV_LORA)),
                  blk(KV_LORA), blk(HT), _const_spec((KV_LORA, MLA_HEADS * HT))],
        out_specs=[blk(MLA_HEADS * HT), blk(16 * KV_LORA), blk(MLA_HEADS)],
        out_shape=[jax.ShapeDtypeStruct((nseq, MLA_HEADS * HT), f32), jax.ShapeDtypeStruct((nseq, 16 * KV_LORA), f32),
                   jax.ShapeDtypeStruct((nseq, MLA_HEADS), f32)],
        compiler_params=_params(("arbitrary",), 32),
        name="absorb",
    )(q_small, gk, wabs, lat_small, kr_small, wuk)


def _decode_body(pt_ref, a_ref, c_ref, latn_ref, snew_ref, wukt_ref, clat_ref, ckrt_ref, o_ref,
                 lat_buf, kr_buf, waug, latb, s_buf, sems, *, n_pages, page, ppt):
    b = pl.program_id(0)
    nb = pl.num_programs(0)
    slot = b % 2
    n_tiles = n_pages // ppt
    tile = ppt * page
    wrows = MLA_HEADS * QK_NOPE

    def start_page(seq, p, sl):
        pid = pt_ref[seq * n_pages + p]
        pltpu.make_async_copy(clat_ref.at[0, pid], lat_buf.at[sl, p], sems.at[0, sl]).start()
        pltpu.make_async_copy(ckrt_ref.at[0, pid], kr_buf.at[sl, p], sems.at[1, sl]).start()

    @pl.when(b == 0)
    def _():
        waug[0:wrows, :] = wukt_ref[...]

        def first(p, carry):
            start_page(0, p, 0)
            return carry

        lax.fori_loop(0, n_pages, first, 0)

    pltpu.make_async_copy(clat_ref.at[0, pl.ds(0, n_pages)], lat_buf.at[slot], sems.at[0, slot]).wait()
    pltpu.make_async_copy(ckrt_ref.at[0, pl.ds(0, n_pages)], kr_buf.at[slot], sems.at[1, slot]).wait()

    waug[wrows:wrows + 16, :] = a_ref[0, 0].astype(bf16)
    cb = c_ref[0].astype(bf16)

    def score_dots(lb, krt):
        kt = _dot_nt(waug[...], lb)
        return kt, _dot(cb, krt.astype(bf16)), krt

    def score_finish(kt, s2, krt):
        n = kt.shape[1]
        k3 = kt[0:wrows, :].reshape(MLA_HEADS, QK_NOPE, n)
        n2 = jnp.sum(k3 * k3, axis=1)
        kr2 = jnp.sum(krt * krt, axis=0, keepdims=True)
        return lax.rsqrt((n2 + kr2) * (1.0 / QK_DIM) + EPS) * (kt[wrows:wrows + MLA_HEADS, :] + s2)

    def body(j, carry):
        @pl.when(b + 1 < nb)
        def _():
            for pp in range(ppt):
                start_page(b + 1, j * ppt + pp, 1 - slot)

        lb = lat_buf[slot, pl.ds(j * ppt, ppt)].reshape(tile, KV_LORA).astype(bf16)
        latb[j] = lb
        kr_pages = kr_buf[slot, pl.ds(j * ppt, ppt)]
        s_buf[j] = score_finish(*score_dots(lb, jnp.concatenate([kr_pages[i] for i in range(ppt)], axis=-1)))
        return carry

    lax.fori_loop(0, n_tiles, body, 0)
    s_new = snew_ref[0]
    s_all = jnp.concatenate([s_buf[j] for j in range(n_tiles)], axis=-1)
    m = jnp.maximum(jnp.max(s_all, axis=-1, keepdims=True), s_new)
    p_all = jnp.exp(s_all - m)
    p_new = jnp.exp(s_new - m)
    l = jnp.sum(p_all, axis=-1, keepdims=True) + p_new
    acc = _dot(p_all.astype(bf16), latb[...].reshape(n_tiles * tile, KV_LORA)) + p_new * latn_ref[0]
    o_ref[0] = acc / l


def _decode(page_table_flat, a, c, lat_new, s_new, wukt, cache_lat, cache_kr_t, nseq, n_pages, page, ppt):
    assert n_pages % ppt == 0
    wrows = MLA_HEADS * QK_NOPE
    grid_spec = pltpu.PrefetchScalarGridSpec(
        num_scalar_prefetch=1,
        grid=(nseq,),
        in_specs=[pl.BlockSpec((1, 1, 16, KV_LORA), lambda b, pt: (b, 0, 0, 0)),
                  pl.BlockSpec((1, MLA_HEADS, QK_ROPE), lambda b, pt: (b, 0, 0)),
                  pl.BlockSpec((1, 1, KV_LORA), lambda b, pt: (b, 0, 0)),
                  pl.BlockSpec((1, MLA_HEADS, 1), lambda b, pt: (b, 0, 0)),
                  pl.BlockSpec((wrows, KV_LORA), lambda b, pt: (0, 0), pipeline_mode=pl.Buffered(1)),
                  pl.BlockSpec(memory_space=pl.ANY), pl.BlockSpec(memory_space=pl.ANY)],
        out_specs=pl.BlockSpec((1, MLA_HEADS, KV_LORA), lambda b, pt: (b, 0, 0)),
        scratch_shapes=[pltpu.VMEM((2, n_pages, page, KV_LORA), f32), pltpu.VMEM((2, n_pages, QK_ROPE, page), f32),
                        pltpu.VMEM((wrows + 16, KV_LORA), bf16),
                        pltpu.VMEM((n_pages // ppt, ppt * page, KV_LORA), bf16),
                        pltpu.VMEM((n_pages // ppt, MLA_HEADS, ppt * page), f32), pltpu.SemaphoreType.DMA((2, 2))],
    )
    return pl.pallas_call(
        functools.partial(_decode_body, n_pages=n_pages, page=page, ppt=ppt),
        grid_spec=grid_spec,
        out_shape=jax.ShapeDtypeStruct((nseq, MLA_HEADS, KV_LORA), f32),
        compiler_params=_params(("arbitrary",), 40),
        name="decode",
    )(page_table_flat, a, c, lat_new, s_new, wukt, cache_lat, cache_kr_t)


def _uv_body(o_ref, wuv_ref, att_ref):
    for h in range(MLA_HEADS):
        oh = o_ref[:, h * KV_LORA:(h + 1) * KV_LORA].astype(bf16)
        att_ref[:, h * V_DIM:(h + 1) * V_DIM] = _dot(oh, wuv_ref[:, h * V_DIM:(h + 1) * V_DIM])


def _uv(o_flat, wuv):
    nseq = o_flat.shape[0]
    return pl.pallas_call(
        _uv_body,
        grid=(1,),
        in_specs=[pl.BlockSpec((nseq, MLA_HEADS * KV_LORA), lambda i: (0, 0)), _const_spec((KV_LORA, MLA_WIDTH))],
        out_specs=pl.BlockSpec((nseq, MLA_WIDTH), lambda i: (0, 0)),
        out_shape=jax.ShapeDtypeStruct((nseq, MLA_WIDTH), f32),
        compiler_params=_params(("arbitrary",), 32),
        name="uv",
    )(o_flat, wuv)


def _rope_tables(pos):
    inv = ROPE_THETA ** (-jnp.arange(ROPE_HALF, dtype=f32) * (2.0 / QK_ROPE))
    ang = pos.astype(f32)[:, None] * inv[None, :]
    cos, sin = jnp.cos(ang), jnp.sin(ang)
    n = pos.shape[0]
    one = jnp.ones((n, QK_NOPE), f32)
    z16 = jnp.zeros((n, ROPE_HALF), f32)
    z32 = jnp.zeros((n, HT - QK_DIM), f32)
    z64 = jnp.zeros((n, QK_NOPE), f32)
    tc = jnp.concatenate([one, cos, cos, z32], axis=1)
    ts1 = jnp.concatenate([z64, -sin, z16, z32], axis=1)
    ts2 = jnp.concatenate([z64, z16, sin, z32], axis=1)
    return tc, ts1, ts2


def _pad_lanes(x, n):
    return jnp.pad(x, ((0, 0), (0, n - x.shape[1])))


def kernel(x_prompt, x_sample, cache_kv_latent, cache_k_rope, state_conv, state_ssm, page_table, meta_tokens, attn_norm_g, w_in, conv_w, conv_b, dt_bias, a_log, d_skip, ssd_norm_g, q_norm_g, w_uq, kv_norm_g, w_ukv, q_head_g_nope, q_head_g_rope, k_head_g_nope, k_head_g_rope, mla_out_g, w_out, ffn_norm_g, w_gate, w_up, w_down):
    l = 0
    nb, seq, _ = x_prompt.shape
    nseq = x_sample.shape[0]
    n_pages, page = page_table.shape[1], cache_kv_latent.shape[2]
    past = n_pages * page
    n_small = 2 * LANE
    meta_lo = n_small - N_META

    s0, s1, s2, s3, s4 = (SSD_WIDTH, SSD_WIDTH + CONV_DIM, SSD_WIDTH + CONV_DIM + SSD_HEADS,
                          SSD_WIDTH + CONV_DIM + SSD_HEADS + Q_LORA, SSD_WIDTH + CONV_DIM + SSD_HEADS + Q_LORA + KV_LORA)
    wi = w_in[l]
    w_last = jnp.concatenate([_pad_lanes(wi[:, s1:s2], ROPE_LO), _pad_lanes(wi[:, s4:], HT - ROPE_LO)], axis=1)
    w_proj = jnp.concatenate([wi[:, :s1], wi[:, s2:s4], w_last], axis=1).astype(bf16)
    g_attn = attn_norm_g[l][None, :]
    wq3 = w_uq[l].reshape(Q_LORA, MLA_HEADS, QK_DIM)
    zq = jnp.zeros((Q_LORA, MLA_HEADS, QK_NOPE), f32)
    wq_partner = jnp.concatenate([zq, wq3[:, :, QK_NOPE + ROPE_HALF:], wq3[:, :, QK_NOPE:QK_NOPE + ROPE_HALF]], axis=2)
    pad_head = lambda w: jnp.pad(w, ((0, 0), (0, 0), (0, HT - QK_DIM))).reshape(Q_LORA, -1)
    wuq = jnp.concatenate([pad_head(wq3), pad_head(wq_partner)], axis=1).astype(bf16)
    wkv = w_ukv[l].reshape(KV_LORA, MLA_HEADS, QK_NOPE + V_DIM)
    wuk_f32 = jnp.pad(wkv[:, :, :QK_NOPE], ((0, 0), (0, 0), (0, HT - QK_NOPE))).reshape(KV_LORA, -1)
    wuk = wuk_f32.astype(bf16)
    wuv = wkv[:, :, QK_NOPE:].reshape(KV_LORA, MLA_WIDTH).astype(bf16)
    wukt = wkv[:, :, :QK_NOPE].reshape(KV_LORA, -1).T.astype(bf16)
    gq = _pad_lanes(jnp.concatenate([q_head_g_nope[l], q_head_g_rope[l], q_head_g_rope[l]])[None, :], HT) * ATTN_SCALE
    gk = _pad_lanes(jnp.concatenate([k_head_g_nope[l], k_head_g_rope[l], k_head_g_rope[l]])[None, :], HT)
    gqn, gkv = q_norm_g[l][None, :], kv_norm_g[l][None, :]
    cw, cb = conv_w[l], conv_b[l][None, :]
    bias_c = jnp.pad(dt_bias[l][:, None], ((0, 16 - SSD_HEADS), (0, 0)))
    alog_c = jnp.pad(a_log[l][:, None], ((0, 16 - SSD_HEADS), (0, 0)))
    dsk = jnp.repeat(d_skip[l], SSD_HEAD_DIM)[None, :]
    ng = ssd_norm_g[l][None, :]
    tri = jnp.tril(jnp.ones((SSD_CHUNK, SSD_CHUNK), f32))
    ssd_consts = (cw, cb, bias_c, alog_c, dsk, ng, tri)
    gm, gf = mla_out_g[l][None, :], ffn_norm_g[l][None, :]
    wo = w_out[l].astype(bf16)
    wg, wu, wd = w_gate[l].astype(bf16), w_up[l].astype(bf16), w_down[l].astype(bf16)

    xp = x_prompt.reshape(nb * seq, D_MODEL)
    xs_rows = x_sample[:, 0, :]
    x_small = jnp.concatenate([xs_rows, jnp.zeros((meta_lo - nseq, D_MODEL), f32), meta_tokens.astype(f32)], axis=0)

    z_p, xbc_p, cq_p, ckv_p, kr_p, dtt_p = _project(xp, g_attn, w_proj, 512)
    z_s, xbc_s, cq_s, ckv_s, kr_s, dtt_s = _project(x_small, g_attn, w_proj, n_small)

    tabs_p = _rope_tables(N_META + jnp.arange(seq))
    pos_small = jnp.concatenate([jnp.full((nseq,), past), jnp.zeros((meta_lo - nseq,), jnp.int32), jnp.arange(N_META)])
    tabs_s = _rope_tables(pos_small)
    tm = 512
    lat_p, kro_p, q_p, k_p, vt_p = _mla_prep(cq_p, ckv_p, kr_p, tabs_p, seq // tm, gqn, gkv, wuq, wuk, wuv.T, gq, gk, tm, bf16)
    lat_s, kro_s, q_s, k_s, vt_s = _mla_prep(cq_s, ckv_s, kr_s, tabs_s, 1, gqn, gkv, wuq, wuk, wuv.T, gq, gk, n_small, f32)

    zero_h = jnp.zeros((1, 4, LANE, LANE), f32)
    zero_hist = jnp.zeros((SUBLANE, CONV_DIM), f32)
    _, h_meta = _ssd(xbc_s, z_s, dtt_s, zero_hist, 0, zero_h, ssd_consts, 1, 1, 1, SSD_CHUNK - N_META)
    ssd_p, h_fin = _ssd(xbc_p, z_p, dtt_p, xbc_s, n_small // SUBLANE - 1, h_meta, ssd_consts, nb, seq // SSD_CHUNK, 0, 0)

    att_p = _flash(q_p, k_p, vt_p, k_s, vt_s[:, meta_lo:], n_small // N_META - 1, nb, seq, 512, 512, 4)
    y_prompt = _finish(xp, ssd_p, att_p, gm, wo, gf, wg, wu, wd, 512).reshape(nb, seq, D_MODEL)

    sc = jnp.transpose(state_conv[l], (1, 0, 2))
    xs_s, xat_s = _sssd_prep(xbc_s, sc, cw, cb, nseq)
    h_new_t, yt_s = _sssd_state(jnp.transpose(state_ssm[l], (1, 2, 3, 0)), xat_s, dtt_s, bias_c, alog_c, nseq)
    h_new = jnp.transpose(h_new_t, (3, 0, 1, 2))
    ssd_s = _sssd_gate(yt_s, xs_s, z_s, dsk, ng, nseq)
    qg, a_abs, s_new = _absorb(q_s, gk, wuk_f32.T, lat_s, kro_s, wuk, nseq)
    a_abs = a_abs.reshape(nseq, 1, 16, KV_LORA)
    c_abs = qg.reshape(nseq, MLA_HEADS, HT)[:, :, ROPE_LO:QK_DIM]
    lat_new = lat_s[:nseq][:, None, :]
    kr_new = kro_s[:nseq, ROPE_LO:QK_DIM][:, None, :]
    o_lat = _decode(page_table.reshape(-1), a_abs, c_abs, lat_new, s_new[:, :, None], wukt,
                    cache_kv_latent, jnp.swapaxes(cache_k_rope, 2, 3), nseq, n_pages, page, DECODE_PAGES_PER_TILE)
    att_s = _uv(o_lat.reshape(nseq, MLA_HEADS * KV_LORA), wuv)
    y_sample = _finish(xs_rows, ssd_s, att_s, gm, wo, gf, wg, wu, wd, nseq)[:, None, :]

    def with_meta(small, main, width):
        meta = jnp.broadcast_to(small[meta_lo:][None], (nb, N_META, width))
        return jnp.concatenate([meta, main.reshape(nb, seq, width)], axis=1)[None]

    kv_latent_prompt = with_meta(lat_s, lat_p, KV_LORA)
    k_rope_prompt = with_meta(kro_s[:, ROPE_LO:QK_DIM], kro_p[:, ROPE_LO:QK_DIM], QK_ROPE)
    conv_prompt = xbc_p.reshape(nb, seq, CONV_DIM)[:, seq - (CONV_W - 1):][None]
    hf = h_fin.reshape(nb, 4, 2, SSD_HEAD_DIM, SSD_GROUPS, D_STATE)
    ssm_prompt = jnp.stack([hf[:, i, :, :, i // 2, :] for i in range(4)], axis=1).reshape(nb, SSD_HEADS, SSD_HEAD_DIM, D_STATE)[None]
    kv_latent_sample = lat_new[None]
    k_rope_sample = kr_new[None]
    conv_sample = jnp.concatenate([state_conv[l][:, 1:], xbc_s[:nseq][:, None, :]], axis=1)[None]
    ssm_sample = h_new[None]
    return (y_prompt, y_sample, kv_latent_prompt, k_rope_prompt, conv_prompt, ssm_prompt.astype(x_prompt.dtype),
            kv_latent_sample, k_rope_sample, conv_sample, ssm_sample.astype(state_ssm.dtype))
```

```python
import functools

import jax
import jax.numpy as jnp
from jax import lax
from jax.experimental import pallas as pl
from jax.experimental.pallas import tpu as pltpu

f32 = jnp.float32
bf16 = jnp.bfloat16

D_MODEL = 1024
N_META = 16
SSD_HEADS = 8
SSD_HEAD_DIM = 64
SSD_WIDTH = SSD_HEADS * SSD_HEAD_DIM
SSD_GROUPS = 2
D_STATE = 64
CONV_W = 4
CONV_DIM = SSD_WIDTH + 2 * SSD_GROUPS * D_STATE
SSD_CHUNK = 128
MLA_HEADS = 8
QK_NOPE = 64
QK_ROPE = 32
QK_DIM = QK_NOPE + QK_ROPE
V_DIM = 64
MLA_WIDTH = MLA_HEADS * V_DIM
Q_LORA = 384
KV_LORA = 256
ROPE_THETA = 10000.0
ATTN_SCALE = QK_DIM ** -0.5
D_FF = 2816
EPS = 1e-6

LANE = 128
SUBLANE = 8
HT = LANE
ROPE_LO = QK_NOPE
ROPE_HALF = QK_ROPE // 2
FF_CHUNK = 256
N_FF_CHUNKS = D_FF // FF_CHUNK
DECODE_PAGES_PER_TILE = 32
MiB = 1024 * 1024

PC_Z = 0
PC_XBC = PC_Z + SSD_WIDTH
PC_CQ = PC_XBC + CONV_DIM
PC_CKV = PC_CQ + Q_LORA
PC_KR = PC_CKV + KV_LORA
PC_END = PC_KR + LANE

_NT = (((1,), (1,)), ((), ()))
_TN = (((0,), (0,)), ((), ()))
_HI = lax.Precision.HIGHEST


def _dot(a, b):
    return jnp.dot(a, b, preferred_element_type=f32)


def _dot_nt(a, b):
    return lax.dot_general(a, b, _NT, preferred_element_type=f32)


def _dot_tn(a, b):
    return lax.dot_general(a, b, _TN, preferred_element_type=f32)


def _rms(x, n):
    return lax.rsqrt(jnp.sum(x * x, axis=-1, keepdims=True) * (1.0 / n) + EPS)


def _silu(x):
    return x * jax.nn.sigmoid(x)


def _softplus(x):
    return jnp.maximum(x, 0.0) + jnp.log1p(jnp.exp(-jnp.abs(x)))


HEAD_ROW = (3, 2, 1, 0, 7, 6, 5, 4)


def _fold_heads(part):
    row = lax.broadcasted_iota(jnp.int32, part.shape[1:], 0)
    v = [part[h] for h in range(8)]
    v = [jnp.where(row < 4, v[j] + pltpu.roll(v[j], 4, 0), v[j + 4] + pltpu.roll(v[j + 4], 4, 0)) for j in range(4)]
    v = [jnp.where(row % 4 >= 2, v[j] + pltpu.roll(v[j], 2, 0), v[j + 2] + pltpu.roll(v[j + 2], 6, 0)) for j in range(2)]
    return jnp.where(row % 2 == 1, v[0] + pltpu.roll(v[0], 1, 0), v[1] + pltpu.roll(v[1], 7, 0))


def _const_spec(shape):
    nd = len(shape)
    return pl.BlockSpec(shape, lambda *_: (0,) * nd, pipeline_mode=pl.Buffered(1))


def _params(sem, vmem_mib):
    return pltpu.CompilerParams(dimension_semantics=sem, vmem_limit_bytes=vmem_mib * MiB)


def _proj_body(x_ref, g_ref, w_ref, z_ref, xbc_ref, cq_ref, ckv_ref, kr_ref, dtt_ref):
    x = x_ref[...]
    xn = (x * _rms(x, D_MODEL) * g_ref[...]).astype(bf16)
    z_ref[...] = _dot(xn, w_ref[:, PC_Z:PC_XBC])
    xbc_ref[...] = _dot(xn, w_ref[:, PC_XBC:PC_CQ])
    cq_ref[...] = _dot(xn, w_ref[:, PC_CQ:PC_CKV])
    ckv_ref[...] = _dot(xn, w_ref[:, PC_CKV:PC_KR])
    last = _dot(xn, w_ref[:, PC_KR:PC_END])
    lane = lax.broadcasted_iota(jnp.int32, (1, LANE), 1)
    kr_ref[...] = jnp.where(lane >= ROPE_LO, last, 0.0)
    dtt_ref[...] = last.T[0:16, :]


def _project(x, g, w, tm):
    m = x.shape[0]
    row = lambda n: pl.BlockSpec((tm, n), lambda i: (i, 0))
    widths = (SSD_WIDTH, CONV_DIM, Q_LORA, KV_LORA, LANE)
    return pl.pallas_call(
        _proj_body,
        grid=(m // tm,),
        in_specs=[row(D_MODEL), _const_spec((1, D_MODEL)), _const_spec((D_MODEL, PC_END))],
        out_specs=[row(n) for n in widths] + [pl.BlockSpec((16, tm), lambda i: (0, i))],
        out_shape=[jax.ShapeDtypeStruct((m, n), f32) for n in widths] + [jax.ShapeDtypeStruct((16, m), f32)],
        compiler_params=_params(("parallel",), 40),
        name="proj",
    )(x, g, w)


def _mla_prep_body(cq_ref, ckv_ref, kr_ref, tc_ref, ts1_ref, ts2_ref, gqn_ref, gkv_ref, wuq_ref, wuk_ref, wuv_ref,
                   gq_ref, gk_ref, lat_ref, kro_ref, q_ref, k_ref, vt_ref):
    tc, ts1, ts2 = tc_ref[...], ts1_ref[...], ts2_ref[...]
    tsw = ts1 + ts2

    def rope(x):
        return x * tc + pltpu.roll(x, LANE - ROPE_HALF, 1) * ts1 + pltpu.roll(x, ROPE_HALF, 1) * ts2

    ckv = ckv_ref[...]
    lat = ckv * _rms(ckv, KV_LORA) * gkv_ref[...]
    lat_ref[...] = lat
    kr = rope(kr_ref[...])
    kro_ref[...] = kr
    cq = cq_ref[...]
    cqn = (cq * _rms(cq, Q_LORA) * gqn_ref[...]).astype(bf16)
    latb = lat.astype(bf16)
    vt_ref[...] = _dot_nt(wuv_ref[...], latb).astype(vt_ref.dtype)
    gq, gk = gq_ref[...], gk_ref[...]
    q_all = _dot(cqn, wuq_ref[:, 0:MLA_HEADS * HT])
    qp_all = _dot(cqn, wuq_ref[:, MLA_HEADS * HT:])
    k_all = _dot(latb, wuk_ref[...])
    for h in range(MLA_HEADS):
        sl = slice(h * HT, (h + 1) * HT)
        qh = q_all[:, sl] * tc + qp_all[:, sl] * tsw
        q_ref[:, sl] = (qh * _rms(qh, QK_DIM) * gq).astype(q_ref.dtype)
        kh = k_all[:, sl] + kr
        k_ref[:, sl] = (kh * _rms(kh, QK_DIM) * gk).astype(k_ref.dtype)


def _mla_prep(cq, ckv, kr, tabs, n_tab_blocks, gqn, gkv, wuq, wuk, wuvt, gq, gk, tm, q_dtype):
    m = cq.shape[0]
    row = lambda n: pl.BlockSpec((tm, n), lambda i: (i, 0))
    tab = pl.BlockSpec((tm, LANE), lambda i: (i % n_tab_blocks, 0))
    widths = (KV_LORA, LANE, MLA_HEADS * HT, MLA_HEADS * HT)
    dtypes = (f32, f32, q_dtype, bf16)
    return pl.pallas_call(
        _mla_prep_body,
        grid=(m // tm,),
        in_specs=[row(Q_LORA), row(KV_LORA), row(LANE), tab, tab, tab,
                  _const_spec((1, Q_LORA)), _const_spec((1, KV_LORA)),
                  _const_spec((Q_LORA, 2 * MLA_HEADS * HT)), _const_spec((KV_LORA, MLA_HEADS * HT)),
                  _const_spec((MLA_WIDTH, KV_LORA)), _const_spec((1, HT)), _const_spec((1, HT))],
        out_specs=[row(n) for n in widths] + [pl.BlockSpec((MLA_WIDTH, tm), lambda i: (0, i))],
        out_shape=[jax.ShapeDtypeStruct((m, n), d) for n, d in zip(widths, dtypes)]
                  + [jax.ShapeDtypeStruct((MLA_WIDTH, m), bf16)],
        compiler_params=_params(("parallel",), 40),
        name="mla_prep",
    )(cq, ckv, kr, *tabs, gqn, gkv, wuq, wuk, wuvt, gq, gk)


def _ssd_body(xbc_ref, z_ref, dtt_ref, hist_ref, h0_ref, cw_ref, cb_ref, bias_c_ref,
              alog_c_ref, dsk_ref, ng_ref, tri_ref, y_ref, hout_ref, xp_ref, hs_ref, *, valid_from):
    c = pl.program_id(1)
    q = SSD_CHUNK

    @pl.when(c == 0)
    def _():
        xp_ref[0:SUBLANE, :] = hist_ref[...]
        hs_ref[...] = h0_ref[0]

    xbc = xbc_ref[...]
    xp_ref[SUBLANE:SUBLANE + q, :] = xbc
    conv = cb_ref[...]
    for k in range(CONV_W):
        lo = SUBLANE - (CONV_W - 1) + k
        conv = conv + xp_ref[lo:lo + q, :] * cw_ref[k:k + 1, :]
    xp_ref[0:SUBLANE, :] = xbc[q - SUBLANE:q, :]
    xa = _silu(conv)
    xs = xa[:, :SSD_WIDTH]
    bm = xa[:, SSD_WIDTH:SSD_WIDTH + LANE]
    cm = xa[:, SSD_WIDTH + LANE:]

    rows = lax.broadcasted_iota(jnp.int32, (q, q), 0)
    cols = lax.broadcasted_iota(jnp.int32, (q, q), 1)
    lane = lax.broadcasted_iota(jnp.int32, (1, LANE), 1)
    low = lane < D_STATE

    dtr = _softplus(dtt_ref[...] + bias_c_ref[...])
    if valid_from:
        dtr = jnp.where(lax.broadcasted_iota(jnp.int32, (16, q), 1) >= valid_from, dtr, 0.0)
    dar = dtr * -jnp.exp(alog_c_ref[...])
    tri = tri_ref[...]
    dtc = dtr.T
    acs_r = lax.dot_general(dar, tri, _NT, precision=_HI, preferred_element_type=f32)
    acs_c = jnp.dot(tri, dar.T, precision=_HI, preferred_element_type=f32)
    w_c = jnp.exp(acs_c[q - 1:q, :] - acs_c) * dtc
    e_c = jnp.exp(acs_c)
    cd_r = jnp.exp(acs_r[:, q - 1:q])

    causal = cols <= rows
    bb = bm.astype(bf16)
    dsk = dsk_ref[...]
    ys = []
    for g in range(SSD_GROUPS):
        gmask = (lane >= g * D_STATE) & (lane < (g + 1) * D_STATE)
        cg = jnp.where(gmask, cm, 0.0).astype(bf16)
        cb_g = _dot_nt(cg, bb)
        for pi in range(2):
            i = 2 * g + pi
            xpair = xs[:, i * LANE:(i + 1) * LANE]
            ypair = jnp.zeros((q, LANE), f32)
            for hh in range(2):
                h = 2 * i + hh
                seg = acs_c[:, h:h + 1] - acs_r[h:h + 1, :]
                lmat = jnp.exp(jnp.where(causal, seg, -jnp.inf))
                sc = (cb_g * lmat * dtr[h:h + 1, :]).astype(bf16)
                xh = jnp.where(low if hh == 0 else jnp.logical_not(low), xpair, 0.0).astype(bf16)
                ypair = ypair + _dot(sc, xh)
            h0, h1 = 2 * i, 2 * i + 1
            wp = jnp.where(low, w_c[:, h0:h0 + 1], w_c[:, h1:h1 + 1])
            ep = jnp.where(low, e_c[:, h0:h0 + 1], e_c[:, h1:h1 + 1])
            hst = hs_ref[i]
            yoff = _dot_nt(cg, hst.astype(bf16)) * ep
            st = _dot_tn((xpair * wp).astype(bf16), bb)
            cdb = jnp.concatenate([jnp.broadcast_to(cd_r[h0:h0 + 1, :], (SSD_HEAD_DIM, LANE)),
                                   jnp.broadcast_to(cd_r[h1:h1 + 1, :], (SSD_HEAD_DIM, LANE))], axis=0)
            hs_ref[i] = hst * cdb + st
            ys.append(ypair + yoff + dsk[:, i * LANE:(i + 1) * LANE] * xpair)
    y = jnp.concatenate(ys, axis=1) * _silu(z_ref[...])
    gw = SSD_WIDTH // SSD_GROUPS
    outs = []
    for g in range(SSD_GROUPS):
        yg = y[:, g * gw:(g + 1) * gw]
        outs.append(yg * _rms(yg, gw))
    y_ref[...] = (jnp.concatenate(outs, axis=1) * ng_ref[...]).astype(y_ref.dtype)

    @pl.when(c == pl.num_programs(1) - 1)
    def _():
        hout_ref[0] = hs_ref[...]


def _ssd(xbc, z, dtt, hist_arr, hist_block, h0, consts, n_batch, n_chunks, row_block0, valid_from):
    q = SSD_CHUNK
    rb = lambda n: pl.BlockSpec((q, n), lambda b, c: (row_block0 + b * n_chunks + c, 0))
    cw, cb, bias_c, alog_c, dsk, ng, tri = consts
    return pl.pallas_call(
        functools.partial(_ssd_body, valid_from=valid_from),
        grid=(n_batch, n_chunks),
        in_specs=[rb(CONV_DIM), rb(SSD_WIDTH),
                  pl.BlockSpec((16, q), lambda b, c: (0, row_block0 + b * n_chunks + c)),
                  pl.BlockSpec((SUBLANE, CONV_DIM), lambda b, c: (hist_block, 0)),
                  _const_spec((1, 4, LANE, LANE)),
                  _const_spec((CONV_W, CONV_DIM)), _const_spec((1, CONV_DIM)),
                  _const_spec((16, 1)), _const_spec((16, 1)),
                  _const_spec((1, SSD_WIDTH)), _const_spec((1, SSD_WIDTH)), _const_spec((q, q))],
        out_specs=[pl.BlockSpec((q, SSD_WIDTH), lambda b, c: (b * n_chunks + c, 0)),
                   pl.BlockSpec((1, 4, LANE, LANE), lambda b, c: (b, 0, 0, 0))],
        out_shape=[jax.ShapeDtypeStruct((n_batch * n_chunks * q, SSD_WIDTH), bf16),
                   jax.ShapeDtypeStruct((n_batch, 4, LANE, LANE), f32)],
        scratch_shapes=[pltpu.VMEM((SUBLANE + q, CONV_DIM), f32), pltpu.VMEM((4, LANE, LANE), f32)],
        compiler_params=_params(("parallel", "arbitrary"), 32),
        name="ssd",
    )(xbc, z, dtt, hist_arr, h0, cw, cb, bias_c, alog_c, dsk, ng, tri)


def _flash_body(q_ref, k_ref, vt_ref, km_ref, vtm_ref, o_ref, *, tq, tk, nh):
    qi = pl.program_id(2)
    qs = [q_ref[:, h * HT:(h + 1) * HT] for h in range(nh)]
    drow = lax.broadcasted_iota(jnp.int32, (2 * V_DIM, 1), 0)
    vmasks = (drow < V_DIM, drow >= V_DIM)
    half = tk // 2

    def heads(kall, vtall, carries, mask=None, qlo=0):
        sts = [_dot_nt(kall[:, h * HT:(h + 1) * HT], qs[h][qlo:, :]) for h in range(nh)]
        out = []
        for h in range(nh):
            m0, l0, acc0 = carries[h]
            m, l, acct = m0[:, qlo:], l0[:, qlo:], acc0[:, qlo:]
            st = sts[h] if mask is None else jnp.where(mask, sts[h], -jnp.inf)
            m2 = jnp.maximum(m, jnp.max(st, axis=0, keepdims=True))
            pt = jnp.exp(st - m2)
            a = jnp.exp(m - m2)
            vtt = vtall[(h // 2) * LANE:(h // 2 + 1) * LANE, :]
            vz = jnp.where(vmasks[h % 2], vtt, jnp.zeros_like(vtt))
            new = (m2, a * l + jnp.sum(pt, axis=0, keepdims=True), a * acct + _dot(vz, pt.astype(bf16)))
            if qlo:
                new = tuple(jnp.concatenate([old[:, :qlo], x], axis=1) for old, x in zip(carries[h], new))
            out.append(new)
        return tuple(out)

    init = (jnp.full((1, tq), -jnp.inf, f32), jnp.zeros((1, tq), f32), jnp.zeros((2 * V_DIM, tq), f32))

    def body(j, carries):
        off = pl.multiple_of(j * tk, tk)
        return heads(k_ref[pl.ds(off, tk), :], vt_ref[:, pl.ds(off, tk)], carries)

    carries = lax.fori_loop(0, qi, body, (init,) * nh)
    off = pl.multiple_of(qi * tk, tk)
    k_a = jnp.concatenate([k_ref[pl.ds(off, half), :], km_ref[...]], axis=0)
    vt_a = jnp.concatenate([vt_ref[:, pl.ds(off, half)], vtm_ref[...]], axis=1)
    krow = lax.broadcasted_iota(jnp.int32, (half + N_META, tq), 0)
    qcol = lax.broadcasted_iota(jnp.int32, (half + N_META, tq), 1)
    carries = heads(k_a, vt_a, carries, (krow >= half) | (krow <= qcol))
    off_b = pl.multiple_of(qi * tk + half, half)
    krow = lax.broadcasted_iota(jnp.int32, (half, tq - half), 0)
    qcol = lax.broadcasted_iota(jnp.int32, (half, tq - half), 1)
    res = heads(k_ref[pl.ds(off_b, half), :], vt_ref[:, pl.ds(off_b, half)], carries, krow <= qcol, qlo=half)
    for pr in range(nh // 2):
        (_, l0, acc0), (_, l1, acc1) = res[2 * pr], res[2 * pr + 1]
        o_ref[:, pr * LANE:(pr + 1) * LANE] = (acc0 / l0 + acc1 / l1).T.astype(o_ref.dtype)


def _flash(q, k, vt, k_small, vt_meta, meta_block, n_batch, seq, tq, tk, nh):
    assert tk == tq and seq % tk == 0 and nh % 2 == 0 and MLA_HEADS % nh == 0
    nq = seq // tq
    return pl.pallas_call(
        functools.partial(_flash_body, tq=tq, tk=tk, nh=nh),
        grid=(n_batch, MLA_HEADS // nh, nq),
        in_specs=[pl.BlockSpec((tq, nh * HT), lambda b, p, i: (b * nq + i, p)),
                  pl.BlockSpec((seq, nh * HT), lambda b, p, i: (b, p)),
                  pl.BlockSpec((nh * V_DIM, seq), lambda b, p, i: (p, b)),
                  pl.BlockSpec((N_META, nh * HT), lambda b, p, i: (meta_block, p)),
                  pl.BlockSpec((nh * V_DIM, N_META), lambda b, p, i: (p, 0))],
        out_specs=pl.BlockSpec((tq, nh * V_DIM), lambda b, p, i: (b * nq + i, p)),
        out_shape=jax.ShapeDtypeStruct((n_batch * seq, MLA_WIDTH), f32),
        compiler_params=_params(("parallel", "parallel", "arbitrary"), 40),
        name="flash",
    )(q, k, vt, k_small, vt_meta)


def _finish_body(x_ref, ssd_ref, att_ref, gm_ref, wo_ref, gf_ref, wg_ref, wu_ref, wd_ref, o_ref):
    att = att_ref[...]
    mla = (att * _rms(att, MLA_WIDTH) * gm_ref[...]).astype(bf16)
    h = x_ref[...] + (_dot(ssd_ref[...], wo_ref[0:SSD_WIDTH, :]) + _dot(mla, wo_ref[SSD_WIDTH:, :]))
    n = (h * _rms(h, D_MODEL) * gf_ref[...]).astype(bf16)
    ff = jnp.zeros_like(h)
    for c in range(N_FF_CHUNKS):
        cs = slice(c * FF_CHUNK, (c + 1) * FF_CHUNK)
        a = (_silu(_dot(n, wg_ref[:, cs])) * _dot(n, wu_ref[:, cs])).astype(bf16)
        ff = ff + _dot(a, wd_ref[cs, :])
    o_ref[...] = h + ff


def _finish(x, ssd, att, gm, wo, gf, wg, wu, wd, tm):
    m = x.shape[0]
    row = lambda n: pl.BlockSpec((tm, n), lambda i: (i, 0))
    return pl.pallas_call(
        _finish_body,
        grid=(m // tm,),
        in_specs=[row(D_MODEL), row(SSD_WIDTH), row(MLA_WIDTH), _const_spec((1, MLA_WIDTH)),
                  _const_spec((D_MODEL, D_MODEL)), _const_spec((1, D_MODEL)),
                  _const_spec((D_MODEL, D_FF)), _const_spec((D_MODEL, D_FF)), _const_spec((D_FF, D_MODEL))],
        out_specs=row(D_MODEL),
        out_shape=jax.ShapeDtypeStruct((m, D_MODEL), f32),
        compiler_params=_params(("parallel",), 56),
        name="finish",
    )(x, ssd, att, gm, wo, gf, wg, wu, wd)


def _sssd_prep_body(xbc_ref, sc_ref, cw_ref, cb_ref, xs_ref, xat_ref):
    conv = cb_ref[...]
    for k in range(CONV_W - 1):
        conv = conv + sc_ref[k] * cw_ref[k:k + 1, :]
    conv = conv + xbc_ref[...] * cw_ref[CONV_W - 1:CONV_W, :]
    xa = _silu(conv)
    xs_ref[...] = xa[:, :SSD_WIDTH]
    xat_ref[...] = xa.T


def _sssd_prep(xbc_small, sc, cw, cb, nseq):
    return pl.pallas_call(
        _sssd_prep_body,
        grid=(1,),
        in_specs=[pl.BlockSpec((nseq, CONV_DIM), lambda i: (0, 0)), _const_spec((CONV_W - 1, nseq, CONV_DIM)),
                  _const_spec((CONV_W, CONV_DIM)), _const_spec((1, CONV_DIM))],
        out_specs=[pl.BlockSpec((nseq, SSD_WIDTH), lambda i: (0, 0)), pl.BlockSpec((CONV_DIM, nseq), lambda i: (0, 0))],
        out_shape=[jax.ShapeDtypeStruct((nseq, SSD_WIDTH), f32), jax.ShapeDtypeStruct((CONV_DIM, nseq), f32)],
        compiler_params=_params(("arbitrary",), 32),
        name="sssd_prep",
    )(xbc_small, sc, cw, cb)


def _sssd_state_body(h0_ref, xst_ref, bt_ref, ct_ref, dtt_ref, bias_c_ref, alog_c_ref, hn_ref, yt_ref):
    h = pl.program_id(0)
    dt = _softplus(dtt_ref[pl.ds(h, 1), :] + bias_c_ref[pl.ds(h, 1), :])
    dec = jnp.exp(dt * -jnp.exp(alog_c_ref[pl.ds(h, 1), :]))
    bt, ct = bt_ref[...], ct_ref[...]

    def body(p, carry):
        xdt = xst_ref[pl.ds(p, 1), :] * dt
        hn = h0_ref[0, p] * dec + xdt * bt
        hn_ref[0, p] = hn
        yt_ref[pl.ds(p, 1), :] = jnp.sum(ct * hn, axis=0, keepdims=True)
        return carry

    lax.fori_loop(0, SSD_HEAD_DIM, body, 0, unroll=4)


def _sssd_state(h0t, xat, dtt_small, bias_c, alog_c, nseq):
    hpg = SSD_HEADS // SSD_GROUPS
    rows = lambda f: pl.BlockSpec((SSD_HEAD_DIM, nseq), f)
    hblk = pl.BlockSpec((1, SSD_HEAD_DIM, D_STATE, nseq), lambda h: (h, 0, 0, 0))
    return pl.pallas_call(
        _sssd_state_body,
        grid=(SSD_HEADS,),
        in_specs=[hblk, rows(lambda h: (h, 0)), rows(lambda h: (SSD_HEADS + h // hpg, 0)),
                  rows(lambda h: (SSD_HEADS + SSD_GROUPS + h // hpg, 0)),
                  pl.BlockSpec((16, nseq), lambda h: (0, 0)), _const_spec((16, 1)), _const_spec((16, 1))],
        out_specs=[hblk, rows(lambda h: (h, 0))],
        out_shape=[jax.ShapeDtypeStruct((SSD_HEADS, SSD_HEAD_DIM, D_STATE, nseq), f32),
                   jax.ShapeDtypeStruct((SSD_WIDTH, nseq), f32)],
        compiler_params=_params(("parallel",), 32),
        name="sssd_state",
    )(h0t, xat, xat, xat, dtt_small, bias_c, alog_c)


def _sssd_gate_body(yt_ref, xs_ref, z_ref, dsk_ref, ng_ref, o_ref):
    y = (yt_ref[...].T + dsk_ref[...] * xs_ref[...]) * _silu(z_ref[...])
    gw = SSD_WIDTH // SSD_GROUPS
    outs = []
    for g in range(SSD_GROUPS):
        yg = y[:, g * gw:(g + 1) * gw]
        outs.append(yg * _rms(yg, gw))
    o_ref[...] = (jnp.concatenate(outs, axis=1) * ng_ref[...]).astype(o_ref.dtype)


def _sssd_gate(yt, xs, z_small, dsk, ng, nseq):
    blk = pl.BlockSpec((nseq, SSD_WIDTH), lambda i: (0, 0))
    return pl.pallas_call(
        _sssd_gate_body,
        grid=(1,),
        in_specs=[pl.BlockSpec((SSD_WIDTH, nseq), lambda i: (0, 0)), blk, blk,
                  _const_spec((1, SSD_WIDTH)), _const_spec((1, SSD_WIDTH))],
        out_specs=blk,
        out_shape=jax.ShapeDtypeStruct((nseq, SSD_WIDTH), bf16),
        compiler_params=_params(("arbitrary",), 32),
        name="sssd_gate",
    )(yt, xs, z_small, dsk, ng)


def _absorb_body(q_ref, gk_ref, wabs_ref, lat_ref, kr_ref, wuk_ref, qg_ref, a_ref, snew_ref):
    gk = gk_ref[...]
    latb = lat_ref[...].astype(bf16)
    latf = latb.astype(f32)
    kr = kr_ref[...]
    krf = kr.astype(bf16).astype(f32)
    kr2 = jnp.sum(kr * kr, axis=-1, keepdims=True)
    s_new = []
    for h in range(MLA_HEADS):
        sl = slice(h * HT, (h + 1) * HT)
        qg = q_ref[:, sl] * gk
        qg_ref[:, sl] = qg
        a = jnp.dot(qg, wabs_ref[sl, :], precision=_HI, preferred_element_type=f32)
        a_ref[:, h * KV_LORA:(h + 1) * KV_LORA] = a
        kn = _dot(latb, wuk_ref[:, sl])
        n2 = jnp.sum(kn * kn, axis=-1, keepdims=True)
        s12 = (jnp.sum(a.astype(bf16).astype(f32) * latf, axis=-1, keepdims=True)
               + jnp.sum(qg.astype(bf16).astype(f32) * krf, axis=-1, keepdims=True))
        s_new.append(lax.rsqrt((n2 + kr2) * (1.0 / QK_DIM) + EPS) * s12)
    a_ref[:, MLA_HEADS * KV_LORA:] = jnp.zeros((q_ref.shape[0], (16 - MLA_HEADS) * KV_LORA), f32)
    snew_ref[...] = jnp.concatenate(s_new, axis=1)


def _absorb(q_small, gk, wabs, lat_small, kr_small, wuk, nseq):
    blk = lambda n: pl.BlockSpec((nseq, n), lambda i: (0, 0))
    return pl.pallas_call(
        _absorb_body,
        grid=(1,),
        in_specs=[blk(MLA_HEADS * HT), _const_spec((1, HT)), _const_spec((MLA_HEADS * HT, KV_LORA)),
                  blk(KV_LORA), blk(HT), _const_spec((KV_LORA, MLA_HEADS * HT))],
        out_specs=[blk(MLA_HEADS * HT), blk(16 * KV_LORA), blk(MLA_HEADS)],
        out_shape=[jax.ShapeDtypeStruct((nseq, MLA_HEADS * HT), f32), jax.ShapeDtypeStruct((nseq, 16 * KV_LORA), f32),
                   jax.ShapeDtypeStruct((nseq, MLA_HEADS), f32)],
        compiler_params=_params(("arbitrary",), 32),
        name="absorb",
    )(q_small, gk, wabs, lat_small, kr_small, wuk)


def _decode_body(pt_ref, a_ref, c_ref, latn_ref, snew_ref, wukt_ref, clat_ref, ckrt_ref, o_ref,
                 lat_buf, kr_buf, waug, latb, s_buf, sems, *, n_pages, page, ppt):
    b = pl.program_id(0)
    nb = pl.num_programs(0)
    slot = b % 2
    n_tiles = n_pages // ppt
    tile = ppt * page
    wrows = MLA_HEADS * QK_NOPE

    def start_page(seq, p, sl):
        pid = pt_ref[seq * n_pages + p]
        pltpu.make_async_copy(clat_ref.at[0, pid], lat_buf.at[sl, p], sems.at[0, sl]).start()
        pltpu.make_async_copy(ckrt_ref.at[0, pid], kr_buf.at[sl, p], sems.at[1, sl]).start()

    @pl.when(b == 0)
    def _():
        waug[0:wrows, :] = wukt_ref[...]

        def first(p, carry):
            start_page(0, p, 0)
            return carry

        lax.fori_loop(0, n_pages, first, 0)

    pltpu.make_async_copy(clat_ref.at[0, pl.ds(0, n_pages)], lat_buf.at[slot], sems.at[0, slot]).wait()
    pltpu.make_async_copy(ckrt_ref.at[0, pl.ds(0, n_pages)], kr_buf.at[slot], sems.at[1, slot]).wait()

    waug[wrows:wrows + 16, :] = a_ref[0, 0].astype(bf16)
    cb = c_ref[0].astype(bf16)

    def score_dots(lb, krt):
        kt = _dot_nt(waug[...], lb)
        return kt, _dot(cb, krt.astype(bf16)), krt

    def score_finish(kt, s2, krt):
        n = kt.shape[1]
        k4 = kt[0:wrows, :].reshape(MLA_HEADS, QK_NOPE // SUBLANE, SUBLANE, n)
        n2 = _fold_heads(jnp.sum(k4 * k4, axis=1))
        kr2 = jnp.sum(krt * krt, axis=0, keepdims=True)
        return lax.rsqrt((n2 + kr2) * (1.0 / QK_DIM) + EPS) * (kt[wrows:wrows + MLA_HEADS, :] + s2)

    def body(j, carry):
        @pl.when(b + 1 < nb)
        def _():
            for pp in range(ppt):
                start_page(b + 1, j * ppt + pp, 1 - slot)

        lb = lat_buf[slot, pl.ds(j * ppt, ppt)].reshape(tile, KV_LORA).astype(bf16)
        latb[j] = lb
        kr_pages = kr_buf[slot, pl.ds(j * ppt, ppt)]
        s_buf[j] = score_finish(*score_dots(lb, jnp.concatenate([kr_pages[i] for i in range(ppt)], axis=-1)))
        return carry

    lax.fori_loop(0, n_tiles, body, 0)
    s_new = snew_ref[0]
    s_all = jnp.concatenate([s_buf[j] for j in range(n_tiles)], axis=-1)
    m = jnp.maximum(jnp.max(s_all, axis=-1, keepdims=True), s_new)
    p_all = jnp.exp(s_all - m)
    p_new = jnp.exp(s_new - m)
    l = jnp.sum(p_all, axis=-1, keepdims=True) + p_new
    acc = _dot(p_all.astype(bf16), latb[...].reshape(n_tiles * tile, KV_LORA)) + p_new * latn_ref[0]
    o_ref[0] = acc / l


def _decode(page_table_flat, a, c, lat_new, s_new, wukt, cache_lat, cache_kr_t, nseq, n_pages, page, ppt):
    assert n_pages % ppt == 0
    wrows = MLA_HEADS * QK_NOPE
    grid_spec = pltpu.PrefetchScalarGridSpec(
        num_scalar_prefetch=1,
        grid=(nseq,),
        in_specs=[pl.BlockSpec((1, 1, 16, KV_LORA), lambda b, pt: (b, 0, 0, 0)),
                  pl.BlockSpec((1, MLA_HEADS, QK_ROPE), lambda b, pt: (b, 0, 0)),
                  pl.BlockSpec((1, 1, KV_LORA), lambda b, pt: (b, 0, 0)),
                  pl.BlockSpec((1, MLA_HEADS, 1), lambda b, pt: (b, 0, 0)),
                  pl.BlockSpec((wrows, KV_LORA), lambda b, pt: (0, 0), pipeline_mode=pl.Buffered(1)),
                  pl.BlockSpec(memory_space=pl.ANY), pl.BlockSpec(memory_space=pl.ANY)],
        out_specs=pl.BlockSpec((1, MLA_HEADS, KV_LORA), lambda b, pt: (b, 0, 0)),
        scratch_shapes=[pltpu.VMEM((2, n_pages, page, KV_LORA), f32), pltpu.VMEM((2, n_pages, QK_ROPE, page), f32),
                        pltpu.VMEM((wrows + 16, KV_LORA), bf16),
                        pltpu.VMEM((n_pages // ppt, ppt * page, KV_LORA), bf16),
                        pltpu.VMEM((n_pages // ppt, MLA_HEADS, ppt * page), f32), pltpu.SemaphoreType.DMA((2, 2))],
    )
    return pl.pallas_call(
        functools.partial(_decode_body, n_pages=n_pages, page=page, ppt=ppt),
        grid_spec=grid_spec,
        out_shape=jax.ShapeDtypeStruct((nseq, MLA_HEADS, KV_LORA), f32),
        compiler_params=_params(("arbitrary",), 40),
        name="decode",
    )(page_table_flat, a, c, lat_new, s_new, wukt, cache_lat, cache_kr_t)


def _uv_body(o_ref, wuv_ref, att_ref):
    for h in range(MLA_HEADS):
        oh = o_ref[:, h * KV_LORA:(h + 1) * KV_LORA].astype(bf16)
        att_ref[:, h * V_DIM:(h + 1) * V_DIM] = _dot(oh, wuv_ref[:, h * V_DIM:(h + 1) * V_DIM])


def _uv(o_flat, wuv):
    nseq = o_flat.shape[0]
    return pl.pallas_call(
        _uv_body,
        grid=(1,),
        in_specs=[pl.BlockSpec((nseq, MLA_HEADS * KV_LORA), lambda i: (0, 0)), _const_spec((KV_LORA, MLA_WIDTH))],
        out_specs=pl.BlockSpec((nseq, MLA_WIDTH), lambda i: (0, 0)),
        out_shape=jax.ShapeDtypeStruct((nseq, MLA_WIDTH), f32),
        compiler_params=_params(("arbitrary",), 32),
        name="uv",
    )(o_flat, wuv)


def _rope_tables(pos):
    inv = ROPE_THETA ** (-jnp.arange(ROPE_HALF, dtype=f32) * (2.0 / QK_ROPE))
    ang = pos.astype(f32)[:, None] * inv[None, :]
    cos, sin = jnp.cos(ang), jnp.sin(ang)
    n = pos.shape[0]
    one = jnp.ones((n, QK_NOPE), f32)
    z16 = jnp.zeros((n, ROPE_HALF), f32)
    z32 = jnp.zeros((n, HT - QK_DIM), f32)
    z64 = jnp.zeros((n, QK_NOPE), f32)
    tc = jnp.concatenate([one, cos, cos, z32], axis=1)
    ts1 = jnp.concatenate([z64, -sin, z16, z32], axis=1)
    ts2 = jnp.concatenate([z64, z16, sin, z32], axis=1)
    return tc, ts1, ts2


def _pad_lanes(x, n):
    return jnp.pad(x, ((0, 0), (0, n - x.shape[1])))


def kernel(x_prompt, x_sample, cache_kv_latent, cache_k_rope, state_conv, state_ssm, page_table, meta_tokens, attn_norm_g, w_in, conv_w, conv_b, dt_bias, a_log, d_skip, ssd_norm_g, q_norm_g, w_uq, kv_norm_g, w_ukv, q_head_g_nope, q_head_g_rope, k_head_g_nope, k_head_g_rope, mla_out_g, w_out, ffn_norm_g, w_gate, w_up, w_down):
    l = 0
    nb, seq, _ = x_prompt.shape
    nseq = x_sample.shape[0]
    n_pages, page = page_table.shape[1], cache_kv_latent.shape[2]
    past = n_pages * page
    n_small = 2 * LANE
    meta_lo = n_small - N_META

    s0, s1, s2, s3, s4 = (SSD_WIDTH, SSD_WIDTH + CONV_DIM, SSD_WIDTH + CONV_DIM + SSD_HEADS,
                          SSD_WIDTH + CONV_DIM + SSD_HEADS + Q_LORA, SSD_WIDTH + CONV_DIM + SSD_HEADS + Q_LORA + KV_LORA)
    wi = w_in[l]
    w_last = jnp.concatenate([_pad_lanes(wi[:, s1:s2], ROPE_LO), _pad_lanes(wi[:, s4:], HT - ROPE_LO)], axis=1)
    w_proj = jnp.concatenate([wi[:, :s1], wi[:, s2:s4], w_last], axis=1).astype(bf16)
    g_attn = attn_norm_g[l][None, :]
    wq3 = w_uq[l].reshape(Q_LORA, MLA_HEADS, QK_DIM)
    zq = jnp.zeros((Q_LORA, MLA_HEADS, QK_NOPE), f32)
    wq_partner = jnp.concatenate([zq, wq3[:, :, QK_NOPE + ROPE_HALF:], wq3[:, :, QK_NOPE:QK_NOPE + ROPE_HALF]], axis=2)
    pad_head = lambda w: jnp.pad(w, ((0, 0), (0, 0), (0, HT - QK_DIM))).reshape(Q_LORA, -1)
    wuq = jnp.concatenate([pad_head(wq3), pad_head(wq_partner)], axis=1).astype(bf16)
    wkv = w_ukv[l].reshape(KV_LORA, MLA_HEADS, QK_NOPE + V_DIM)
    wuk_f32 = jnp.pad(wkv[:, :, :QK_NOPE], ((0, 0), (0, 0), (0, HT - QK_NOPE))).reshape(KV_LORA, -1)
    wuk = wuk_f32.astype(bf16)
    wuv = wkv[:, :, QK_NOPE:].reshape(KV_LORA, MLA_WIDTH).astype(bf16)
    wukt = wkv[:, :, :QK_NOPE].reshape(KV_LORA, -1).T.astype(bf16)
    gq = _pad_lanes(jnp.concatenate([q_head_g_nope[l], q_head_g_rope[l], q_head_g_rope[l]])[None, :], HT) * ATTN_SCALE
    gk = _pad_lanes(jnp.concatenate([k_head_g_nope[l], k_head_g_rope[l], k_head_g_rope[l]])[None, :], HT)
    gqn, gkv = q_norm_g[l][None, :], kv_norm_g[l][None, :]
    cw, cb = conv_w[l], conv_b[l][None, :]
    bias_c = jnp.pad(dt_bias[l][:, None], ((0, 16 - SSD_HEADS), (0, 0)))
    alog_c = jnp.pad(a_log[l][:, None], ((0, 16 - SSD_HEADS), (0, 0)))
    dsk = jnp.repeat(d_skip[l], SSD_HEAD_DIM)[None, :]
    ng = ssd_norm_g[l][None, :]
    tri = jnp.tril(jnp.ones((SSD_CHUNK, SSD_CHUNK), f32))
    ssd_consts = (cw, cb, bias_c, alog_c, dsk, ng, tri)
    gm, gf = mla_out_g[l][None, :], ffn_norm_g[l][None, :]
    wo = w_out[l].astype(bf16)
    wg, wu, wd = w_gate[l].astype(bf16), w_up[l].astype(bf16), w_down[l].astype(bf16)

    xp = x_prompt.reshape(nb * seq, D_MODEL)
    xs_rows = x_sample[:, 0, :]
    x_small = jnp.concatenate([xs_rows, jnp.zeros((meta_lo - nseq, D_MODEL), f32), meta_tokens.astype(f32)], axis=0)

    z_p, xbc_p, cq_p, ckv_p, kr_p, dtt_p = _project(xp, g_attn, w_proj, 512)
    z_s, xbc_s, cq_s, ckv_s, kr_s, dtt_s = _project(x_small, g_attn, w_proj, n_small)

    tabs_p = _rope_tables(N_META + jnp.arange(seq))
    pos_small = jnp.concatenate([jnp.full((nseq,), past), jnp.zeros((meta_lo - nseq,), jnp.int32), jnp.arange(N_META)])
    tabs_s = _rope_tables(pos_small)
    tm = 512
    lat_p, kro_p, q_p, k_p, vt_p = _mla_prep(cq_p, ckv_p, kr_p, tabs_p, seq // tm, gqn, gkv, wuq, wuk, wuv.T, gq, gk, tm, bf16)
    lat_s, kro_s, q_s, k_s, vt_s = _mla_prep(cq_s, ckv_s, kr_s, tabs_s, 1, gqn, gkv, wuq, wuk, wuv.T, gq, gk, n_small, f32)

    zero_h = jnp.zeros((1, 4, LANE, LANE), f32)
    zero_hist = jnp.zeros((SUBLANE, CONV_DIM), f32)
    _, h_meta = _ssd(xbc_s, z_s, dtt_s, zero_hist, 0, zero_h, ssd_consts, 1, 1, 1, SSD_CHUNK - N_META)
    ssd_p, h_fin = _ssd(xbc_p, z_p, dtt_p, xbc_s, n_small // SUBLANE - 1, h_meta, ssd_consts, nb, seq // SSD_CHUNK, 0, 0)

    att_p = _flash(q_p, k_p, vt_p, k_s, vt_s[:, meta_lo:], n_small // N_META - 1, nb, seq, 512, 512, 4)
    y_prompt = _finish(xp, ssd_p, att_p, gm, wo, gf, wg, wu, wd, 512).reshape(nb, seq, D_MODEL)

    sc = jnp.transpose(state_conv[l], (1, 0, 2))
    xs_s, xat_s = _sssd_prep(xbc_s, sc, cw, cb, nseq)
    h_new_t, yt_s = _sssd_state(jnp.transpose(state_ssm[l], (1, 2, 3, 0)), xat_s, dtt_s, bias_c, alog_c, nseq)
    h_new = jnp.transpose(h_new_t, (3, 0, 1, 2))
    ssd_s = _sssd_gate(yt_s, xs_s, z_s, dsk, ng, nseq)
    qg, a_abs, s_new = _absorb(q_s, gk, wuk_f32.T, lat_s, kro_s, wuk, nseq)
    hr = list(HEAD_ROW)
    a_abs = a_abs.reshape(nseq, 16, KV_LORA)[:, hr + list(range(MLA_HEADS, 16))].reshape(nseq, 1, 16, KV_LORA)
    c_abs = qg.reshape(nseq, MLA_HEADS, HT)[:, hr, ROPE_LO:QK_DIM]
    lat_new = lat_s[:nseq][:, None, :]
    kr_new = kro_s[:nseq, ROPE_LO:QK_DIM][:, None, :]
    o_lat = _decode(page_table.reshape(-1), a_abs, c_abs, lat_new, s_new[:, hr, None], wukt,
                    cache_kv_latent, jnp.swapaxes(cache_k_rope, 2, 3), nseq, n_pages, page, DECODE_PAGES_PER_TILE)
    att_s = _uv(o_lat[:, hr].reshape(nseq, MLA_HEADS * KV_LORA), wuv)
    y_sample = _finish(xs_rows, ssd_s, att_s, gm, wo, gf, wg, wu, wd, nseq)[:, None, :]

    def with_meta(small, main, width):
        meta = jnp.broadcast_to(small[meta_lo:][None], (nb, N_META, width))
        return jnp.concatenate([meta, main.reshape(nb, seq, width)], axis=1)[None]

    kv_latent_prompt = with_meta(lat_s, lat_p, KV_LORA)
    k_rope_prompt = with_meta(kro_s[:, ROPE_LO:QK_DIM], kro_p[:, ROPE_LO:QK_DIM], QK_ROPE)
    conv_prompt = xbc_p.reshape(nb, seq, CONV_DIM)[:, seq - (CONV_W - 1):][None]
    hf = h_fin.reshape(nb, 4, 2, SSD_HEAD_DIM, SSD_GROUPS, D_STATE)
    ssm_prompt = jnp.stack([hf[:, i, :, :, i // 2, :] for i in range(4)], axis=1).reshape(nb, SSD_HEADS, SSD_HEAD_DIM, D_STATE)[None]
    kv_latent_sample = lat_new[None]
    k_rope_sample = kr_new[None]
    conv_sample = jnp.concatenate([state_conv[l][:, 1:], xbc_s[:nseq][:, None, :]], axis=1)[None]
    ssm_sample = h_new[None]
    return (y_prompt, y_sample, kv_latent_prompt, k_rope_prompt, conv_prompt, ssm_prompt.astype(x_prompt.dtype),
            kv_latent_sample, k_rope_sample, conv_sample, ssm_sample.astype(state_ssm.dtype))
```

```python
import functools

import jax
import jax.numpy as jnp
from jax import lax
from jax.experimental import pallas as pl
from jax.experimental.pallas import tpu as pltpu

f32 = jnp.float32
bf16 = jnp.bfloat16

D_MODEL = 1024
N_META = 16
SSD_HEADS = 8
SSD_HEAD_DIM = 64
SSD_WIDTH = SSD_HEADS * SSD_HEAD_DIM
SSD_GROUPS = 2
D_STATE = 64
CONV_W = 4
CONV_DIM = SSD_WIDTH + 2 * SSD_GROUPS * D_STATE
SSD_CHUNK = 128
MLA_HEADS = 8
QK_NOPE = 64
QK_ROPE = 32
QK_DIM = QK_NOPE + QK_ROPE
V_DIM = 64
MLA_WIDTH = MLA_HEADS * V_DIM
Q_LORA = 384
KV_LORA = 256
ROPE_THETA = 10000.0
ATTN_SCALE = QK_DIM ** -0.5
D_FF = 2816
EPS = 1e-6

LANE = 128
SUBLANE = 8
HT = LANE
ROPE_LO = QK_NOPE
ROPE_HALF = QK_ROPE // 2
FF_CHUNK = 256
N_FF_CHUNKS = D_FF // FF_CHUNK
DECODE_PAGES_PER_TILE = 32
ROW_TILE = 512
FLASH_TILE = 512
FLASH_HEADS_PER_STEP = 4
MiB = 1024 * 1024

PC_Z = 0
PC_XBC = PC_Z + SSD_WIDTH
PC_CQ = PC_XBC + CONV_DIM
PC_CKV = PC_CQ + Q_LORA
PC_KR = PC_CKV + KV_LORA
PC_END = PC_KR + LANE

_NT = (((1,), (1,)), ((), ()))
_TN = (((0,), (0,)), ((), ()))
_HI = lax.Precision.HIGHEST


def _dot(a, b):
    return jnp.dot(a, b, preferred_element_type=f32)


def _dot_nt(a, b):
    return lax.dot_general(a, b, _NT, preferred_element_type=f32)


def _dot_tn(a, b):
    return lax.dot_general(a, b, _TN, preferred_element_type=f32)


def _rms(x, n):
    return lax.rsqrt(jnp.sum(x * x, axis=-1, keepdims=True) * (1.0 / n) + EPS)


def _silu(x):
    return x * jax.nn.sigmoid(x)


def _softplus(x):
    return jnp.maximum(x, 0.0) + jnp.log1p(jnp.exp(-jnp.abs(x)))


HEAD_ROW = (3, 2, 1, 0, 7, 6, 5, 4)


def _fold_heads(part):
    row = lax.broadcasted_iota(jnp.int32, part.shape[1:], 0)
    v = [part[h] for h in range(8)]
    v = [jnp.where(row < 4, v[j] + pltpu.roll(v[j], 4, 0), v[j + 4] + pltpu.roll(v[j + 4], 4, 0)) for j in range(4)]
    v = [jnp.where(row % 4 >= 2, v[j] + pltpu.roll(v[j], 2, 0), v[j + 2] + pltpu.roll(v[j + 2], 6, 0)) for j in range(2)]
    return jnp.where(row % 2 == 1, v[0] + pltpu.roll(v[0], 1, 0), v[1] + pltpu.roll(v[1], 7, 0))


def _const_spec(shape):
    nd = len(shape)
    return pl.BlockSpec(shape, lambda *_: (0,) * nd, pipeline_mode=pl.Buffered(1))


def _params(sem, vmem_mib):
    return pltpu.CompilerParams(dimension_semantics=sem, vmem_limit_bytes=vmem_mib * MiB)


def _proj_body(x_ref, g_ref, w_ref, z_ref, xbc_ref, cq_ref, ckv_ref, kr_ref, dtt_ref):
    x = x_ref[...]
    xn = (x * _rms(x, D_MODEL) * g_ref[...]).astype(bf16)
    z_ref[...] = _dot(xn, w_ref[:, PC_Z:PC_XBC])
    xbc_ref[...] = _dot(xn, w_ref[:, PC_XBC:PC_CQ])
    cq_ref[...] = _dot(xn, w_ref[:, PC_CQ:PC_CKV])
    ckv_ref[...] = _dot(xn, w_ref[:, PC_CKV:PC_KR])
    last = _dot(xn, w_ref[:, PC_KR:PC_END])
    lane = lax.broadcasted_iota(jnp.int32, (1, LANE), 1)
    kr_ref[...] = jnp.where(lane >= ROPE_LO, last, 0.0)
    dtt_ref[...] = last.T[0:16, :]


def _project(x, g, w, tm):
    m = x.shape[0]
    row = lambda n: pl.BlockSpec((tm, n), lambda i: (i, 0))
    widths = (SSD_WIDTH, CONV_DIM, Q_LORA, KV_LORA, LANE)
    return pl.pallas_call(
        _proj_body,
        grid=(m // tm,),
        in_specs=[row(D_MODEL), _const_spec((1, D_MODEL)), _const_spec((D_MODEL, PC_END))],
        out_specs=[row(n) for n in widths] + [pl.BlockSpec((16, tm), lambda i: (0, i))],
        out_shape=[jax.ShapeDtypeStruct((m, n), f32) for n in widths] + [jax.ShapeDtypeStruct((16, m), f32)],
        compiler_params=_params(("parallel",), 40),
        name="proj",
    )(x, g, w)


def _mla_prep_body(cq_ref, ckv_ref, kr_ref, tc_ref, ts1_ref, ts2_ref, gqn_ref, gkv_ref, wuq_ref, wuk_ref, wuv_ref,
                   gq_ref, gk_ref, lat_ref, kro_ref, q_ref, k_ref, vt_ref):
    tc, ts1, ts2 = tc_ref[...], ts1_ref[...], ts2_ref[...]
    tsw = ts1 + ts2

    def rope(x):
        return x * tc + pltpu.roll(x, LANE - ROPE_HALF, 1) * ts1 + pltpu.roll(x, ROPE_HALF, 1) * ts2

    ckv = ckv_ref[...]
    lat = ckv * _rms(ckv, KV_LORA) * gkv_ref[...]
    lat_ref[...] = lat
    kr = rope(kr_ref[...])
    kro_ref[...] = kr
    cq = cq_ref[...]
    cqn = (cq * _rms(cq, Q_LORA) * gqn_ref[...]).astype(bf16)
    latb = lat.astype(bf16)
    vt_ref[...] = _dot_nt(wuv_ref[...], latb).astype(vt_ref.dtype)
    gq, gk = gq_ref[...], gk_ref[...]
    q_all = _dot(cqn, wuq_ref[:, 0:MLA_HEADS * HT])
    qp_all = _dot(cqn, wuq_ref[:, MLA_HEADS * HT:])
    k_all = _dot(latb, wuk_ref[...])
    for h in range(MLA_HEADS):
        sl = slice(h * HT, (h + 1) * HT)
        qh = q_all[:, sl] * tc + qp_all[:, sl] * tsw
        q_ref[:, sl] = (qh * _rms(qh, QK_DIM) * gq).astype(q_ref.dtype)
        kh = k_all[:, sl] + kr
        k_ref[:, sl] = (kh * _rms(kh, QK_DIM) * gk).astype(k_ref.dtype)


def _mla_prep(cq, ckv, kr, tabs, n_tab_blocks, gqn, gkv, wuq, wuk, wuvt, gq, gk, tm, q_dtype):
    m = cq.shape[0]
    row = lambda n: pl.BlockSpec((tm, n), lambda i: (i, 0))
    tab = pl.BlockSpec((tm, LANE), lambda i: (i % n_tab_blocks, 0))
    widths = (KV_LORA, LANE, MLA_HEADS * HT, MLA_HEADS * HT)
    dtypes = (f32, f32, q_dtype, bf16)
    return pl.pallas_call(
        _mla_prep_body,
        grid=(m // tm,),
        in_specs=[row(Q_LORA), row(KV_LORA), row(LANE), tab, tab, tab,
                  _const_spec((1, Q_LORA)), _const_spec((1, KV_LORA)),
                  _const_spec((Q_LORA, 2 * MLA_HEADS * HT)), _const_spec((KV_LORA, MLA_HEADS * HT)),
                  _const_spec((MLA_WIDTH, KV_LORA)), _const_spec((1, HT)), _const_spec((1, HT))],
        out_specs=[row(n) for n in widths] + [pl.BlockSpec((MLA_WIDTH, tm), lambda i: (0, i))],
        out_shape=[jax.ShapeDtypeStruct((m, n), d) for n, d in zip(widths, dtypes)]
                  + [jax.ShapeDtypeStruct((MLA_WIDTH, m), bf16)],
        compiler_params=_params(("parallel",), 40),
        name="mla_prep",
    )(cq, ckv, kr, *tabs, gqn, gkv, wuq, wuk, wuvt, gq, gk)


def _ssd_body(xbc_ref, z_ref, dtt_ref, hist_ref, h0_ref, cw_ref, cb_ref, bias_c_ref,
              alog_c_ref, dsk_ref, ng_ref, tri_ref, y_ref, hout_ref, hfin_ref, xp_ref, hs_ref, *, valid_from):
    c = pl.program_id(1)
    q = SSD_CHUNK

    @pl.when(c == 0)
    def _():
        xp_ref[0:SUBLANE, :] = hist_ref[...]
        hs_ref[...] = h0_ref[0]

    xbc = xbc_ref[...]
    xp_ref[SUBLANE:SUBLANE + q, :] = xbc
    conv = cb_ref[...]
    for k in range(CONV_W):
        lo = SUBLANE - (CONV_W - 1) + k
        conv = conv + xp_ref[lo:lo + q, :] * cw_ref[k:k + 1, :]
    xp_ref[0:SUBLANE, :] = xbc[q - SUBLANE:q, :]
    xa = _silu(conv)
    xs = xa[:, :SSD_WIDTH]
    bm = xa[:, SSD_WIDTH:SSD_WIDTH + LANE]
    cm = xa[:, SSD_WIDTH + LANE:]

    rows = lax.broadcasted_iota(jnp.int32, (q, q), 0)
    cols = lax.broadcasted_iota(jnp.int32, (q, q), 1)
    lane = lax.broadcasted_iota(jnp.int32, (1, LANE), 1)
    low = lane < D_STATE

    dtr = _softplus(dtt_ref[...] + bias_c_ref[...])
    if valid_from:
        dtr = jnp.where(lax.broadcasted_iota(jnp.int32, (16, q), 1) >= valid_from, dtr, 0.0)
    dar = dtr * -jnp.exp(alog_c_ref[...])
    tri = tri_ref[...]
    dtc = dtr.T
    acs_r = lax.dot_general(dar, tri, _NT, precision=_HI, preferred_element_type=f32)
    acs_c = jnp.dot(tri, dar.T, precision=_HI, preferred_element_type=f32)
    w_c = jnp.exp(acs_c[q - 1:q, :] - acs_c) * dtc
    e_c = jnp.exp(acs_c)
    cd_r = jnp.exp(acs_r[:, q - 1:q])

    causal = cols <= rows
    bb = bm.astype(bf16)
    dsk = dsk_ref[...]
    ys = []
    for g in range(SSD_GROUPS):
        gmask = (lane >= g * D_STATE) & (lane < (g + 1) * D_STATE)
        cg = jnp.where(gmask, cm, 0.0).astype(bf16)
        cb_g = _dot_nt(cg, bb)
        for pi in range(2):
            i = 2 * g + pi
            xpair = xs[:, i * LANE:(i + 1) * LANE]
            ypair = jnp.zeros((q, LANE), f32)
            for hh in range(2):
                h = 2 * i + hh
                seg = acs_c[:, h:h + 1] - acs_r[h:h + 1, :]
                lmat = jnp.exp(jnp.where(causal, seg, -jnp.inf))
                sc = (cb_g * lmat * dtr[h:h + 1, :]).astype(bf16)
                xh = jnp.where(low if hh == 0 else jnp.logical_not(low), xpair, 0.0).astype(bf16)
                ypair = ypair + _dot(sc, xh)
            h0, h1 = 2 * i, 2 * i + 1
            wp = jnp.where(low, w_c[:, h0:h0 + 1], w_c[:, h1:h1 + 1])
            ep = jnp.where(low, e_c[:, h0:h0 + 1], e_c[:, h1:h1 + 1])
            hst = hs_ref[i]
            yoff = _dot_nt(cg, hst.astype(bf16)) * ep
            st = _dot_tn((xpair * wp).astype(bf16), bb)
            cdb = jnp.concatenate([jnp.broadcast_to(cd_r[h0:h0 + 1, :], (SSD_HEAD_DIM, LANE)),
                                   jnp.broadcast_to(cd_r[h1:h1 + 1, :], (SSD_HEAD_DIM, LANE))], axis=0)
            hs_ref[i] = hst * cdb + st
            ys.append(ypair + yoff + dsk[:, i * LANE:(i + 1) * LANE] * xpair)
    y = jnp.concatenate(ys, axis=1) * _silu(z_ref[...])
    gw = SSD_WIDTH // SSD_GROUPS
    outs = []
    for g in range(SSD_GROUPS):
        yg = y[:, g * gw:(g + 1) * gw]
        outs.append(yg * _rms(yg, gw))
    y_ref[...] = (jnp.concatenate(outs, axis=1) * ng_ref[...]).astype(y_ref.dtype)

    @pl.when(c == pl.num_programs(1) - 1)
    def _():
        hout_ref[0] = hs_ref[...]
        for i in range(2 * SSD_GROUPS):
            g = i // 2
            for hh in range(2):
                hfin_ref[0, 2 * i + hh] = hs_ref[i, hh * SSD_HEAD_DIM:(hh + 1) * SSD_HEAD_DIM, g * D_STATE:(g + 1) * D_STATE]


def _ssd(xbc, z, dtt, hist_arr, hist_block, h0, consts, n_batch, n_chunks, row_block0, valid_from):
    q = SSD_CHUNK
    rb = lambda n: pl.BlockSpec((q, n), lambda b, c: (row_block0 + b * n_chunks + c, 0))
    cw, cb, bias_c, alog_c, dsk, ng, tri = consts
    return pl.pallas_call(
        functools.partial(_ssd_body, valid_from=valid_from),
        grid=(n_batch, n_chunks),
        in_specs=[rb(CONV_DIM), rb(SSD_WIDTH),
                  pl.BlockSpec((16, q), lambda b, c: (0, row_block0 + b * n_chunks + c)),
                  pl.BlockSpec((SUBLANE, CONV_DIM), lambda b, c: (hist_block, 0)),
                  _const_spec((1, 4, LANE, LANE)),
                  _const_spec((CONV_W, CONV_DIM)), _const_spec((1, CONV_DIM)),
                  _const_spec((16, 1)), _const_spec((16, 1)),
                  _const_spec((1, SSD_WIDTH)), _const_spec((1, SSD_WIDTH)), _const_spec((q, q))],
        out_specs=[pl.BlockSpec((q, SSD_WIDTH), lambda b, c: (b * n_chunks + c, 0)),
                   pl.BlockSpec((1, 4, LANE, LANE), lambda b, c: (b, 0, 0, 0)),
                   pl.BlockSpec((1, SSD_HEADS, SSD_HEAD_DIM, D_STATE), lambda b, c: (b, 0, 0, 0))],
        out_shape=[jax.ShapeDtypeStruct((n_batch * n_chunks * q, SSD_WIDTH), bf16),
                   jax.ShapeDtypeStruct((n_batch, 4, LANE, LANE), f32),
                   jax.ShapeDtypeStruct((n_batch, SSD_HEADS, SSD_HEAD_DIM, D_STATE), f32)],
        scratch_shapes=[pltpu.VMEM((SUBLANE + q, CONV_DIM), f32), pltpu.VMEM((4, LANE, LANE), f32)],
        compiler_params=_params(("parallel", "arbitrary"), 32),
        name="ssd",
    )(xbc, z, dtt, hist_arr, h0, cw, cb, bias_c, alog_c, dsk, ng, tri)


def _flash_body(q_ref, k_ref, vt_ref, km_ref, vtm_ref, o_ref, *, tq, tk, nh):
    qi = pl.program_id(2)
    qs = [q_ref[:, h * HT:(h + 1) * HT] for h in range(nh)]
    drow = lax.broadcasted_iota(jnp.int32, (2 * V_DIM, 1), 0)
    vmasks = (drow < V_DIM, drow >= V_DIM)
    half = tk // 2

    def heads(kall, vtall, carries, mask=None, qlo=0):
        sts = [_dot_nt(kall[:, h * HT:(h + 1) * HT], qs[h][qlo:, :]) for h in range(nh)]
        out = []
        for h in range(nh):
            m0, l0, acc0 = carries[h]
            m, l, acct = m0[:, qlo:], l0[:, qlo:], acc0[:, qlo:]
            st = sts[h] if mask is None else jnp.where(mask, sts[h], -jnp.inf)
            m2 = jnp.maximum(m, jnp.max(st, axis=0, keepdims=True))
            pt = jnp.exp(st - m2)
            a = jnp.exp(m - m2)
            vtt = vtall[(h // 2) * LANE:(h // 2 + 1) * LANE, :]
            vz = jnp.where(vmasks[h % 2], vtt, jnp.zeros_like(vtt))
            new = (m2, a * l + jnp.sum(pt, axis=0, keepdims=True), a * acct + _dot(vz, pt.astype(bf16)))
            if qlo:
                new = tuple(jnp.concatenate([old[:, :qlo], x], axis=1) for old, x in zip(carries[h], new))
            out.append(new)
        return tuple(out)

    init = (jnp.full((1, tq), -jnp.inf, f32), jnp.zeros((1, tq), f32), jnp.zeros((2 * V_DIM, tq), f32))

    def body(j, carries):
        off = pl.multiple_of(j * tk, tk)
        return heads(k_ref[pl.ds(off, tk), :], vt_ref[:, pl.ds(off, tk)], carries)

    carries = lax.fori_loop(0, qi, body, (init,) * nh)
    off = pl.multiple_of(qi * tk, tk)
    k_a = jnp.concatenate([k_ref[pl.ds(off, half), :], km_ref[...]], axis=0)
    vt_a = jnp.concatenate([vt_ref[:, pl.ds(off, half)], vtm_ref[...]], axis=1)
    krow = lax.broadcasted_iota(jnp.int32, (half + N_META, tq), 0)
    qcol = lax.broadcasted_iota(jnp.int32, (half + N_META, tq), 1)
    carries = heads(k_a, vt_a, carries, (krow >= half) | (krow <= qcol))
    off_b = pl.multiple_of(qi * tk + half, half)
    krow = lax.broadcasted_iota(jnp.int32, (half, tq - half), 0)
    qcol = lax.broadcasted_iota(jnp.int32, (half, tq - half), 1)
    res = heads(k_ref[pl.ds(off_b, half), :], vt_ref[:, pl.ds(off_b, half)], carries, krow <= qcol, qlo=half)
    for pr in range(nh // 2):
        (_, l0, acc0), (_, l1, acc1) = res[2 * pr], res[2 * pr + 1]
        o_ref[:, pr * LANE:(pr + 1) * LANE] = (acc0 / l0 + acc1 / l1).T.astype(o_ref.dtype)


def _flash(q, k, vt, k_small, vt_meta, meta_block, n_batch, seq, tq, tk, nh):
    assert tk == tq and seq % tk == 0 and nh % 2 == 0 and MLA_HEADS % nh == 0
    nq = seq // tq
    return pl.pallas_call(
        functools.partial(_flash_body, tq=tq, tk=tk, nh=nh),
        grid=(n_batch, MLA_HEADS // nh, nq),
        in_specs=[pl.BlockSpec((tq, nh * HT), lambda b, p, i: (b * nq + i, p)),
                  pl.BlockSpec((seq, nh * HT), lambda b, p, i: (b, p)),
                  pl.BlockSpec((nh * V_DIM, seq), lambda b, p, i: (p, b)),
                  pl.BlockSpec((N_META, nh * HT), lambda b, p, i: (meta_block, p)),
                  pl.BlockSpec((nh * V_DIM, N_META), lambda b, p, i: (p, 0))],
        out_specs=pl.BlockSpec((tq, nh * V_DIM), lambda b, p, i: (b * nq + i, p)),
        out_shape=jax.ShapeDtypeStruct((n_batch * seq, MLA_WIDTH), f32),
        compiler_params=_params(("parallel", "parallel", "arbitrary"), 40),
        name="flash",
    )(q, k, vt, k_small, vt_meta)


def _finish_body(x_ref, ssd_ref, att_ref, gm_ref, wo_ref, gf_ref, wg_ref, wu_ref, wd_ref, o_ref):
    att = att_ref[...]
    mla = (att * _rms(att, MLA_WIDTH) * gm_ref[...]).astype(bf16)
    h = x_ref[...] + (_dot(ssd_ref[...], wo_ref[0:SSD_WIDTH, :]) + _dot(mla, wo_ref[SSD_WIDTH:, :]))
    n = (h * _rms(h, D_MODEL) * gf_ref[...]).astype(bf16)
    ff = jnp.zeros_like(h)
    for c in range(N_FF_CHUNKS):
        cs = slice(c * FF_CHUNK, (c + 1) * FF_CHUNK)
        a = (_silu(_dot(n, wg_ref[:, cs])) * _dot(n, wu_ref[:, cs])).astype(bf16)
        ff = ff + _dot(a, wd_ref[cs, :])
    o_ref[...] = h + ff


def _finish(x, ssd, att, gm, wo, gf, wg, wu, wd, tm):
    m = x.shape[0]
    row = lambda n: pl.BlockSpec((tm, n), lambda i: (i, 0))
    return pl.pallas_call(
        _finish_body,
        grid=(m // tm,),
        in_specs=[row(D_MODEL), row(SSD_WIDTH), row(MLA_WIDTH), _const_spec((1, MLA_WIDTH)),
                  _const_spec((D_MODEL, D_MODEL)), _const_spec((1, D_MODEL)),
                  _const_spec((D_MODEL, D_FF)), _const_spec((D_MODEL, D_FF)), _const_spec((D_FF, D_MODEL))],
        out_specs=row(D_MODEL),
        out_shape=jax.ShapeDtypeStruct((m, D_MODEL), f32),
        compiler_params=_params(("parallel",), 56),
        name="finish",
    )(x, ssd, att, gm, wo, gf, wg, wu, wd)


def _sssd_prep_body(xbc_ref, sc_ref, cw_ref, cb_ref, xs_ref, xat_ref):
    conv = cb_ref[...]
    for k in range(CONV_W - 1):
        conv = conv + sc_ref[k] * cw_ref[k:k + 1, :]
    conv = conv + xbc_ref[...] * cw_ref[CONV_W - 1:CONV_W, :]
    xa = _silu(conv)
    xs_ref[...] = xa[:, :SSD_WIDTH]
    xat_ref[...] = xa.T


def _sssd_prep(xbc_small, sc, cw, cb, nseq):
    return pl.pallas_call(
        _sssd_prep_body,
        grid=(1,),
        in_specs=[pl.BlockSpec((nseq, CONV_DIM), lambda i: (0, 0)), _const_spec((CONV_W - 1, nseq, CONV_DIM)),
                  _const_spec((CONV_W, CONV_DIM)), _const_spec((1, CONV_DIM))],
        out_specs=[pl.BlockSpec((nseq, SSD_WIDTH), lambda i: (0, 0)), pl.BlockSpec((CONV_DIM, nseq), lambda i: (0, 0))],
        out_shape=[jax.ShapeDtypeStruct((nseq, SSD_WIDTH), f32), jax.ShapeDtypeStruct((CONV_DIM, nseq), f32)],
        compiler_params=_params(("arbitrary",), 32),
        name="sssd_prep",
    )(xbc_small, sc, cw, cb)


def _sssd_state_body(h0_ref, xst_ref, bt_ref, ct_ref, dtt_ref, bias_c_ref, alog_c_ref, hn_ref, yt_ref):
    h = pl.program_id(0)
    dt = _softplus(dtt_ref[pl.ds(h, 1), :] + bias_c_ref[pl.ds(h, 1), :])
    dec = jnp.exp(dt * -jnp.exp(alog_c_ref[pl.ds(h, 1), :]))
    bt, ct = bt_ref[...], ct_ref[...]

    def body(p, carry):
        xdt = xst_ref[pl.ds(p, 1), :] * dt
        hn = h0_ref[0, p] * dec + xdt * bt
        hn_ref[0, p] = hn
        yt_ref[pl.ds(p, 1), :] = jnp.sum(ct * hn, axis=0, keepdims=True)
        return carry

    lax.fori_loop(0, SSD_HEAD_DIM, body, 0, unroll=4)


def _sssd_state(h0t, xat, dtt_small, bias_c, alog_c, nseq):
    hpg = SSD_HEADS // SSD_GROUPS
    rows = lambda f: pl.BlockSpec((SSD_HEAD_DIM, nseq), f)
    hblk = pl.BlockSpec((1, SSD_HEAD_DIM, D_STATE, nseq), lambda h: (h, 0, 0, 0))
    return pl.pallas_call(
        _sssd_state_body,
        grid=(SSD_HEADS,),
        in_specs=[hblk, rows(lambda h: (h, 0)), rows(lambda h: (SSD_HEADS + h // hpg, 0)),
                  rows(lambda h: (SSD_HEADS + SSD_GROUPS + h // hpg, 0)),
                  pl.BlockSpec((16, nseq), lambda h: (0, 0)), _const_spec((16, 1)), _const_spec((16, 1))],
        out_specs=[hblk, rows(lambda h: (h, 0))],
        out_shape=[jax.ShapeDtypeStruct((SSD_HEADS, SSD_HEAD_DIM, D_STATE, nseq), f32),
                   jax.ShapeDtypeStruct((SSD_WIDTH, nseq), f32)],
        compiler_params=_params(("parallel",), 32),
        name="sssd_state",
    )(h0t, xat, xat, xat, dtt_small, bias_c, alog_c)


def _sssd_gate_body(yt_ref, xs_ref, z_ref, dsk_ref, ng_ref, o_ref):
    y = (yt_ref[...].T + dsk_ref[...] * xs_ref[...]) * _silu(z_ref[...])
    gw = SSD_WIDTH // SSD_GROUPS
    outs = []
    for g in range(SSD_GROUPS):
        yg = y[:, g * gw:(g + 1) * gw]
        outs.append(yg * _rms(yg, gw))
    o_ref[...] = (jnp.concatenate(outs, axis=1) * ng_ref[...]).astype(o_ref.dtype)


def _sssd_gate(yt, xs, z_small, dsk, ng, nseq):
    blk = pl.BlockSpec((nseq, SSD_WIDTH), lambda i: (0, 0))
    return pl.pallas_call(
        _sssd_gate_body,
        grid=(1,),
        in_specs=[pl.BlockSpec((SSD_WIDTH, nseq), lambda i: (0, 0)), blk, blk,
                  _const_spec((1, SSD_WIDTH)), _const_spec((1, SSD_WIDTH))],
        out_specs=blk,
        out_shape=jax.ShapeDtypeStruct((nseq, SSD_WIDTH), bf16),
        compiler_params=_params(("arbitrary",), 32),
        name="sssd_gate",
    )(yt, xs, z_small, dsk, ng)


def _absorb_body(q_ref, gk_ref, wabs_ref, lat_ref, kr_ref, wuk_ref, qg_ref, a_ref, snew_ref):
    gk = gk_ref[...]
    latb = lat_ref[...].astype(bf16)
    latf = latb.astype(f32)
    kr = kr_ref[...]
    krf = kr.astype(bf16).astype(f32)
    kr2 = jnp.sum(kr * kr, axis=-1, keepdims=True)
    s_new = []
    for h in range(MLA_HEADS):
        sl = slice(h * HT, (h + 1) * HT)
        qg = q_ref[:, sl] * gk
        qg_ref[:, sl] = qg
        a = jnp.dot(qg, wabs_ref[sl, :], precision=_HI, preferred_element_type=f32)
        a_ref[:, h * KV_LORA:(h + 1) * KV_LORA] = a
        kn = _dot(latb, wuk_ref[:, sl])
        n2 = jnp.sum(kn * kn, axis=-1, keepdims=True)
        s12 = (jnp.sum(a.astype(bf16).astype(f32) * latf, axis=-1, keepdims=True)
               + jnp.sum(qg.astype(bf16).astype(f32) * krf, axis=-1, keepdims=True))
        s_new.append(lax.rsqrt((n2 + kr2) * (1.0 / QK_DIM) + EPS) * s12)
    a_ref[:, MLA_HEADS * KV_LORA:] = jnp.zeros((q_ref.shape[0], (16 - MLA_HEADS) * KV_LORA), f32)
    snew_ref[...] = jnp.concatenate(s_new, axis=1)


def _absorb(q_small, gk, wabs, lat_small, kr_small, wuk, nseq):
    blk = lambda n: pl.BlockSpec((nseq, n), lambda i: (0, 0))
    return pl.pallas_call(
        _absorb_body,
        grid=(1,),
        in_specs=[blk(MLA_HEADS * HT), _const_spec((1, HT)), _const_spec((MLA_HEADS * HT, KV_LORA)),
                  blk(KV_LORA), blk(HT), _const_spec((KV_LORA, MLA_HEADS * HT))],
        out_specs=[blk(MLA_HEADS * HT), blk(16 * KV_LORA), blk(MLA_HEADS)],
        out_shape=[jax.ShapeDtypeStruct((nseq, MLA_HEADS * HT), f32), jax.ShapeDtypeStruct((nseq, 16 * KV_LORA), f32),
                   jax.ShapeDtypeStruct((nseq, MLA_HEADS), f32)],
        compiler_params=_params(("arbitrary",), 32),
        name="absorb",
    )(q_small, gk, wabs, lat_small, kr_small, wuk)


def _decode_body(pt_ref, a_ref, c_ref, latn_ref, snew_ref, wukt_ref, clat_ref, ckrt_ref, o_ref,
                 lat_buf, kr_buf, waug, latb, s_buf, sems, *, n_pages, page, ppt):
    b = pl.program_id(0)
    nb = pl.num_programs(0)
    slot = b % 2
    n_tiles = n_pages // ppt
    tile = ppt * page
    wrows = MLA_HEADS * QK_NOPE

    def start_page(seq, p, sl):
        pid = pt_ref[seq * n_pages + p]
        pltpu.make_async_copy(clat_ref.at[0, pid], lat_buf.at[sl, p], sems.at[0, sl]).start()
        pltpu.make_async_copy(ckrt_ref.at[0, pid], kr_buf.at[sl, p], sems.at[1, sl]).start()

    @pl.when(b == 0)
    def _():
        waug[0:wrows, :] = wukt_ref[...]

        def first(p, carry):
            start_page(0, p, 0)
            return carry

        lax.fori_loop(0, n_pages, first, 0)

    pltpu.make_async_copy(clat_ref.at[0, pl.ds(0, n_pages)], lat_buf.at[slot], sems.at[0, slot]).wait()
    pltpu.make_async_copy(ckrt_ref.at[0, pl.ds(0, n_pages)], kr_buf.at[slot], sems.at[1, slot]).wait()

    waug[wrows:wrows + 16, :] = a_ref[0, 0].astype(bf16)
    cb = c_ref[0].astype(bf16)

    def score_dots(lb, krt):
        kt = _dot_nt(waug[...], lb)
        return kt, _dot(cb, krt.astype(bf16)), krt

    def score_finish(kt, s2, krt):
        n = kt.shape[1]
        k4 = kt[0:wrows, :].reshape(MLA_HEADS, QK_NOPE // SUBLANE, SUBLANE, n)
        n2 = _fold_heads(jnp.sum(k4 * k4, axis=1))
        kr2 = jnp.sum(krt * krt, axis=0, keepdims=True)
        return lax.rsqrt((n2 + kr2) * (1.0 / QK_DIM) + EPS) * (kt[wrows:wrows + MLA_HEADS, :] + s2)

    def body(j, carry):
        @pl.when(b + 1 < nb)
        def _():
            for pp in range(ppt):
                start_page(b + 1, j * ppt + pp, 1 - slot)

        lb = lat_buf[slot, pl.ds(j * ppt, ppt)].reshape(tile, KV_LORA).astype(bf16)
        latb[j] = lb
        kr_pages = kr_buf[slot, pl.ds(j * ppt, ppt)]
        s_buf[j] = score_finish(*score_dots(lb, jnp.concatenate([kr_pages[i] for i in range(ppt)], axis=-1)))
        return carry

    lax.fori_loop(0, n_tiles, body, 0)
    s_new = snew_ref[0]
    s_all = jnp.concatenate([s_buf[j] for j in range(n_tiles)], axis=-1)
    m = jnp.maximum(jnp.max(s_all, axis=-1, keepdims=True), s_new)
    p_all = jnp.exp(s_all - m)
    p_new = jnp.exp(s_new - m)
    l = jnp.sum(p_all, axis=-1, keepdims=True) + p_new
    acc = _dot(p_all.astype(bf16), latb[...].reshape(n_tiles * tile, KV_LORA)) + p_new * latn_ref[0]
    o_ref[0] = acc / l


def _decode(page_table_flat, a, c, lat_new, s_new, wukt, cache_lat, cache_kr_t, nseq, n_pages, page, ppt):
    assert n_pages % ppt == 0
    wrows = MLA_HEADS * QK_NOPE
    grid_spec = pltpu.PrefetchScalarGridSpec(
        num_scalar_prefetch=1,
        grid=(nseq,),
        in_specs=[pl.BlockSpec((1, 1, 16, KV_LORA), lambda b, pt: (b, 0, 0, 0)),
                  pl.BlockSpec((1, MLA_HEADS, QK_ROPE), lambda b, pt: (b, 0, 0)),
                  pl.BlockSpec((1, 1, KV_LORA), lambda b, pt: (b, 0, 0)),
                  pl.BlockSpec((1, MLA_HEADS, 1), lambda b, pt: (b, 0, 0)),
                  pl.BlockSpec((wrows, KV_LORA), lambda b, pt: (0, 0), pipeline_mode=pl.Buffered(1)),
                  pl.BlockSpec(memory_space=pl.ANY), pl.BlockSpec(memory_space=pl.ANY)],
        out_specs=pl.BlockSpec((1, MLA_HEADS, KV_LORA), lambda b, pt: (b, 0, 0)),
        scratch_shapes=[pltpu.VMEM((2, n_pages, page, KV_LORA), f32), pltpu.VMEM((2, n_pages, QK_ROPE, page), f32),
                        pltpu.VMEM((wrows + 16, KV_LORA), bf16),
                        pltpu.VMEM((n_pages // ppt, ppt * page, KV_LORA), bf16),
                        pltpu.VMEM((n_pages // ppt, MLA_HEADS, ppt * page), f32), pltpu.SemaphoreType.DMA((2, 2))],
    )
    return pl.pallas_call(
        functools.partial(_decode_body, n_pages=n_pages, page=page, ppt=ppt),
        grid_spec=grid_spec,
        out_shape=jax.ShapeDtypeStruct((nseq, MLA_HEADS, KV_LORA), f32),
        compiler_params=_params(("arbitrary",), 40),
        name="decode",
    )(page_table_flat, a, c, lat_new, s_new, wukt, cache_lat, cache_kr_t)


def _uv_body(o_ref, wuv_ref, att_ref):
    for h in range(MLA_HEADS):
        oh = o_ref[:, h * KV_LORA:(h + 1) * KV_LORA].astype(bf16)
        att_ref[:, h * V_DIM:(h + 1) * V_DIM] = _dot(oh, wuv_ref[:, h * V_DIM:(h + 1) * V_DIM])


def _uv(o_flat, wuv):
    nseq = o_flat.shape[0]
    return pl.pallas_call(
        _uv_body,
        grid=(1,),
        in_specs=[pl.BlockSpec((nseq, MLA_HEADS * KV_LORA), lambda i: (0, 0)), _const_spec((KV_LORA, MLA_WIDTH))],
        out_specs=pl.BlockSpec((nseq, MLA_WIDTH), lambda i: (0, 0)),
        out_shape=jax.ShapeDtypeStruct((nseq, MLA_WIDTH), f32),
        compiler_params=_params(("arbitrary",), 32),
        name="uv",
    )(o_flat, wuv)


def _rope_tables(pos):
    inv = ROPE_THETA ** (-jnp.arange(ROPE_HALF, dtype=f32) * (2.0 / QK_ROPE))
    ang = pos.astype(f32)[:, None] * inv[None, :]
    cos, sin = jnp.cos(ang), jnp.sin(ang)
    n = pos.shape[0]
    one = jnp.ones((n, QK_NOPE), f32)
    z16 = jnp.zeros((n, ROPE_HALF), f32)
    z32 = jnp.zeros((n, HT - QK_DIM), f32)
    z64 = jnp.zeros((n, QK_NOPE), f32)
    tc = jnp.concatenate([one, cos, cos, z32], axis=1)
    ts1 = jnp.concatenate([z64, -sin, z16, z32], axis=1)
    ts2 = jnp.concatenate([z64, z16, sin, z32], axis=1)
    return tc, ts1, ts2


def _pad_lanes(x, n):
    return jnp.pad(x, ((0, 0), (0, n - x.shape[1])))


def kernel(x_prompt, x_sample, cache_kv_latent, cache_k_rope, state_conv, state_ssm, page_table, meta_tokens, attn_norm_g, w_in, conv_w, conv_b, dt_bias, a_log, d_skip, ssd_norm_g, q_norm_g, w_uq, kv_norm_g, w_ukv, q_head_g_nope, q_head_g_rope, k_head_g_nope, k_head_g_rope, mla_out_g, w_out, ffn_norm_g, w_gate, w_up, w_down):
    l = 0
    nb, seq, _ = x_prompt.shape
    nseq = x_sample.shape[0]
    n_pages, page = page_table.shape[1], cache_kv_latent.shape[2]
    past = n_pages * page
    n_small = 2 * LANE
    meta_lo = n_small - N_META

    s0, s1, s2, s3, s4 = (SSD_WIDTH, SSD_WIDTH + CONV_DIM, SSD_WIDTH + CONV_DIM + SSD_HEADS,
                          SSD_WIDTH + CONV_DIM + SSD_HEADS + Q_LORA, SSD_WIDTH + CONV_DIM + SSD_HEADS + Q_LORA + KV_LORA)
    wi = w_in[l]
    w_last = jnp.concatenate([_pad_lanes(wi[:, s1:s2], ROPE_LO), _pad_lanes(wi[:, s4:], HT - ROPE_LO)], axis=1)
    w_proj = jnp.concatenate([wi[:, :s1], wi[:, s2:s4], w_last], axis=1).astype(bf16)
    g_attn = attn_norm_g[l][None, :]
    wq3 = w_uq[l].reshape(Q_LORA, MLA_HEADS, QK_DIM)
    zq = jnp.zeros((Q_LORA, MLA_HEADS, QK_NOPE), f32)
    wq_partner = jnp.concatenate([zq, wq3[:, :, QK_NOPE + ROPE_HALF:], wq3[:, :, QK_NOPE:QK_NOPE + ROPE_HALF]], axis=2)
    pad_head = lambda w: jnp.pad(w, ((0, 0), (0, 0), (0, HT - QK_DIM))).reshape(Q_LORA, -1)
    wuq = jnp.concatenate([pad_head(wq3), pad_head(wq_partner)], axis=1).astype(bf16)
    wkv = w_ukv[l].reshape(KV_LORA, MLA_HEADS, QK_NOPE + V_DIM)
    wuk_f32 = jnp.pad(wkv[:, :, :QK_NOPE], ((0, 0), (0, 0), (0, HT - QK_NOPE))).reshape(KV_LORA, -1)
    wuk = wuk_f32.astype(bf16)
    wuv = wkv[:, :, QK_NOPE:].reshape(KV_LORA, MLA_WIDTH).astype(bf16)
    wukt = wkv[:, :, :QK_NOPE].reshape(KV_LORA, -1).T.astype(bf16)
    gq = _pad_lanes(jnp.concatenate([q_head_g_nope[l], q_head_g_rope[l], q_head_g_rope[l]])[None, :], HT) * ATTN_SCALE
    gk = _pad_lanes(jnp.concatenate([k_head_g_nope[l], k_head_g_rope[l], k_head_g_rope[l]])[None, :], HT)
    gqn, gkv = q_norm_g[l][None, :], kv_norm_g[l][None, :]
    cw, cb = conv_w[l], conv_b[l][None, :]
    bias_c = jnp.pad(dt_bias[l][:, None], ((0, 16 - SSD_HEADS), (0, 0)))
    alog_c = jnp.pad(a_log[l][:, None], ((0, 16 - SSD_HEADS), (0, 0)))
    dsk = jnp.repeat(d_skip[l], SSD_HEAD_DIM)[None, :]
    ng = ssd_norm_g[l][None, :]
    tri = jnp.tril(jnp.ones((SSD_CHUNK, SSD_CHUNK), f32))
    ssd_consts = (cw, cb, bias_c, alog_c, dsk, ng, tri)
    gm, gf = mla_out_g[l][None, :], ffn_norm_g[l][None, :]
    wo = w_out[l].astype(bf16)
    wg, wu, wd = w_gate[l].astype(bf16), w_up[l].astype(bf16), w_down[l].astype(bf16)

    xp = x_prompt.reshape(nb * seq, D_MODEL)
    xs_rows = x_sample[:, 0, :]
    x_small = jnp.concatenate([xs_rows, jnp.zeros((meta_lo - nseq, D_MODEL), f32), meta_tokens.astype(f32)], axis=0)

    z_p, xbc_p, cq_p, ckv_p, kr_p, dtt_p = _project(xp, g_attn, w_proj, ROW_TILE)
    z_s, xbc_s, cq_s, ckv_s, kr_s, dtt_s = _project(x_small, g_attn, w_proj, n_small)

    tabs_p = _rope_tables(N_META + jnp.arange(seq))
    pos_small = jnp.concatenate([jnp.full((nseq,), past), jnp.zeros((meta_lo - nseq,), jnp.int32), jnp.arange(N_META)])
    tabs_s = _rope_tables(pos_small)
    tm = ROW_TILE
    lat_p, kro_p, q_p, k_p, vt_p = _mla_prep(cq_p, ckv_p, kr_p, tabs_p, seq // tm, gqn, gkv, wuq, wuk, wuv.T, gq, gk, tm, bf16)
    lat_s, kro_s, q_s, k_s, vt_s = _mla_prep(cq_s, ckv_s, kr_s, tabs_s, 1, gqn, gkv, wuq, wuk, wuv.T, gq, gk, n_small, f32)

    zero_h = jnp.zeros((1, 4, LANE, LANE), f32)
    zero_hist = jnp.zeros((SUBLANE, CONV_DIM), f32)
    _, h_meta, _ = _ssd(xbc_s, z_s, dtt_s, zero_hist, 0, zero_h, ssd_consts, 1, 1, 1, SSD_CHUNK - N_META)
    ssd_p, _, h_fin = _ssd(xbc_p, z_p, dtt_p, xbc_s, n_small // SUBLANE - 1, h_meta, ssd_consts, nb, seq // SSD_CHUNK, 0, 0)

    att_p = _flash(q_p, k_p, vt_p, k_s, vt_s[:, meta_lo:], n_small // N_META - 1, nb, seq, FLASH_TILE, FLASH_TILE,
                   FLASH_HEADS_PER_STEP)
    y_prompt = _finish(xp, ssd_p, att_p, gm, wo, gf, wg, wu, wd, ROW_TILE).reshape(nb, seq, D_MODEL)

    sc = jnp.transpose(state_conv[l], (1, 0, 2))
    xs_s, xat_s = _sssd_prep(xbc_s, sc, cw, cb, nseq)
    h_new_t, yt_s = _sssd_state(jnp.transpose(state_ssm[l], (1, 2, 3, 0)), xat_s, dtt_s, bias_c, alog_c, nseq)
    h_new = jnp.transpose(h_new_t, (3, 0, 1, 2))
    ssd_s = _sssd_gate(yt_s, xs_s, z_s, dsk, ng, nseq)
    qg, a_abs, s_new = _absorb(q_s, gk, wuk_f32.T, lat_s, kro_s, wuk, nseq)
    hr = list(HEAD_ROW)
    a_abs = a_abs.reshape(nseq, 16, KV_LORA)[:, hr + list(range(MLA_HEADS, 16))].reshape(nseq, 1, 16, KV_LORA)
    c_abs = qg.reshape(nseq, MLA_HEADS, HT)[:, hr, ROPE_LO:QK_DIM]
    lat_new = lat_s[:nseq][:, None, :]
    kr_new = kro_s[:nseq, ROPE_LO:QK_DIM][:, None, :]
    o_lat = _decode(page_table.reshape(-1), a_abs, c_abs, lat_new, s_new[:, hr, None], wukt,
                    cache_kv_latent, jnp.swapaxes(cache_k_rope, 2, 3), nseq, n_pages, page, DECODE_PAGES_PER_TILE)
    att_s = _uv(o_lat[:, hr].reshape(nseq, MLA_HEADS * KV_LORA), wuv)
    y_sample = _finish(xs_rows, ssd_s, att_s, gm, wo, gf, wg, wu, wd, nseq)[:, None, :]

    def with_meta(small, main, width):
        meta = jnp.broadcast_to(small[meta_lo:][None], (nb, N_META, width))
        return jnp.concatenate([meta, main.reshape(nb, seq, width)], axis=1)[None]

    kv_latent_prompt = with_meta(lat_s, lat_p, KV_LORA)
    k_rope_prompt = with_meta(kro_s[:, ROPE_LO:QK_DIM], kro_p[:, ROPE_LO:QK_DIM], QK_ROPE)
    conv_prompt = xbc_p.reshape(nb, seq, CONV_DIM)[:, seq - (CONV_W - 1):][None]
    ssm_prompt = h_fin[None]
    kv_latent_sample = lat_new[None]
    k_rope_sample = kr_new[None]
    conv_sample = jnp.concatenate([state_conv[l][:, 1:], xbc_s[:nseq][:, None, :]], axis=1)[None]
    ssm_sample = h_new[None]
    return (y_prompt, y_sample, kv_latent_prompt, k_rope_prompt, conv_prompt, ssm_prompt.astype(x_prompt.dtype),
            kv_latent_sample, k_rope_sample, conv_sample, ssm_sample.astype(state_ssm.dtype))
```

```python
import functools

import jax
import jax.numpy as jnp
from jax import lax
from jax.experimental import pallas as pl
from jax.experimental.pallas import tpu as pltpu

f32 = jnp.float32
bf16 = jnp.bfloat16

D_MODEL = 1024
N_META = 16
SSD_HEADS = 8
SSD_HEAD_DIM = 64
SSD_WIDTH = SSD_HEADS * SSD_HEAD_DIM
SSD_GROUPS = 2
D_STATE = 64
CONV_W = 4
CONV_DIM = SSD_WIDTH + 2 * SSD_GROUPS * D_STATE
SSD_CHUNK = 128
MLA_HEADS = 8
QK_NOPE = 64
QK_ROPE = 32
QK_DIM = QK_NOPE + QK_ROPE
V_DIM = 64
MLA_WIDTH = MLA_HEADS * V_DIM
Q_LORA = 384
KV_LORA = 256
ROPE_THETA = 10000.0
ATTN_SCALE = QK_DIM ** -0.5
D_FF = 2816
EPS = 1e-6

LANE = 128
SUBLANE = 8
HT = LANE
ROPE_LO = QK_NOPE
ROPE_HALF = QK_ROPE // 2
FF_CHUNK = 256
N_FF_CHUNKS = D_FF // FF_CHUNK
DECODE_PAGES_PER_TILE = 32
ROW_TILE = 512
SSD_CHUNKS_PER_STEP = 4
FLASH_TILE = 512
FLASH_HEADS_PER_STEP = 4
MiB = 1024 * 1024

PC_Z = 0
PC_XBC = PC_Z + SSD_WIDTH
PC_CQ = PC_XBC + CONV_DIM
PC_CKV = PC_CQ + Q_LORA
PC_KR = PC_CKV + KV_LORA
PC_END = PC_KR + LANE

_NT = (((1,), (1,)), ((), ()))
_TN = (((0,), (0,)), ((), ()))
_HI = lax.Precision.HIGHEST


def _dot(a, b):
    return jnp.dot(a, b, preferred_element_type=f32)


def _dot_nt(a, b):
    return lax.dot_general(a, b, _NT, preferred_element_type=f32)


def _dot_tn(a, b):
    return lax.dot_general(a, b, _TN, preferred_element_type=f32)


def _rms(x, n):
    return lax.rsqrt(jnp.sum(x * x, axis=-1, keepdims=True) * (1.0 / n) + EPS)


def _silu(x):
    return x * jax.nn.sigmoid(x)


def _softplus(x):
    return jnp.maximum(x, 0.0) + jnp.log1p(jnp.exp(-jnp.abs(x)))


HEAD_ROW = (3, 2, 1, 0, 7, 6, 5, 4)


def _fold_heads(part):
    row = lax.broadcasted_iota(jnp.int32, part.shape[1:], 0)
    v = [part[h] for h in range(8)]
    v = [jnp.where(row < 4, v[j] + pltpu.roll(v[j], 4, 0), v[j + 4] + pltpu.roll(v[j + 4], 4, 0)) for j in range(4)]
    v = [jnp.where(row % 4 >= 2, v[j] + pltpu.roll(v[j], 2, 0), v[j + 2] + pltpu.roll(v[j + 2], 6, 0)) for j in range(2)]
    return jnp.where(row % 2 == 1, v[0] + pltpu.roll(v[0], 1, 0), v[1] + pltpu.roll(v[1], 7, 0))


def _const_spec(shape):
    nd = len(shape)
    return pl.BlockSpec(shape, lambda *_: (0,) * nd, pipeline_mode=pl.Buffered(1))


def _params(sem, vmem_mib):
    return pltpu.CompilerParams(dimension_semantics=sem, vmem_limit_bytes=vmem_mib * MiB)


def _proj_body(x_ref, g_ref, w_ref, z_ref, xbc_ref, cq_ref, ckv_ref, kr_ref, dtt_ref):
    x = x_ref[...]
    xn = (x * _rms(x, D_MODEL) * g_ref[...]).astype(bf16)
    z_ref[...] = _dot(xn, w_ref[:, PC_Z:PC_XBC])
    xbc_ref[...] = _dot(xn, w_ref[:, PC_XBC:PC_CQ])
    cq_ref[...] = _dot(xn, w_ref[:, PC_CQ:PC_CKV])
    ckv_ref[...] = _dot(xn, w_ref[:, PC_CKV:PC_KR])
    last = _dot(xn, w_ref[:, PC_KR:PC_END])
    lane = lax.broadcasted_iota(jnp.int32, (1, LANE), 1)
    kr_ref[...] = jnp.where(lane >= ROPE_LO, last, 0.0)
    dtt_ref[...] = last.T[0:16, :]


def _project(x, g, w, tm):
    m = x.shape[0]
    row = lambda n: pl.BlockSpec((tm, n), lambda i: (i, 0))
    widths = (SSD_WIDTH, CONV_DIM, Q_LORA, KV_LORA, LANE)
    return pl.pallas_call(
        _proj_body,
        grid=(m // tm,),
        in_specs=[row(D_MODEL), _const_spec((1, D_MODEL)), _const_spec((D_MODEL, PC_END))],
        out_specs=[row(n) for n in widths] + [pl.BlockSpec((16, tm), lambda i: (0, i))],
        out_shape=[jax.ShapeDtypeStruct((m, n), f32) for n in widths] + [jax.ShapeDtypeStruct((16, m), f32)],
        compiler_params=_params(("parallel",), 40),
        name="proj",
    )(x, g, w)


def _mla_prep_body(cq_ref, ckv_ref, kr_ref, tc_ref, ts1_ref, ts2_ref, gqn_ref, gkv_ref, wuq_ref, wuk_ref, wuv_ref,
                   gq_ref, gk_ref, lat_ref, kro_ref, q_ref, k_ref, vt_ref):
    tc, ts1, ts2 = tc_ref[...], ts1_ref[...], ts2_ref[...]
    tsw = ts1 + ts2

    def rope(x):
        return x * tc + pltpu.roll(x, LANE - ROPE_HALF, 1) * ts1 + pltpu.roll(x, ROPE_HALF, 1) * ts2

    ckv = ckv_ref[...]
    lat = ckv * _rms(ckv, KV_LORA) * gkv_ref[...]
    lat_ref[...] = lat
    kr = rope(kr_ref[...])
    kro_ref[...] = kr
    cq = cq_ref[...]
    cqn = (cq * _rms(cq, Q_LORA) * gqn_ref[...]).astype(bf16)
    latb = lat.astype(bf16)
    vt_ref[...] = _dot_nt(wuv_ref[...], latb).astype(vt_ref.dtype)
    gq, gk = gq_ref[...], gk_ref[...]
    q_all = _dot(cqn, wuq_ref[:, 0:MLA_HEADS * HT])
    qp_all = _dot(cqn, wuq_ref[:, MLA_HEADS * HT:])
    k_all = _dot(latb, wuk_ref[...])
    for h in range(MLA_HEADS):
        sl = slice(h * HT, (h + 1) * HT)
        qh = q_all[:, sl] * tc + qp_all[:, sl] * tsw
        q_ref[:, sl] = (qh * _rms(qh, QK_DIM) * gq).astype(q_ref.dtype)
        kh = k_all[:, sl] + kr
        k_ref[:, sl] = (kh * _rms(kh, QK_DIM) * gk).astype(k_ref.dtype)


def _mla_prep(cq, ckv, kr, tabs, n_tab_blocks, gqn, gkv, wuq, wuk, wuvt, gq, gk, tm, q_dtype):
    m = cq.shape[0]
    row = lambda n: pl.BlockSpec((tm, n), lambda i: (i, 0))
    tab = pl.BlockSpec((tm, LANE), lambda i: (i % n_tab_blocks, 0))
    widths = (KV_LORA, LANE, MLA_HEADS * HT, MLA_HEADS * HT)
    dtypes = (f32, f32, q_dtype, bf16)
    return pl.pallas_call(
        _mla_prep_body,
        grid=(m // tm,),
        in_specs=[row(Q_LORA), row(KV_LORA), row(LANE), tab, tab, tab,
                  _const_spec((1, Q_LORA)), _const_spec((1, KV_LORA)),
                  _const_spec((Q_LORA, 2 * MLA_HEADS * HT)), _const_spec((KV_LORA, MLA_HEADS * HT)),
                  _const_spec((MLA_WIDTH, KV_LORA)), _const_spec((1, HT)), _const_spec((1, HT))],
        out_specs=[row(n) for n in widths] + [pl.BlockSpec((MLA_WIDTH, tm), lambda i: (0, i))],
        out_shape=[jax.ShapeDtypeStruct((m, n), d) for n, d in zip(widths, dtypes)]
                  + [jax.ShapeDtypeStruct((MLA_WIDTH, m), bf16)],
        compiler_params=_params(("parallel",), 40),
        name="mla_prep",
    )(cq, ckv, kr, *tabs, gqn, gkv, wuq, wuk, wuvt, gq, gk)


def _ssd_body(xbc_ref, z_ref, dtt_ref, hist_ref, h0_ref, cw_ref, cb_ref, bias_c_ref,
              alog_c_ref, dsk_ref, ng_ref, tri_ref, y_ref, hout_ref, hfin_ref, xp_ref, hs_ref, *, valid_from, cps):
    c = pl.program_id(1)
    q = SSD_CHUNK

    @pl.when(c == 0)
    def _():
        xp_ref[0:SUBLANE, :] = hist_ref[...]
        hs_ref[...] = h0_ref[0]

    rows = lax.broadcasted_iota(jnp.int32, (q, q), 0)
    cols = lax.broadcasted_iota(jnp.int32, (q, q), 1)
    causal = cols <= rows
    lane = lax.broadcasted_iota(jnp.int32, (1, LANE), 1)
    low = lane < D_STATE
    tri = tri_ref[...]
    dsk = dsk_ref[...]

    def chunk(xbc, dtt, z):
        xp_ref[SUBLANE:SUBLANE + q, :] = xbc
        conv = cb_ref[...]
        for k in range(CONV_W):
            lo = SUBLANE - (CONV_W - 1) + k
            conv = conv + xp_ref[lo:lo + q, :] * cw_ref[k:k + 1, :]
        xp_ref[0:SUBLANE, :] = xbc[q - SUBLANE:q, :]
        xa = _silu(conv)
        xs = xa[:, :SSD_WIDTH]
        bm = xa[:, SSD_WIDTH:SSD_WIDTH + LANE]
        cm = xa[:, SSD_WIDTH + LANE:]

        dtr = _softplus(dtt + bias_c_ref[...])
        if valid_from:
            dtr = jnp.where(lax.broadcasted_iota(jnp.int32, (16, q), 1) >= valid_from, dtr, 0.0)
        dar = dtr * -jnp.exp(alog_c_ref[...])
        dtc = dtr.T
        acs_r = lax.dot_general(dar, tri, _NT, precision=_HI, preferred_element_type=f32)
        acs_c = jnp.dot(tri, dar.T, precision=_HI, preferred_element_type=f32)
        w_c = jnp.exp(acs_c[q - 1:q, :] - acs_c) * dtc
        e_c = jnp.exp(acs_c)
        cd_r = jnp.exp(acs_r[:, q - 1:q])

        bb = bm.astype(bf16)
        ys = []
        for g in range(SSD_GROUPS):
            gmask = (lane >= g * D_STATE) & (lane < (g + 1) * D_STATE)
            cg = jnp.where(gmask, cm, 0.0).astype(bf16)
            cb_g = _dot_nt(cg, bb)
            for pi in range(2):
                i = 2 * g + pi
                xpair = xs[:, i * LANE:(i + 1) * LANE]
                ypair = jnp.zeros((q, LANE), f32)
                for hh in range(2):
                    h = 2 * i + hh
                    seg = acs_c[:, h:h + 1] - acs_r[h:h + 1, :]
                    lmat = jnp.exp(jnp.where(causal, seg, -jnp.inf))
                    sc = (cb_g * lmat * dtr[h:h + 1, :]).astype(bf16)
                    xh = jnp.where(low if hh == 0 else jnp.logical_not(low), xpair, 0.0).astype(bf16)
                    ypair = ypair + _dot(sc, xh)
                h0, h1 = 2 * i, 2 * i + 1
                wp = jnp.where(low, w_c[:, h0:h0 + 1], w_c[:, h1:h1 + 1])
                ep = jnp.where(low, e_c[:, h0:h0 + 1], e_c[:, h1:h1 + 1])
                hst = hs_ref[i]
                yoff = _dot_nt(cg, hst.astype(bf16)) * ep
                st = _dot_tn((xpair * wp).astype(bf16), bb)
                cdb = jnp.concatenate([jnp.broadcast_to(cd_r[h0:h0 + 1, :], (SSD_HEAD_DIM, LANE)),
                                       jnp.broadcast_to(cd_r[h1:h1 + 1, :], (SSD_HEAD_DIM, LANE))], axis=0)
                hs_ref[i] = hst * cdb + st
                ys.append(ypair + yoff + dsk[:, i * LANE:(i + 1) * LANE] * xpair)
        y = jnp.concatenate(ys, axis=1) * _silu(z)
        gw = SSD_WIDTH // SSD_GROUPS
        outs = []
        for g in range(SSD_GROUPS):
            yg = y[:, g * gw:(g + 1) * gw]
            outs.append(yg * _rms(yg, gw))
        return (jnp.concatenate(outs, axis=1) * ng_ref[...]).astype(y_ref.dtype)

    for ci in range(cps):
        rs = slice(ci * q, (ci + 1) * q)
        y_ref[rs, :] = chunk(xbc_ref[rs, :], dtt_ref[:, rs], z_ref[rs, :])

    @pl.when(c == pl.num_programs(1) - 1)
    def _():
        hout_ref[0] = hs_ref[...]
        for i in range(2 * SSD_GROUPS):
            g = i // 2
            for hh in range(2):
                hfin_ref[0, 2 * i + hh] = hs_ref[i, hh * SSD_HEAD_DIM:(hh + 1) * SSD_HEAD_DIM, g * D_STATE:(g + 1) * D_STATE]


def _ssd(xbc, z, dtt, hist_arr, hist_block, h0, consts, n_batch, n_chunks, row_block0, valid_from, cps):
    assert n_chunks % cps == 0
    q = SSD_CHUNK
    n_steps = n_chunks // cps
    rb = lambda n: pl.BlockSpec((cps * q, n), lambda b, c: (row_block0 + b * n_steps + c, 0))
    cw, cb, bias_c, alog_c, dsk, ng, tri = consts
    return pl.pallas_call(
        functools.partial(_ssd_body, valid_from=valid_from, cps=cps),
        grid=(n_batch, n_steps),
        in_specs=[rb(CONV_DIM), rb(SSD_WIDTH),
                  pl.BlockSpec((16, cps * q), lambda b, c: (0, row_block0 + b * n_steps + c)),
                  pl.BlockSpec((SUBLANE, CONV_DIM), lambda b, c: (hist_block, 0)),
                  _const_spec((1, 4, LANE, LANE)),
                  _const_spec((CONV_W, CONV_DIM)), _const_spec((1, CONV_DIM)),
                  _const_spec((16, 1)), _const_spec((16, 1)),
                  _const_spec((1, SSD_WIDTH)), _const_spec((1, SSD_WIDTH)), _const_spec((q, q))],
        out_specs=[pl.BlockSpec((cps * q, SSD_WIDTH), lambda b, c: (b * n_steps + c, 0)),
                   pl.BlockSpec((1, 4, LANE, LANE), lambda b, c: (b, 0, 0, 0)),
                   pl.BlockSpec((1, SSD_HEADS, SSD_HEAD_DIM, D_STATE), lambda b, c: (b, 0, 0, 0))],
        out_shape=[jax.ShapeDtypeStruct((n_batch * n_chunks * q, SSD_WIDTH), bf16),
                   jax.ShapeDtypeStruct((n_batch, 4, LANE, LANE), f32),
                   jax.ShapeDtypeStruct((n_batch, SSD_HEADS, SSD_HEAD_DIM, D_STATE), f32)],
        scratch_shapes=[pltpu.VMEM((SUBLANE + q, CONV_DIM), f32), pltpu.VMEM((4, LANE, LANE), f32)],
        compiler_params=_params(("parallel", "arbitrary"), 32),
        name="ssd",
    )(xbc, z, dtt, hist_arr, h0, cw, cb, bias_c, alog_c, dsk, ng, tri)


def _flash_body(q_ref, k_ref, vt_ref, km_ref, vtm_ref, o_ref, *, tq, tk, nh):
    qi = pl.program_id(2)
    qs = [q_ref[:, h * HT:(h + 1) * HT] for h in range(nh)]
    drow = lax.broadcasted_iota(jnp.int32, (2 * V_DIM, 1), 0)
    vmasks = (drow < V_DIM, drow >= V_DIM)
    half = tk // 2

    def heads(kall, vtall, carries, mask=None, qlo=0):
        sts = [_dot_nt(kall[:, h * HT:(h + 1) * HT], qs[h][qlo:, :]) for h in range(nh)]
        out = []
        for h in range(nh):
            m0, l0, acc0 = carries[h]
            m, l, acct = m0[:, qlo:], l0[:, qlo:], acc0[:, qlo:]
            st = sts[h] if mask is None else jnp.where(mask, sts[h], -jnp.inf)
            m2 = jnp.maximum(m, jnp.max(st, axis=0, keepdims=True))
            pt = jnp.exp(st - m2)
            a = jnp.exp(m - m2)
            vtt = vtall[(h // 2) * LANE:(h // 2 + 1) * LANE, :]
            vz = jnp.where(vmasks[h % 2], vtt, jnp.zeros_like(vtt))
            new = (m2, a * l + jnp.sum(pt, axis=0, keepdims=True), a * acct + _dot(vz, pt.astype(bf16)))
            if qlo:
                new = tuple(jnp.concatenate([old[:, :qlo], x], axis=1) for old, x in zip(carries[h], new))
            out.append(new)
        return tuple(out)

    init = (jnp.full((1, tq), -jnp.inf, f32), jnp.zeros((1, tq), f32), jnp.zeros((2 * V_DIM, tq), f32))

    def body(j, carries):
        off = pl.multiple_of(j * tk, tk)
        return heads(k_ref[pl.ds(off, tk), :], vt_ref[:, pl.ds(off, tk)], carries)

    carries = lax.fori_loop(0, qi, body, (init,) * nh)
    off = pl.multiple_of(qi * tk, tk)
    k_a = jnp.concatenate([k_ref[pl.ds(off, half), :], km_ref[...]], axis=0)
    vt_a = jnp.concatenate([vt_ref[:, pl.ds(off, half)], vtm_ref[...]], axis=1)
    krow = lax.broadcasted_iota(jnp.int32, (half + N_META, tq), 0)
    qcol = lax.broadcasted_iota(jnp.int32, (half + N_META, tq), 1)
    carries = heads(k_a, vt_a, carries, (krow >= half) | (krow <= qcol))
    off_b = pl.multiple_of(qi * tk + half, half)
    krow = lax.broadcasted_iota(jnp.int32, (half, tq - half), 0)
    qcol = lax.broadcasted_iota(jnp.int32, (half, tq - half), 1)
    res = heads(k_ref[pl.ds(off_b, half), :], vt_ref[:, pl.ds(off_b, half)], carries, krow <= qcol, qlo=half)
    for pr in range(nh // 2):
        (_, l0, acc0), (_, l1, acc1) = res[2 * pr], res[2 * pr + 1]
        o_ref[:, pr * LANE:(pr + 1) * LANE] = (acc0 / l0 + acc1 / l1).T.astype(o_ref.dtype)


def _flash(q, k, vt, k_small, vt_meta, meta_block, n_batch, seq, tq, tk, nh):
    assert tk == tq and seq % tk == 0 and nh % 2 == 0 and MLA_HEADS % nh == 0
    nq = seq // tq
    return pl.pallas_call(
        functools.partial(_flash_body, tq=tq, tk=tk, nh=nh),
        grid=(n_batch, MLA_HEADS // nh, nq),
        in_specs=[pl.BlockSpec((tq, nh * HT), lambda b, p, i: (b * nq + i, p)),
                  pl.BlockSpec((seq, nh * HT), lambda b, p, i: (b, p)),
                  pl.BlockSpec((nh * V_DIM, seq), lambda b, p, i: (p, b)),
                  pl.BlockSpec((N_META, nh * HT), lambda b, p, i: (meta_block, p)),
                  pl.BlockSpec((nh * V_DIM, N_META), lambda b, p, i: (p, 0))],
        out_specs=pl.BlockSpec((tq, nh * V_DIM), lambda b, p, i: (b * nq + i, p)),
        out_shape=jax.ShapeDtypeStruct((n_batch * seq, MLA_WIDTH), f32),
        compiler_params=_params(("parallel", "parallel", "arbitrary"), 40),
        name="flash",
    )(q, k, vt, k_small, vt_meta)


def _finish_body(x_ref, ssd_ref, att_ref, gm_ref, wo_ref, gf_ref, wg_ref, wu_ref, wd_ref, o_ref):
    att = att_ref[...]
    mla = (att * _rms(att, MLA_WIDTH) * gm_ref[...]).astype(bf16)
    h = x_ref[...] + (_dot(ssd_ref[...], wo_ref[0:SSD_WIDTH, :]) + _dot(mla, wo_ref[SSD_WIDTH:, :]))
    n = (h * _rms(h, D_MODEL) * gf_ref[...]).astype(bf16)
    ff = jnp.zeros_like(h)
    for c in range(N_FF_CHUNKS):
        cs = slice(c * FF_CHUNK, (c + 1) * FF_CHUNK)
        a = (_silu(_dot(n, wg_ref[:, cs])) * _dot(n, wu_ref[:, cs])).astype(bf16)
        ff = ff + _dot(a, wd_ref[cs, :])
    o_ref[...] = h + ff


def _finish(x, ssd, att, gm, wo, gf, wg, wu, wd, tm):
    m = x.shape[0]
    row = lambda n: pl.BlockSpec((tm, n), lambda i: (i, 0))
    return pl.pallas_call(
        _finish_body,
        grid=(m // tm,),
        in_specs=[row(D_MODEL), row(SSD_WIDTH), row(MLA_WIDTH), _const_spec((1, MLA_WIDTH)),
                  _const_spec((D_MODEL, D_MODEL)), _const_spec((1, D_MODEL)),
                  _const_spec((D_MODEL, D_FF)), _const_spec((D_MODEL, D_FF)), _const_spec((D_FF, D_MODEL))],
        out_specs=row(D_MODEL),
        out_shape=jax.ShapeDtypeStruct((m, D_MODEL), f32),
        compiler_params=_params(("parallel",), 56),
        name="finish",
    )(x, ssd, att, gm, wo, gf, wg, wu, wd)


def _sssd_prep_body(xbc_ref, sc_ref, cw_ref, cb_ref, xs_ref, xat_ref):
    conv = cb_ref[...]
    for k in range(CONV_W - 1):
        conv = conv + sc_ref[k] * cw_ref[k:k + 1, :]
    conv = conv + xbc_ref[...] * cw_ref[CONV_W - 1:CONV_W, :]
    xa = _silu(conv)
    xs_ref[...] = xa[:, :SSD_WIDTH]
    xat_ref[...] = xa.T


def _sssd_prep(xbc_small, sc, cw, cb, nseq):
    return pl.pallas_call(
        _sssd_prep_body,
        grid=(1,),
        in_specs=[pl.BlockSpec((nseq, CONV_DIM), lambda i: (0, 0)), _const_spec((CONV_W - 1, nseq, CONV_DIM)),
                  _const_spec((CONV_W, CONV_DIM)), _const_spec((1, CONV_DIM))],
        out_specs=[pl.BlockSpec((nseq, SSD_WIDTH), lambda i: (0, 0)), pl.BlockSpec((CONV_DIM, nseq), lambda i: (0, 0))],
        out_shape=[jax.ShapeDtypeStruct((nseq, SSD_WIDTH), f32), jax.ShapeDtypeStruct((CONV_DIM, nseq), f32)],
        compiler_params=_params(("arbitrary",), 32),
        name="sssd_prep",
    )(xbc_small, sc, cw, cb)


def _sssd_state_body(h0_ref, xst_ref, bt_ref, ct_ref, dtt_ref, bias_c_ref, alog_c_ref, hn_ref, yt_ref):
    h = pl.program_id(0)
    dt = _softplus(dtt_ref[pl.ds(h, 1), :] + bias_c_ref[pl.ds(h, 1), :])
    dec = jnp.exp(dt * -jnp.exp(alog_c_ref[pl.ds(h, 1), :]))
    bt, ct = bt_ref[...], ct_ref[...]

    def body(p, carry):
        xdt = xst_ref[pl.ds(p, 1), :] * dt
        hn = h0_ref[0, p] * dec + xdt * bt
        hn_ref[0, p] = hn
        yt_ref[pl.ds(p, 1), :] = jnp.sum(ct * hn, axis=0, keepdims=True)
        return carry

    lax.fori_loop(0, SSD_HEAD_DIM, body, 0, unroll=4)


def _sssd_state(h0t, xat, dtt_small, bias_c, alog_c, nseq):
    hpg = SSD_HEADS // SSD_GROUPS
    rows = lambda f: pl.BlockSpec((SSD_HEAD_DIM, nseq), f)
    hblk = pl.BlockSpec((1, SSD_HEAD_DIM, D_STATE, nseq), lambda h: (h, 0, 0, 0))
    return pl.pallas_call(
        _sssd_state_body,
        grid=(SSD_HEADS,),
        in_specs=[hblk, rows(lambda h: (h, 0)), rows(lambda h: (SSD_HEADS + h // hpg, 0)),
                  rows(lambda h: (SSD_HEADS + SSD_GROUPS + h // hpg, 0)),
                  pl.BlockSpec((16, nseq), lambda h: (0, 0)), _const_spec((16, 1)), _const_spec((16, 1))],
        out_specs=[hblk, rows(lambda h: (h, 0))],
        out_shape=[jax.ShapeDtypeStruct((SSD_HEADS, SSD_HEAD_DIM, D_STATE, nseq), f32),
                   jax.ShapeDtypeStruct((SSD_WIDTH, nseq), f32)],
        compiler_params=_params(("parallel",), 32),
        name="sssd_state",
    )(h0t, xat, xat, xat, dtt_small, bias_c, alog_c)


def _sssd_gate_body(yt_ref, xs_ref, z_ref, dsk_ref, ng_ref, o_ref):
    y = (yt_ref[...].T + dsk_ref[...] * xs_ref[...]) * _silu(z_ref[...])
    gw = SSD_WIDTH // SSD_GROUPS
    outs = []
    for g in range(SSD_GROUPS):
        yg = y[:, g * gw:(g + 1) * gw]
        outs.append(yg * _rms(yg, gw))
    o_ref[...] = (jnp.concatenate(outs, axis=1) * ng_ref[...]).astype(o_ref.dtype)


def _sssd_gate(yt, xs, z_small, dsk, ng, nseq):
    blk = pl.BlockSpec((nseq, SSD_WIDTH), lambda i: (0, 0))
    return pl.pallas_call(
        _sssd_gate_body,
        grid=(1,),
        in_specs=[pl.BlockSpec((SSD_WIDTH, nseq), lambda i: (0, 0)), blk, blk,
                  _const_spec((1, SSD_WIDTH)), _const_spec((1, SSD_WIDTH))],
        out_specs=blk,
        out_shape=jax.ShapeDtypeStruct((nseq, SSD_WIDTH), bf16),
        compiler_params=_params(("arbitrary",), 32),
        name="sssd_gate",
    )(yt, xs, z_small, dsk, ng)


def _absorb_body(q_ref, gk_ref, wabs_ref, lat_ref, kr_ref, wuk_ref, qg_ref, a_ref, snew_ref):
    gk = gk_ref[...]
    latb = lat_ref[...].astype(bf16)
    latf = latb.astype(f32)
    kr = kr_ref[...]
    krf = kr.astype(bf16).astype(f32)
    kr2 = jnp.sum(kr * kr, axis=-1, keepdims=True)
    s_new = []
    for h in range(MLA_HEADS):
        sl = slice(h * HT, (h + 1) * HT)
        qg = q_ref[:, sl] * gk
        qg_ref[:, sl] = qg
        a = jnp.dot(qg, wabs_ref[sl, :], precision=_HI, preferred_element_type=f32)
        a_ref[:, h * KV_LORA:(h + 1) * KV_LORA] = a
        kn = _dot(latb, wuk_ref[:, sl])
        n2 = jnp.sum(kn * kn, axis=-1, keepdims=True)
        s12 = (jnp.sum(a.astype(bf16).astype(f32) * latf, axis=-1, keepdims=True)
               + jnp.sum(qg.astype(bf16).astype(f32) * krf, axis=-1, keepdims=True))
        s_new.append(lax.rsqrt((n2 + kr2) * (1.0 / QK_DIM) + EPS) * s12)
    a_ref[:, MLA_HEADS * KV_LORA:] = jnp.zeros((q_ref.shape[0], (16 - MLA_HEADS) * KV_LORA), f32)
    snew_ref[...] = jnp.concatenate(s_new, axis=1)


def _absorb(q_small, gk, wabs, lat_small, kr_small, wuk, nseq):
    blk = lambda n: pl.BlockSpec((nseq, n), lambda i: (0, 0))
    return pl.pallas_call(
        _absorb_body,
        grid=(1,),
        in_specs=[blk(MLA_HEADS * HT), _const_spec((1, HT)), _const_spec((MLA_HEADS * HT, KV_LORA)),
                  blk(KV_LORA), blk(HT), _const_spec((KV_LORA, MLA_HEADS * HT))],
        out_specs=[blk(MLA_HEADS * HT), blk(16 * KV_LORA), blk(MLA_HEADS)],
        out_shape=[jax.ShapeDtypeStruct((nseq, MLA_HEADS * HT), f32), jax.ShapeDtypeStruct((nseq, 16 * KV_LORA), f32),
                   jax.ShapeDtypeStruct((nseq, MLA_HEADS), f32)],
        compiler_params=_params(("arbitrary",), 32),
        name="absorb",
    )(q_small, gk, wabs, lat_small, kr_small, wuk)


def _decode_body(pt_ref, a_ref, c_ref, latn_ref, snew_ref, wukt_ref, clat_ref, ckrt_ref, o_ref,
                 lat_buf, kr_buf, waug, latb, s_buf, sems, *, n_pages, page, ppt):
    b = pl.program_id(0)
    nb = pl.num_programs(0)
    slot = b % 2
    n_tiles = n_pages // ppt
    tile = ppt * page
    wrows = MLA_HEADS * QK_NOPE

    def start_page(seq, p, sl):
        pid = pt_ref[seq * n_pages + p]
        pltpu.make_async_copy(clat_ref.at[0, pid], lat_buf.at[sl, p], sems.at[0, sl]).start()
        pltpu.make_async_copy(ckrt_ref.at[0, pid], kr_buf.at[sl, p], sems.at[1, sl]).start()

    @pl.when(b == 0)
    def _():
        waug[0:wrows, :] = wukt_ref[...]

        def first(p, carry):
            start_page(0, p, 0)
            return carry

        lax.fori_loop(0, n_pages, first, 0)

    pltpu.make_async_copy(clat_ref.at[0, pl.ds(0, n_pages)], lat_buf.at[slot], sems.at[0, slot]).wait()
    pltpu.make_async_copy(ckrt_ref.at[0, pl.ds(0, n_pages)], kr_buf.at[slot], sems.at[1, slot]).wait()

    waug[wrows:wrows + 16, :] = a_ref[0, 0].astype(bf16)
    cb = c_ref[0].astype(bf16)

    def score_dots(lb, krt):
        kt = _dot_nt(waug[...], lb)
        return kt, _dot(cb, krt.astype(bf16)), krt

    def score_finish(kt, s2, krt):
        n = kt.shape[1]
        k4 = kt[0:wrows, :].reshape(MLA_HEADS, QK_NOPE // SUBLANE, SUBLANE, n)
        n2 = _fold_heads(jnp.sum(k4 * k4, axis=1))
        kr2 = jnp.sum(krt * krt, axis=0, keepdims=True)
        return lax.rsqrt((n2 + kr2) * (1.0 / QK_DIM) + EPS) * (kt[wrows:wrows + MLA_HEADS, :] + s2)

    def body(j, carry):
        @pl.when(b + 1 < nb)
        def _():
            for pp in range(ppt):
                start_page(b + 1, j * ppt + pp, 1 - slot)

        lb = lat_buf[slot, pl.ds(j * ppt, ppt)].reshape(tile, KV_LORA).astype(bf16)
        latb[j] = lb
        kr_pages = kr_buf[slot, pl.ds(j * ppt, ppt)]
        s_buf[j] = score_finish(*score_dots(lb, jnp.concatenate([kr_pages[i] for i in range(ppt)], axis=-1)))
        return carry

    lax.fori_loop(0, n_tiles, body, 0)
    s_new = snew_ref[0]
    s_all = jnp.concatenate([s_buf[j] for j in range(n_tiles)], axis=-1)
    m = jnp.maximum(jnp.max(s_all, axis=-1, keepdims=True), s_new)
    p_all = jnp.exp(s_all - m)
    p_new = jnp.exp(s_new - m)
    l = jnp.sum(p_all, axis=-1, keepdims=True) + p_new
    acc = _dot(p_all.astype(bf16), latb[...].reshape(n_tiles * tile, KV_LORA)) + p_new * latn_ref[0]
    o_ref[0] = acc / l


def _decode(page_table_flat, a, c, lat_new, s_new, wukt, cache_lat, cache_kr_t, nseq, n_pages, page, ppt):
    assert n_pages % ppt == 0
    wrows = MLA_HEADS * QK_NOPE
    grid_spec = pltpu.PrefetchScalarGridSpec(
        num_scalar_prefetch=1,
        grid=(nseq,),
        in_specs=[pl.BlockSpec((1, 1, 16, KV_LORA), lambda b, pt: (b, 0, 0, 0)),
                  pl.BlockSpec((1, MLA_HEADS, QK_ROPE), lambda b, pt: (b, 0, 0)),
                  pl.BlockSpec((1, 1, KV_LORA), lambda b, pt: (b, 0, 0)),
                  pl.BlockSpec((1, MLA_HEADS, 1), lambda b, pt: (b, 0, 0)),
                  pl.BlockSpec((wrows, KV_LORA), lambda b, pt: (0, 0), pipeline_mode=pl.Buffered(1)),
                  pl.BlockSpec(memory_space=pl.ANY), pl.BlockSpec(memory_space=pl.ANY)],
        out_specs=pl.BlockSpec((1, MLA_HEADS, KV_LORA), lambda b, pt: (b, 0, 0)),
        scratch_shapes=[pltpu.VMEM((2, n_pages, page, KV_LORA), f32), pltpu.VMEM((2, n_pages, QK_ROPE, page), f32),
                        pltpu.VMEM((wrows + 16, KV_LORA), bf16),
                        pltpu.VMEM((n_pages // ppt, ppt * page, KV_LORA), bf16),
                        pltpu.VMEM((n_pages // ppt, MLA_HEADS, ppt * page), f32), pltpu.SemaphoreType.DMA((2, 2))],
    )
    return pl.pallas_call(
        functools.partial(_decode_body, n_pages=n_pages, page=page, ppt=ppt),
        grid_spec=grid_spec,
        out_shape=jax.ShapeDtypeStruct((nseq, MLA_HEADS, KV_LORA), f32),
        compiler_params=_params(("arbitrary",), 40),
        name="decode",
    )(page_table_flat, a, c, lat_new, s_new, wukt, cache_lat, cache_kr_t)


def _uv_body(o_ref, wuv_ref, att_ref):
    for h in range(MLA_HEADS):
        oh = o_ref[:, h * KV_LORA:(h + 1) * KV_LORA].astype(bf16)
        att_ref[:, h * V_DIM:(h + 1) * V_DIM] = _dot(oh, wuv_ref[:, h * V_DIM:(h + 1) * V_DIM])


def _uv(o_flat, wuv):
    nseq = o_flat.shape[0]
    return pl.pallas_call(
        _uv_body,
        grid=(1,),
        in_specs=[pl.BlockSpec((nseq, MLA_HEADS * KV_LORA), lambda i: (0, 0)), _const_spec((KV_LORA, MLA_WIDTH))],
        out_specs=pl.BlockSpec((nseq, MLA_WIDTH), lambda i: (0, 0)),
        out_shape=jax.ShapeDtypeStruct((nseq, MLA_WIDTH), f32),
        compiler_params=_params(("arbitrary",), 32),
        name="uv",
    )(o_flat, wuv)


def _rope_tables(pos):
    inv = ROPE_THETA ** (-jnp.arange(ROPE_HALF, dtype=f32) * (2.0 / QK_ROPE))
    ang = pos.astype(f32)[:, None] * inv[None, :]
    cos, sin = jnp.cos(ang), jnp.sin(ang)
    n = pos.shape[0]
    one = jnp.ones((n, QK_NOPE), f32)
    z16 = jnp.zeros((n, ROPE_HALF), f32)
    z32 = jnp.zeros((n, HT - QK_DIM), f32)
    z64 = jnp.zeros((n, QK_NOPE), f32)
    tc = jnp.concatenate([one, cos, cos, z32], axis=1)
    ts1 = jnp.concatenate([z64, -sin, z16, z32], axis=1)
    ts2 = jnp.concatenate([z64, z16, sin, z32], axis=1)
    return tc, ts1, ts2


def _pad_lanes(x, n):
    return jnp.pad(x, ((0, 0), (0, n - x.shape[1])))


def kernel(x_prompt, x_sample, cache_kv_latent, cache_k_rope, state_conv, state_ssm, page_table, meta_tokens, attn_norm_g, w_in, conv_w, conv_b, dt_bias, a_log, d_skip, ssd_norm_g, q_norm_g, w_uq, kv_norm_g, w_ukv, q_head_g_nope, q_head_g_rope, k_head_g_nope, k_head_g_rope, mla_out_g, w_out, ffn_norm_g, w_gate, w_up, w_down):
    l = 0
    nb, seq, _ = x_prompt.shape
    nseq = x_sample.shape[0]
    n_pages, page = page_table.shape[1], cache_kv_latent.shape[2]
    past = n_pages * page
    n_small = 2 * LANE
    meta_lo = n_small - N_META

    s0, s1, s2, s3, s4 = (SSD_WIDTH, SSD_WIDTH + CONV_DIM, SSD_WIDTH + CONV_DIM + SSD_HEADS,
                          SSD_WIDTH + CONV_DIM + SSD_HEADS + Q_LORA, SSD_WIDTH + CONV_DIM + SSD_HEADS + Q_LORA + KV_LORA)
    wi = w_in[l]
    w_last = jnp.concatenate([_pad_lanes(wi[:, s1:s2], ROPE_LO), _pad_lanes(wi[:, s4:], HT - ROPE_LO)], axis=1)
    w_proj = jnp.concatenate([wi[:, :s1], wi[:, s2:s4], w_last], axis=1).astype(bf16)
    g_attn = attn_norm_g[l][None, :]
    wq3 = w_uq[l].reshape(Q_LORA, MLA_HEADS, QK_DIM)
    zq = jnp.zeros((Q_LORA, MLA_HEADS, QK_NOPE), f32)
    wq_partner = jnp.concatenate([zq, wq3[:, :, QK_NOPE + ROPE_HALF:], wq3[:, :, QK_NOPE:QK_NOPE + ROPE_HALF]], axis=2)
    pad_head = lambda w: jnp.pad(w, ((0, 0), (0, 0), (0, HT - QK_DIM))).reshape(Q_LORA, -1)
    wuq = jnp.concatenate([pad_head(wq3), pad_head(wq_partner)], axis=1).astype(bf16)
    wkv = w_ukv[l].reshape(KV_LORA, MLA_HEADS, QK_NOPE + V_DIM)
    wuk_f32 = jnp.pad(wkv[:, :, :QK_NOPE], ((0, 0), (0, 0), (0, HT - QK_NOPE))).reshape(KV_LORA, -1)
    wuk = wuk_f32.astype(bf16)
    wuv = wkv[:, :, QK_NOPE:].reshape(KV_LORA, MLA_WIDTH).astype(bf16)
    wukt = wkv[:, :, :QK_NOPE].reshape(KV_LORA, -1).T.astype(bf16)
    gq = _pad_lanes(jnp.concatenate([q_head_g_nope[l], q_head_g_rope[l], q_head_g_rope[l]])[None, :], HT) * ATTN_SCALE
    gk = _pad_lanes(jnp.concatenate([k_head_g_nope[l], k_head_g_rope[l], k_head_g_rope[l]])[None, :], HT)
    gqn, gkv = q_norm_g[l][None, :], kv_norm_g[l][None, :]
    cw, cb = conv_w[l], conv_b[l][None, :]
    bias_c = jnp.pad(dt_bias[l][:, None], ((0, 16 - SSD_HEADS), (0, 0)))
    alog_c = jnp.pad(a_log[l][:, None], ((0, 16 - SSD_HEADS), (0, 0)))
    dsk = jnp.repeat(d_skip[l], SSD_HEAD_DIM)[None, :]
    ng = ssd_norm_g[l][None, :]
    tri = jnp.tril(jnp.ones((SSD_CHUNK, SSD_CHUNK), f32))
    ssd_consts = (cw, cb, bias_c, alog_c, dsk, ng, tri)
    gm, gf = mla_out_g[l][None, :], ffn_norm_g[l][None, :]
    wo = w_out[l].astype(bf16)
    wg, wu, wd = w_gate[l].astype(bf16), w_up[l].astype(bf16), w_down[l].astype(bf16)

    xp = x_prompt.reshape(nb * seq, D_MODEL)
    xs_rows = x_sample[:, 0, :]
    x_small = jnp.concatenate([xs_rows, jnp.zeros((meta_lo - nseq, D_MODEL), f32), meta_tokens.astype(f32)], axis=0)

    z_p, xbc_p, cq_p, ckv_p, kr_p, dtt_p = _project(xp, g_attn, w_proj, ROW_TILE)
    z_s, xbc_s, cq_s, ckv_s, kr_s, dtt_s = _project(x_small, g_attn, w_proj, n_small)

    tabs_p = _rope_tables(N_META + jnp.arange(seq))
    pos_small = jnp.concatenate([jnp.full((nseq,), past), jnp.zeros((meta_lo - nseq,), jnp.int32), jnp.arange(N_META)])
    tabs_s = _rope_tables(pos_small)
    tm = ROW_TILE
    lat_p, kro_p, q_p, k_p, vt_p = _mla_prep(cq_p, ckv_p, kr_p, tabs_p, seq // tm, gqn, gkv, wuq, wuk, wuv.T, gq, gk, tm, bf16)
    lat_s, kro_s, q_s, k_s, vt_s = _mla_prep(cq_s, ckv_s, kr_s, tabs_s, 1, gqn, gkv, wuq, wuk, wuv.T, gq, gk, n_small, f32)

    zero_h = jnp.zeros((1, 4, LANE, LANE), f32)
    zero_hist = jnp.zeros((SUBLANE, CONV_DIM), f32)
    _, h_meta, _ = _ssd(xbc_s, z_s, dtt_s, zero_hist, 0, zero_h, ssd_consts, 1, 1, 1, SSD_CHUNK - N_META, 1)
    ssd_p, _, h_fin = _ssd(xbc_p, z_p, dtt_p, xbc_s, n_small // SUBLANE - 1, h_meta, ssd_consts, nb, seq // SSD_CHUNK, 0, 0,
                           SSD_CHUNKS_PER_STEP)

    att_p = _flash(q_p, k_p, vt_p, k_s, vt_s[:, meta_lo:], n_small // N_META - 1, nb, seq, FLASH_TILE, FLASH_TILE,
                   FLASH_HEADS_PER_STEP)
    y_prompt = _finish(xp, ssd_p, att_p, gm, wo, gf, wg, wu, wd, ROW_TILE).reshape(nb, seq, D_MODEL)

    sc = jnp.transpose(state_conv[l], (1, 0, 2))
    xs_s, xat_s = _sssd_prep(xbc_s, sc, cw, cb, nseq)
    h_new_t, yt_s = _sssd_state(jnp.transpose(state_ssm[l], (1, 2, 3, 0)), xat_s, dtt_s, bias_c, alog_c, nseq)
    h_new = jnp.transpose(h_new_t, (3, 0, 1, 2))
    ssd_s = _sssd_gate(yt_s, xs_s, z_s, dsk, ng, nseq)
    qg, a_abs, s_new = _absorb(q_s, gk, wuk_f32.T, lat_s, kro_s, wuk, nseq)
    hr = list(HEAD_ROW)
    a_abs = a_abs.reshape(nseq, 16, KV_LORA)[:, hr + list(range(MLA_HEADS, 16))].reshape(nseq, 1, 16, KV_LORA)
    c_abs = qg.reshape(nseq, MLA_HEADS, HT)[:, hr, ROPE_LO:QK_DIM]
    lat_new = lat_s[:nseq][:, None, :]
    kr_new = kro_s[:nseq, ROPE_LO:QK_DIM][:, None, :]
    o_lat = _decode(page_table.reshape(-1), a_abs, c_abs, lat_new, s_new[:, hr, None], wukt,
                    cache_kv_latent, jnp.swapaxes(cache_k_rope, 2, 3), nseq, n_pages, page, DECODE_PAGES_PER_TILE)
    att_s = _uv(o_lat[:, hr].reshape(nseq, MLA_HEADS * KV_LORA), wuv)
    y_sample = _finish(xs_rows, ssd_s, att_s, gm, wo, gf, wg, wu, wd, nseq)[:, None, :]

    def with_meta(small, main, width):
        meta = jnp.broadcast_to(small[meta_lo:][None], (nb, N_META, width))
        return jnp.concatenate([meta, main.reshape(nb, seq, width)], axis=1)[None]

    kv_latent_prompt = with_meta(lat_s, lat_p, KV_LORA)
    k_rope_prompt = with_meta(kro_s[:, ROPE_LO:QK_DIM], kro_p[:, ROPE_LO:QK_DIM], QK_ROPE)
    conv_prompt = xbc_p.reshape(nb, seq, CONV_DIM)[:, seq - (CONV_W - 1):][None]
    ssm_prompt = h_fin[None]
    kv_latent_sample = lat_new[None]
    k_rope_sample = kr_new[None]
    conv_sample = jnp.concatenate([state_conv[l][:, 1:], xbc_s[:nseq][:, None, :]], axis=1)[None]
    ssm_sample = h_new[None]
    return (y_prompt, y_sample, kv_latent_prompt, k_rope_prompt, conv_prompt, ssm_prompt.astype(x_prompt.dtype),
            kv_latent_sample, k_rope_sample, conv_sample, ssm_sample.astype(state_ssm.dtype))
```

```python
import functools

import jax
import jax.numpy as jnp
from jax import lax
from jax.experimental import pallas as pl
from jax.experimental.pallas import tpu as pltpu

f32 = jnp.float32
bf16 = jnp.bfloat16

D_MODEL = 1024
N_META = 16
SSD_HEADS = 8
SSD_HEAD_DIM = 64
SSD_WIDTH = SSD_HEADS * SSD_HEAD_DIM
SSD_GROUPS = 2
D_STATE = 64
CONV_W = 4
CONV_DIM = SSD_WIDTH + 2 * SSD_GROUPS * D_STATE
SSD_CHUNK = 128
MLA_HEADS = 8
QK_NOPE = 64
QK_ROPE = 32
QK_DIM = QK_NOPE + QK_ROPE
V_DIM = 64
MLA_WIDTH = MLA_HEADS * V_DIM
Q_LORA = 384
KV_LORA = 256
ROPE_THETA = 10000.0
ATTN_SCALE = QK_DIM ** -0.5
D_FF = 2816
EPS = 1e-6

LANE = 128
SUBLANE = 8
HT = LANE
ROPE_LO = QK_NOPE
ROPE_HALF = QK_ROPE // 2
FF_CHUNK = 256
N_FF_CHUNKS = D_FF // FF_CHUNK
DECODE_PAGES_PER_TILE = 32
ROW_TILE = 512
SSD_CHUNKS_PER_STEP = 4
FLASH_TILE = 512
FLASH_HEADS_PER_STEP = 4
MiB = 1024 * 1024

PC_Z = 0
PC_XBC = PC_Z + SSD_WIDTH
PC_CQ = PC_XBC + CONV_DIM
PC_CKV = PC_CQ + Q_LORA
PC_KR = PC_CKV + KV_LORA
PC_END = PC_KR + LANE

_NT = (((1,), (1,)), ((), ()))
_TN = (((0,), (0,)), ((), ()))
_HI = lax.Precision.HIGHEST


def _dot(a, b):
    return jnp.dot(a, b, preferred_element_type=f32)


def _dot_nt(a, b):
    return lax.dot_general(a, b, _NT, preferred_element_type=f32)


def _dot_tn(a, b):
    return lax.dot_general(a, b, _TN, preferred_element_type=f32)


def _rms(x, n):
    return lax.rsqrt(jnp.sum(x * x, axis=-1, keepdims=True) * (1.0 / n) + EPS)


def _silu(x):
    return x * jax.nn.sigmoid(x)


def _softplus(x):
    return jnp.maximum(x, 0.0) + jnp.log1p(jnp.exp(-jnp.abs(x)))


HEAD_ROW = (3, 2, 1, 0, 7, 6, 5, 4)


def _fold_heads(part):
    row = lax.broadcasted_iota(jnp.int32, part.shape[1:], 0)
    v = [part[h] for h in range(8)]
    v = [jnp.where(row < 4, v[j] + pltpu.roll(v[j], 4, 0), v[j + 4] + pltpu.roll(v[j + 4], 4, 0)) for j in range(4)]
    v = [jnp.where(row % 4 >= 2, v[j] + pltpu.roll(v[j], 2, 0), v[j + 2] + pltpu.roll(v[j + 2], 6, 0)) for j in range(2)]
    return jnp.where(row % 2 == 1, v[0] + pltpu.roll(v[0], 1, 0), v[1] + pltpu.roll(v[1], 7, 0))


def _const_spec(shape):
    nd = len(shape)
    return pl.BlockSpec(shape, lambda *_: (0,) * nd, pipeline_mode=pl.Buffered(1))


def _params(sem, vmem_mib):
    return pltpu.CompilerParams(dimension_semantics=sem, vmem_limit_bytes=vmem_mib * MiB)


def _proj_body(x_ref, g_ref, w_ref, z_ref, xbc_ref, cq_ref, ckv_ref, kr_ref, dtt_ref):
    x = x_ref[...]
    xn = (x * _rms(x, D_MODEL) * g_ref[...]).astype(bf16)
    z_ref[...] = _dot(xn, w_ref[:, PC_Z:PC_XBC])
    xbc_ref[...] = _dot(xn, w_ref[:, PC_XBC:PC_CQ])
    cq_ref[...] = _dot(xn, w_ref[:, PC_CQ:PC_CKV])
    ckv_ref[...] = _dot(xn, w_ref[:, PC_CKV:PC_KR])
    last = _dot(xn, w_ref[:, PC_KR:PC_END])
    lane = lax.broadcasted_iota(jnp.int32, (1, LANE), 1)
    kr_ref[...] = jnp.where(lane >= ROPE_LO, last, 0.0)
    dtt_ref[...] = last.T[0:16, :]


def _project(x, g, w, tm):
    m = x.shape[0]
    row = lambda n: pl.BlockSpec((tm, n), lambda i: (i, 0))
    widths = (SSD_WIDTH, CONV_DIM, Q_LORA, KV_LORA, LANE)
    return pl.pallas_call(
        _proj_body,
        grid=(m // tm,),
        in_specs=[row(D_MODEL), _const_spec((1, D_MODEL)), _const_spec((D_MODEL, PC_END))],
        out_specs=[row(n) for n in widths] + [pl.BlockSpec((16, tm), lambda i: (0, i))],
        out_shape=[jax.ShapeDtypeStruct((m, n), f32) for n in widths] + [jax.ShapeDtypeStruct((16, m), f32)],
        compiler_params=_params(("parallel",), 40),
        name="proj",
    )(x, g, w)


def _mla_prep_body(cq_ref, ckv_ref, kr_ref, tc_ref, ts1_ref, ts2_ref, gqn_ref, gkv_ref, wuq_ref, wuk_ref, wuv_ref,
                   gq_ref, gk_ref, lat_ref, kro_ref, q_ref, k_ref, vt_ref):
    tc, ts1, ts2 = tc_ref[...], ts1_ref[...], ts2_ref[...]
    tsw = ts1 + ts2

    def rope(x):
        return x * tc + pltpu.roll(x, LANE - ROPE_HALF, 1) * ts1 + pltpu.roll(x, ROPE_HALF, 1) * ts2

    ckv = ckv_ref[...]
    lat = ckv * _rms(ckv, KV_LORA) * gkv_ref[...]
    lat_ref[...] = lat
    kr = rope(kr_ref[...])
    kro_ref[...] = kr
    cq = cq_ref[...]
    cqn = (cq * _rms(cq, Q_LORA) * gqn_ref[...]).astype(bf16)
    latb = lat.astype(bf16)
    vt_ref[...] = _dot_nt(wuv_ref[...], latb).astype(vt_ref.dtype)
    gq, gk = gq_ref[...], gk_ref[...]
    q_all = _dot(cqn, wuq_ref[:, 0:MLA_HEADS * HT])
    qp_all = _dot(cqn, wuq_ref[:, MLA_HEADS * HT:])
    k_all = _dot(latb, wuk_ref[...])
    for h in range(MLA_HEADS):
        sl = slice(h * HT, (h + 1) * HT)
        qh = q_all[:, sl] * tc + qp_all[:, sl] * tsw
        q_ref[:, sl] = (qh * _rms(qh, QK_DIM) * gq).astype(q_ref.dtype)
        kh = k_all[:, sl] + kr
        k_ref[:, sl] = (kh * _rms(kh, QK_DIM) * gk).astype(k_ref.dtype)


def _mla_prep(cq, ckv, kr, tabs, n_tab_blocks, gqn, gkv, wuq, wuk, wuvt, gq, gk, tm, q_dtype):
    m = cq.shape[0]
    row = lambda n: pl.BlockSpec((tm, n), lambda i: (i, 0))
    tab = pl.BlockSpec((tm, LANE), lambda i: (i % n_tab_blocks, 0))
    widths = (KV_LORA, LANE, MLA_HEADS * HT, MLA_HEADS * HT)
    dtypes = (f32, f32, q_dtype, bf16)
    return pl.pallas_call(
        _mla_prep_body,
        grid=(m // tm,),
        in_specs=[row(Q_LORA), row(KV_LORA), row(LANE), tab, tab, tab,
                  _const_spec((1, Q_LORA)), _const_spec((1, KV_LORA)),
                  _const_spec((Q_LORA, 2 * MLA_HEADS * HT)), _const_spec((KV_LORA, MLA_HEADS * HT)),
                  _const_spec((MLA_WIDTH, KV_LORA)), _const_spec((1, HT)), _const_spec((1, HT))],
        out_specs=[row(n) for n in widths] + [pl.BlockSpec((MLA_WIDTH, tm), lambda i: (0, i))],
        out_shape=[jax.ShapeDtypeStruct((m, n), d) for n, d in zip(widths, dtypes)]
                  + [jax.ShapeDtypeStruct((MLA_WIDTH, m), bf16)],
        compiler_params=_params(("parallel",), 40),
        name="mla_prep",
    )(cq, ckv, kr, *tabs, gqn, gkv, wuq, wuk, wuvt, gq, gk)


def _ssd_body(xbc_ref, z_ref, dtt_ref, hist_ref, h0_ref, cw_ref, cb_ref, bias_c_ref,
              alog_c_ref, dsk_ref, ng_ref, tri_ref, y_ref, hout_ref, hfin_ref, xp_ref, hs_ref, *, valid_from, cps):
    c = pl.program_id(1)
    q = SSD_CHUNK

    @pl.when(c == 0)
    def _():
        xp_ref[0:SUBLANE, :] = hist_ref[...]
        hs_ref[...] = h0_ref[0]

    rows = lax.broadcasted_iota(jnp.int32, (q, q), 0)
    cols = lax.broadcasted_iota(jnp.int32, (q, q), 1)
    causal = cols <= rows
    lane = lax.broadcasted_iota(jnp.int32, (1, LANE), 1)
    low = lane < D_STATE
    tri = tri_ref[...]
    dsk = dsk_ref[...]

    def chunk(xbc, dtt, z):
        xp_ref[SUBLANE:SUBLANE + q, :] = xbc
        conv = cb_ref[...]
        for k in range(CONV_W):
            lo = SUBLANE - (CONV_W - 1) + k
            conv = conv + xp_ref[lo:lo + q, :] * cw_ref[k:k + 1, :]
        xp_ref[0:SUBLANE, :] = xbc[q - SUBLANE:q, :]
        xa = _silu(conv)
        xs = xa[:, :SSD_WIDTH]
        bm = xa[:, SSD_WIDTH:SSD_WIDTH + LANE]
        cm = xa[:, SSD_WIDTH + LANE:]

        dtr = _softplus(dtt + bias_c_ref[...])
        if valid_from:
            dtr = jnp.where(lax.broadcasted_iota(jnp.int32, (16, q), 1) >= valid_from, dtr, 0.0)
        dar = dtr * -jnp.exp(alog_c_ref[...])
        dtc = dtr.T
        acs_r = lax.dot_general(dar, tri, _NT, precision=_HI, preferred_element_type=f32)
        acs_c = jnp.dot(tri, dar.T, precision=_HI, preferred_element_type=f32)
        w_c = jnp.exp(acs_c[q - 1:q, :] - acs_c) * dtc
        e_c = jnp.exp(acs_c)
        cd_r = jnp.exp(acs_r[:, q - 1:q])

        bb = bm.astype(bf16)
        ys = []
        for g in range(SSD_GROUPS):
            gmask = (lane >= g * D_STATE) & (lane < (g + 1) * D_STATE)
            cg = jnp.where(gmask, cm, 0.0).astype(bf16)
            cb_g = _dot_nt(cg, bb)
            for pi in range(2):
                i = 2 * g + pi
                xpair = xs[:, i * LANE:(i + 1) * LANE]
                ypair = jnp.zeros((q, LANE), f32)
                for hh in range(2):
                    h = 2 * i + hh
                    seg = acs_c[:, h:h + 1] - acs_r[h:h + 1, :]
                    lmat = jnp.exp(jnp.where(causal, seg, -jnp.inf))
                    sc = (cb_g * lmat * dtr[h:h + 1, :]).astype(bf16)
                    xh = jnp.where(low if hh == 0 else jnp.logical_not(low), xpair, 0.0).astype(bf16)
                    ypair = ypair + _dot(sc, xh)
                h0, h1 = 2 * i, 2 * i + 1
                wp = jnp.where(low, w_c[:, h0:h0 + 1], w_c[:, h1:h1 + 1])
                ep = jnp.where(low, e_c[:, h0:h0 + 1], e_c[:, h1:h1 + 1])
                hst = hs_ref[i]
                yoff = _dot_nt(cg, hst.astype(bf16)) * ep
                st = _dot_tn((xpair * wp).astype(bf16), bb)
                cdb = jnp.concatenate([jnp.broadcast_to(cd_r[h0:h0 + 1, :], (SSD_HEAD_DIM, LANE)),
                                       jnp.broadcast_to(cd_r[h1:h1 + 1, :], (SSD_HEAD_DIM, LANE))], axis=0)
                hs_ref[i] = hst * cdb + st
                ys.append(ypair + yoff + dsk[:, i * LANE:(i + 1) * LANE] * xpair)
        y = jnp.concatenate(ys, axis=1) * _silu(z)
        gw = SSD_WIDTH // SSD_GROUPS
        outs = []
        for g in range(SSD_GROUPS):
            yg = y[:, g * gw:(g + 1) * gw]
            outs.append(yg * _rms(yg, gw))
        return (jnp.concatenate(outs, axis=1) * ng_ref[...]).astype(y_ref.dtype)

    for ci in range(cps):
        rs = slice(ci * q, (ci + 1) * q)
        y_ref[rs, :] = chunk(xbc_ref[rs, :], dtt_ref[:, rs], z_ref[rs, :])

    @pl.when(c == pl.num_programs(1) - 1)
    def _():
        hout_ref[0] = hs_ref[...]
        for i in range(2 * SSD_GROUPS):
            g = i // 2
            for hh in range(2):
                hfin_ref[0, 2 * i + hh] = hs_ref[i, hh * SSD_HEAD_DIM:(hh + 1) * SSD_HEAD_DIM, g * D_STATE:(g + 1) * D_STATE]


def _ssd(xbc, z, dtt, hist_arr, hist_block, h0, consts, n_batch, n_chunks, row_block0, valid_from, cps):
    assert n_chunks % cps == 0
    q = SSD_CHUNK
    n_steps = n_chunks // cps
    rb = lambda n: pl.BlockSpec((cps * q, n), lambda b, c: (row_block0 + b * n_steps + c, 0))
    cw, cb, bias_c, alog_c, dsk, ng, tri = consts
    return pl.pallas_call(
        functools.partial(_ssd_body, valid_from=valid_from, cps=cps),
        grid=(n_batch, n_steps),
        in_specs=[rb(CONV_DIM), rb(SSD_WIDTH),
                  pl.BlockSpec((16, cps * q), lambda b, c: (0, row_block0 + b * n_steps + c)),
                  pl.BlockSpec((SUBLANE, CONV_DIM), lambda b, c: (hist_block, 0)),
                  _const_spec((1, 4, LANE, LANE)),
                  _const_spec((CONV_W, CONV_DIM)), _const_spec((1, CONV_DIM)),
                  _const_spec((16, 1)), _const_spec((16, 1)),
                  _const_spec((1, SSD_WIDTH)), _const_spec((1, SSD_WIDTH)), _const_spec((q, q))],
        out_specs=[pl.BlockSpec((cps * q, SSD_WIDTH), lambda b, c: (b * n_steps + c, 0)),
                   pl.BlockSpec((1, 4, LANE, LANE), lambda b, c: (b, 0, 0, 0)),
                   pl.BlockSpec((1, SSD_HEADS, SSD_HEAD_DIM, D_STATE), lambda b, c: (b, 0, 0, 0))],
        out_shape=[jax.ShapeDtypeStruct((n_batch * n_chunks * q, SSD_WIDTH), bf16),
                   jax.ShapeDtypeStruct((n_batch, 4, LANE, LANE), f32),
                   jax.ShapeDtypeStruct((n_batch, SSD_HEADS, SSD_HEAD_DIM, D_STATE), f32)],
        scratch_shapes=[pltpu.VMEM((SUBLANE + q, CONV_DIM), f32), pltpu.VMEM((4, LANE, LANE), f32)],
        compiler_params=_params(("parallel", "arbitrary"), 32),
        name="ssd",
    )(xbc, z, dtt, hist_arr, h0, cw, cb, bias_c, alog_c, dsk, ng, tri)


def _flash_body(q_ref, k_ref, vt_ref, km_ref, vtm_ref, o_ref, *, tq, tk, nh):
    qi = pl.program_id(2)
    qs = [q_ref[:, h * HT:(h + 1) * HT] for h in range(nh)]
    half = tk // 2

    def heads(kall, vtall, carries, mask=None, qlo=0):
        sts = [_dot_nt(kall[:, h * HT:(h + 1) * HT], qs[h][qlo:, :]) for h in range(nh)]
        out = []
        for h in range(nh):
            m0, l0, acc0 = carries[h]
            m, l, acct = m0[:, qlo:], l0[:, qlo:], acc0[:, qlo:]
            st = sts[h] if mask is None else jnp.where(mask, sts[h], -jnp.inf)
            m2 = jnp.maximum(m, jnp.max(st, axis=0, keepdims=True))
            pt = jnp.exp(st - m2)
            a = jnp.exp(m - m2)
            vth = vtall[h * V_DIM:(h + 1) * V_DIM, :]
            new = (m2, a * l + jnp.sum(pt, axis=0, keepdims=True), a * acct + _dot(vth, pt.astype(bf16)))
            if qlo:
                new = tuple(jnp.concatenate([old[:, :qlo], x], axis=1) for old, x in zip(carries[h], new))
            out.append(new)
        return tuple(out)

    init = (jnp.full((1, tq), -jnp.inf, f32), jnp.zeros((1, tq), f32), jnp.zeros((V_DIM, tq), f32))

    def body(j, carries):
        off = pl.multiple_of(j * tk, tk)
        return heads(k_ref[pl.ds(off, tk), :], vt_ref[:, pl.ds(off, tk)], carries)

    carries = lax.fori_loop(0, qi, body, (init,) * nh)
    off = pl.multiple_of(qi * tk, tk)
    k_a = jnp.concatenate([k_ref[pl.ds(off, half), :], km_ref[...]], axis=0)
    vt_a = jnp.concatenate([vt_ref[:, pl.ds(off, half)], vtm_ref[...]], axis=1)
    krow = lax.broadcasted_iota(jnp.int32, (half + N_META, tq), 0)
    qcol = lax.broadcasted_iota(jnp.int32, (half + N_META, tq), 1)
    carries = heads(k_a, vt_a, carries, (krow >= half) | (krow <= qcol))
    off_b = pl.multiple_of(qi * tk + half, half)
    krow = lax.broadcasted_iota(jnp.int32, (half, tq - half), 0)
    qcol = lax.broadcasted_iota(jnp.int32, (half, tq - half), 1)
    res = heads(k_ref[pl.ds(off_b, half), :], vt_ref[:, pl.ds(off_b, half)], carries, krow <= qcol, qlo=half)
    for pr in range(nh // 2):
        (_, l0, acc0), (_, l1, acc1) = res[2 * pr], res[2 * pr + 1]
        pair = jnp.concatenate([acc0 / l0, acc1 / l1], axis=0)
        o_ref[:, pr * LANE:(pr + 1) * LANE] = pair.T.astype(o_ref.dtype)


def _flash(q, k, vt, k_small, vt_meta, meta_block, n_batch, seq, tq, tk, nh):
    assert tk == tq and seq % tk == 0 and nh % 2 == 0 and MLA_HEADS % nh == 0
    nq = seq // tq
    return pl.pallas_call(
        functools.partial(_flash_body, tq=tq, tk=tk, nh=nh),
        grid=(n_batch, MLA_HEADS // nh, nq),
        in_specs=[pl.BlockSpec((tq, nh * HT), lambda b, p, i: (b * nq + i, p)),
                  pl.BlockSpec((seq, nh * HT), lambda b, p, i: (b, p)),
                  pl.BlockSpec((nh * V_DIM, seq), lambda b, p, i: (p, b)),
                  pl.BlockSpec((N_META, nh * HT), lambda b, p, i: (meta_block, p)),
                  pl.BlockSpec((nh * V_DIM, N_META), lambda b, p, i: (p, 0))],
        out_specs=pl.BlockSpec((tq, nh * V_DIM), lambda b, p, i: (b * nq + i, p)),
        out_shape=jax.ShapeDtypeStruct((n_batch * seq, MLA_WIDTH), f32),
        compiler_params=_params(("parallel", "parallel", "arbitrary"), 40),
        name="flash",
    )(q, k, vt, k_small, vt_meta)


def _finish_body(x_ref, ssd_ref, att_ref, gm_ref, wo_ref, gf_ref, wg_ref, wu_ref, wd_ref, o_ref):
    att = att_ref[...]
    mla = (att * _rms(att, MLA_WIDTH) * gm_ref[...]).astype(bf16)
    h = x_ref[...] + (_dot(ssd_ref[...], wo_ref[0:SSD_WIDTH, :]) + _dot(mla, wo_ref[SSD_WIDTH:, :]))
    n = (h * _rms(h, D_MODEL) * gf_ref[...]).astype(bf16)
    ff = jnp.zeros_like(h)
    for c in range(N_FF_CHUNKS):
        cs = slice(c * FF_CHUNK, (c + 1) * FF_CHUNK)
        a = (_silu(_dot(n, wg_ref[:, cs])) * _dot(n, wu_ref[:, cs])).astype(bf16)
        ff = ff + _dot(a, wd_ref[cs, :])
    o_ref[...] = h + ff


def _finish(x, ssd, att, gm, wo, gf, wg, wu, wd, tm):
    m = x.shape[0]
    row = lambda n: pl.BlockSpec((tm, n), lambda i: (i, 0))
    return pl.pallas_call(
        _finish_body,
        grid=(m // tm,),
        in_specs=[row(D_MODEL), row(SSD_WIDTH), row(MLA_WIDTH), _const_spec((1, MLA_WIDTH)),
                  _const_spec((D_MODEL, D_MODEL)), _const_spec((1, D_MODEL)),
                  _const_spec((D_MODEL, D_FF)), _const_spec((D_MODEL, D_FF)), _const_spec((D_FF, D_MODEL))],
        out_specs=row(D_MODEL),
        out_shape=jax.ShapeDtypeStruct((m, D_MODEL), f32),
        compiler_params=_params(("parallel",), 56),
        name="finish",
    )(x, ssd, att, gm, wo, gf, wg, wu, wd)


def _sssd_prep_body(xbc_ref, sc_ref, cw_ref, cb_ref, xs_ref, xat_ref):
    conv = cb_ref[...]
    for k in range(CONV_W - 1):
        conv = conv + sc_ref[k] * cw_ref[k:k + 1, :]
    conv = conv + xbc_ref[...] * cw_ref[CONV_W - 1:CONV_W, :]
    xa = _silu(conv)
    xs_ref[...] = xa[:, :SSD_WIDTH]
    xat_ref[...] = xa.T


def _sssd_prep(xbc_small, sc, cw, cb, nseq):
    return pl.pallas_call(
        _sssd_prep_body,
        grid=(1,),
        in_specs=[pl.BlockSpec((nseq, CONV_DIM), lambda i: (0, 0)), _const_spec((CONV_W - 1, nseq, CONV_DIM)),
                  _const_spec((CONV_W, CONV_DIM)), _const_spec((1, CONV_DIM))],
        out_specs=[pl.BlockSpec((nseq, SSD_WIDTH), lambda i: (0, 0)), pl.BlockSpec((CONV_DIM, nseq), lambda i: (0, 0))],
        out_shape=[jax.ShapeDtypeStruct((nseq, SSD_WIDTH), f32), jax.ShapeDtypeStruct((CONV_DIM, nseq), f32)],
        compiler_params=_params(("arbitrary",), 32),
        name="sssd_prep",
    )(xbc_small, sc, cw, cb)


def _sssd_state_body(h0_ref, xst_ref, bt_ref, ct_ref, dtt_ref, bias_c_ref, alog_c_ref, hn_ref, yt_ref):
    h = pl.program_id(0)
    dt = _softplus(dtt_ref[pl.ds(h, 1), :] + bias_c_ref[pl.ds(h, 1), :])
    dec = jnp.exp(dt * -jnp.exp(alog_c_ref[pl.ds(h, 1), :]))
    bt, ct = bt_ref[...], ct_ref[...]

    def body(p, carry):
        xdt = xst_ref[pl.ds(p, 1), :] * dt
        hn = h0_ref[0, p] * dec + xdt * bt
        hn_ref[0, p] = hn
        yt_ref[pl.ds(p, 1), :] = jnp.sum(ct * hn, axis=0, keepdims=True)
        return carry

    lax.fori_loop(0, SSD_HEAD_DIM, body, 0, unroll=4)


def _sssd_state(h0t, xat, dtt_small, bias_c, alog_c, nseq):
    hpg = SSD_HEADS // SSD_GROUPS
    rows = lambda f: pl.BlockSpec((SSD_HEAD_DIM, nseq), f)
    hblk = pl.BlockSpec((1, SSD_HEAD_DIM, D_STATE, nseq), lambda h: (h, 0, 0, 0))
    return pl.pallas_call(
        _sssd_state_body,
        grid=(SSD_HEADS,),
        in_specs=[hblk, rows(lambda h: (h, 0)), rows(lambda h: (SSD_HEADS + h // hpg, 0)),
                  rows(lambda h: (SSD_HEADS + SSD_GROUPS + h // hpg, 0)),
                  pl.BlockSpec((16, nseq), lambda h: (0, 0)), _const_spec((16, 1)), _const_spec((16, 1))],
        out_specs=[hblk, rows(lambda h: (h, 0))],
        out_shape=[jax.ShapeDtypeStruct((SSD_HEADS, SSD_HEAD_DIM, D_STATE, nseq), f32),
                   jax.ShapeDtypeStruct((SSD_WIDTH, nseq), f32)],
        compiler_params=_params(("parallel",), 32),
        name="sssd_state",
    )(h0t, xat, xat, xat, dtt_small, bias_c, alog_c)


def _sssd_gate_body(yt_ref, xs_ref, z_ref, dsk_ref, ng_ref, o_ref):
    y = (yt_ref[...].T + dsk_ref[...] * xs_ref[...]) * _silu(z_ref[...])
    gw = SSD_WIDTH // SSD_GROUPS
    outs = []
    for g in range(SSD_GROUPS):
        yg = y[:, g * gw:(g + 1) * gw]
        outs.append(yg * _rms(yg, gw))
    o_ref[...] = (jnp.concatenate(outs, axis=1) * ng_ref[...]).astype(o_ref.dtype)


def _sssd_gate(yt, xs, z_small, dsk, ng, nseq):
    blk = pl.BlockSpec((nseq, SSD_WIDTH), lambda i: (0, 0))
    return pl.pallas_call(
        _sssd_gate_body,
        grid=(1,),
        in_specs=[pl.BlockSpec((SSD_WIDTH, nseq), lambda i: (0, 0)), blk, blk,
                  _const_spec((1, SSD_WIDTH)), _const_spec((1, SSD_WIDTH))],
        out_specs=blk,
        out_shape=jax.ShapeDtypeStruct((nseq, SSD_WIDTH), bf16),
        compiler_params=_params(("arbitrary",), 32),
        name="sssd_gate",
    )(yt, xs, z_small, dsk, ng)


def _absorb_body(q_ref, gk_ref, wabs_ref, lat_ref, kr_ref, wuk_ref, qg_ref, a_ref, snew_ref):
    gk = gk_ref[...]
    latb = lat_ref[...].astype(bf16)
    latf = latb.astype(f32)
    kr = kr_ref[...]
    krf = kr.astype(bf16).astype(f32)
    kr2 = jnp.sum(kr * kr, axis=-1, keepdims=True)
    s_new = []
    for h in range(MLA_HEADS):
        sl = slice(h * HT, (h + 1) * HT)
        qg = q_ref[:, sl] * gk
        qg_ref[:, sl] = qg
        a = jnp.dot(qg, wabs_ref[sl, :], precision=_HI, preferred_element_type=f32)
        a_ref[:, h * KV_LORA:(h + 1) * KV_LORA] = a
        kn = _dot(latb, wuk_ref[:, sl])
        n2 = jnp.sum(kn * kn, axis=-1, keepdims=True)
        s12 = (jnp.sum(a.astype(bf16).astype(f32) * latf, axis=-1, keepdims=True)
               + jnp.sum(qg.astype(bf16).astype(f32) * krf, axis=-1, keepdims=True))
        s_new.append(lax.rsqrt((n2 + kr2) * (1.0 / QK_DIM) + EPS) * s12)
    a_ref[:, MLA_HEADS * KV_LORA:] = jnp.zeros((q_ref.shape[0], (16 - MLA_HEADS) * KV_LORA), f32)
    snew_ref[...] = jnp.concatenate(s_new, axis=1)


def _absorb(q_small, gk, wabs, lat_small, kr_small, wuk, nseq):
    blk = lambda n: pl.BlockSpec((nseq, n), lambda i: (0, 0))
    return pl.pallas_call(
        _absorb_body,
        grid=(1,),
        in_specs=[blk(MLA_HEADS * HT), _const_spec((1, HT)), _const_spec((MLA_HEADS * HT, KV_LORA)),
                  blk(KV_LORA), blk(HT), _const_spec((KV_LORA, MLA_HEADS * HT))],
        out_specs=[blk(MLA_HEADS * HT), blk(16 * KV_LORA), blk(MLA_HEADS)],
        out_shape=[jax.ShapeDtypeStruct((nseq, MLA_HEADS * HT), f32), jax.ShapeDtypeStruct((nseq, 16 * KV_LORA), f32),
                   jax.ShapeDtypeStruct((nseq, MLA_HEADS), f32)],
        compiler_params=_params(("arbitrary",), 32),
        name="absorb",
    )(q_small, gk, wabs, lat_small, kr_small, wuk)


def _decode_body(pt_ref, a_ref, c_ref, latn_ref, snew_ref, wukt_ref, clat_ref, ckrt_ref, o_ref,
                 lat_buf, kr_buf, waug, latb, s_buf, sems, *, n_pages, page, ppt):
    b = pl.program_id(0)
    nb = pl.num_programs(0)
    slot = b % 2
    n_tiles = n_pages // ppt
    tile = ppt * page
    wrows = MLA_HEADS * QK_NOPE

    def start_page(seq, p, sl):
        pid = pt_ref[seq * n_pages + p]
        pltpu.make_async_copy(clat_ref.at[0, pid], lat_buf.at[sl, p], sems.at[0, sl]).start()
        pltpu.make_async_copy(ckrt_ref.at[0, pid], kr_buf.at[sl, p], sems.at[1, sl]).start()

    @pl.when(b == 0)
    def _():
        waug[0:wrows, :] = wukt_ref[...]

        def first(p, carry):
            start_page(0, p, 0)
            return carry

        lax.fori_loop(0, n_pages, first, 0)

    pltpu.make_async_copy(clat_ref.at[0, pl.ds(0, n_pages)], lat_buf.at[slot], sems.at[0, slot]).wait()
    pltpu.make_async_copy(ckrt_ref.at[0, pl.ds(0, n_pages)], kr_buf.at[slot], sems.at[1, slot]).wait()

    waug[wrows:wrows + 16, :] = a_ref[0, 0].astype(bf16)
    cb = c_ref[0].astype(bf16)

    def score_dots(lb, krt):
        kt = _dot_nt(waug[...], lb)
        return kt, _dot(cb, krt.astype(bf16)), krt

    def score_finish(kt, s2, krt):
        n = kt.shape[1]
        k4 = kt[0:wrows, :].reshape(MLA_HEADS, QK_NOPE // SUBLANE, SUBLANE, n)
        n2 = _fold_heads(jnp.sum(k4 * k4, axis=1))
        kr2 = jnp.sum(krt * krt, axis=0, keepdims=True)
        return lax.rsqrt((n2 + kr2) * (1.0 / QK_DIM) + EPS) * (kt[wrows:wrows + MLA_HEADS, :] + s2)

    def body(j, carry):
        @pl.when(b + 1 < nb)
        def _():
            for pp in range(ppt):
                start_page(b + 1, j * ppt + pp, 1 - slot)

        lb = lat_buf[slot, pl.ds(j * ppt, ppt)].reshape(tile, KV_LORA).astype(bf16)
        latb[j] = lb
        kr_pages = kr_buf[slot, pl.ds(j * ppt, ppt)]
        s_buf[j] = score_finish(*score_dots(lb, jnp.concatenate([kr_pages[i] for i in range(ppt)], axis=-1)))
        return carry

    lax.fori_loop(0, n_tiles, body, 0)
    s_new = snew_ref[0]
    s_all = jnp.concatenate([s_buf[j] for j in range(n_tiles)], axis=-1)
    m = jnp.maximum(jnp.max(s_all, axis=-1, keepdims=True), s_new)
    p_all = jnp.exp(s_all - m)
    p_new = jnp.exp(s_new - m)
    l = jnp.sum(p_all, axis=-1, keepdims=True) + p_new
    acc = _dot(p_all.astype(bf16), latb[...].reshape(n_tiles * tile, KV_LORA)) + p_new * latn_ref[0]
    o_ref[0] = acc / l


def _decode(page_table_flat, a, c, lat_new, s_new, wukt, cache_lat, cache_kr_t, nseq, n_pages, page, ppt):
    assert n_pages % ppt == 0
    wrows = MLA_HEADS * QK_NOPE
    grid_spec = pltpu.PrefetchScalarGridSpec(
        num_scalar_prefetch=1,
        grid=(nseq,),
        in_specs=[pl.BlockSpec((1, 1, 16, KV_LORA), lambda b, pt: (b, 0, 0, 0)),
                  pl.BlockSpec((1, MLA_HEADS, QK_ROPE), lambda b, pt: (b, 0, 0)),
                  pl.BlockSpec((1, 1, KV_LORA), lambda b, pt: (b, 0, 0)),
                  pl.BlockSpec((1, MLA_HEADS, 1), lambda b, pt: (b, 0, 0)),
                  pl.BlockSpec((wrows, KV_LORA), lambda b, pt: (0, 0), pipeline_mode=pl.Buffered(1)),
                  pl.BlockSpec(memory_space=pl.ANY), pl.BlockSpec(memory_space=pl.ANY)],
        out_specs=pl.BlockSpec((1, MLA_HEADS, KV_LORA), lambda b, pt: (b, 0, 0)),
        scratch_shapes=[pltpu.VMEM((2, n_pages, page, KV_LORA), f32), pltpu.VMEM((2, n_pages, QK_ROPE, page), f32),
                        pltpu.VMEM((wrows + 16, KV_LORA), bf16),
                        pltpu.VMEM((n_pages // ppt, ppt * page, KV_LORA), bf16),
                        pltpu.VMEM((n_pages // ppt, MLA_HEADS, ppt * page), f32), pltpu.SemaphoreType.DMA((2, 2))],
    )
    return pl.pallas_call(
        functools.partial(_decode_body, n_pages=n_pages, page=page, ppt=ppt),
        grid_spec=grid_spec,
        out_shape=jax.ShapeDtypeStruct((nseq, MLA_HEADS, KV_LORA), f32),
        compiler_params=_params(("arbitrary",), 40),
        name="decode",
    )(page_table_flat, a, c, lat_new, s_new, wukt, cache_lat, cache_kr_t)


def _uv_body(o_ref, wuv_ref, att_ref):
    for h in range(MLA_HEADS):
        oh = o_ref[:, h * KV_LORA:(h + 1) * KV_LORA].astype(bf16)
        att_ref[:, h * V_DIM:(h + 1) * V_DIM] = _dot(oh, wuv_ref[:, h * V_DIM:(h + 1) * V_DIM])


def _uv(o_flat, wuv):
    nseq = o_flat.shape[0]
    return pl.pallas_call(
        _uv_body,
        grid=(1,),
        in_specs=[pl.BlockSpec((nseq, MLA_HEADS * KV_LORA), lambda i: (0, 0)), _const_spec((KV_LORA, MLA_WIDTH))],
        out_specs=pl.BlockSpec((nseq, MLA_WIDTH), lambda i: (0, 0)),
        out_shape=jax.ShapeDtypeStruct((nseq, MLA_WIDTH), f32),
        compiler_params=_params(("arbitrary",), 32),
        name="uv",
    )(o_flat, wuv)


def _rope_tables(pos):
    inv = ROPE_THETA ** (-jnp.arange(ROPE_HALF, dtype=f32) * (2.0 / QK_ROPE))
    ang = pos.astype(f32)[:, None] * inv[None, :]
    cos, sin = jnp.cos(ang), jnp.sin(ang)
    n = pos.shape[0]
    one = jnp.ones((n, QK_NOPE), f32)
    z16 = jnp.zeros((n, ROPE_HALF), f32)
    z32 = jnp.zeros((n, HT - QK_DIM), f32)
    z64 = jnp.zeros((n, QK_NOPE), f32)
    tc = jnp.concatenate([one, cos, cos, z32], axis=1)
    ts1 = jnp.concatenate([z64, -sin, z16, z32], axis=1)
    ts2 = jnp.concatenate([z64, z16, sin, z32], axis=1)
    return tc, ts1, ts2


def _pad_lanes(x, n):
    return jnp.pad(x, ((0, 0), (0, n - x.shape[1])))


def kernel(x_prompt, x_sample, cache_kv_latent, cache_k_rope, state_conv, state_ssm, page_table, meta_tokens, attn_norm_g, w_in, conv_w, conv_b, dt_bias, a_log, d_skip, ssd_norm_g, q_norm_g, w_uq, kv_norm_g, w_ukv, q_head_g_nope, q_head_g_rope, k_head_g_nope, k_head_g_rope, mla_out_g, w_out, ffn_norm_g, w_gate, w_up, w_down):
    l = 0
    nb, seq, _ = x_prompt.shape
    nseq = x_sample.shape[0]
    n_pages, page = page_table.shape[1], cache_kv_latent.shape[2]
    past = n_pages * page
    n_small = 2 * LANE
    meta_lo = n_small - N_META

    s0, s1, s2, s3, s4 = (SSD_WIDTH, SSD_WIDTH + CONV_DIM, SSD_WIDTH + CONV_DIM + SSD_HEADS,
                          SSD_WIDTH + CONV_DIM + SSD_HEADS + Q_LORA, SSD_WIDTH + CONV_DIM + SSD_HEADS + Q_LORA + KV_LORA)
    wi = w_in[l]
    w_last = jnp.concatenate([_pad_lanes(wi[:, s1:s2], ROPE_LO), _pad_lanes(wi[:, s4:], HT - ROPE_LO)], axis=1)
    w_proj = jnp.concatenate([wi[:, :s1], wi[:, s2:s4], w_last], axis=1).astype(bf16)
    g_attn = attn_norm_g[l][None, :]
    wq3 = w_uq[l].reshape(Q_LORA, MLA_HEADS, QK_DIM)
    zq = jnp.zeros((Q_LORA, MLA_HEADS, QK_NOPE), f32)
    wq_partner = jnp.concatenate([zq, wq3[:, :, QK_NOPE + ROPE_HALF:], wq3[:, :, QK_NOPE:QK_NOPE + ROPE_HALF]], axis=2)
    pad_head = lambda w: jnp.pad(w, ((0, 0), (0, 0), (0, HT - QK_DIM))).reshape(Q_LORA, -1)
    wuq = jnp.concatenate([pad_head(wq3), pad_head(wq_partner)], axis=1).astype(bf16)
    wkv = w_ukv[l].reshape(KV_LORA, MLA_HEADS, QK_NOPE + V_DIM)
    wuk_f32 = jnp.pad(wkv[:, :, :QK_NOPE], ((0, 0), (0, 0), (0, HT - QK_NOPE))).reshape(KV_LORA, -1)
    wuk = wuk_f32.astype(bf16)
    wuv = wkv[:, :, QK_NOPE:].reshape(KV_LORA, MLA_WIDTH).astype(bf16)
    wukt = wkv[:, :, :QK_NOPE].reshape(KV_LORA, -1).T.astype(bf16)
    gq = _pad_lanes(jnp.concatenate([q_head_g_nope[l], q_head_g_rope[l], q_head_g_rope[l]])[None, :], HT) * ATTN_SCALE
    gk = _pad_lanes(jnp.concatenate([k_head_g_nope[l], k_head_g_rope[l], k_head_g_rope[l]])[None, :], HT)
    gqn, gkv = q_norm_g[l][None, :], kv_norm_g[l][None, :]
    cw, cb = conv_w[l], conv_b[l][None, :]
    bias_c = jnp.pad(dt_bias[l][:, None], ((0, 16 - SSD_HEADS), (0, 0)))
    alog_c = jnp.pad(a_log[l][:, None], ((0, 16 - SSD_HEADS), (0, 0)))
    dsk = jnp.repeat(d_skip[l], SSD_HEAD_DIM)[None, :]
    ng = ssd_norm_g[l][None, :]
    tri = jnp.tril(jnp.ones((SSD_CHUNK, SSD_CHUNK), f32))
    ssd_consts = (cw, cb, bias_c, alog_c, dsk, ng, tri)
    gm, gf = mla_out_g[l][None, :], ffn_norm_g[l][None, :]
    wo = w_out[l].astype(bf16)
    wg, wu, wd = w_gate[l].astype(bf16), w_up[l].astype(bf16), w_down[l].astype(bf16)

    xp = x_prompt.reshape(nb * seq, D_MODEL)
    xs_rows = x_sample[:, 0, :]
    x_small = jnp.concatenate([xs_rows, jnp.zeros((meta_lo - nseq, D_MODEL), f32), meta_tokens.astype(f32)], axis=0)

    z_p, xbc_p, cq_p, ckv_p, kr_p, dtt_p = _project(xp, g_attn, w_proj, ROW_TILE)
    z_s, xbc_s, cq_s, ckv_s, kr_s, dtt_s = _project(x_small, g_attn, w_proj, n_small)

    tabs_p = _rope_tables(N_META + jnp.arange(seq))
    pos_small = jnp.concatenate([jnp.full((nseq,), past), jnp.zeros((meta_lo - nseq,), jnp.int32), jnp.arange(N_META)])
    tabs_s = _rope_tables(pos_small)
    tm = ROW_TILE
    lat_p, kro_p, q_p, k_p, vt_p = _mla_prep(cq_p, ckv_p, kr_p, tabs_p, seq // tm, gqn, gkv, wuq, wuk, wuv.T, gq, gk, tm, bf16)
    lat_s, kro_s, q_s, k_s, vt_s = _mla_prep(cq_s, ckv_s, kr_s, tabs_s, 1, gqn, gkv, wuq, wuk, wuv.T, gq, gk, n_small, f32)

    zero_h = jnp.zeros((1, 4, LANE, LANE), f32)
    zero_hist = jnp.zeros((SUBLANE, CONV_DIM), f32)
    _, h_meta, _ = _ssd(xbc_s, z_s, dtt_s, zero_hist, 0, zero_h, ssd_consts, 1, 1, 1, SSD_CHUNK - N_META, 1)
    ssd_p, _, h_fin = _ssd(xbc_p, z_p, dtt_p, xbc_s, n_small // SUBLANE - 1, h_meta, ssd_consts, nb, seq // SSD_CHUNK, 0, 0,
                           SSD_CHUNKS_PER_STEP)

    att_p = _flash(q_p, k_p, vt_p, k_s, vt_s[:, meta_lo:], n_small // N_META - 1, nb, seq, FLASH_TILE, FLASH_TILE,
                   FLASH_HEADS_PER_STEP)
    y_prompt = _finish(xp, ssd_p, att_p, gm, wo, gf, wg, wu, wd, ROW_TILE).reshape(nb, seq, D_MODEL)

    sc = jnp.transpose(state_conv[l], (1, 0, 2))
    xs_s, xat_s = _sssd_prep(xbc_s, sc, cw, cb, nseq)
    h_new_t, yt_s = _sssd_state(jnp.transpose(state_ssm[l], (1, 2, 3, 0)), xat_s, dtt_s, bias_c, alog_c, nseq)
    h_new = jnp.transpose(h_new_t, (3, 0, 1, 2))
    ssd_s = _sssd_gate(yt_s, xs_s, z_s, dsk, ng, nseq)
    qg, a_abs, s_new = _absorb(q_s, gk, wuk_f32.T, lat_s, kro_s, wuk, nseq)
    hr = list(HEAD_ROW)
    a_abs = a_abs.reshape(nseq, 16, KV_LORA)[:, hr + list(range(MLA_HEADS, 16))].reshape(nseq, 1, 16, KV_LORA)
    c_abs = qg.reshape(nseq, MLA_HEADS, HT)[:, hr, ROPE_LO:QK_DIM]
    lat_new = lat_s[:nseq][:, None, :]
    kr_new = kro_s[:nseq, ROPE_LO:QK_DIM][:, None, :]
    o_lat = _decode(page_table.reshape(-1), a_abs, c_abs, lat_new, s_new[:, hr, None], wukt,
                    cache_kv_latent, jnp.swapaxes(cache_k_rope, 2, 3), nseq, n_pages, page, DECODE_PAGES_PER_TILE)
    att_s = _uv(o_lat[:, hr].reshape(nseq, MLA_HEADS * KV_LORA), wuv)
    y_sample = _finish(xs_rows, ssd_s, att_s, gm, wo, gf, wg, wu, wd, nseq)[:, None, :]

    def with_meta(small, main, width):
        meta = jnp.broadcast_to(small[meta_lo:][None], (nb, N_META, width))
        return jnp.concatenate([meta, main.reshape(nb, seq, width)], axis=1)[None]

    kv_latent_prompt = with_meta(lat_s, lat_p, KV_LORA)
    k_rope_prompt = with_meta(kro_s[:, ROPE_LO:QK_DIM], kro_p[:, ROPE_LO:QK_DIM], QK_ROPE)
    conv_prompt = xbc_p.reshape(nb, seq, CONV_DIM)[:, seq - (CONV_W - 1):][None]
    ssm_prompt = h_fin[None]
    kv_latent_sample = lat_new[None]
    k_rope_sample = kr_new[None]
    conv_sample = jnp.concatenate([state_conv[l][:, 1:], xbc_s[:nseq][:, None, :]], axis=1)[None]
    ssm_sample = h_new[None]
    return (y_prompt, y_sample, kv_latent_prompt, k_rope_prompt, conv_prompt, ssm_prompt.astype(x_prompt.dtype),
            kv_latent_sample, k_rope_sample, conv_sample, ssm_sample.astype(state_ssm.dtype))
```

```python
import functools

import jax
import jax.numpy as jnp
from jax import lax
from jax.experimental import pallas as pl
from jax.experimental.pallas import tpu as pltpu

f32 = jnp.float32
bf16 = jnp.bfloat16

D_MODEL = 1024
N_META = 16
SSD_HEADS = 8
SSD_HEAD_DIM = 64
SSD_WIDTH = SSD_HEADS * SSD_HEAD_DIM
SSD_GROUPS = 2
D_STATE = 64
CONV_W = 4
CONV_DIM = SSD_WIDTH + 2 * SSD_GROUPS * D_STATE
SSD_CHUNK = 128
MLA_HEADS = 8
QK_NOPE = 64
QK_ROPE = 32
QK_DIM = QK_NOPE + QK_ROPE
V_DIM = 64
MLA_WIDTH = MLA_HEADS * V_DIM
Q_LORA = 384
KV_LORA = 256
ROPE_THETA = 10000.0
ATTN_SCALE = QK_DIM ** -0.5
D_FF = 2816
EPS = 1e-6

LANE = 128
SUBLANE = 8
HT = LANE
ROPE_LO = QK_NOPE
ROPE_HALF = QK_ROPE // 2
FF_CHUNK = 256
N_FF_CHUNKS = D_FF // FF_CHUNK
DECODE_PAGES_PER_TILE = 32
ROW_TILE = 512
SSD_CHUNKS_PER_STEP = 8
FLASH_TILE = 512
FLASH_HEADS_PER_STEP = 8
MiB = 1024 * 1024

PC_Z = 0
PC_XBC = PC_Z + SSD_WIDTH
PC_CQ = PC_XBC + CONV_DIM
PC_CKV = PC_CQ + Q_LORA
PC_KR = PC_CKV + KV_LORA
PC_END = PC_KR + LANE

_NT = (((1,), (1,)), ((), ()))
_TN = (((0,), (0,)), ((), ()))
_HI = lax.Precision.HIGHEST


def _dot(a, b):
    return jnp.dot(a, b, preferred_element_type=f32)


def _dot_nt(a, b):
    return lax.dot_general(a, b, _NT, preferred_element_type=f32)


def _dot_tn(a, b):
    return lax.dot_general(a, b, _TN, preferred_element_type=f32)


def _rms(x, n):
    return lax.rsqrt(jnp.sum(x * x, axis=-1, keepdims=True) * (1.0 / n) + EPS)


def _silu(x):
    return x * jax.nn.sigmoid(x)


def _softplus(x):
    return jnp.maximum(x, 0.0) + jnp.log1p(jnp.exp(-jnp.abs(x)))


HEAD_ROW = (3, 2, 1, 0, 7, 6, 5, 4)


def _fold_heads(part):
    row = lax.broadcasted_iota(jnp.int32, part.shape[1:], 0)
    v = [part[h] for h in range(8)]
    v = [jnp.where(row < 4, v[j] + pltpu.roll(v[j], 4, 0), v[j + 4] + pltpu.roll(v[j + 4], 4, 0)) for j in range(4)]
    v = [jnp.where(row % 4 >= 2, v[j] + pltpu.roll(v[j], 2, 0), v[j + 2] + pltpu.roll(v[j + 2], 6, 0)) for j in range(2)]
    return jnp.where(row % 2 == 1, v[0] + pltpu.roll(v[0], 1, 0), v[1] + pltpu.roll(v[1], 7, 0))


def _const_spec(shape):
    nd = len(shape)
    return pl.BlockSpec(shape, lambda *_: (0,) * nd, pipeline_mode=pl.Buffered(1))


def _params(sem, vmem_mib):
    return pltpu.CompilerParams(dimension_semantics=sem, vmem_limit_bytes=vmem_mib * MiB)


def _proj_body(x_ref, g_ref, w_ref, z_ref, xbc_ref, cq_ref, ckv_ref, kr_ref, dtt_ref):
    x = x_ref[...]
    xn = (x * _rms(x, D_MODEL) * g_ref[...]).astype(bf16)
    z_ref[...] = _dot(xn, w_ref[:, PC_Z:PC_XBC])
    xbc_ref[...] = _dot(xn, w_ref[:, PC_XBC:PC_CQ])
    cq_ref[...] = _dot(xn, w_ref[:, PC_CQ:PC_CKV])
    ckv_ref[...] = _dot(xn, w_ref[:, PC_CKV:PC_KR])
    last = _dot(xn, w_ref[:, PC_KR:PC_END])
    lane = lax.broadcasted_iota(jnp.int32, (1, LANE), 1)
    kr_ref[...] = jnp.where(lane >= ROPE_LO, last, 0.0)
    dtt_ref[...] = last.T[0:16, :]


def _project(x, g, w, tm):
    m = x.shape[0]
    row = lambda n: pl.BlockSpec((tm, n), lambda i: (i, 0))
    widths = (SSD_WIDTH, CONV_DIM, Q_LORA, KV_LORA, LANE)
    return pl.pallas_call(
        _proj_body,
        grid=(m // tm,),
        in_specs=[row(D_MODEL), _const_spec((1, D_MODEL)), _const_spec((D_MODEL, PC_END))],
        out_specs=[row(n) for n in widths] + [pl.BlockSpec((16, tm), lambda i: (0, i))],
        out_shape=[jax.ShapeDtypeStruct((m, n), f32) for n in widths] + [jax.ShapeDtypeStruct((16, m), f32)],
        compiler_params=_params(("parallel",), 40),
        name="proj",
    )(x, g, w)


def _mla_prep_body(cq_ref, ckv_ref, kr_ref, tc_ref, ts1_ref, ts2_ref, gqn_ref, gkv_ref, wuq_ref, wuk_ref, wuv_ref,
                   gq_ref, gk_ref, lat_ref, kro_ref, q_ref, k_ref, vt_ref):
    tc, ts1, ts2 = tc_ref[...], ts1_ref[...], ts2_ref[...]
    tsw = ts1 + ts2

    def rope(x):
        return x * tc + pltpu.roll(x, LANE - ROPE_HALF, 1) * ts1 + pltpu.roll(x, ROPE_HALF, 1) * ts2

    ckv = ckv_ref[...]
    lat = ckv * _rms(ckv, KV_LORA) * gkv_ref[...]
    lat_ref[...] = lat
    kr = rope(kr_ref[...])
    kro_ref[...] = kr
    cq = cq_ref[...]
    cqn = (cq * _rms(cq, Q_LORA) * gqn_ref[...]).astype(bf16)
    latb = lat.astype(bf16)
    vt_ref[...] = _dot_nt(wuv_ref[...], latb).astype(vt_ref.dtype)
    gq, gk = gq_ref[...], gk_ref[...]
    q_all = _dot(cqn, wuq_ref[:, 0:MLA_HEADS * HT])
    qp_all = _dot(cqn, wuq_ref[:, MLA_HEADS * HT:])
    k_all = _dot(latb, wuk_ref[...])
    for h in range(MLA_HEADS):
        sl = slice(h * HT, (h + 1) * HT)
        qh = q_all[:, sl] * tc + qp_all[:, sl] * tsw
        q_ref[:, sl] = (qh * _rms(qh, QK_DIM) * gq).astype(q_ref.dtype)
        kh = k_all[:, sl] + kr
        k_ref[:, sl] = (kh * _rms(kh, QK_DIM) * gk).astype(k_ref.dtype)


def _mla_prep(cq, ckv, kr, tabs, n_tab_blocks, gqn, gkv, wuq, wuk, wuvt, gq, gk, tm, q_dtype):
    m = cq.shape[0]
    row = lambda n: pl.BlockSpec((tm, n), lambda i: (i, 0))
    tab = pl.BlockSpec((tm, LANE), lambda i: (i % n_tab_blocks, 0))
    widths = (KV_LORA, LANE, MLA_HEADS * HT, MLA_HEADS * HT)
    dtypes = (f32, f32, q_dtype, bf16)
    return pl.pallas_call(
        _mla_prep_body,
        grid=(m // tm,),
        in_specs=[row(Q_LORA), row(KV_LORA), row(LANE), tab, tab, tab,
                  _const_spec((1, Q_LORA)), _const_spec((1, KV_LORA)),
                  _const_spec((Q_LORA, 2 * MLA_HEADS * HT)), _const_spec((KV_LORA, MLA_HEADS * HT)),
                  _const_spec((MLA_WIDTH, KV_LORA)), _const_spec((1, HT)), _const_spec((1, HT))],
        out_specs=[row(n) for n in widths] + [pl.BlockSpec((MLA_WIDTH, tm), lambda i: (0, i))],
        out_shape=[jax.ShapeDtypeStruct((m, n), d) for n, d in zip(widths, dtypes)]
                  + [jax.ShapeDtypeStruct((MLA_WIDTH, m), bf16)],
        compiler_params=_params(("parallel",), 40),
        name="mla_prep",
    )(cq, ckv, kr, *tabs, gqn, gkv, wuq, wuk, wuvt, gq, gk)


def _ssd_body(xbc_ref, z_ref, dtt_ref, hist_ref, h0_ref, cw_ref, cb_ref, bias_c_ref,
              alog_c_ref, dsk_ref, ng_ref, tri_ref, y_ref, hout_ref, hfin_ref, xp_ref, hs_ref, *, valid_from, cps):
    c = pl.program_id(1)
    q = SSD_CHUNK

    @pl.when(c == 0)
    def _():
        xp_ref[0:SUBLANE, :] = hist_ref[...]
        hs_ref[...] = h0_ref[0]

    rows = lax.broadcasted_iota(jnp.int32, (q, q), 0)
    cols = lax.broadcasted_iota(jnp.int32, (q, q), 1)
    causal = cols <= rows
    lane = lax.broadcasted_iota(jnp.int32, (1, LANE), 1)
    low = lane < D_STATE
    tri = tri_ref[...]
    dsk = dsk_ref[...]

    def chunk(xbc, dtt, z):
        xp_ref[SUBLANE:SUBLANE + q, :] = xbc
        conv = cb_ref[...]
        for k in range(CONV_W):
            lo = SUBLANE - (CONV_W - 1) + k
            conv = conv + xp_ref[lo:lo + q, :] * cw_ref[k:k + 1, :]
        xp_ref[0:SUBLANE, :] = xbc[q - SUBLANE:q, :]
        xa = _silu(conv)
        xs = xa[:, :SSD_WIDTH]
        bm = xa[:, SSD_WIDTH:SSD_WIDTH + LANE]
        cm = xa[:, SSD_WIDTH + LANE:]

        dtr = _softplus(dtt + bias_c_ref[...])
        if valid_from:
            dtr = jnp.where(lax.broadcasted_iota(jnp.int32, (16, q), 1) >= valid_from, dtr, 0.0)
        dar = dtr * -jnp.exp(alog_c_ref[...])
        dtc = dtr.T
        acs_r = lax.dot_general(dar, tri, _NT, precision=_HI, preferred_element_type=f32)
        acs_c = jnp.dot(tri, dar.T, precision=_HI, preferred_element_type=f32)
        w_c = jnp.exp(acs_c[q - 1:q, :] - acs_c) * dtc
        e_c = jnp.exp(acs_c)
        cd_r = jnp.exp(acs_r[:, q - 1:q])

        bb = bm.astype(bf16)
        ys = []
        for g in range(SSD_GROUPS):
            gmask = (lane >= g * D_STATE) & (lane < (g + 1) * D_STATE)
            cg = jnp.where(gmask, cm, 0.0).astype(bf16)
            cb_g = _dot_nt(cg, bb)
            for pi in range(2):
                i = 2 * g + pi
                xpair = xs[:, i * LANE:(i + 1) * LANE]
                ypair = jnp.zeros((q, LANE), f32)
                for hh in range(2):
                    h = 2 * i + hh
                    seg = acs_c[:, h:h + 1] - acs_r[h:h + 1, :]
                    lmat = jnp.exp(jnp.where(causal, seg, -jnp.inf))
                    sc = (cb_g * lmat * dtr[h:h + 1, :]).astype(bf16)
                    xh = jnp.where(low if hh == 0 else jnp.logical_not(low), xpair, 0.0).astype(bf16)
                    ypair = ypair + _dot(sc, xh)
                h0, h1 = 2 * i, 2 * i + 1
                wp = jnp.where(low, w_c[:, h0:h0 + 1], w_c[:, h1:h1 + 1])
                ep = jnp.where(low, e_c[:, h0:h0 + 1], e_c[:, h1:h1 + 1])
                hst = hs_ref[i]
                yoff = _dot_nt(cg, hst.astype(bf16)) * ep
                st = _dot_tn((xpair * wp).astype(bf16), bb)
                cdb = jnp.concatenate([jnp.broadcast_to(cd_r[h0:h0 + 1, :], (SSD_HEAD_DIM, LANE)),
                                       jnp.broadcast_to(cd_r[h1:h1 + 1, :], (SSD_HEAD_DIM, LANE))], axis=0)
                hs_ref[i] = hst * cdb + st
                ys.append(ypair + yoff + dsk[:, i * LANE:(i + 1) * LANE] * xpair)
        y = jnp.concatenate(ys, axis=1) * _silu(z)
        gw = SSD_WIDTH // SSD_GROUPS
        outs = []
        for g in range(SSD_GROUPS):
            yg = y[:, g * gw:(g + 1) * gw]
            outs.append(yg * _rms(yg, gw))
        return (jnp.concatenate(outs, axis=1) * ng_ref[...]).astype(y_ref.dtype)

    for ci in range(cps):
        rs = slice(ci * q, (ci + 1) * q)
        y_ref[rs, :] = chunk(xbc_ref[rs, :], dtt_ref[:, rs], z_ref[rs, :])

    @pl.when(c == pl.num_programs(1) - 1)
    def _():
        hout_ref[0] = hs_ref[...]
        for i in range(2 * SSD_GROUPS):
            g = i // 2
            for hh in range(2):
                hfin_ref[0, 2 * i + hh] = hs_ref[i, hh * SSD_HEAD_DIM:(hh + 1) * SSD_HEAD_DIM, g * D_STATE:(g + 1) * D_STATE]


def _ssd(xbc, z, dtt, hist_arr, hist_block, h0, consts, n_batch, n_chunks, row_block0, valid_from, cps):
    assert n_chunks % cps == 0
    q = SSD_CHUNK
    n_steps = n_chunks // cps
    rb = lambda n: pl.BlockSpec((cps * q, n), lambda b, c: (row_block0 + b * n_steps + c, 0))
    cw, cb, bias_c, alog_c, dsk, ng, tri = consts
    return pl.pallas_call(
        functools.partial(_ssd_body, valid_from=valid_from, cps=cps),
        grid=(n_batch, n_steps),
        in_specs=[rb(CONV_DIM), rb(SSD_WIDTH),
                  pl.BlockSpec((16, cps * q), lambda b, c: (0, row_block0 + b * n_steps + c)),
                  pl.BlockSpec((SUBLANE, CONV_DIM), lambda b, c: (hist_block, 0)),
                  _const_spec((1, 4, LANE, LANE)),
                  _const_spec((CONV_W, CONV_DIM)), _const_spec((1, CONV_DIM)),
                  _const_spec((16, 1)), _const_spec((16, 1)),
                  _const_spec((1, SSD_WIDTH)), _const_spec((1, SSD_WIDTH)), _const_spec((q, q))],
        out_specs=[pl.BlockSpec((cps * q, SSD_WIDTH), lambda b, c: (b * n_steps + c, 0)),
                   pl.BlockSpec((1, 4, LANE, LANE), lambda b, c: (b, 0, 0, 0)),
                   pl.BlockSpec((1, SSD_HEADS, SSD_HEAD_DIM, D_STATE), lambda b, c: (b, 0, 0, 0))],
        out_shape=[jax.ShapeDtypeStruct((n_batch * n_chunks * q, SSD_WIDTH), bf16),
                   jax.ShapeDtypeStruct((n_batch, 4, LANE, LANE), f32),
                   jax.ShapeDtypeStruct((n_batch, SSD_HEADS, SSD_HEAD_DIM, D_STATE), f32)],
        scratch_shapes=[pltpu.VMEM((SUBLANE + q, CONV_DIM), f32), pltpu.VMEM((4, LANE, LANE), f32)],
        compiler_params=_params(("parallel", "arbitrary"), 32),
        name="ssd",
    )(xbc, z, dtt, hist_arr, h0, cw, cb, bias_c, alog_c, dsk, ng, tri)


def _flash_body(q_ref, k_ref, vt_ref, km_ref, vtm_ref, o_ref, *, tq, tk, nh):
    qi = pl.program_id(2)
    qs = [q_ref[:, h * HT:(h + 1) * HT] for h in range(nh)]
    half = tk // 2

    def heads(kall, vtall, carries, mask=None, qlo=0):
        sts = [_dot_nt(kall[:, h * HT:(h + 1) * HT], qs[h][qlo:, :]) for h in range(nh)]
        out = []
        for h in range(nh):
            m0, l0, acc0 = carries[h]
            m, l, acct = m0[:, qlo:], l0[:, qlo:], acc0[:, qlo:]
            st = sts[h] if mask is None else jnp.where(mask, sts[h], -jnp.inf)
            m2 = jnp.maximum(m, jnp.max(st, axis=0, keepdims=True))
            pt = jnp.exp(st - m2)
            a = jnp.exp(m - m2)
            vth = vtall[h * V_DIM:(h + 1) * V_DIM, :]
            new = (m2, a * l + jnp.sum(pt, axis=0, keepdims=True), a * acct + _dot(vth, pt.astype(bf16)))
            if qlo:
                new = tuple(jnp.concatenate([old[:, :qlo], x], axis=1) for old, x in zip(carries[h], new))
            out.append(new)
        return tuple(out)

    init = (jnp.full((1, tq), -jnp.inf, f32), jnp.zeros((1, tq), f32), jnp.zeros((V_DIM, tq), f32))

    def body(j, carries):
        off = pl.multiple_of(j * tk, tk)
        return heads(k_ref[pl.ds(off, tk), :], vt_ref[:, pl.ds(off, tk)], carries)

    carries = lax.fori_loop(0, qi, body, (init,) * nh)
    off = pl.multiple_of(qi * tk, tk)
    k_a = jnp.concatenate([k_ref[pl.ds(off, half), :], km_ref[...]], axis=0)
    vt_a = jnp.concatenate([vt_ref[:, pl.ds(off, half)], vtm_ref[...]], axis=1)
    krow = lax.broadcasted_iota(jnp.int32, (half + N_META, tq), 0)
    qcol = lax.broadcasted_iota(jnp.int32, (half + N_META, tq), 1)
    carries = heads(k_a, vt_a, carries, (krow >= half) | (krow <= qcol))
    off_b = pl.multiple_of(qi * tk + half, half)
    krow = lax.broadcasted_iota(jnp.int32, (half, tq - half), 0)
    qcol = lax.broadcasted_iota(jnp.int32, (half, tq - half), 1)
    res = heads(k_ref[pl.ds(off_b, half), :], vt_ref[:, pl.ds(off_b, half)], carries, krow <= qcol, qlo=half)
    for pr in range(nh // 2):
        (_, l0, acc0), (_, l1, acc1) = res[2 * pr], res[2 * pr + 1]
        pair = jnp.concatenate([acc0 / l0, acc1 / l1], axis=0)
        o_ref[:, pr * LANE:(pr + 1) * LANE] = pair.T.astype(o_ref.dtype)


def _flash(q, k, vt, k_small, vt_meta, meta_block, n_batch, seq, tq, tk, nh):
    assert tk == tq and seq % tk == 0 and nh % 2 == 0 and MLA_HEADS % nh == 0
    nq = seq // tq
    return pl.pallas_call(
        functools.partial(_flash_body, tq=tq, tk=tk, nh=nh),
        grid=(n_batch, MLA_HEADS // nh, nq),
        in_specs=[pl.BlockSpec((tq, nh * HT), lambda b, p, i: (b * nq + i, p)),
                  pl.BlockSpec((seq, nh * HT), lambda b, p, i: (b, p)),
                  pl.BlockSpec((nh * V_DIM, seq), lambda b, p, i: (p, b)),
                  pl.BlockSpec((N_META, nh * HT), lambda b, p, i: (meta_block, p)),
                  pl.BlockSpec((nh * V_DIM, N_META), lambda b, p, i: (p, 0))],
        out_specs=pl.BlockSpec((tq, nh * V_DIM), lambda b, p, i: (b * nq + i, p)),
        out_shape=jax.ShapeDtypeStruct((n_batch * seq, MLA_WIDTH), f32),
        compiler_params=_params(("parallel", "parallel", "arbitrary"), 40),
        name="flash",
    )(q, k, vt, k_small, vt_meta)


def _finish_body(x_ref, ssd_ref, att_ref, gm_ref, wo_ref, gf_ref, wg_ref, wu_ref, wd_ref, o_ref):
    att = att_ref[...]
    mla = (att * _rms(att, MLA_WIDTH) * gm_ref[...]).astype(bf16)
    h = x_ref[...] + (_dot(ssd_ref[...], wo_ref[0:SSD_WIDTH, :]) + _dot(mla, wo_ref[SSD_WIDTH:, :]))
    n = (h * _rms(h, D_MODEL) * gf_ref[...]).astype(bf16)
    ff = jnp.zeros_like(h)
    for c in range(N_FF_CHUNKS):
        cs = slice(c * FF_CHUNK, (c + 1) * FF_CHUNK)
        a = (_silu(_dot(n, wg_ref[:, cs])) * _dot(n, wu_ref[:, cs])).astype(bf16)
        ff = ff + _dot(a, wd_ref[cs, :])
    o_ref[...] = h + ff


def _finish(x, ssd, att, gm, wo, gf, wg, wu, wd, tm):
    m = x.shape[0]
    row = lambda n: pl.BlockSpec((tm, n), lambda i: (i, 0))
    return pl.pallas_call(
        _finish_body,
        grid=(m // tm,),
        in_specs=[row(D_MODEL), row(SSD_WIDTH), row(MLA_WIDTH), _const_spec((1, MLA_WIDTH)),
                  _const_spec((D_MODEL, D_MODEL)), _const_spec((1, D_MODEL)),
                  _const_spec((D_MODEL, D_FF)), _const_spec((D_MODEL, D_FF)), _const_spec((D_FF, D_MODEL))],
        out_specs=row(D_MODEL),
        out_shape=jax.ShapeDtypeStruct((m, D_MODEL), f32),
        compiler_params=_params(("parallel",), 56),
        name="finish",
    )(x, ssd, att, gm, wo, gf, wg, wu, wd)


def _sssd_prep_body(xbc_ref, sc_ref, cw_ref, cb_ref, xs_ref, xat_ref):
    conv = cb_ref[...]
    for k in range(CONV_W - 1):
        conv = conv + sc_ref[k] * cw_ref[k:k + 1, :]
    conv = conv + xbc_ref[...] * cw_ref[CONV_W - 1:CONV_W, :]
    xa = _silu(conv)
    xs_ref[...] = xa[:, :SSD_WIDTH]
    xat_ref[...] = xa.T


def _sssd_prep(xbc_small, sc, cw, cb, nseq):
    return pl.pallas_call(
        _sssd_prep_body,
        grid=(1,),
        in_specs=[pl.BlockSpec((nseq, CONV_DIM), lambda i: (0, 0)), _const_spec((CONV_W - 1, nseq, CONV_DIM)),
                  _const_spec((CONV_W, CONV_DIM)), _const_spec((1, CONV_DIM))],
        out_specs=[pl.BlockSpec((nseq, SSD_WIDTH), lambda i: (0, 0)), pl.BlockSpec((CONV_DIM, nseq), lambda i: (0, 0))],
        out_shape=[jax.ShapeDtypeStruct((nseq, SSD_WIDTH), f32), jax.ShapeDtypeStruct((CONV_DIM, nseq), f32)],
        compiler_params=_params(("arbitrary",), 32),
        name="sssd_prep",
    )(xbc_small, sc, cw, cb)


def _sssd_state_body(h0_ref, xst_ref, bt_ref, ct_ref, dtt_ref, bias_c_ref, alog_c_ref, hn_ref, yt_ref):
    h = pl.program_id(0)
    dt = _softplus(dtt_ref[pl.ds(h, 1), :] + bias_c_ref[pl.ds(h, 1), :])
    dec = jnp.exp(dt * -jnp.exp(alog_c_ref[pl.ds(h, 1), :]))
    bt, ct = bt_ref[...], ct_ref[...]

    def body(p, carry):
        xdt = xst_ref[pl.ds(p, 1), :] * dt
        hn = h0_ref[0, p] * dec + xdt * bt
        hn_ref[0, p] = hn
        yt_ref[pl.ds(p, 1), :] = jnp.sum(ct * hn, axis=0, keepdims=True)
        return carry

    lax.fori_loop(0, SSD_HEAD_DIM, body, 0, unroll=4)


def _sssd_state(h0t, xat, dtt_small, bias_c, alog_c, nseq):
    hpg = SSD_HEADS // SSD_GROUPS
    rows = lambda f: pl.BlockSpec((SSD_HEAD_DIM, nseq), f)
    hblk = pl.BlockSpec((1, SSD_HEAD_DIM, D_STATE, nseq), lambda h: (h, 0, 0, 0))
    return pl.pallas_call(
        _sssd_state_body,
        grid=(SSD_HEADS,),
        in_specs=[hblk, rows(lambda h: (h, 0)), rows(lambda h: (SSD_HEADS + h // hpg, 0)),
                  rows(lambda h: (SSD_HEADS + SSD_GROUPS + h // hpg, 0)),
                  pl.BlockSpec((16, nseq), lambda h: (0, 0)), _const_spec((16, 1)), _const_spec((16, 1))],
        out_specs=[hblk, rows(lambda h: (h, 0))],
        out_shape=[jax.ShapeDtypeStruct((SSD_HEADS, SSD_HEAD_DIM, D_STATE, nseq), f32),
                   jax.ShapeDtypeStruct((SSD_WIDTH, nseq), f32)],
        compiler_params=_params(("parallel",), 32),
        name="sssd_state",
    )(h0t, xat, xat, xat, dtt_small, bias_c, alog_c)


def _sssd_gate_body(yt_ref, xs_ref, z_ref, dsk_ref, ng_ref, o_ref):
    y = (yt_ref[...].T + dsk_ref[...] * xs_ref[...]) * _silu(z_ref[...])
    gw = SSD_WIDTH // SSD_GROUPS
    outs = []
    for g in range(SSD_GROUPS):
        yg = y[:, g * gw:(g + 1) * gw]
        outs.append(yg * _rms(yg, gw))
    o_ref[...] = (jnp.concatenate(outs, axis=1) * ng_ref[...]).astype(o_ref.dtype)


def _sssd_gate(yt, xs, z_small, dsk, ng, nseq):
    blk = pl.BlockSpec((nseq, SSD_WIDTH), lambda i: (0, 0))
    return pl.pallas_call(
        _sssd_gate_body,
        grid=(1,),
        in_specs=[pl.BlockSpec((SSD_WIDTH, nseq), lambda i: (0, 0)), blk, blk,
                  _const_spec((1, SSD_WIDTH)), _const_spec((1, SSD_WIDTH))],
        out_specs=blk,
        out_shape=jax.ShapeDtypeStruct((nseq, SSD_WIDTH), bf16),
        compiler_params=_params(("arbitrary",), 32),
        name="sssd_gate",
    )(yt, xs, z_small, dsk, ng)


def _absorb_body(q_ref, gk_ref, wabs_ref, lat_ref, kr_ref, wuk_ref, qg_ref, a_ref, snew_ref):
    gk = gk_ref[...]
    latb = lat_ref[...].astype(bf16)
    latf = latb.astype(f32)
    kr = kr_ref[...]
    krf = kr.astype(bf16).astype(f32)
    kr2 = jnp.sum(kr * kr, axis=-1, keepdims=True)
    s_new = []
    for h in range(MLA_HEADS):
        sl = slice(h * HT, (h + 1) * HT)
        qg = q_ref[:, sl] * gk
        qg_ref[:, sl] = qg
        a = jnp.dot(qg, wabs_ref[sl, :], precision=_HI, preferred_element_type=f32)
        a_ref[:, h * KV_LORA:(h + 1) * KV_LORA] = a
        kn = _dot(latb, wuk_ref[:, sl])
        n2 = jnp.sum(kn * kn, axis=-1, keepdims=True)
        s12 = (jnp.sum(a.astype(bf16).astype(f32) * latf, axis=-1, keepdims=True)
               + jnp.sum(qg.astype(bf16).astype(f32) * krf, axis=-1, keepdims=True))
        s_new.append(lax.rsqrt((n2 + kr2) * (1.0 / QK_DIM) + EPS) * s12)
    a_ref[:, MLA_HEADS * KV_LORA:] = jnp.zeros((q_ref.shape[0], (16 - MLA_HEADS) * KV_LORA), f32)
    snew_ref[...] = jnp.concatenate(s_new, axis=1)


def _absorb(q_small, gk, wabs, lat_small, kr_small, wuk, nseq):
    blk = lambda n: pl.BlockSpec((nseq, n), lambda i: (0, 0))
    return pl.pallas_call(
        _absorb_body,
        grid=(1,),
        in_specs=[blk(MLA_HEADS * HT), _const_spec((1, HT)), _const_spec((MLA_HEADS * HT, KV_LORA)),
                  blk(KV_LORA), blk(HT), _const_spec((KV_LORA, MLA_HEADS * HT))],
        out_specs=[blk(MLA_HEADS * HT), blk(16 * KV_LORA), blk(MLA_HEADS)],
        out_shape=[jax.ShapeDtypeStruct((nseq, MLA_HEADS * HT), f32), jax.ShapeDtypeStruct((nseq, 16 * KV_LORA), f32),
                   jax.ShapeDtypeStruct((nseq, MLA_HEADS), f32)],
        compiler_params=_params(("arbitrary",), 32),
        name="absorb",
    )(q_small, gk, wabs, lat_small, kr_small, wuk)


def _decode_body(pt_ref, a_ref, c_ref, latn_ref, snew_ref, wukt_ref, clat_ref, ckrt_ref, o_ref,
                 lat_buf, kr_buf, waug, latb, s_buf, sems, *, n_pages, page, ppt):
    b = pl.program_id(0)
    nb = pl.num_programs(0)
    slot = b % 2
    n_tiles = n_pages // ppt
    tile = ppt * page
    wrows = MLA_HEADS * QK_NOPE

    def start_page(seq, p, sl):
        pid = pt_ref[seq * n_pages + p]
        pltpu.make_async_copy(clat_ref.at[0, pid], lat_buf.at[sl, p], sems.at[0, sl]).start()
        pltpu.make_async_copy(ckrt_ref.at[0, pid], kr_buf.at[sl, p], sems.at[1, sl]).start()

    @pl.when(b == 0)
    def _():
        waug[0:wrows, :] = wukt_ref[...]

        def first(p, carry):
            start_page(0, p, 0)
            return carry

        lax.fori_loop(0, n_pages, first, 0)

    pltpu.make_async_copy(clat_ref.at[0, pl.ds(0, n_pages)], lat_buf.at[slot], sems.at[0, slot]).wait()
    pltpu.make_async_copy(ckrt_ref.at[0, pl.ds(0, n_pages)], kr_buf.at[slot], sems.at[1, slot]).wait()

    waug[wrows:wrows + 16, :] = a_ref[0, 0].astype(bf16)
    cb = c_ref[0].astype(bf16)

    def score_dots(lb, krt):
        kt = _dot_nt(waug[...], lb)
        return kt, _dot(cb, krt.astype(bf16)), krt

    def score_finish(kt, s2, krt):
        n = kt.shape[1]
        k4 = kt[0:wrows, :].reshape(MLA_HEADS, QK_NOPE // SUBLANE, SUBLANE, n)
        n2 = _fold_heads(jnp.sum(k4 * k4, axis=1))
        kr2 = jnp.sum(krt * krt, axis=0, keepdims=True)
        return lax.rsqrt((n2 + kr2) * (1.0 / QK_DIM) + EPS) * (kt[wrows:wrows + MLA_HEADS, :] + s2)

    def body(j, carry):
        @pl.when(b + 1 < nb)
        def _():
            for pp in range(ppt):
                start_page(b + 1, j * ppt + pp, 1 - slot)

        lb = lat_buf[slot, pl.ds(j * ppt, ppt)].reshape(tile, KV_LORA).astype(bf16)
        latb[j] = lb
        kr_pages = kr_buf[slot, pl.ds(j * ppt, ppt)]
        s_buf[j] = score_finish(*score_dots(lb, jnp.concatenate([kr_pages[i] for i in range(ppt)], axis=-1)))
        return carry

    lax.fori_loop(0, n_tiles, body, 0)
    s_new = snew_ref[0]
    s_all = jnp.concatenate([s_buf[j] for j in range(n_tiles)], axis=-1)
    m = jnp.maximum(jnp.max(s_all, axis=-1, keepdims=True), s_new)
    p_all = jnp.exp(s_all - m)
    p_new = jnp.exp(s_new - m)
    l = jnp.sum(p_all, axis=-1, keepdims=True) + p_new
    acc = _dot(p_all.astype(bf16), latb[...].reshape(n_tiles * tile, KV_LORA)) + p_new * latn_ref[0]
    o_ref[0] = acc / l


def _decode(page_table_flat, a, c, lat_new, s_new, wukt, cache_lat, cache_kr_t, nseq, n_pages, page, ppt):
    assert n_pages % ppt == 0
    wrows = MLA_HEADS * QK_NOPE
    grid_spec = pltpu.PrefetchScalarGridSpec(
        num_scalar_prefetch=1,
        grid=(nseq,),
        in_specs=[pl.BlockSpec((1, 1, 16, KV_LORA), lambda b, pt: (b, 0, 0, 0)),
                  pl.BlockSpec((1, MLA_HEADS, QK_ROPE), lambda b, pt: (b, 0, 0)),
                  pl.BlockSpec((1, 1, KV_LORA), lambda b, pt: (b, 0, 0)),
                  pl.BlockSpec((1, MLA_HEADS, 1), lambda b, pt: (b, 0, 0)),
                  pl.BlockSpec((wrows, KV_LORA), lambda b, pt: (0, 0), pipeline_mode=pl.Buffered(1)),
                  pl.BlockSpec(memory_space=pl.ANY), pl.BlockSpec(memory_space=pl.ANY)],
        out_specs=pl.BlockSpec((1, MLA_HEADS, KV_LORA), lambda b, pt: (b, 0, 0)),
        scratch_shapes=[pltpu.VMEM((2, n_pages, page, KV_LORA), f32), pltpu.VMEM((2, n_pages, QK_ROPE, page), f32),
                        pltpu.VMEM((wrows + 16, KV_LORA), bf16),
                        pltpu.VMEM((n_pages // ppt, ppt * page, KV_LORA), bf16),
                        pltpu.VMEM((n_pages // ppt, MLA_HEADS, ppt * page), f32), pltpu.SemaphoreType.DMA((2, 2))],
    )
    return pl.pallas_call(
        functools.partial(_decode_body, n_pages=n_pages, page=page, ppt=ppt),
        grid_spec=grid_spec,
        out_shape=jax.ShapeDtypeStruct((nseq, MLA_HEADS, KV_LORA), f32),
        compiler_params=_params(("arbitrary",), 40),
        name="decode",
    )(page_table_flat, a, c, lat_new, s_new, wukt, cache_lat, cache_kr_t)


def _uv_body(o_ref, wuv_ref, att_ref):
    for h in range(MLA_HEADS):
        oh = o_ref[:, h * KV_LORA:(h + 1) * KV_LORA].astype(bf16)
        att_ref[:, h * V_DIM:(h + 1) * V_DIM] = _dot(oh, wuv_ref[:, h * V_DIM:(h + 1) * V_DIM])


def _uv(o_flat, wuv):
    nseq = o_flat.shape[0]
    return pl.pallas_call(
        _uv_body,
        grid=(1,),
        in_specs=[pl.BlockSpec((nseq, MLA_HEADS * KV_LORA), lambda i: (0, 0)), _const_spec((KV_LORA, MLA_WIDTH))],
        out_specs=pl.BlockSpec((nseq, MLA_WIDTH), lambda i: (0, 0)),
        out_shape=jax.ShapeDtypeStruct((nseq, MLA_WIDTH), f32),
        compiler_params=_params(("arbitrary",), 32),
        name="uv",
    )(o_flat, wuv)


def _rope_tables(pos):
    inv = ROPE_THETA ** (-jnp.arange(ROPE_HALF, dtype=f32) * (2.0 / QK_ROPE))
    ang = pos.astype(f32)[:, None] * inv[None, :]
    cos, sin = jnp.cos(ang), jnp.sin(ang)
    n = pos.shape[0]
    one = jnp.ones((n, QK_NOPE), f32)
    z16 = jnp.zeros((n, ROPE_HALF), f32)
    z32 = jnp.zeros((n, HT - QK_DIM), f32)
    z64 = jnp.zeros((n, QK_NOPE), f32)
    tc = jnp.concatenate([one, cos, cos, z32], axis=1)
    ts1 = jnp.concatenate([z64, -sin, z16, z32], axis=1)
    ts2 = jnp.concatenate([z64, z16, sin, z32], axis=1)
    return tc, ts1, ts2


def _pad_lanes(x, n):
    return jnp.pad(x, ((0, 0), (0, n - x.shape[1])))


def kernel(x_prompt, x_sample, cache_kv_latent, cache_k_rope, state_conv, state_ssm, page_table, meta_tokens, attn_norm_g, w_in, conv_w, conv_b, dt_bias, a_log, d_skip, ssd_norm_g, q_norm_g, w_uq, kv_norm_g, w_ukv, q_head_g_nope, q_head_g_rope, k_head_g_nope, k_head_g_rope, mla_out_g, w_out, ffn_norm_g, w_gate, w_up, w_down):
    l = 0
    nb, seq, _ = x_prompt.shape
    nseq = x_sample.shape[0]
    n_pages, page = page_table.shape[1], cache_kv_latent.shape[2]
    past = n_pages * page
    n_small = 2 * LANE
    meta_lo = n_small - N_META

    s0, s1, s2, s3, s4 = (SSD_WIDTH, SSD_WIDTH + CONV_DIM, SSD_WIDTH + CONV_DIM + SSD_HEADS,
                          SSD_WIDTH + CONV_DIM + SSD_HEADS + Q_LORA, SSD_WIDTH + CONV_DIM + SSD_HEADS + Q_LORA + KV_LORA)
    wi = w_in[l]
    w_last = jnp.concatenate([_pad_lanes(wi[:, s1:s2], ROPE_LO), _pad_lanes(wi[:, s4:], HT - ROPE_LO)], axis=1)
    w_proj = jnp.concatenate([wi[:, :s1], wi[:, s2:s4], w_last], axis=1).astype(bf16)
    g_attn = attn_norm_g[l][None, :]
    wq3 = w_uq[l].reshape(Q_LORA, MLA_HEADS, QK_DIM)
    zq = jnp.zeros((Q_LORA, MLA_HEADS, QK_NOPE), f32)
    wq_partner = jnp.concatenate([zq, wq3[:, :, QK_NOPE + ROPE_HALF:], wq3[:, :, QK_NOPE:QK_NOPE + ROPE_HALF]], axis=2)
    pad_head = lambda w: jnp.pad(w, ((0, 0), (0, 0), (0, HT - QK_DIM))).reshape(Q_LORA, -1)
    wuq = jnp.concatenate([pad_head(wq3), pad_head(wq_partner)], axis=1).astype(bf16)
    wkv = w_ukv[l].reshape(KV_LORA, MLA_HEADS, QK_NOPE + V_DIM)
    wuk_f32 = jnp.pad(wkv[:, :, :QK_NOPE], ((0, 0), (0, 0), (0, HT - QK_NOPE))).reshape(KV_LORA, -1)
    wuk = wuk_f32.astype(bf16)
    wuv = wkv[:, :, QK_NOPE:].reshape(KV_LORA, MLA_WIDTH).astype(bf16)
    wukt = wkv[:, :, :QK_NOPE].reshape(KV_LORA, -1).T.astype(bf16)
    gq = _pad_lanes(jnp.concatenate([q_head_g_nope[l], q_head_g_rope[l], q_head_g_rope[l]])[None, :], HT) * ATTN_SCALE
    gk = _pad_lanes(jnp.concatenate([k_head_g_nope[l], k_head_g_rope[l], k_head_g_rope[l]])[None, :], HT)
    gqn, gkv = q_norm_g[l][None, :], kv_norm_g[l][None, :]
    cw, cb = conv_w[l], conv_b[l][None, :]
    bias_c = jnp.pad(dt_bias[l][:, None], ((0, 16 - SSD_HEADS), (0, 0)))
    alog_c = jnp.pad(a_log[l][:, None], ((0, 16 - SSD_HEADS), (0, 0)))
    dsk = jnp.repeat(d_skip[l], SSD_HEAD_DIM)[None, :]
    ng = ssd_norm_g[l][None, :]
    tri = jnp.tril(jnp.ones((SSD_CHUNK, SSD_CHUNK), f32))
    ssd_consts = (cw, cb, bias_c, alog_c, dsk, ng, tri)
    gm, gf = mla_out_g[l][None, :], ffn_norm_g[l][None, :]
    wo = w_out[l].astype(bf16)
    wg, wu, wd = w_gate[l].astype(bf16), w_up[l].astype(bf16), w_down[l].astype(bf16)

    xp = x_prompt.reshape(nb * seq, D_MODEL)
    xs_rows = x_sample[:, 0, :]
    x_small = jnp.concatenate([xs_rows, jnp.zeros((meta_lo - nseq, D_MODEL), f32), meta_tokens.astype(f32)], axis=0)

    z_p, xbc_p, cq_p, ckv_p, kr_p, dtt_p = _project(xp, g_attn, w_proj, ROW_TILE)
    z_s, xbc_s, cq_s, ckv_s, kr_s, dtt_s = _project(x_small, g_attn, w_proj, n_small)

    tabs_p = _rope_tables(N_META + jnp.arange(seq))
    pos_small = jnp.concatenate([jnp.full((nseq,), past), jnp.zeros((meta_lo - nseq,), jnp.int32), jnp.arange(N_META)])
    tabs_s = _rope_tables(pos_small)
    tm = ROW_TILE
    lat_p, kro_p, q_p, k_p, vt_p = _mla_prep(cq_p, ckv_p, kr_p, tabs_p, seq // tm, gqn, gkv, wuq, wuk, wuv.T, gq, gk, tm, bf16)
    lat_s, kro_s, q_s, k_s, vt_s = _mla_prep(cq_s, ckv_s, kr_s, tabs_s, 1, gqn, gkv, wuq, wuk, wuv.T, gq, gk, n_small, f32)

    zero_h = jnp.zeros((1, 4, LANE, LANE), f32)
    zero_hist = jnp.zeros((SUBLANE, CONV_DIM), f32)
    _, h_meta, _ = _ssd(xbc_s, z_s, dtt_s, zero_hist, 0, zero_h, ssd_consts, 1, 1, 1, SSD_CHUNK - N_META, 1)
    ssd_p, _, h_fin = _ssd(xbc_p, z_p, dtt_p, xbc_s, n_small // SUBLANE - 1, h_meta, ssd_consts, nb, seq // SSD_CHUNK, 0, 0,
                           SSD_CHUNKS_PER_STEP)

    att_p = _flash(q_p, k_p, vt_p, k_s, vt_s[:, meta_lo:], n_small // N_META - 1, nb, seq, FLASH_TILE, FLASH_TILE,
                   FLASH_HEADS_PER_STEP)
    y_prompt = _finish(xp, ssd_p, att_p, gm, wo, gf, wg, wu, wd, ROW_TILE).reshape(nb, seq, D_MODEL)

    sc = jnp.transpose(state_conv[l], (1, 0, 2))
    xs_s, xat_s = _sssd_prep(xbc_s, sc, cw, cb, nseq)
    h_new_t, yt_s = _sssd_state(jnp.transpose(state_ssm[l], (1, 2, 3, 0)), xat_s, dtt_s, bias_c, alog_c, nseq)
    h_new = jnp.transpose(h_new_t, (3, 0, 1, 2))
    ssd_s = _sssd_gate(yt_s, xs_s, z_s, dsk, ng, nseq)
    qg, a_abs, s_new = _absorb(q_s, gk, wuk_f32.T, lat_s, kro_s, wuk, nseq)
    hr = list(HEAD_ROW)
    a_abs = a_abs.reshape(nseq, 16, KV_LORA)[:, hr + list(range(MLA_HEADS, 16))].reshape(nseq, 1, 16, KV_LORA)
    c_abs = qg.reshape(nseq, MLA_HEADS, HT)[:, hr, ROPE_LO:QK_DIM]
    lat_new = lat_s[:nseq][:, None, :]
    kr_new = kro_s[:nseq, ROPE_LO:QK_DIM][:, None, :]
    o_lat = _decode(page_table.reshape(-1), a_abs, c_abs, lat_new, s_new[:, hr, None], wukt,
                    cache_kv_latent, jnp.swapaxes(cache_k_rope, 2, 3), nseq, n_pages, page, DECODE_PAGES_PER_TILE)
    att_s = _uv(o_lat[:, hr].reshape(nseq, MLA_HEADS * KV_LORA), wuv)
    y_sample = _finish(xs_rows, ssd_s, att_s, gm, wo, gf, wg, wu, wd, nseq)[:, None, :]

    def with_meta(small, main, width):
        meta = jnp.broadcast_to(small[meta_lo:][None], (nb, N_META, width))
        return jnp.concatenate([meta, main.reshape(nb, seq, width)], axis=1)[None]

    kv_latent_prompt = with_meta(lat_s, lat_p, KV_LORA)
    k_rope_prompt = with_meta(kro_s[:, ROPE_LO:QK_DIM], kro_p[:, ROPE_LO:QK_DIM], QK_ROPE)
    conv_prompt = xbc_p.reshape(nb, seq, CONV_DIM)[:, seq - (CONV_W - 1):][None]
    ssm_prompt = h_fin[None]
    kv_latent_sample = lat_new[None]
    k_rope_sample = kr_new[None]
    conv_sample = jnp.concatenate([state_conv[l][:, 1:], xbc_s[:nseq][:, None, :]], axis=1)[None]
    ssm_sample = h_new[None]
    return (y_prompt, y_sample, kv_latent_prompt, k_rope_prompt, conv_prompt, ssm_prompt.astype(x_prompt.dtype),
            kv_latent_sample, k_rope_sample, conv_sample, ssm_sample.astype(state_ssm.dtype))
```

```python
import functools

import jax
import jax.numpy as jnp
from jax import lax
from jax.experimental import pallas as pl
from jax.experimental.pallas import tpu as pltpu

f32 = jnp.float32
bf16 = jnp.bfloat16

D_MODEL = 1024
N_META = 16
SSD_HEADS = 8
SSD_HEAD_DIM = 64
SSD_WIDTH = SSD_HEADS * SSD_HEAD_DIM
SSD_GROUPS = 2
D_STATE = 64
CONV_W = 4
CONV_DIM = SSD_WIDTH + 2 * SSD_GROUPS * D_STATE
SSD_CHUNK = 128
MLA_HEADS = 8
QK_NOPE = 64
QK_ROPE = 32
QK_DIM = QK_NOPE + QK_ROPE
V_DIM = 64
MLA_WIDTH = MLA_HEADS * V_DIM
Q_LORA = 384
KV_LORA = 256
ROPE_THETA = 10000.0
ATTN_SCALE = QK_DIM ** -0.5
D_FF = 2816
EPS = 1e-6

LANE = 128
SUBLANE = 8
HT = LANE
ROPE_LO = QK_NOPE
ROPE_HALF = QK_ROPE // 2
FF_CHUNK = 256
N_FF_CHUNKS = D_FF // FF_CHUNK
DECODE_PAGES_PER_TILE = 32
ROW_TILE = 512
SSD_CHUNKS_PER_STEP = 8
FLASH_TILE = 512
FLASH_HEADS_PER_STEP = 8
MiB = 1024 * 1024

PC_Z = 0
PC_XBC = PC_Z + SSD_WIDTH
PC_CQ = PC_XBC + CONV_DIM
PC_CKV = PC_CQ + Q_LORA
PC_KR = PC_CKV + KV_LORA
PC_END = PC_KR + LANE

_NT = (((1,), (1,)), ((), ()))
_TN = (((0,), (0,)), ((), ()))
_HI = lax.Precision.HIGHEST


def _dot(a, b):
    return jnp.dot(a, b, preferred_element_type=f32)


def _dot_nt(a, b):
    return lax.dot_general(a, b, _NT, preferred_element_type=f32)


def _dot_tn(a, b):
    return lax.dot_general(a, b, _TN, preferred_element_type=f32)


def _rms(x, n):
    return lax.rsqrt(jnp.sum(x * x, axis=-1, keepdims=True) * (1.0 / n) + EPS)


def _silu(x):
    return x * jax.nn.sigmoid(x)


def _softplus(x):
    return jnp.maximum(x, 0.0) + jnp.log1p(jnp.exp(-jnp.abs(x)))


HEAD_ROW = (3, 2, 1, 0, 7, 6, 5, 4)


def _fold_heads(part):
    row = lax.broadcasted_iota(jnp.int32, part.shape[1:], 0)
    v = [part[h] for h in range(8)]
    v = [jnp.where(row < 4, v[j] + pltpu.roll(v[j], 4, 0), v[j + 4] + pltpu.roll(v[j + 4], 4, 0)) for j in range(4)]
    v = [jnp.where(row % 4 >= 2, v[j] + pltpu.roll(v[j], 2, 0), v[j + 2] + pltpu.roll(v[j + 2], 6, 0)) for j in range(2)]
    return jnp.where(row % 2 == 1, v[0] + pltpu.roll(v[0], 1, 0), v[1] + pltpu.roll(v[1], 7, 0))


def _const_spec(shape):
    nd = len(shape)
    return pl.BlockSpec(shape, lambda *_: (0,) * nd, pipeline_mode=pl.Buffered(1))


def _params(sem, vmem_mib):
    return pltpu.CompilerParams(dimension_semantics=sem, vmem_limit_bytes=vmem_mib * MiB)


def _proj_body(x_ref, g_ref, w_ref, z_ref, xbc_ref, cq_ref, ckv_ref, kr_ref, dtt_ref):
    x = x_ref[...]
    xn = (x * _rms(x, D_MODEL) * g_ref[...]).astype(bf16)
    z_ref[...] = _dot(xn, w_ref[:, PC_Z:PC_XBC])
    xbc_ref[...] = _dot(xn, w_ref[:, PC_XBC:PC_CQ])
    cq_ref[...] = _dot(xn, w_ref[:, PC_CQ:PC_CKV])
    ckv_ref[...] = _dot(xn, w_ref[:, PC_CKV:PC_KR])
    last = _dot(xn, w_ref[:, PC_KR:PC_END])
    lane = lax.broadcasted_iota(jnp.int32, (1, LANE), 1)
    kr_ref[...] = jnp.where(lane >= ROPE_LO, last, 0.0)
    dtt_ref[...] = last.T[0:16, :]


def _project(x, g, w, tm):
    m = x.shape[0]
    row = lambda n: pl.BlockSpec((tm, n), lambda i: (i, 0))
    widths = (SSD_WIDTH, CONV_DIM, Q_LORA, KV_LORA, LANE)
    return pl.pallas_call(
        _proj_body,
        grid=(m // tm,),
        in_specs=[row(D_MODEL), _const_spec((1, D_MODEL)), _const_spec((D_MODEL, PC_END))],
        out_specs=[row(n) for n in widths] + [pl.BlockSpec((16, tm), lambda i: (0, i))],
        out_shape=[jax.ShapeDtypeStruct((m, n), f32) for n in widths] + [jax.ShapeDtypeStruct((16, m), f32)],
        compiler_params=_params(("parallel",), 40),
        name="proj",
    )(x, g, w)


def _mla_prep_body(cq_ref, ckv_ref, kr_ref, tc_ref, ts1_ref, ts2_ref, gqn_ref, gkv_ref, wuq_ref, wuk_ref, wuv_ref,
                   gq_ref, gk_ref, lat_ref, kro_ref, q_ref, k_ref, vt_ref):
    tc, ts1, ts2 = tc_ref[...], ts1_ref[...], ts2_ref[...]
    tsw = ts1 + ts2

    def rope(x):
        return x * tc + pltpu.roll(x, LANE - ROPE_HALF, 1) * ts1 + pltpu.roll(x, ROPE_HALF, 1) * ts2

    ckv = ckv_ref[...]
    lat = ckv * _rms(ckv, KV_LORA) * gkv_ref[...]
    lat_ref[...] = lat
    kr = rope(kr_ref[...])
    kro_ref[...] = kr
    cq = cq_ref[...]
    cqn = (cq * _rms(cq, Q_LORA) * gqn_ref[...]).astype(bf16)
    latb = lat.astype(bf16)
    vt_ref[...] = _dot_nt(wuv_ref[...], latb).astype(vt_ref.dtype)
    gq, gk = gq_ref[...], gk_ref[...]
    q_all = _dot(cqn, wuq_ref[:, 0:MLA_HEADS * HT])
    qp_all = _dot(cqn, wuq_ref[:, MLA_HEADS * HT:])
    k_all = _dot(latb, wuk_ref[...])
    for h in range(MLA_HEADS):
        sl = slice(h * HT, (h + 1) * HT)
        qh = q_all[:, sl] * tc + qp_all[:, sl] * tsw
        q_ref[:, sl] = (qh * _rms(qh, QK_DIM) * gq).astype(q_ref.dtype)
        kh = k_all[:, sl] + kr
        k_ref[:, sl] = (kh * _rms(kh, QK_DIM) * gk).astype(k_ref.dtype)


def _mla_prep(cq, ckv, kr, tabs, n_tab_blocks, gqn, gkv, wuq, wuk, wuvt, gq, gk, tm, q_dtype):
    m = cq.shape[0]
    row = lambda n: pl.BlockSpec((tm, n), lambda i: (i, 0))
    tab = pl.BlockSpec((tm, LANE), lambda i: (i % n_tab_blocks, 0))
    widths = (KV_LORA, LANE, MLA_HEADS * HT, MLA_HEADS * HT)
    dtypes = (f32, f32, q_dtype, bf16)
    return pl.pallas_call(
        _mla_prep_body,
        grid=(m // tm,),
        in_specs=[row(Q_LORA), row(KV_LORA), row(LANE), tab, tab, tab,
                  _const_spec((1, Q_LORA)), _const_spec((1, KV_LORA)),
                  _const_spec((Q_LORA, 2 * MLA_HEADS * HT)), _const_spec((KV_LORA, MLA_HEADS * HT)),
                  _const_spec((MLA_WIDTH, KV_LORA)), _const_spec((1, HT)), _const_spec((1, HT))],
        out_specs=[row(n) for n in widths] + [pl.BlockSpec((MLA_WIDTH, tm), lambda i: (0, i))],
        out_shape=[jax.ShapeDtypeStruct((m, n), d) for n, d in zip(widths, dtypes)]
                  + [jax.ShapeDtypeStruct((MLA_WIDTH, m), bf16)],
        compiler_params=_params(("parallel",), 40),
        name="mla_prep",
    )(cq, ckv, kr, *tabs, gqn, gkv, wuq, wuk, wuvt, gq, gk)


def _ssd_body(xbc_ref, z_ref, dtt_ref, hist_ref, h0_ref, cw_ref, cb_ref, bias_c_ref,
              alog_c_ref, dsk_ref, ng_ref, tri_ref, y_ref, hout_ref, hfin_ref, xp_ref, hs_ref, *, valid_from, cps):
    c = pl.program_id(1)
    q = SSD_CHUNK

    @pl.when(c == 0)
    def _():
        xp_ref[0:SUBLANE, :] = hist_ref[...]
        hs_ref[...] = h0_ref[0]

    rows = lax.broadcasted_iota(jnp.int32, (q, q), 0)
    cols = lax.broadcasted_iota(jnp.int32, (q, q), 1)
    causal = cols <= rows
    lane = lax.broadcasted_iota(jnp.int32, (1, LANE), 1)
    low = lane < D_STATE
    tri = tri_ref[...]
    dsk = dsk_ref[...]

    def chunk(xbc, dtt, z):
        xp_ref[SUBLANE:SUBLANE + q, :] = xbc
        conv = cb_ref[...]
        for k in range(CONV_W):
            lo = SUBLANE - (CONV_W - 1) + k
            conv = conv + xp_ref[lo:lo + q, :] * cw_ref[k:k + 1, :]
        xp_ref[0:SUBLANE, :] = xbc[q - SUBLANE:q, :]
        xa = _silu(conv)
        xs = xa[:, :SSD_WIDTH]
        bm = xa[:, SSD_WIDTH:SSD_WIDTH + LANE]
        cm = xa[:, SSD_WIDTH + LANE:]

        dtr = _softplus(dtt + bias_c_ref[...])
        if valid_from:
            dtr = jnp.where(lax.broadcasted_iota(jnp.int32, (16, q), 1) >= valid_from, dtr, 0.0)
        dar = dtr * -jnp.exp(alog_c_ref[...])
        dtc = dtr.T
        acs_r = lax.dot_general(dar, tri, _NT, precision=_HI, preferred_element_type=f32)
        acs_c = jnp.dot(tri, dar.T, precision=_HI, preferred_element_type=f32)
        w_c = jnp.exp(acs_c[q - 1:q, :] - acs_c) * dtc
        e_c = jnp.exp(acs_c)
        cd_r = jnp.exp(acs_r[:, q - 1:q])

        bb = bm.astype(bf16)
        ys = []
        for g in range(SSD_GROUPS):
            gmask = (lane >= g * D_STATE) & (lane < (g + 1) * D_STATE)
            cg = jnp.where(gmask, cm, 0.0).astype(bf16)
            cb_g = _dot_nt(cg, bb)
            for pi in range(2):
                i = 2 * g + pi
                xpair = xs[:, i * LANE:(i + 1) * LANE]
                ypair = jnp.zeros((q, LANE), f32)
                for hh in range(2):
                    h = 2 * i + hh
                    seg = acs_c[:, h:h + 1] - acs_r[h:h + 1, :]
                    lmat = jnp.exp(jnp.where(causal, seg, -jnp.inf))
                    sc = (cb_g * lmat * dtr[h:h + 1, :]).astype(bf16)
                    xh = jnp.where(low if hh == 0 else jnp.logical_not(low), xpair, 0.0).astype(bf16)
                    ypair = ypair + _dot(sc, xh)
                h0, h1 = 2 * i, 2 * i + 1
                wp = jnp.where(low, w_c[:, h0:h0 + 1], w_c[:, h1:h1 + 1])
                ep = jnp.where(low, e_c[:, h0:h0 + 1], e_c[:, h1:h1 + 1])
                hst = hs_ref[i]
                yoff = _dot_nt(cg, hst.astype(bf16)) * ep
                st = _dot_tn((xpair * wp).astype(bf16), bb)
                cdb = jnp.concatenate([jnp.broadcast_to(cd_r[h0:h0 + 1, :], (SSD_HEAD_DIM, LANE)),
                                       jnp.broadcast_to(cd_r[h1:h1 + 1, :], (SSD_HEAD_DIM, LANE))], axis=0)
                hs_ref[i] = hst * cdb + st
                ys.append(ypair + yoff + dsk[:, i * LANE:(i + 1) * LANE] * xpair)
        y = jnp.concatenate(ys, axis=1) * _silu(z)
        gw = SSD_WIDTH // SSD_GROUPS
        outs = []
        for g in range(SSD_GROUPS):
            yg = y[:, g * gw:(g + 1) * gw]
            outs.append(yg * _rms(yg, gw))
        return (jnp.concatenate(outs, axis=1) * ng_ref[...]).astype(y_ref.dtype)

    for ci in range(cps):
        rs = slice(ci * q, (ci + 1) * q)
        y_ref[rs, :] = chunk(xbc_ref[rs, :], dtt_ref[:, rs], z_ref[rs, :])

    @pl.when(c == pl.num_programs(1) - 1)
    def _():
        hout_ref[0] = hs_ref[...]
        for i in range(2 * SSD_GROUPS):
            g = i // 2
            for hh in range(2):
                hfin_ref[0, 2 * i + hh] = hs_ref[i, hh * SSD_HEAD_DIM:(hh + 1) * SSD_HEAD_DIM, g * D_STATE:(g + 1) * D_STATE]


def _ssd(xbc, z, dtt, hist_arr, hist_block, h0, consts, n_batch, n_chunks, row_block0, valid_from, cps):
    assert n_chunks % cps == 0
    q = SSD_CHUNK
    n_steps = n_chunks // cps
    rb = lambda n: pl.BlockSpec((cps * q, n), lambda b, c: (row_block0 + b * n_steps + c, 0))
    cw, cb, bias_c, alog_c, dsk, ng, tri = consts
    return pl.pallas_call(
        functools.partial(_ssd_body, valid_from=valid_from, cps=cps),
        grid=(n_batch, n_steps),
        in_specs=[rb(CONV_DIM), rb(SSD_WIDTH),
                  pl.BlockSpec((16, cps * q), lambda b, c: (0, row_block0 + b * n_steps + c)),
                  pl.BlockSpec((SUBLANE, CONV_DIM), lambda b, c: (hist_block, 0)),
                  _const_spec((1, 4, LANE, LANE)),
                  _const_spec((CONV_W, CONV_DIM)), _const_spec((1, CONV_DIM)),
                  _const_spec((16, 1)), _const_spec((16, 1)),
                  _const_spec((1, SSD_WIDTH)), _const_spec((1, SSD_WIDTH)), _const_spec((q, q))],
        out_specs=[pl.BlockSpec((cps * q, SSD_WIDTH), lambda b, c: (b * n_steps + c, 0)),
                   pl.BlockSpec((1, 4, LANE, LANE), lambda b, c: (b, 0, 0, 0)),
                   pl.BlockSpec((1, SSD_HEADS, SSD_HEAD_DIM, D_STATE), lambda b, c: (b, 0, 0, 0))],
        out_shape=[jax.ShapeDtypeStruct((n_batch * n_chunks * q, SSD_WIDTH), bf16),
                   jax.ShapeDtypeStruct((n_batch, 4, LANE, LANE), f32),
                   jax.ShapeDtypeStruct((n_batch, SSD_HEADS, SSD_HEAD_DIM, D_STATE), f32)],
        scratch_shapes=[pltpu.VMEM((SUBLANE + q, CONV_DIM), f32), pltpu.VMEM((4, LANE, LANE), f32)],
        compiler_params=_params(("parallel", "arbitrary"), 32),
        name="ssd",
    )(xbc, z, dtt, hist_arr, h0, cw, cb, bias_c, alog_c, dsk, ng, tri)


def _flash_body(q_ref, k_ref, vt_ref, km_ref, vtm_ref, o_ref, *, tq, tk, nh):
    qi = pl.program_id(2)
    qs = [q_ref[:, h * HT:(h + 1) * HT] for h in range(nh)]
    half = tk // 2

    def heads(kall, vtall, carries, mask=None, qlo=0):
        sts = [_dot_nt(kall[:, h * HT:(h + 1) * HT], qs[h][qlo:, :]) for h in range(nh)]
        out = []
        for h in range(nh):
            m0, l0, acc0 = carries[h]
            m, l, acct = m0[:, qlo:], l0[:, qlo:], acc0[:, qlo:]
            st = sts[h] if mask is None else jnp.where(mask, sts[h], -jnp.inf)
            m2 = jnp.maximum(m, jnp.max(st, axis=0, keepdims=True))
            pt = jnp.exp(st - m2)
            a = jnp.exp(m - m2)
            vth = vtall[h * V_DIM:(h + 1) * V_DIM, :]
            new = (m2, a * l + jnp.sum(pt, axis=0, keepdims=True), a * acct + _dot(vth, pt.astype(bf16)))
            if qlo:
                new = tuple(jnp.concatenate([old[:, :qlo], x], axis=1) for old, x in zip(carries[h], new))
            out.append(new)
        return tuple(out)

    init = (jnp.full((1, tq), -jnp.inf, f32), jnp.zeros((1, tq), f32), jnp.zeros((V_DIM, tq), f32))

    def body(j, carries):
        off = pl.multiple_of(j * tk, tk)
        return heads(k_ref[pl.ds(off, tk), :], vt_ref[:, pl.ds(off, tk)], carries)

    carries = lax.fori_loop(0, qi, body, (init,) * nh)
    off = pl.multiple_of(qi * tk, tk)
    k_a = jnp.concatenate([k_ref[pl.ds(off, half), :], km_ref[...]], axis=0)
    vt_a = jnp.concatenate([vt_ref[:, pl.ds(off, half)], vtm_ref[...]], axis=1)
    krow = lax.broadcasted_iota(jnp.int32, (half + N_META, tq), 0)
    qcol = lax.broadcasted_iota(jnp.int32, (half + N_META, tq), 1)
    carries = heads(k_a, vt_a, carries, (krow >= half) | (krow <= qcol))
    off_b = pl.multiple_of(qi * tk + half, half)
    krow = lax.broadcasted_iota(jnp.int32, (half, tq - half), 0)
    qcol = lax.broadcasted_iota(jnp.int32, (half, tq - half), 1)
    res = heads(k_ref[pl.ds(off_b, half), :], vt_ref[:, pl.ds(off_b, half)], carries, krow <= qcol, qlo=half)
    for pr in range(nh // 2):
        (_, l0, acc0), (_, l1, acc1) = res[2 * pr], res[2 * pr + 1]
        pair = jnp.concatenate([acc0 / l0, acc1 / l1], axis=0)
        o_ref[:, pr * LANE:(pr + 1) * LANE] = pair.T.astype(o_ref.dtype)


def _flash(q, k, vt, k_small, vt_meta, meta_block, n_batch, seq, tq, tk, nh):
    assert tk == tq and seq % tk == 0 and nh % 2 == 0 and MLA_HEADS % nh == 0
    nq = seq // tq
    return pl.pallas_call(
        functools.partial(_flash_body, tq=tq, tk=tk, nh=nh),
        grid=(n_batch, MLA_HEADS // nh, nq),
        in_specs=[pl.BlockSpec((tq, nh * HT), lambda b, p, i: (b * nq + i, p)),
                  pl.BlockSpec((seq, nh * HT), lambda b, p, i: (b, p)),
                  pl.BlockSpec((nh * V_DIM, seq), lambda b, p, i: (p, b)),
                  pl.BlockSpec((N_META, nh * HT), lambda b, p, i: (meta_block, p)),
                  pl.BlockSpec((nh * V_DIM, N_META), lambda b, p, i: (p, 0))],
        out_specs=pl.BlockSpec((tq, nh * V_DIM), lambda b, p, i: (b * nq + i, p)),
        out_shape=jax.ShapeDtypeStruct((n_batch * seq, MLA_WIDTH), f32),
        compiler_params=_params(("parallel", "parallel", "arbitrary"), 40),
        name="flash",
    )(q, k, vt, k_small, vt_meta)


def _finish_body(x_ref, ssd_ref, att_ref, gm_ref, wo_ref, gf_ref, wg_ref, wu_ref, wd_ref, o_ref):
    att = att_ref[...]
    mla = (att * _rms(att, MLA_WIDTH) * gm_ref[...]).astype(bf16)
    h = x_ref[...] + (_dot(ssd_ref[...], wo_ref[0:SSD_WIDTH, :]) + _dot(mla, wo_ref[SSD_WIDTH:, :]))
    n = (h * _rms(h, D_MODEL) * gf_ref[...]).astype(bf16)
    ff = jnp.zeros_like(h)
    for c in range(N_FF_CHUNKS):
        cs = slice(c * FF_CHUNK, (c + 1) * FF_CHUNK)
        a = (_silu(_dot(n, wg_ref[:, cs])) * _dot(n, wu_ref[:, cs])).astype(bf16)
        ff = ff + _dot(a, wd_ref[cs, :])
    o_ref[...] = h + ff


def _finish(x, ssd, att, gm, wo, gf, wg, wu, wd, tm):
    m = x.shape[0]
    row = lambda n: pl.BlockSpec((tm, n), lambda i: (i, 0))
    return pl.pallas_call(
        _finish_body,
        grid=(m // tm,),
        in_specs=[row(D_MODEL), row(SSD_WIDTH), row(MLA_WIDTH), _const_spec((1, MLA_WIDTH)),
                  _const_spec((D_MODEL, D_MODEL)), _const_spec((1, D_MODEL)),
                  _const_spec((D_MODEL, D_FF)), _const_spec((D_MODEL, D_FF)), _const_spec((D_FF, D_MODEL))],
        out_specs=row(D_MODEL),
        out_shape=jax.ShapeDtypeStruct((m, D_MODEL), f32),
        compiler_params=_params(("parallel",), 56),
        name="finish",
    )(x, ssd, att, gm, wo, gf, wg, wu, wd)


def _sssd_prep_body(xbc_ref, sc_ref, cw_ref, cb_ref, xs_ref, xat_ref):
    conv = cb_ref[...]
    for k in range(CONV_W - 1):
        conv = conv + sc_ref[k] * cw_ref[k:k + 1, :]
    conv = conv + xbc_ref[...] * cw_ref[CONV_W - 1:CONV_W, :]
    xa = _silu(conv)
    xs_ref[...] = xa[:, :SSD_WIDTH]
    xat_ref[...] = xa.T


def _sssd_prep(xbc_small, sc, cw, cb, nseq):
    return pl.pallas_call(
        _sssd_prep_body,
        grid=(1,),
        in_specs=[pl.BlockSpec((nseq, CONV_DIM), lambda i: (0, 0)), _const_spec((CONV_W - 1, nseq, CONV_DIM)),
                  _const_spec((CONV_W, CONV_DIM)), _const_spec((1, CONV_DIM))],
        out_specs=[pl.BlockSpec((nseq, SSD_WIDTH), lambda i: (0, 0)), pl.BlockSpec((CONV_DIM, nseq), lambda i: (0, 0))],
        out_shape=[jax.ShapeDtypeStruct((nseq, SSD_WIDTH), f32), jax.ShapeDtypeStruct((CONV_DIM, nseq), f32)],
        compiler_params=_params(("arbitrary",), 32),
        name="sssd_prep",
    )(xbc_small, sc, cw, cb)


def _sssd_state_body(h0_ref, xst_ref, bt_ref, ct_ref, dtt_ref, bias_c_ref, alog_c_ref, hn_ref, yt_ref):
    h = pl.program_id(0)
    dt = _softplus(dtt_ref[pl.ds(h, 1), :] + bias_c_ref[pl.ds(h, 1), :])
    dec = jnp.exp(dt * -jnp.exp(alog_c_ref[pl.ds(h, 1), :]))
    bt, ct = bt_ref[...], ct_ref[...]

    def body(p, carry):
        xdt = xst_ref[pl.ds(p, 1), :] * dt
        hn = h0_ref[0, p] * dec + xdt * bt
        hn_ref[0, p] = hn
        yt_ref[pl.ds(p, 1), :] = jnp.sum(ct * hn, axis=0, keepdims=True)
        return carry

    lax.fori_loop(0, SSD_HEAD_DIM, body, 0, unroll=4)


def _sssd_state(h0t, xat, dtt_small, bias_c, alog_c, nseq):
    hpg = SSD_HEADS // SSD_GROUPS
    rows = lambda f: pl.BlockSpec((SSD_HEAD_DIM, nseq), f)
    hblk = pl.BlockSpec((1, SSD_HEAD_DIM, D_STATE, nseq), lambda h: (h, 0, 0, 0))
    return pl.pallas_call(
        _sssd_state_body,
        grid=(SSD_HEADS,),
        in_specs=[hblk, rows(lambda h: (h, 0)), rows(lambda h: (SSD_HEADS + h // hpg, 0)),
                  rows(lambda h: (SSD_HEADS + SSD_GROUPS + h // hpg, 0)),
                  pl.BlockSpec((16, nseq), lambda h: (0, 0)), _const_spec((16, 1)), _const_spec((16, 1))],
        out_specs=[hblk, rows(lambda h: (h, 0))],
        out_shape=[jax.ShapeDtypeStruct((SSD_HEADS, SSD_HEAD_DIM, D_STATE, nseq), f32),
                   jax.ShapeDtypeStruct((SSD_WIDTH, nseq), f32)],
        compiler_params=_params(("parallel",), 32),
        name="sssd_state",
    )(h0t, xat, xat, xat, dtt_small, bias_c, alog_c)


def _sssd_gate_body(yt_ref, xs_ref, z_ref, dsk_ref, ng_ref, o_ref):
    y = (yt_ref[...].T + dsk_ref[...] * xs_ref[...]) * _silu(z_ref[...])
    gw = SSD_WIDTH // SSD_GROUPS
    outs = []
    for g in range(SSD_GROUPS):
        yg = y[:, g * gw:(g + 1) * gw]
        outs.append(yg * _rms(yg, gw))
    o_ref[...] = (jnp.concatenate(outs, axis=1) * ng_ref[...]).astype(o_ref.dtype)


def _sssd_gate(yt, xs, z_small, dsk, ng, nseq):
    blk = pl.BlockSpec((nseq, SSD_WIDTH), lambda i: (0, 0))
    return pl.pallas_call(
        _sssd_gate_body,
        grid=(1,),
        in_specs=[pl.BlockSpec((SSD_WIDTH, nseq), lambda i: (0, 0)), blk, blk,
                  _const_spec((1, SSD_WIDTH)), _const_spec((1, SSD_WIDTH))],
        out_specs=blk,
        out_shape=jax.ShapeDtypeStruct((nseq, SSD_WIDTH), bf16),
        compiler_params=_params(("arbitrary",), 32),
        name="sssd_gate",
    )(yt, xs, z_small, dsk, ng)


def _absorb_body(q_ref, gk_ref, wabs_ref, lat_ref, kr_ref, wuk_ref, qg_ref, a_ref, snew_ref):
    gk = gk_ref[...]
    latb = lat_ref[...].astype(bf16)
    latf = latb.astype(f32)
    kr = kr_ref[...]
    krf = kr.astype(bf16).astype(f32)
    kr2 = jnp.sum(kr * kr, axis=-1, keepdims=True)
    s_new = []
    for h in range(MLA_HEADS):
        sl = slice(h * HT, (h + 1) * HT)
        qg = q_ref[:, sl] * gk
        qg_ref[:, sl] = qg
        a = jnp.dot(qg, wabs_ref[sl, :], precision=_HI, preferred_element_type=f32)
        a_ref[:, h * KV_LORA:(h + 1) * KV_LORA] = a
        kn = _dot(latb, wuk_ref[:, sl])
        n2 = jnp.sum(kn * kn, axis=-1, keepdims=True)
        s12 = (jnp.sum(a.astype(bf16).astype(f32) * latf, axis=-1, keepdims=True)
               + jnp.sum(qg.astype(bf16).astype(f32) * krf, axis=-1, keepdims=True))
        s_new.append(lax.rsqrt((n2 + kr2) * (1.0 / QK_DIM) + EPS) * s12)
    a_ref[:, MLA_HEADS * KV_LORA:] = jnp.zeros((q_ref.shape[0], (16 - MLA_HEADS) * KV_LORA), f32)
    snew_ref[...] = jnp.concatenate(s_new, axis=1)


def _absorb(q_small, gk, wabs, lat_small, kr_small, wuk, nseq):
    blk = lambda n: pl.BlockSpec((nseq, n), lambda i: (0, 0))
    return pl.pallas_call(
        _absorb_body,
        grid=(1,),
        in_specs=[blk(MLA_HEADS * HT), _const_spec((1, HT)), _const_spec((MLA_HEADS * HT, KV_LORA)),
                  blk(KV_LORA), blk(HT), _const_spec((KV_LORA, MLA_HEADS * HT))],
        out_specs=[blk(MLA_HEADS * HT), blk(16 * KV_LORA), blk(MLA_HEADS)],
        out_shape=[jax.ShapeDtypeStruct((nseq, MLA_HEADS * HT), f32), jax.ShapeDtypeStruct((nseq, 16 * KV_LORA), f32),
                   jax.ShapeDtypeStruct((nseq, MLA_HEADS), f32)],
        compiler_params=_params(("arbitrary",), 32),
        name="absorb",
    )(q_small, gk, wabs, lat_small, kr_small, wuk)


def _decode_body(pt_ref, a_ref, c_ref, latn_ref, snew_ref, wukt_ref, clat_ref, ckrt_ref, o_ref,
                 lat_buf, kr_buf, waug, latb, s_buf, sems, *, n_pages, page, ppt):
    b = pl.program_id(0)
    nb = pl.num_programs(0)
    slot = b % 2
    n_tiles = n_pages // ppt
    tile = ppt * page
    wrows = MLA_HEADS * QK_NOPE

    def start_page(seq, p, sl, prio=0):
        pid = pt_ref[seq * n_pages + p]
        pltpu.make_async_copy(clat_ref.at[0, pid], lat_buf.at[sl, p], sems.at[0, sl]).start(priority=prio)
        pltpu.make_async_copy(ckrt_ref.at[0, pid], kr_buf.at[sl, p], sems.at[1, sl]).start(priority=prio)

    @pl.when(b == 0)
    def _():
        waug[0:wrows, :] = wukt_ref[...]

        def first(p, carry):
            start_page(0, p, 0)
            return carry

        lax.fori_loop(0, n_pages, first, 0)

    pltpu.make_async_copy(clat_ref.at[0, pl.ds(0, n_pages)], lat_buf.at[slot], sems.at[0, slot]).wait()
    pltpu.make_async_copy(ckrt_ref.at[0, pl.ds(0, n_pages)], kr_buf.at[slot], sems.at[1, slot]).wait()

    waug[wrows:wrows + 16, :] = a_ref[0, 0].astype(bf16)
    cb = c_ref[0].astype(bf16)

    def score_dots(lb, krt):
        kt = _dot_nt(waug[...], lb)
        return kt, _dot(cb, krt.astype(bf16)), krt

    def score_finish(kt, s2, krt):
        n = kt.shape[1]
        k4 = kt[0:wrows, :].reshape(MLA_HEADS, QK_NOPE // SUBLANE, SUBLANE, n)
        n2 = _fold_heads(jnp.sum(k4 * k4, axis=1))
        kr2 = jnp.sum(krt * krt, axis=0, keepdims=True)
        return lax.rsqrt((n2 + kr2) * (1.0 / QK_DIM) + EPS) * (kt[wrows:wrows + MLA_HEADS, :] + s2)

    def body(j, carry):
        @pl.when(b + 1 < nb)
        def _():
            for pp in range(ppt):
                start_page(b + 1, j * ppt + pp, 1 - slot, pp % 2)

        lb = lat_buf[slot, pl.ds(j * ppt, ppt)].reshape(tile, KV_LORA).astype(bf16)
        latb[j] = lb
        kr_pages = kr_buf[slot, pl.ds(j * ppt, ppt)]
        s_buf[j] = score_finish(*score_dots(lb, jnp.concatenate([kr_pages[i] for i in range(ppt)], axis=-1)))
        return carry

    lax.fori_loop(0, n_tiles, body, 0)
    s_new = snew_ref[0]
    s_all = jnp.concatenate([s_buf[j] for j in range(n_tiles)], axis=-1)
    m = jnp.maximum(jnp.max(s_all, axis=-1, keepdims=True), s_new)
    p_all = jnp.exp(s_all - m)
    p_new = jnp.exp(s_new - m)
    l = jnp.sum(p_all, axis=-1, keepdims=True) + p_new
    acc = _dot(p_all.astype(bf16), latb[...].reshape(n_tiles * tile, KV_LORA)) + p_new * latn_ref[0]
    o_ref[0] = acc / l


def _decode(page_table_flat, a, c, lat_new, s_new, wukt, cache_lat, cache_kr_t, nseq, n_pages, page, ppt):
    assert n_pages % ppt == 0
    wrows = MLA_HEADS * QK_NOPE
    grid_spec = pltpu.PrefetchScalarGridSpec(
        num_scalar_prefetch=1,
        grid=(nseq,),
        in_specs=[pl.BlockSpec((1, 1, 16, KV_LORA), lambda b, pt: (b, 0, 0, 0)),
                  pl.BlockSpec((1, MLA_HEADS, QK_ROPE), lambda b, pt: (b, 0, 0)),
                  pl.BlockSpec((1, 1, KV_LORA), lambda b, pt: (b, 0, 0)),
                  pl.BlockSpec((1, MLA_HEADS, 1), lambda b, pt: (b, 0, 0)),
                  pl.BlockSpec((wrows, KV_LORA), lambda b, pt: (0, 0), pipeline_mode=pl.Buffered(1)),
                  pl.BlockSpec(memory_space=pl.ANY), pl.BlockSpec(memory_space=pl.ANY)],
        out_specs=pl.BlockSpec((1, MLA_HEADS, KV_LORA), lambda b, pt: (b, 0, 0)),
        scratch_shapes=[pltpu.VMEM((2, n_pages, page, KV_LORA), f32), pltpu.VMEM((2, n_pages, QK_ROPE, page), f32),
                        pltpu.VMEM((wrows + 16, KV_LORA), bf16),
                        pltpu.VMEM((n_pages // ppt, ppt * page, KV_LORA), bf16),
                        pltpu.VMEM((n_pages // ppt, MLA_HEADS, ppt * page), f32), pltpu.SemaphoreType.DMA((2, 2))],
    )
    return pl.pallas_call(
        functools.partial(_decode_body, n_pages=n_pages, page=page, ppt=ppt),
        grid_spec=grid_spec,
        out_shape=jax.ShapeDtypeStruct((nseq, MLA_HEADS, KV_LORA), f32),
        compiler_params=_params(("arbitrary",), 40),
        name="decode",
    )(page_table_flat, a, c, lat_new, s_new, wukt, cache_lat, cache_kr_t)


def _uv_body(o_ref, wuv_ref, att_ref):
    for h in range(MLA_HEADS):
        oh = o_ref[:, h * KV_LORA:(h + 1) * KV_LORA].astype(bf16)
        att_ref[:, h * V_DIM:(h + 1) * V_DIM] = _dot(oh, wuv_ref[:, h * V_DIM:(h + 1) * V_DIM])


def _uv(o_flat, wuv):
    nseq = o_flat.shape[0]
    return pl.pallas_call(
        _uv_body,
        grid=(1,),
        in_specs=[pl.BlockSpec((nseq, MLA_HEADS * KV_LORA), lambda i: (0, 0)), _const_spec((KV_LORA, MLA_WIDTH))],
        out_specs=pl.BlockSpec((nseq, MLA_WIDTH), lambda i: (0, 0)),
        out_shape=jax.ShapeDtypeStruct((nseq, MLA_WIDTH), f32),
        compiler_params=_params(("arbitrary",), 32),
        name="uv",
    )(o_flat, wuv)


def _rope_tables(pos):
    inv = ROPE_THETA ** (-jnp.arange(ROPE_HALF, dtype=f32) * (2.0 / QK_ROPE))
    ang = pos.astype(f32)[:, None] * inv[None, :]
    cos, sin = jnp.cos(ang), jnp.sin(ang)
    n = pos.shape[0]
    one = jnp.ones((n, QK_NOPE), f32)
    z16 = jnp.zeros((n, ROPE_HALF), f32)
    z32 = jnp.zeros((n, HT - QK_DIM), f32)
    z64 = jnp.zeros((n, QK_NOPE), f32)
    tc = jnp.concatenate([one, cos, cos, z32], axis=1)
    ts1 = jnp.concatenate([z64, -sin, z16, z32], axis=1)
    ts2 = jnp.concatenate([z64, z16, sin, z32], axis=1)
    return tc, ts1, ts2


def _pad_lanes(x, n):
    return jnp.pad(x, ((0, 0), (0, n - x.shape[1])))


def kernel(x_prompt, x_sample, cache_kv_latent, cache_k_rope, state_conv, state_ssm, page_table, meta_tokens, attn_norm_g, w_in, conv_w, conv_b, dt_bias, a_log, d_skip, ssd_norm_g, q_norm_g, w_uq, kv_norm_g, w_ukv, q_head_g_nope, q_head_g_rope, k_head_g_nope, k_head_g_rope, mla_out_g, w_out, ffn_norm_g, w_gate, w_up, w_down):
    l = 0
    nb, seq, _ = x_prompt.shape
    nseq = x_sample.shape[0]
    n_pages, page = page_table.shape[1], cache_kv_latent.shape[2]
    past = n_pages * page
    n_small = 2 * LANE
    meta_lo = n_small - N_META

    s0, s1, s2, s3, s4 = (SSD_WIDTH, SSD_WIDTH + CONV_DIM, SSD_WIDTH + CONV_DIM + SSD_HEADS,
                          SSD_WIDTH + CONV_DIM + SSD_HEADS + Q_LORA, SSD_WIDTH + CONV_DIM + SSD_HEADS + Q_LORA + KV_LORA)
    wi = w_in[l]
    w_last = jnp.concatenate([_pad_lanes(wi[:, s1:s2], ROPE_LO), _pad_lanes(wi[:, s4:], HT - ROPE_LO)], axis=1)
    w_proj = jnp.concatenate([wi[:, :s1], wi[:, s2:s4], w_last], axis=1).astype(bf16)
    g_attn = attn_norm_g[l][None, :]
    wq3 = w_uq[l].reshape(Q_LORA, MLA_HEADS, QK_DIM)
    zq = jnp.zeros((Q_LORA, MLA_HEADS, QK_NOPE), f32)
    wq_partner = jnp.concatenate([zq, wq3[:, :, QK_NOPE + ROPE_HALF:], wq3[:, :, QK_NOPE:QK_NOPE + ROPE_HALF]], axis=2)
    pad_head = lambda w: jnp.pad(w, ((0, 0), (0, 0), (0, HT - QK_DIM))).reshape(Q_LORA, -1)
    wuq = jnp.concatenate([pad_head(wq3), pad_head(wq_partner)], axis=1).astype(bf16)
    wkv = w_ukv[l].reshape(KV_LORA, MLA_HEADS, QK_NOPE + V_DIM)
    wuk_f32 = jnp.pad(wkv[:, :, :QK_NOPE], ((0, 0), (0, 0), (0, HT - QK_NOPE))).reshape(KV_LORA, -1)
    wuk = wuk_f32.astype(bf16)
    wuv = wkv[:, :, QK_NOPE:].reshape(KV_LORA, MLA_WIDTH).astype(bf16)
    wukt = wkv[:, :, :QK_NOPE].reshape(KV_LORA, -1).T.astype(bf16)
    gq = _pad_lanes(jnp.concatenate([q_head_g_nope[l], q_head_g_rope[l], q_head_g_rope[l]])[None, :], HT) * ATTN_SCALE
    gk = _pad_lanes(jnp.concatenate([k_head_g_nope[l], k_head_g_rope[l], k_head_g_rope[l]])[None, :], HT)
    gqn, gkv = q_norm_g[l][None, :], kv_norm_g[l][None, :]
    cw, cb = conv_w[l], conv_b[l][None, :]
    bias_c = jnp.pad(dt_bias[l][:, None], ((0, 16 - SSD_HEADS), (0, 0)))
    alog_c = jnp.pad(a_log[l][:, None], ((0, 16 - SSD_HEADS), (0, 0)))
    dsk = jnp.repeat(d_skip[l], SSD_HEAD_DIM)[None, :]
    ng = ssd_norm_g[l][None, :]
    tri = jnp.tril(jnp.ones((SSD_CHUNK, SSD_CHUNK), f32))
    ssd_consts = (cw, cb, bias_c, alog_c, dsk, ng, tri)
    gm, gf = mla_out_g[l][None, :], ffn_norm_g[l][None, :]
    wo = w_out[l].astype(bf16)
    wg, wu, wd = w_gate[l].astype(bf16), w_up[l].astype(bf16), w_down[l].astype(bf16)

    xp = x_prompt.reshape(nb * seq, D_MODEL)
    xs_rows = x_sample[:, 0, :]
    x_small = jnp.concatenate([xs_rows, jnp.zeros((meta_lo - nseq, D_MODEL), f32), meta_tokens.astype(f32)], axis=0)

    z_p, xbc_p, cq_p, ckv_p, kr_p, dtt_p = _project(xp, g_attn, w_proj, ROW_TILE)
    z_s, xbc_s, cq_s, ckv_s, kr_s, dtt_s = _project(x_small, g_attn, w_proj, n_small)

    tabs_p = _rope_tables(N_META + jnp.arange(seq))
    pos_small = jnp.concatenate([jnp.full((nseq,), past), jnp.zeros((meta_lo - nseq,), jnp.int32), jnp.arange(N_META)])
    tabs_s = _rope_tables(pos_small)
    tm = ROW_TILE
    lat_p, kro_p, q_p, k_p, vt_p = _mla_prep(cq_p, ckv_p, kr_p, tabs_p, seq // tm, gqn, gkv, wuq, wuk, wuv.T, gq, gk, tm, bf16)
    lat_s, kro_s, q_s, k_s, vt_s = _mla_prep(cq_s, ckv_s, kr_s, tabs_s, 1, gqn, gkv, wuq, wuk, wuv.T, gq, gk, n_small, f32)

    zero_h = jnp.zeros((1, 4, LANE, LANE), f32)
    zero_hist = jnp.zeros((SUBLANE, CONV_DIM), f32)
    _, h_meta, _ = _ssd(xbc_s, z_s, dtt_s, zero_hist, 0, zero_h, ssd_consts, 1, 1, 1, SSD_CHUNK - N_META, 1)
    ssd_p, _, h_fin = _ssd(xbc_p, z_p, dtt_p, xbc_s, n_small // SUBLANE - 1, h_meta, ssd_consts, nb, seq // SSD_CHUNK, 0, 0,
                           SSD_CHUNKS_PER_STEP)

    att_p = _flash(q_p, k_p, vt_p, k_s, vt_s[:, meta_lo:], n_small // N_META - 1, nb, seq, FLASH_TILE, FLASH_TILE,
                   FLASH_HEADS_PER_STEP)
    y_prompt = _finish(xp, ssd_p, att_p, gm, wo, gf, wg, wu, wd, ROW_TILE).reshape(nb, seq, D_MODEL)

    sc = jnp.transpose(state_conv[l], (1, 0, 2))
    xs_s, xat_s = _sssd_prep(xbc_s, sc, cw, cb, nseq)
    h_new_t, yt_s = _sssd_state(jnp.transpose(state_ssm[l], (1, 2, 3, 0)), xat_s, dtt_s, bias_c, alog_c, nseq)
    h_new = jnp.transpose(h_new_t, (3, 0, 1, 2))
    ssd_s = _sssd_gate(yt_s, xs_s, z_s, dsk, ng, nseq)
    qg, a_abs, s_new = _absorb(q_s, gk, wuk_f32.T, lat_s, kro_s, wuk, nseq)
    hr = list(HEAD_ROW)
    a_abs = a_abs.reshape(nseq, 16, KV_LORA)[:, hr + list(range(MLA_HEADS, 16))].reshape(nseq, 1, 16, KV_LORA)
    c_abs = qg.reshape(nseq, MLA_HEADS, HT)[:, hr, ROPE_LO:QK_DIM]
    lat_new = lat_s[:nseq][:, None, :]
    kr_new = kro_s[:nseq, ROPE_LO:QK_DIM][:, None, :]
    o_lat = _decode(page_table.reshape(-1), a_abs, c_abs, lat_new, s_new[:, hr, None], wukt,
                    cache_kv_latent, jnp.swapaxes(cache_k_rope, 2, 3), nseq, n_pages, page, DECODE_PAGES_PER_TILE)
    att_s = _uv(o_lat[:, hr].reshape(nseq, MLA_HEADS * KV_LORA), wuv)
    y_sample = _finish(xs_rows, ssd_s, att_s, gm, wo, gf, wg, wu, wd, nseq)[:, None, :]

    def with_meta(small, main, width):
        meta = jnp.broadcast_to(small[meta_lo:][None], (nb, N_META, width))
        return jnp.concatenate([meta, main.reshape(nb, seq, width)], axis=1)[None]

    kv_latent_prompt = with_meta(lat_s, lat_p, KV_LORA)
    k_rope_prompt = with_meta(kro_s[:, ROPE_LO:QK_DIM], kro_p[:, ROPE_LO:QK_DIM], QK_ROPE)
    conv_prompt = xbc_p.reshape(nb, seq, CONV_DIM)[:, seq - (CONV_W - 1):][None]
    ssm_prompt = h_fin[None]
    kv_latent_sample = lat_new[None]
    k_rope_sample = kr_new[None]
    conv_sample = jnp.concatenate([state_conv[l][:, 1:], xbc_s[:nseq][:, None, :]], axis=1)[None]
    ssm_sample = h_new[None]
    return (y_prompt, y_sample, kv_latent_prompt, k_rope_prompt, conv_prompt, ssm_prompt.astype(x_prompt.dtype),
            kv_latent_sample, k_rope_sample, conv_sample, ssm_sample.astype(state_ssm.dtype))
```

```python
import functools

import jax
import jax.numpy as jnp
from jax import lax
from jax.experimental import pallas as pl
from jax.experimental.pallas import tpu as pltpu

f32 = jnp.float32
bf16 = jnp.bfloat16

D_MODEL = 1024
N_META = 16
SSD_HEADS = 8
SSD_HEAD_DIM = 64
SSD_WIDTH = SSD_HEADS * SSD_HEAD_DIM
SSD_GROUPS = 2
D_STATE = 64
CONV_W = 4
CONV_DIM = SSD_WIDTH + 2 * SSD_GROUPS * D_STATE
SSD_CHUNK = 128
MLA_HEADS = 8
QK_NOPE = 64
QK_ROPE = 32
QK_DIM = QK_NOPE + QK_ROPE
V_DIM = 64
MLA_WIDTH = MLA_HEADS * V_DIM
Q_LORA = 384
KV_LORA = 256
ROPE_THETA = 10000.0
ATTN_SCALE = QK_DIM ** -0.5
D_FF = 2816
EPS = 1e-6

LANE = 128
SUBLANE = 8
HT = LANE
ROPE_LO = QK_NOPE
ROPE_HALF = QK_ROPE // 2
FF_CHUNK = 256
N_FF_CHUNKS = D_FF // FF_CHUNK
DECODE_PAGES_PER_TILE = 32
ROW_TILE = 512
PREP_ROW_TILE = 1024
SSD_CHUNKS_PER_STEP = 8
FLASH_TILE = 512
FLASH_HEADS_PER_STEP = 8
MiB = 1024 * 1024

PC_Z = 0
PC_XBC = PC_Z + SSD_WIDTH
PC_CQ = PC_XBC + CONV_DIM
PC_CKV = PC_CQ + Q_LORA
PC_KR = PC_CKV + KV_LORA
PC_END = PC_KR + LANE

_NT = (((1,), (1,)), ((), ()))
_TN = (((0,), (0,)), ((), ()))
_HI = lax.Precision.HIGHEST


def _dot(a, b):
    return jnp.dot(a, b, preferred_element_type=f32)


def _dot_nt(a, b):
    return lax.dot_general(a, b, _NT, preferred_element_type=f32)


def _dot_tn(a, b):
    return lax.dot_general(a, b, _TN, preferred_element_type=f32)


def _rms(x, n):
    return lax.rsqrt(jnp.sum(x * x, axis=-1, keepdims=True) * (1.0 / n) + EPS)


def _silu(x):
    return x * jax.nn.sigmoid(x)


def _softplus(x):
    return jnp.maximum(x, 0.0) + jnp.log1p(jnp.exp(-jnp.abs(x)))


HEAD_ROW = (3, 2, 1, 0, 7, 6, 5, 4)


def _fold_heads(part):
    row = lax.broadcasted_iota(jnp.int32, part.shape[1:], 0)
    v = [part[h] for h in range(8)]
    v = [jnp.where(row < 4, v[j] + pltpu.roll(v[j], 4, 0), v[j + 4] + pltpu.roll(v[j + 4], 4, 0)) for j in range(4)]
    v = [jnp.where(row % 4 >= 2, v[j] + pltpu.roll(v[j], 2, 0), v[j + 2] + pltpu.roll(v[j + 2], 6, 0)) for j in range(2)]
    return jnp.where(row % 2 == 1, v[0] + pltpu.roll(v[0], 1, 0), v[1] + pltpu.roll(v[1], 7, 0))


def _const_spec(shape):
    nd = len(shape)
    return pl.BlockSpec(shape, lambda *_: (0,) * nd, pipeline_mode=pl.Buffered(1))


def _params(sem, vmem_mib):
    return pltpu.CompilerParams(dimension_semantics=sem, vmem_limit_bytes=vmem_mib * MiB)


def _proj_body(x_ref, g_ref, w_ref, z_ref, xbc_ref, cq_ref, ckv_ref, kr_ref, dtt_ref):
    x = x_ref[...]
    xn = (x * _rms(x, D_MODEL) * g_ref[...]).astype(bf16)
    z_ref[...] = _dot(xn, w_ref[:, PC_Z:PC_XBC])
    xbc_ref[...] = _dot(xn, w_ref[:, PC_XBC:PC_CQ])
    cq_ref[...] = _dot(xn, w_ref[:, PC_CQ:PC_CKV])
    ckv_ref[...] = _dot(xn, w_ref[:, PC_CKV:PC_KR])
    last = _dot(xn, w_ref[:, PC_KR:PC_END])
    lane = lax.broadcasted_iota(jnp.int32, (1, LANE), 1)
    kr_ref[...] = jnp.where(lane >= ROPE_LO, last, 0.0)
    dtt_ref[...] = last.T[0:16, :]


def _project(x, g, w, tm):
    m = x.shape[0]
    row = lambda n: pl.BlockSpec((tm, n), lambda i: (i, 0))
    widths = (SSD_WIDTH, CONV_DIM, Q_LORA, KV_LORA, LANE)
    return pl.pallas_call(
        _proj_body,
        grid=(m // tm,),
        in_specs=[row(D_MODEL), _const_spec((1, D_MODEL)), _const_spec((D_MODEL, PC_END))],
        out_specs=[row(n) for n in widths] + [pl.BlockSpec((16, tm), lambda i: (0, i))],
        out_shape=[jax.ShapeDtypeStruct((m, n), f32) for n in widths] + [jax.ShapeDtypeStruct((16, m), f32)],
        compiler_params=_params(("parallel",), 40),
        name="proj",
    )(x, g, w)


def _mla_prep_body(cq_ref, ckv_ref, kr_ref, tc_ref, ts1_ref, ts2_ref, gqn_ref, gkv_ref, wuq_ref, wuk_ref, wuv_ref,
                   gq_ref, gk_ref, lat_ref, kro_ref, q_ref, k_ref, vt_ref):
    tc, ts1, ts2 = tc_ref[...], ts1_ref[...], ts2_ref[...]
    tsw = ts1 + ts2

    def rope(x):
        return x * tc + pltpu.roll(x, LANE - ROPE_HALF, 1) * ts1 + pltpu.roll(x, ROPE_HALF, 1) * ts2

    ckv = ckv_ref[...]
    lat = ckv * _rms(ckv, KV_LORA) * gkv_ref[...]
    lat_ref[...] = lat
    kr = rope(kr_ref[...])
    kro_ref[...] = kr
    cq = cq_ref[...]
    cqn = (cq * _rms(cq, Q_LORA) * gqn_ref[...]).astype(bf16)
    latb = lat.astype(bf16)
    vt_ref[...] = _dot_nt(wuv_ref[...], latb).astype(vt_ref.dtype)
    gq, gk = gq_ref[...], gk_ref[...]
    q_all = _dot(cqn, wuq_ref[:, 0:MLA_HEADS * HT])
    qp_all = _dot(cqn, wuq_ref[:, MLA_HEADS * HT:])
    k_all = _dot(latb, wuk_ref[...])
    for h in range(MLA_HEADS):
        sl = slice(h * HT, (h + 1) * HT)
        qh = q_all[:, sl] * tc + qp_all[:, sl] * tsw
        q_ref[:, sl] = (qh * _rms(qh, QK_DIM) * gq).astype(q_ref.dtype)
        kh = k_all[:, sl] + kr
        k_ref[:, sl] = (kh * _rms(kh, QK_DIM) * gk).astype(k_ref.dtype)


def _mla_prep(cq, ckv, kr, tabs, n_tab_blocks, gqn, gkv, wuq, wuk, wuvt, gq, gk, tm, q_dtype):
    m = cq.shape[0]
    row = lambda n: pl.BlockSpec((tm, n), lambda i: (i, 0))
    tab = pl.BlockSpec((tm, LANE), lambda i: (i % n_tab_blocks, 0))
    widths = (KV_LORA, LANE, MLA_HEADS * HT, MLA_HEADS * HT)
    dtypes = (f32, f32, q_dtype, bf16)
    return pl.pallas_call(
        _mla_prep_body,
        grid=(m // tm,),
        in_specs=[row(Q_LORA), row(KV_LORA), row(LANE), tab, tab, tab,
                  _const_spec((1, Q_LORA)), _const_spec((1, KV_LORA)),
                  _const_spec((Q_LORA, 2 * MLA_HEADS * HT)), _const_spec((KV_LORA, MLA_HEADS * HT)),
                  _const_spec((MLA_WIDTH, KV_LORA)), _const_spec((1, HT)), _const_spec((1, HT))],
        out_specs=[row(n) for n in widths] + [pl.BlockSpec((MLA_WIDTH, tm), lambda i: (0, i))],
        out_shape=[jax.ShapeDtypeStruct((m, n), d) for n, d in zip(widths, dtypes)]
                  + [jax.ShapeDtypeStruct((MLA_WIDTH, m), bf16)],
        compiler_params=_params(("parallel",), 40),
        name="mla_prep",
    )(cq, ckv, kr, *tabs, gqn, gkv, wuq, wuk, wuvt, gq, gk)


def _ssd_body(xbc_ref, z_ref, dtt_ref, hist_ref, h0_ref, cw_ref, cb_ref, bias_c_ref,
              alog_c_ref, dsk_ref, ng_ref, tri_ref, y_ref, hout_ref, hfin_ref, xp_ref, hs_ref, *, valid_from, cps):
    c = pl.program_id(1)
    q = SSD_CHUNK

    @pl.when(c == 0)
    def _():
        xp_ref[0:SUBLANE, :] = hist_ref[...]
        hs_ref[...] = h0_ref[0]

    rows = lax.broadcasted_iota(jnp.int32, (q, q), 0)
    cols = lax.broadcasted_iota(jnp.int32, (q, q), 1)
    causal = cols <= rows
    lane = lax.broadcasted_iota(jnp.int32, (1, LANE), 1)
    low = lane < D_STATE
    tri = tri_ref[...]
    dsk = dsk_ref[...]

    def chunk(xbc, dtt, z):
        xp_ref[SUBLANE:SUBLANE + q, :] = xbc
        conv = cb_ref[...]
        for k in range(CONV_W):
            lo = SUBLANE - (CONV_W - 1) + k
            conv = conv + xp_ref[lo:lo + q, :] * cw_ref[k:k + 1, :]
        xp_ref[0:SUBLANE, :] = xbc[q - SUBLANE:q, :]
        xa = _silu(conv)
        xs = xa[:, :SSD_WIDTH]
        bm = xa[:, SSD_WIDTH:SSD_WIDTH + LANE]
        cm = xa[:, SSD_WIDTH + LANE:]

        dtr = _softplus(dtt + bias_c_ref[...])
        if valid_from:
            dtr = jnp.where(lax.broadcasted_iota(jnp.int32, (16, q), 1) >= valid_from, dtr, 0.0)
        dar = dtr * -jnp.exp(alog_c_ref[...])
        dtc = dtr.T
        acs_r = lax.dot_general(dar, tri, _NT, precision=_HI, preferred_element_type=f32)
        acs_c = jnp.dot(tri, dar.T, precision=_HI, preferred_element_type=f32)
        w_c = jnp.exp(acs_c[q - 1:q, :] - acs_c) * dtc
        e_c = jnp.exp(acs_c)
        cd_r = jnp.exp(acs_r[:, q - 1:q])

        bb = bm.astype(bf16)
        ys = []
        for g in range(SSD_GROUPS):
            gmask = (lane >= g * D_STATE) & (lane < (g + 1) * D_STATE)
            cg = jnp.where(gmask, cm, 0.0).astype(bf16)
            cb_g = _dot_nt(cg, bb)
            for pi in range(2):
                i = 2 * g + pi
                xpair = xs[:, i * LANE:(i + 1) * LANE]
                ypair = jnp.zeros((q, LANE), f32)
                for hh in range(2):
                    h = 2 * i + hh
                    seg = acs_c[:, h:h + 1] - acs_r[h:h + 1, :]
                    lmat = jnp.exp(jnp.where(causal, seg, -jnp.inf))
                    sc = (cb_g * lmat * dtr[h:h + 1, :]).astype(bf16)
                    xh = jnp.where(low if hh == 0 else jnp.logical_not(low), xpair, 0.0).astype(bf16)
                    ypair = ypair + _dot(sc, xh)
                h0, h1 = 2 * i, 2 * i + 1
                wp = jnp.where(low, w_c[:, h0:h0 + 1], w_c[:, h1:h1 + 1])
                ep = jnp.where(low, e_c[:, h0:h0 + 1], e_c[:, h1:h1 + 1])
                hst = hs_ref[i]
                yoff = _dot_nt(cg, hst.astype(bf16)) * ep
                st = _dot_tn((xpair * wp).astype(bf16), bb)
                cdb = jnp.concatenate([jnp.broadcast_to(cd_r[h0:h0 + 1, :], (SSD_HEAD_DIM, LANE)),
                                       jnp.broadcast_to(cd_r[h1:h1 + 1, :], (SSD_HEAD_DIM, LANE))], axis=0)
                hs_ref[i] = hst * cdb + st
                ys.append(ypair + yoff + dsk[:, i * LANE:(i + 1) * LANE] * xpair)
        y = jnp.concatenate(ys, axis=1) * _silu(z)
        gw = SSD_WIDTH // SSD_GROUPS
        outs = []
        for g in range(SSD_GROUPS):
            yg = y[:, g * gw:(g + 1) * gw]
            outs.append(yg * _rms(yg, gw))
        return (jnp.concatenate(outs, axis=1) * ng_ref[...]).astype(y_ref.dtype)

    for ci in range(cps):
        rs = slice(ci * q, (ci + 1) * q)
        y_ref[rs, :] = chunk(xbc_ref[rs, :], dtt_ref[:, rs], z_ref[rs, :])

    @pl.when(c == pl.num_programs(1) - 1)
    def _():
        hout_ref[0] = hs_ref[...]
        for i in range(2 * SSD_GROUPS):
            g = i // 2
            for hh in range(2):
                hfin_ref[0, 2 * i + hh] = hs_ref[i, hh * SSD_HEAD_DIM:(hh + 1) * SSD_HEAD_DIM, g * D_STATE:(g + 1) * D_STATE]


def _ssd(xbc, z, dtt, hist_arr, hist_block, h0, consts, n_batch, n_chunks, row_block0, valid_from, cps):
    assert n_chunks % cps == 0
    q = SSD_CHUNK
    n_steps = n_chunks // cps
    rb = lambda n: pl.BlockSpec((cps * q, n), lambda b, c: (row_block0 + b * n_steps + c, 0))
    cw, cb, bias_c, alog_c, dsk, ng, tri = consts
    return pl.pallas_call(
        functools.partial(_ssd_body, valid_from=valid_from, cps=cps),
        grid=(n_batch, n_steps),
        in_specs=[rb(CONV_DIM), rb(SSD_WIDTH),
                  pl.BlockSpec((16, cps * q), lambda b, c: (0, row_block0 + b * n_steps + c)),
                  pl.BlockSpec((SUBLANE, CONV_DIM), lambda b, c: (hist_block, 0)),
                  _const_spec((1, 4, LANE, LANE)),
                  _const_spec((CONV_W, CONV_DIM)), _const_spec((1, CONV_DIM)),
                  _const_spec((16, 1)), _const_spec((16, 1)),
                  _const_spec((1, SSD_WIDTH)), _const_spec((1, SSD_WIDTH)), _const_spec((q, q))],
        out_specs=[pl.BlockSpec((cps * q, SSD_WIDTH), lambda b, c: (b * n_steps + c, 0)),
                   pl.BlockSpec((1, 4, LANE, LANE), lambda b, c: (b, 0, 0, 0)),
                   pl.BlockSpec((1, SSD_HEADS, SSD_HEAD_DIM, D_STATE), lambda b, c: (b, 0, 0, 0))],
        out_shape=[jax.ShapeDtypeStruct((n_batch * n_chunks * q, SSD_WIDTH), bf16),
                   jax.ShapeDtypeStruct((n_batch, 4, LANE, LANE), f32),
                   jax.ShapeDtypeStruct((n_batch, SSD_HEADS, SSD_HEAD_DIM, D_STATE), f32)],
        scratch_shapes=[pltpu.VMEM((SUBLANE + q, CONV_DIM), f32), pltpu.VMEM((4, LANE, LANE), f32)],
        compiler_params=_params(("parallel", "arbitrary"), 32),
        name="ssd",
    )(xbc, z, dtt, hist_arr, h0, cw, cb, bias_c, alog_c, dsk, ng, tri)


def _flash_body(q_ref, k_ref, vt_ref, km_ref, vtm_ref, o_ref, *, tq, tk, nh):
    qi = pl.program_id(2)
    qs = [q_ref[:, h * HT:(h + 1) * HT] for h in range(nh)]
    half = tk // 2

    def heads(kall, vtall, carries, mask=None, qlo=0):
        sts = [_dot_nt(kall[:, h * HT:(h + 1) * HT], qs[h][qlo:, :]) for h in range(nh)]
        out = []
        for h in range(nh):
            m0, l0, acc0 = carries[h]
            m, l, acct = m0[:, qlo:], l0[:, qlo:], acc0[:, qlo:]
            st = sts[h] if mask is None else jnp.where(mask, sts[h], -jnp.inf)
            m2 = jnp.maximum(m, jnp.max(st, axis=0, keepdims=True))
            pt = jnp.exp(st - m2)
            a = jnp.exp(m - m2)
            vth = vtall[h * V_DIM:(h + 1) * V_DIM, :]
            new = (m2, a * l + jnp.sum(pt, axis=0, keepdims=True), a * acct + _dot(vth, pt.astype(bf16)))
            if qlo:
                new = tuple(jnp.concatenate([old[:, :qlo], x], axis=1) for old, x in zip(carries[h], new))
            out.append(new)
        return tuple(out)

    init = (jnp.full((1, tq), -jnp.inf, f32), jnp.zeros((1, tq), f32), jnp.zeros((V_DIM, tq), f32))

    def body(j, carries):
        off = pl.multiple_of(j * tk, tk)
        return heads(k_ref[pl.ds(off, tk), :], vt_ref[:, pl.ds(off, tk)], carries)

    carries = lax.fori_loop(0, qi, body, (init,) * nh)
    off = pl.multiple_of(qi * tk, tk)
    k_a = jnp.concatenate([k_ref[pl.ds(off, half), :], km_ref[...]], axis=0)
    vt_a = jnp.concatenate([vt_ref[:, pl.ds(off, half)], vtm_ref[...]], axis=1)
    krow = lax.broadcasted_iota(jnp.int32, (half + N_META, tq), 0)
    qcol = lax.broadcasted_iota(jnp.int32, (half + N_META, tq), 1)
    carries = heads(k_a, vt_a, carries, (krow >= half) | (krow <= qcol))
    off_b = pl.multiple_of(qi * tk + half, half)
    krow = lax.broadcasted_iota(jnp.int32, (half, tq - half), 0)
    qcol = lax.broadcasted_iota(jnp.int32, (half, tq - half), 1)
    res = heads(k_ref[pl.ds(off_b, half), :], vt_ref[:, pl.ds(off_b, half)], carries, krow <= qcol, qlo=half)
    for pr in range(nh // 2):
        (_, l0, acc0), (_, l1, acc1) = res[2 * pr], res[2 * pr + 1]
        pair = jnp.concatenate([acc0 / l0, acc1 / l1], axis=0)
        o_ref[:, pr * LANE:(pr + 1) * LANE] = pair.T.astype(o_ref.dtype)


def _flash(q, k, vt, k_small, vt_meta, meta_block, n_batch, seq, tq, tk, nh):
    assert tk == tq and seq % tk == 0 and nh % 2 == 0 and MLA_HEADS % nh == 0
    nq = seq // tq
    return pl.pallas_call(
        functools.partial(_flash_body, tq=tq, tk=tk, nh=nh),
        grid=(n_batch, MLA_HEADS // nh, nq),
        in_specs=[pl.BlockSpec((tq, nh * HT), lambda b, p, i: (b * nq + i, p)),
                  pl.BlockSpec((seq, nh * HT), lambda b, p, i: (b, p)),
                  pl.BlockSpec((nh * V_DIM, seq), lambda b, p, i: (p, b)),
                  pl.BlockSpec((N_META, nh * HT), lambda b, p, i: (meta_block, p)),
                  pl.BlockSpec((nh * V_DIM, N_META), lambda b, p, i: (p, 0))],
        out_specs=pl.BlockSpec((tq, nh * V_DIM), lambda b, p, i: (b * nq + i, p)),
        out_shape=jax.ShapeDtypeStruct((n_batch * seq, MLA_WIDTH), f32),
        compiler_params=_params(("parallel", "parallel", "arbitrary"), 40),
        name="flash",
    )(q, k, vt, k_small, vt_meta)


def _finish_body(x_ref, ssd_ref, att_ref, gm_ref, wo_ref, gf_ref, wg_ref, wu_ref, wd_ref, o_ref):
    att = att_ref[...]
    mla = (att * _rms(att, MLA_WIDTH) * gm_ref[...]).astype(bf16)
    h = x_ref[...] + (_dot(ssd_ref[...], wo_ref[0:SSD_WIDTH, :]) + _dot(mla, wo_ref[SSD_WIDTH:, :]))
    n = (h * _rms(h, D_MODEL) * gf_ref[...]).astype(bf16)
    ff = jnp.zeros_like(h)
    for c in range(N_FF_CHUNKS):
        cs = slice(c * FF_CHUNK, (c + 1) * FF_CHUNK)
        a = (_silu(_dot(n, wg_ref[:, cs])) * _dot(n, wu_ref[:, cs])).astype(bf16)
        ff = ff + _dot(a, wd_ref[cs, :])
    o_ref[...] = h + ff


def _finish(x, ssd, att, gm, wo, gf, wg, wu, wd, tm):
    m = x.shape[0]
    row = lambda n: pl.BlockSpec((tm, n), lambda i: (i, 0))
    return pl.pallas_call(
        _finish_body,
        grid=(m // tm,),
        in_specs=[row(D_MODEL), row(SSD_WIDTH), row(MLA_WIDTH), _const_spec((1, MLA_WIDTH)),
                  _const_spec((D_MODEL, D_MODEL)), _const_spec((1, D_MODEL)),
                  _const_spec((D_MODEL, D_FF)), _const_spec((D_MODEL, D_FF)), _const_spec((D_FF, D_MODEL))],
        out_specs=row(D_MODEL),
        out_shape=jax.ShapeDtypeStruct((m, D_MODEL), f32),
        compiler_params=_params(("parallel",), 56),
        name="finish",
    )(x, ssd, att, gm, wo, gf, wg, wu, wd)


def _sssd_prep_body(xbc_ref, sc_ref, cw_ref, cb_ref, xs_ref, xat_ref):
    conv = cb_ref[...]
    for k in range(CONV_W - 1):
        conv = conv + sc_ref[k] * cw_ref[k:k + 1, :]
    conv = conv + xbc_ref[...] * cw_ref[CONV_W - 1:CONV_W, :]
    xa = _silu(conv)
    xs_ref[...] = xa[:, :SSD_WIDTH]
    xat_ref[...] = xa.T


def _sssd_prep(xbc_small, sc, cw, cb, nseq):
    return pl.pallas_call(
        _sssd_prep_body,
        grid=(1,),
        in_specs=[pl.BlockSpec((nseq, CONV_DIM), lambda i: (0, 0)), _const_spec((CONV_W - 1, nseq, CONV_DIM)),
                  _const_spec((CONV_W, CONV_DIM)), _const_spec((1, CONV_DIM))],
        out_specs=[pl.BlockSpec((nseq, SSD_WIDTH), lambda i: (0, 0)), pl.BlockSpec((CONV_DIM, nseq), lambda i: (0, 0))],
        out_shape=[jax.ShapeDtypeStruct((nseq, SSD_WIDTH), f32), jax.ShapeDtypeStruct((CONV_DIM, nseq), f32)],
        compiler_params=_params(("arbitrary",), 32),
        name="sssd_prep",
    )(xbc_small, sc, cw, cb)


def _sssd_state_body(h0_ref, xst_ref, bt_ref, ct_ref, dtt_ref, bias_c_ref, alog_c_ref, hn_ref, yt_ref):
    h = pl.program_id(0)
    dt = _softplus(dtt_ref[pl.ds(h, 1), :] + bias_c_ref[pl.ds(h, 1), :])
    dec = jnp.exp(dt * -jnp.exp(alog_c_ref[pl.ds(h, 1), :]))
    bt, ct = bt_ref[...], ct_ref[...]

    def body(p, carry):
        xdt = xst_ref[pl.ds(p, 1), :] * dt
        hn = h0_ref[0, p] * dec + xdt * bt
        hn_ref[0, p] = hn
        yt_ref[pl.ds(p, 1), :] = jnp.sum(ct * hn, axis=0, keepdims=True)
        return carry

    lax.fori_loop(0, SSD_HEAD_DIM, body, 0, unroll=4)


def _sssd_state(h0t, xat, dtt_small, bias_c, alog_c, nseq):
    hpg = SSD_HEADS // SSD_GROUPS
    rows = lambda f: pl.BlockSpec((SSD_HEAD_DIM, nseq), f)
    hblk = pl.BlockSpec((1, SSD_HEAD_DIM, D_STATE, nseq), lambda h: (h, 0, 0, 0))
    return pl.pallas_call(
        _sssd_state_body,
        grid=(SSD_HEADS,),
        in_specs=[hblk, rows(lambda h: (h, 0)), rows(lambda h: (SSD_HEADS + h // hpg, 0)),
                  rows(lambda h: (SSD_HEADS + SSD_GROUPS + h // hpg, 0)),
                  pl.BlockSpec((16, nseq), lambda h: (0, 0)), _const_spec((16, 1)), _const_spec((16, 1))],
        out_specs=[hblk, rows(lambda h: (h, 0))],
        out_shape=[jax.ShapeDtypeStruct((SSD_HEADS, SSD_HEAD_DIM, D_STATE, nseq), f32),
                   jax.ShapeDtypeStruct((SSD_WIDTH, nseq), f32)],
        compiler_params=_params(("parallel",), 32),
        name="sssd_state",
    )(h0t, xat, xat, xat, dtt_small, bias_c, alog_c)


def _sssd_gate_body(yt_ref, xs_ref, z_ref, dsk_ref, ng_ref, o_ref):
    y = (yt_ref[...].T + dsk_ref[...] * xs_ref[...]) * _silu(z_ref[...])
    gw = SSD_WIDTH // SSD_GROUPS
    outs = []
    for g in range(SSD_GROUPS):
        yg = y[:, g * gw:(g + 1) * gw]
        outs.append(yg * _rms(yg, gw))
    o_ref[...] = (jnp.concatenate(outs, axis=1) * ng_ref[...]).astype(o_ref.dtype)


def _sssd_gate(yt, xs, z_small, dsk, ng, nseq):
    blk = pl.BlockSpec((nseq, SSD_WIDTH), lambda i: (0, 0))
    return pl.pallas_call(
        _sssd_gate_body,
        grid=(1,),
        in_specs=[pl.BlockSpec((SSD_WIDTH, nseq), lambda i: (0, 0)), blk, blk,
                  _const_spec((1, SSD_WIDTH)), _const_spec((1, SSD_WIDTH))],
        out_specs=blk,
        out_shape=jax.ShapeDtypeStruct((nseq, SSD_WIDTH), bf16),
        compiler_params=_params(("arbitrary",), 32),
        name="sssd_gate",
    )(yt, xs, z_small, dsk, ng)


def _absorb_body(q_ref, gk_ref, wabs_ref, lat_ref, kr_ref, wuk_ref, qg_ref, a_ref, snew_ref):
    gk = gk_ref[...]
    latb = lat_ref[...].astype(bf16)
    latf = latb.astype(f32)
    kr = kr_ref[...]
    krf = kr.astype(bf16).astype(f32)
    kr2 = jnp.sum(kr * kr, axis=-1, keepdims=True)
    s_new = []
    for h in range(MLA_HEADS):
        sl = slice(h * HT, (h + 1) * HT)
        qg = q_ref[:, sl] * gk
        qg_ref[:, sl] = qg
        a = jnp.dot(qg, wabs_ref[sl, :], precision=_HI, preferred_element_type=f32)
        a_ref[:, h * KV_LORA:(h + 1) * KV_LORA] = a
        kn = _dot(latb, wuk_ref[:, sl])
        n2 = jnp.sum(kn * kn, axis=-1, keepdims=True)
        s12 = (jnp.sum(a.astype(bf16).astype(f32) * latf, axis=-1, keepdims=True)
               + jnp.sum(qg.astype(bf16).astype(f32) * krf, axis=-1, keepdims=True))
        s_new.append(lax.rsqrt((n2 + kr2) * (1.0 / QK_DIM) + EPS) * s12)
    a_ref[:, MLA_HEADS * KV_LORA:] = jnp.zeros((q_ref.shape[0], (16 - MLA_HEADS) * KV_LORA), f32)
    snew_ref[...] = jnp.concatenate(s_new, axis=1)


def _absorb(q_small, gk, wabs, lat_small, kr_small, wuk, nseq):
    blk = lambda n: pl.BlockSpec((nseq, n), lambda i: (0, 0))
    return pl.pallas_call(
        _absorb_body,
        grid=(1,),
        in_specs=[blk(MLA_HEADS * HT), _const_spec((1, HT)), _const_spec((MLA_HEADS * HT, KV_LORA)),
                  blk(KV_LORA), blk(HT), _const_spec((KV_LORA, MLA_HEADS * HT))],
        out_specs=[blk(MLA_HEADS * HT), blk(16 * KV_LORA), blk(MLA_HEADS)],
        out_shape=[jax.ShapeDtypeStruct((nseq, MLA_HEADS * HT), f32), jax.ShapeDtypeStruct((nseq, 16 * KV_LORA), f32),
                   jax.ShapeDtypeStruct((nseq, MLA_HEADS), f32)],
        compiler_params=_params(("arbitrary",), 32),
        name="absorb",
    )(q_small, gk, wabs, lat_small, kr_small, wuk)


def _decode_body(pt_ref, a_ref, c_ref, latn_ref, snew_ref, wukt_ref, clat_ref, ckrt_ref, o_ref,
                 lat_buf, kr_buf, waug, latb, s_buf, sems, *, n_pages, page, ppt):
    b = pl.program_id(0)
    nb = pl.num_programs(0)
    slot = b % 2
    n_tiles = n_pages // ppt
    tile = ppt * page
    wrows = MLA_HEADS * QK_NOPE

    def start_page(seq, p, sl, prio=0):
        pid = pt_ref[seq * n_pages + p]
        pltpu.make_async_copy(clat_ref.at[0, pid], lat_buf.at[sl, p], sems.at[0, sl]).start(priority=prio)
        pltpu.make_async_copy(ckrt_ref.at[0, pid], kr_buf.at[sl, p], sems.at[1, sl]).start(priority=prio)

    @pl.when(b == 0)
    def _():
        waug[0:wrows, :] = wukt_ref[...]

        def first(p, carry):
            start_page(0, p, 0)
            return carry

        lax.fori_loop(0, n_pages, first, 0)

    pltpu.make_async_copy(clat_ref.at[0, pl.ds(0, n_pages)], lat_buf.at[slot], sems.at[0, slot]).wait()
    pltpu.make_async_copy(ckrt_ref.at[0, pl.ds(0, n_pages)], kr_buf.at[slot], sems.at[1, slot]).wait()

    waug[wrows:wrows + 16, :] = a_ref[0, 0].astype(bf16)
    cb = c_ref[0].astype(bf16)

    def score_dots(lb, krt):
        kt = _dot_nt(waug[...], lb)
        return kt, _dot(cb, krt.astype(bf16)), krt

    def score_finish(kt, s2, krt):
        n = kt.shape[1]
        k4 = kt[0:wrows, :].reshape(MLA_HEADS, QK_NOPE // SUBLANE, SUBLANE, n)
        n2 = _fold_heads(jnp.sum(k4 * k4, axis=1))
        kr2 = jnp.sum(krt * krt, axis=0, keepdims=True)
        return lax.rsqrt((n2 + kr2) * (1.0 / QK_DIM) + EPS) * (kt[wrows:wrows + MLA_HEADS, :] + s2)

    def body(j, carry):
        @pl.when(b + 1 < nb)
        def _():
            for pp in range(ppt):
                start_page(b + 1, j * ppt + pp, 1 - slot, pp % 2)

        lb = lat_buf[slot, pl.ds(j * ppt, ppt)].reshape(tile, KV_LORA).astype(bf16)
        latb[j] = lb
        kr_pages = kr_buf[slot, pl.ds(j * ppt, ppt)]
        s_buf[j] = score_finish(*score_dots(lb, jnp.concatenate([kr_pages[i] for i in range(ppt)], axis=-1)))
        return carry

    lax.fori_loop(0, n_tiles, body, 0)
    s_new = snew_ref[0]
    s_all = jnp.concatenate([s_buf[j] for j in range(n_tiles)], axis=-1)
    m = jnp.maximum(jnp.max(s_all, axis=-1, keepdims=True), s_new)
    p_all = jnp.exp(s_all - m)
    p_new = jnp.exp(s_new - m)
    l = jnp.sum(p_all, axis=-1, keepdims=True) + p_new
    acc = _dot(p_all.astype(bf16), latb[...].reshape(n_tiles * tile, KV_LORA)) + p_new * latn_ref[0]
    o_ref[0] = acc / l


def _decode(page_table_flat, a, c, lat_new, s_new, wukt, cache_lat, cache_kr_t, nseq, n_pages, page, ppt):
    assert n_pages % ppt == 0
    wrows = MLA_HEADS * QK_NOPE
    grid_spec = pltpu.PrefetchScalarGridSpec(
        num_scalar_prefetch=1,
        grid=(nseq,),
        in_specs=[pl.BlockSpec((1, 1, 16, KV_LORA), lambda b, pt: (b, 0, 0, 0)),
                  pl.BlockSpec((1, MLA_HEADS, QK_ROPE), lambda b, pt: (b, 0, 0)),
                  pl.BlockSpec((1, 1, KV_LORA), lambda b, pt: (b, 0, 0)),
                  pl.BlockSpec((1, MLA_HEADS, 1), lambda b, pt: (b, 0, 0)),
                  pl.BlockSpec((wrows, KV_LORA), lambda b, pt: (0, 0), pipeline_mode=pl.Buffered(1)),
                  pl.BlockSpec(memory_space=pl.ANY), pl.BlockSpec(memory_space=pl.ANY)],
        out_specs=pl.BlockSpec((1, MLA_HEADS, KV_LORA), lambda b, pt: (b, 0, 0)),
        scratch_shapes=[pltpu.VMEM((2, n_pages, page, KV_LORA), f32), pltpu.VMEM((2, n_pages, QK_ROPE, page), f32),
                        pltpu.VMEM((wrows + 16, KV_LORA), bf16),
                        pltpu.VMEM((n_pages // ppt, ppt * page, KV_LORA), bf16),
                        pltpu.VMEM((n_pages // ppt, MLA_HEADS, ppt * page), f32), pltpu.SemaphoreType.DMA((2, 2))],
    )
    return pl.pallas_call(
        functools.partial(_decode_body, n_pages=n_pages, page=page, ppt=ppt),
        grid_spec=grid_spec,
        out_shape=jax.ShapeDtypeStruct((nseq, MLA_HEADS, KV_LORA), f32),
        compiler_params=_params(("arbitrary",), 40),
        name="decode",
    )(page_table_flat, a, c, lat_new, s_new, wukt, cache_lat, cache_kr_t)


def _uv_body(o_ref, wuv_ref, att_ref):
    for h in range(MLA_HEADS):
        oh = o_ref[:, h * KV_LORA:(h + 1) * KV_LORA].astype(bf16)
        att_ref[:, h * V_DIM:(h + 1) * V_DIM] = _dot(oh, wuv_ref[:, h * V_DIM:(h + 1) * V_DIM])


def _uv(o_flat, wuv):
    nseq = o_flat.shape[0]
    return pl.pallas_call(
        _uv_body,
        grid=(1,),
        in_specs=[pl.BlockSpec((nseq, MLA_HEADS * KV_LORA), lambda i: (0, 0)), _const_spec((KV_LORA, MLA_WIDTH))],
        out_specs=pl.BlockSpec((nseq, MLA_WIDTH), lambda i: (0, 0)),
        out_shape=jax.ShapeDtypeStruct((nseq, MLA_WIDTH), f32),
        compiler_params=_params(("arbitrary",), 32),
        name="uv",
    )(o_flat, wuv)


def _rope_tables(pos):
    inv = ROPE_THETA ** (-jnp.arange(ROPE_HALF, dtype=f32) * (2.0 / QK_ROPE))
    ang = pos.astype(f32)[:, None] * inv[None, :]
    cos, sin = jnp.cos(ang), jnp.sin(ang)
    n = pos.shape[0]
    one = jnp.ones((n, QK_NOPE), f32)
    z16 = jnp.zeros((n, ROPE_HALF), f32)
    z32 = jnp.zeros((n, HT - QK_DIM), f32)
    z64 = jnp.zeros((n, QK_NOPE), f32)
    tc = jnp.concatenate([one, cos, cos, z32], axis=1)
    ts1 = jnp.concatenate([z64, -sin, z16, z32], axis=1)
    ts2 = jnp.concatenate([z64, z16, sin, z32], axis=1)
    return tc, ts1, ts2


def _pad_lanes(x, n):
    return jnp.pad(x, ((0, 0), (0, n - x.shape[1])))


def kernel(x_prompt, x_sample, cache_kv_latent, cache_k_rope, state_conv, state_ssm, page_table, meta_tokens, attn_norm_g, w_in, conv_w, conv_b, dt_bias, a_log, d_skip, ssd_norm_g, q_norm_g, w_uq, kv_norm_g, w_ukv, q_head_g_nope, q_head_g_rope, k_head_g_nope, k_head_g_rope, mla_out_g, w_out, ffn_norm_g, w_gate, w_up, w_down):
    l = 0
    nb, seq, _ = x_prompt.shape
    nseq = x_sample.shape[0]
    n_pages, page = page_table.shape[1], cache_kv_latent.shape[2]
    past = n_pages * page
    n_small = 2 * LANE
    meta_lo = n_small - N_META

    s0, s1, s2, s3, s4 = (SSD_WIDTH, SSD_WIDTH + CONV_DIM, SSD_WIDTH + CONV_DIM + SSD_HEADS,
                          SSD_WIDTH + CONV_DIM + SSD_HEADS + Q_LORA, SSD_WIDTH + CONV_DIM + SSD_HEADS + Q_LORA + KV_LORA)
    wi = w_in[l]
    w_last = jnp.concatenate([_pad_lanes(wi[:, s1:s2], ROPE_LO), _pad_lanes(wi[:, s4:], HT - ROPE_LO)], axis=1)
    w_proj = jnp.concatenate([wi[:, :s1], wi[:, s2:s4], w_last], axis=1).astype(bf16)
    g_attn = attn_norm_g[l][None, :]
    wq3 = w_uq[l].reshape(Q_LORA, MLA_HEADS, QK_DIM)
    zq = jnp.zeros((Q_LORA, MLA_HEADS, QK_NOPE), f32)
    wq_partner = jnp.concatenate([zq, wq3[:, :, QK_NOPE + ROPE_HALF:], wq3[:, :, QK_NOPE:QK_NOPE + ROPE_HALF]], axis=2)
    pad_head = lambda w: jnp.pad(w, ((0, 0), (0, 0), (0, HT - QK_DIM))).reshape(Q_LORA, -1)
    wuq = jnp.concatenate([pad_head(wq3), pad_head(wq_partner)], axis=1).astype(bf16)
    wkv = w_ukv[l].reshape(KV_LORA, MLA_HEADS, QK_NOPE + V_DIM)
    wuk_f32 = jnp.pad(wkv[:, :, :QK_NOPE], ((0, 0), (0, 0), (0, HT - QK_NOPE))).reshape(KV_LORA, -1)
    wuk = wuk_f32.astype(bf16)
    wuv = wkv[:, :, QK_NOPE:].reshape(KV_LORA, MLA_WIDTH).astype(bf16)
    wukt = wkv[:, :, :QK_NOPE].reshape(KV_LORA, -1).T.astype(bf16)
    gq = _pad_lanes(jnp.concatenate([q_head_g_nope[l], q_head_g_rope[l], q_head_g_rope[l]])[None, :], HT) * ATTN_SCALE
    gk = _pad_lanes(jnp.concatenate([k_head_g_nope[l], k_head_g_rope[l], k_head_g_rope[l]])[None, :], HT)
    gqn, gkv = q_norm_g[l][None, :], kv_norm_g[l][None, :]
    cw, cb = conv_w[l], conv_b[l][None, :]
    bias_c = jnp.pad(dt_bias[l][:, None], ((0, 16 - SSD_HEADS), (0, 0)))
    alog_c = jnp.pad(a_log[l][:, None], ((0, 16 - SSD_HEADS), (0, 0)))
    dsk = jnp.repeat(d_skip[l], SSD_HEAD_DIM)[None, :]
    ng = ssd_norm_g[l][None, :]
    tri = jnp.tril(jnp.ones((SSD_CHUNK, SSD_CHUNK), f32))
    ssd_consts = (cw, cb, bias_c, alog_c, dsk, ng, tri)
    gm, gf = mla_out_g[l][None, :], ffn_norm_g[l][None, :]
    wo = w_out[l].astype(bf16)
    wg, wu, wd = w_gate[l].astype(bf16), w_up[l].astype(bf16), w_down[l].astype(bf16)

    xp = x_prompt.reshape(nb * seq, D_MODEL)
    xs_rows = x_sample[:, 0, :]
    x_small = jnp.concatenate([xs_rows, jnp.zeros((meta_lo - nseq, D_MODEL), f32), meta_tokens.astype(f32)], axis=0)

    z_p, xbc_p, cq_p, ckv_p, kr_p, dtt_p = _project(xp, g_attn, w_proj, PREP_ROW_TILE)
    z_s, xbc_s, cq_s, ckv_s, kr_s, dtt_s = _project(x_small, g_attn, w_proj, n_small)

    tabs_p = _rope_tables(N_META + jnp.arange(seq))
    pos_small = jnp.concatenate([jnp.full((nseq,), past), jnp.zeros((meta_lo - nseq,), jnp.int32), jnp.arange(N_META)])
    tabs_s = _rope_tables(pos_small)
    tm = PREP_ROW_TILE
    lat_p, kro_p, q_p, k_p, vt_p = _mla_prep(cq_p, ckv_p, kr_p, tabs_p, seq // tm, gqn, gkv, wuq, wuk, wuv.T, gq, gk, tm, bf16)
    lat_s, kro_s, q_s, k_s, vt_s = _mla_prep(cq_s, ckv_s, kr_s, tabs_s, 1, gqn, gkv, wuq, wuk, wuv.T, gq, gk, n_small, f32)

    zero_h = jnp.zeros((1, 4, LANE, LANE), f32)
    zero_hist = jnp.zeros((SUBLANE, CONV_DIM), f32)
    _, h_meta, _ = _ssd(xbc_s, z_s, dtt_s, zero_hist, 0, zero_h, ssd_consts, 1, 1, 1, SSD_CHUNK - N_META, 1)
    ssd_p, _, h_fin = _ssd(xbc_p, z_p, dtt_p, xbc_s, n_small // SUBLANE - 1, h_meta, ssd_consts, nb, seq // SSD_CHUNK, 0, 0,
                           SSD_CHUNKS_PER_STEP)

    att_p = _flash(q_p, k_p, vt_p, k_s, vt_s[:, meta_lo:], n_small // N_META - 1, nb, seq, FLASH_TILE, FLASH_TILE,
                   FLASH_HEADS_PER_STEP)
    y_prompt = _finish(xp, ssd_p, att_p, gm, wo, gf, wg, wu, wd, ROW_TILE).reshape(nb, seq, D_MODEL)

    sc = jnp.transpose(state_conv[l], (1, 0, 2))
    xs_s, xat_s = _sssd_prep(xbc_s, sc, cw, cb, nseq)
    h_new_t, yt_s = _sssd_state(jnp.transpose(state_ssm[l], (1, 2, 3, 0)), xat_s, dtt_s, bias_c, alog_c, nseq)
    h_new = jnp.transpose(h_new_t, (3, 0, 1, 2))
    ssd_s = _sssd_gate(yt_s, xs_s, z_s, dsk, ng, nseq)
    qg, a_abs, s_new = _absorb(q_s, gk, wuk_f32.T, lat_s, kro_s, wuk, nseq)
    hr = list(HEAD_ROW)
    a_abs = a_abs.reshape(nseq, 16, KV_LORA)[:, hr + list(range(MLA_HEADS, 16))].reshape(nseq, 1, 16, KV_LORA)
    c_abs = qg.reshape(nseq, MLA_HEADS, HT)[:, hr, ROPE_LO:QK_DIM]
    lat_new = lat_s[:nseq][:, None, :]
    kr_new = kro_s[:nseq, ROPE_LO:QK_DIM][:, None, :]
    o_lat = _decode(page_table.reshape(-1), a_abs, c_abs, lat_new, s_new[:, hr, None], wukt,
                    cache_kv_latent, jnp.swapaxes(cache_k_rope, 2, 3), nseq, n_pages, page, DECODE_PAGES_PER_TILE)
    att_s = _uv(o_lat[:, hr].reshape(nseq, MLA_HEADS * KV_LORA), wuv)
    y_sample = _finish(xs_rows, ssd_s, att_s, gm, wo, gf, wg, wu, wd, nseq)[:, None, :]

    def with_meta(small, main, width):
        meta = jnp.broadcast_to(small[meta_lo:][None], (nb, N_META, width))
        return jnp.concatenate([meta, main.reshape(nb, seq, width)], axis=1)[None]

    kv_latent_prompt = with_meta(lat_s, lat_p, KV_LORA)
    k_rope_prompt = with_meta(kro_s[:, ROPE_LO:QK_DIM], kro_p[:, ROPE_LO:QK_DIM], QK_ROPE)
    conv_prompt = xbc_p.reshape(nb, seq, CONV_DIM)[:, seq - (CONV_W - 1):][None]
    ssm_prompt = h_fin[None]
    kv_latent_sample = lat_new[None]
    k_rope_sample = kr_new[None]
    conv_sample = jnp.concatenate([state_conv[l][:, 1:], xbc_s[:nseq][:, None, :]], axis=1)[None]
    ssm_sample = h_new[None]
    return (y_prompt, y_sample, kv_latent_prompt, k_rope_prompt, conv_prompt, ssm_prompt.astype(x_prompt.dtype),
            kv_latent_sample, k_rope_sample, conv_sample, ssm_sample.astype(state_ssm.dtype))
```
